```python
import math
import jax, jax.numpy as jnp
from jax import lax
import numpy as np

D_MODEL = 2048
BATCH = 16
SEQ = 2048
DEPTH = 4

D_MIX = D_MODEL
ATT_WIDTH = D_MIX // 2
SSM_WIDTH = D_MIX - ATT_WIDTH
ATT_HEAD_DIM = 128
ATT_HEADS = ATT_WIDTH // ATT_HEAD_DIM
ATT_BLOCK = 128
SSM_HEAD_DIM = 64
SSM_HEADS = SSM_WIDTH // SSM_HEAD_DIM
SSM_GROUPS = 2
SSM_HEADS_PER_GROUP = SSM_HEADS // SSM_GROUPS
SSM_STATE = 128
CONV_WIDTH = 4
CONV_DIM = SSM_WIDTH + 2 * SSM_GROUPS * SSM_STATE
SSD_CHUNK = 128
D_FF = -(-8 * D_MODEL // (3 * 256)) * 256
IN_DIM = 3 * ATT_WIDTH + SSM_WIDTH + CONV_DIM + SSM_HEADS
EPS = 1e-6

kernel_name = 'hymba_stickbreaking_ssd_swiglu'


def rmsnorm(x, g):
    xf = x.astype(jnp.float32)
    y = xf * lax.rsqrt(jnp.mean(xf * xf, axis=-1, keepdims=True) + EPS)
    return (y * g.astype(jnp.float32)).astype(x.dtype)


def stick_breaking_attention(q, k, v):
    bsz, s, h, dh = q.shape
    nb = s // ATT_BLOCK
    scale = dh ** -0.5
    qf = q.astype(jnp.float32).transpose(0, 2, 1, 3)
    kf = k.astype(jnp.float32).transpose(0, 2, 1, 3)
    vf = v.astype(jnp.float32).transpose(0, 2, 1, 3)
    q_blocks = qf.reshape(bsz, h, nb, ATT_BLOCK, dh).transpose(2, 0, 1, 3, 4)
    key_pos = jnp.arange(s)

    def one_block(args):
        qb, blk = args
        z = jnp.einsum('bhqd,bhkd->bhqk', qb, kf) * scale
        query_pos = blk * ATT_BLOCK + jnp.arange(ATT_BLOCK)
        mask = key_pos[None, :] < query_pos[:, None]
        log_beta = jax.nn.log_sigmoid(z)
        log_remain = jnp.where(mask, log_beta - z, 0.0)
        later = lax.cumsum(log_remain, axis=3, reverse=True) - log_remain
        weights = jnp.where(mask, jnp.exp(log_beta + later), 0.0)
        return jnp.einsum('bhqk,bhkd->bhqd', weights, vf)

    out = lax.map(one_block, (q_blocks, jnp.arange(nb)))
    out = out.transpose(1, 0, 3, 2, 4).reshape(bsz, s, h, dh)
    return out.astype(q.dtype)


def causal_depthwise_conv(u, w, bias):
    s = u.shape[1]
    up = jnp.pad(u, ((0, 0), (CONV_WIDTH - 1, 0), (0, 0)))
    out = bias
    for i in range(CONV_WIDTH):
        out = out + up[:, i:i + s] * w[i]
    return out


def ssd_chunked(x, dt, a, b_in, c_in, d_skip):
    bsz, s, h, p = x.shape
    nc = s // SSD_CHUNK
    g, hg, n = SSM_GROUPS, SSM_HEADS_PER_GROUP, SSM_STATE
    xf = x.astype(jnp.float32)
    dtf = dt.astype(jnp.float32)
    xdt = (xf * dtf[..., None]).reshape(bsz, nc, SSD_CHUNK, g, hg, p)
    bc = b_in.astype(jnp.float32).reshape(bsz, nc, SSD_CHUNK, g, n)
    cc = c_in.astype(jnp.float32).reshape(bsz, nc, SSD_CHUNK, g, n)
    da = (dtf * a.astype(jnp.float32)).reshape(bsz, nc, SSD_CHUNK, g, hg)
    a_cum = jnp.cumsum(da, axis=2)
    pos = jnp.arange(SSD_CHUNK)
    causal = (pos[:, None] >= pos[None, :])[:, :, None, None]
    seg = a_cum[:, :, :, None] - a_cum[:, :, None, :]
    decay = jnp.exp(jnp.where(causal, seg, -jnp.inf))
    cb = jnp.einsum('bclgn,bcsgn->bclsg', cc, bc)
    y_diag = jnp.einsum('bclsg,bclsgi,bcsgip->bclgip', cb, decay, xdt)
    decay_to_end = jnp.exp(a_cum[:, :, -1:] - a_cum)
    states = jnp.einsum('bcsgn,bcsgi,bcsgip->bcgipn', bc, decay_to_end, xdt)
    chunk_decay = jnp.exp(a_cum[:, :, -1])

    def step(h_prev, inp):
        st, dec = inp
        return h_prev * dec[..., None, None] + st, h_prev

    h0 = jnp.zeros((bsz, g, hg, p, n), jnp.float32)
    _, h_in = lax.scan(step, h0, (states.transpose(1, 0, 2, 3, 4, 5), chunk_decay.transpose(1, 0, 2, 3)))
    h_in = h_in.transpose(1, 0, 2, 3, 4, 5)
    y_off = jnp.einsum('bclgn,bcgipn,bclgi->bclgip', cc, h_in, jnp.exp(a_cum))
    y = (y_diag + y_off).reshape(bsz, s, h, p) + xf * d_skip.astype(jnp.float32)[:, None]
    return y


def hybrid_layer(x, norm_mix, w_in, q_gain, k_gain, conv_w, conv_b, dt_bias, a_log, d_skip,
                 attn_out_gain, ssm_out_gain, w_out, norm_ffn, w_gate, w_up, w_down):
    bsz, s, _ = x.shape
    h = rmsnorm(x, norm_mix)
    proj = h @ w_in
    splits = [ATT_WIDTH, 2 * ATT_WIDTH, 3 * ATT_WIDTH, 3 * ATT_WIDTH + SSM_WIDTH,
              3 * ATT_WIDTH + SSM_WIDTH + CONV_DIM]
    q, k, v, z, xbc, dt = jnp.split(proj, splits, axis=-1)

    q = rmsnorm(q.reshape(bsz, s, ATT_HEADS, ATT_HEAD_DIM), q_gain)
    k = rmsnorm(k.reshape(bsz, s, ATT_HEADS, ATT_HEAD_DIM), k_gain)
    v = v.reshape(bsz, s, ATT_HEADS, ATT_HEAD_DIM)
    o_att = stick_breaking_attention(q, k, v).reshape(bsz, s, ATT_WIDTH)
    o_att = rmsnorm(o_att, attn_out_gain)

    xbc = jax.nn.silu(causal_depthwise_conv(xbc, conv_w, conv_b))
    xs, bm, cm = jnp.split(xbc, [SSM_WIDTH, SSM_WIDTH + SSM_GROUPS * SSM_STATE], axis=-1)
    dt = jax.nn.softplus(dt.astype(jnp.float32) + dt_bias.astype(jnp.float32))
    a = -jnp.exp(a_log.astype(jnp.float32))
    y = ssd_chunked(xs.reshape(bsz, s, SSM_HEADS, SSM_HEAD_DIM), dt, a,
                    bm.reshape(bsz, s, SSM_GROUPS, SSM_STATE),
                    cm.reshape(bsz, s, SSM_GROUPS, SSM_STATE), d_skip)
    y = y.reshape(bsz, s, SSM_WIDTH) * jax.nn.silu(z.astype(jnp.float32))
    yg = y.reshape(bsz, s, SSM_GROUPS, SSM_WIDTH // SSM_GROUPS)
    yg = yg * lax.rsqrt(jnp.mean(yg * yg, axis=-1, keepdims=True) + EPS)
    o_ssm = (yg.reshape(bsz, s, SSM_WIDTH) * ssm_out_gain.astype(jnp.float32)).astype(x.dtype)

    x = x + jnp.concatenate([o_att, o_ssm], axis=-1) @ w_out

    h = rmsnorm(x, norm_ffn)
    x = x + (jax.nn.silu(h @ w_gate) * (h @ w_up)) @ w_down
    return x


def _fwd_setup_inputs(seed: int = 0) -> dict:
    key = jax.random.key(seed)
    ks = jax.random.split(key, 18)
    f32 = jnp.float32

    def normal(k, shape, scale):
        return jax.random.normal(k, shape, f32) * scale

    x = normal(ks[0], (BATCH, SEQ, D_MODEL), 1.0)
    norm_mix = 1.0 + normal(ks[1], (DEPTH, D_MODEL), 0.02)
    w_in = normal(ks[2], (DEPTH, D_MODEL, IN_DIM), D_MODEL ** -0.5)
    q_gain = 1.0 + normal(ks[3], (DEPTH, ATT_HEAD_DIM), 0.02)
    k_gain = 1.0 + normal(ks[4], (DEPTH, ATT_HEAD_DIM), 0.02)
    conv_w = normal(ks[5], (DEPTH, CONV_WIDTH, CONV_DIM), CONV_WIDTH ** -0.5)
    conv_b = normal(ks[6], (DEPTH, CONV_DIM), 0.01)
    dt0 = jnp.exp(jax.random.uniform(ks[7], (DEPTH, SSM_HEADS), f32, math.log(1e-3), math.log(1e-1)))
    dt_bias = dt0 + jnp.log(-jnp.expm1(-dt0))
    a_log = jnp.log(jax.random.uniform(ks[8], (DEPTH, SSM_HEADS), f32, 1.0, 16.0))
    d_skip = 1.0 + normal(ks[9], (DEPTH, SSM_HEADS), 0.02)
    attn_out_gain = 1.0 + normal(ks[10], (DEPTH, ATT_WIDTH), 0.02)
    ssm_out_gain = 1.0 + normal(ks[11], (DEPTH, SSM_WIDTH), 0.02)
    w_out = normal(ks[12], (DEPTH, D_MIX, D_MODEL), D_MIX ** -0.5)
    norm_ffn = 1.0 + normal(ks[13], (DEPTH, D_MODEL), 0.02)
    w_gate = normal(ks[14], (DEPTH, D_MODEL, D_FF), D_MODEL ** -0.5)
    w_up = normal(ks[15], (DEPTH, D_MODEL, D_FF), D_MODEL ** -0.5)
    w_down = normal(ks[16], (DEPTH, D_FF, D_MODEL), D_FF ** -0.5)
    return {'x': x, 'norm_mix': norm_mix, 'w_in': w_in, 'q_gain': q_gain, 'k_gain': k_gain,
            'conv_w': conv_w, 'conv_b': conv_b, 'dt_bias': dt_bias, 'a_log': a_log, 'd_skip': d_skip,
            'attn_out_gain': attn_out_gain, 'ssm_out_gain': ssm_out_gain, 'w_out': w_out,
            'norm_ffn': norm_ffn, 'w_gate': w_gate, 'w_up': w_up, 'w_down': w_down}


def _fwd_reference(x, norm_mix, w_in, q_gain, k_gain, conv_w, conv_b, dt_bias, a_log, d_skip,
              attn_out_gain, ssm_out_gain, w_out, norm_ffn, w_gate, w_up, w_down):
    for i in range(DEPTH):
        x = hybrid_layer(x, norm_mix[i], w_in[i], q_gain[i], k_gain[i], conv_w[i], conv_b[i],
                         dt_bias[i], a_log[i], d_skip[i], attn_out_gain[i], ssm_out_gain[i],
                         w_out[i], norm_ffn[i], w_gate[i], w_up[i], w_down[i])
    return x


import jax as _jax
import jax.numpy as _jnp

TWIN_FORMAT = 'train_step'
FWD_PARAMS = ['x', 'norm_mix', 'w_in', 'q_gain', 'k_gain', 'conv_w', 'conv_b', 'dt_bias', 'a_log', 'd_skip', 'attn_out_gain', 'ssm_out_gain', 'w_out', 'norm_ffn', 'w_gate', 'w_up', 'w_down']
TWIN_WEIGHTS = ['norm_mix', 'w_in', 'q_gain', 'k_gain', 'conv_w', 'conv_b', 'dt_bias', 'a_log', 'd_skip', 'attn_out_gain', 'ssm_out_gain', 'w_out', 'norm_ffn', 'w_gate', 'w_up', 'w_down']
TWIN_DIFF_INPUT = 'x'
TWIN_INPUTS = ['x', 'norm_mix', 'w_in', 'q_gain', 'k_gain', 'conv_w', 'conv_b', 'dt_bias', 'a_log', 'd_skip', 'attn_out_gain', 'ssm_out_gain', 'w_out', 'norm_ffn', 'w_gate', 'w_up', 'w_down', 'loss_target', 'm_norm_mix', 'm_w_in', 'm_q_gain', 'm_k_gain', 'm_conv_w', 'm_conv_b', 'm_dt_bias', 'm_a_log', 'm_d_skip', 'm_attn_out_gain', 'm_ssm_out_gain', 'm_w_out', 'm_norm_ffn', 'm_w_gate', 'm_w_up', 'm_w_down', 'v_norm_mix', 'v_w_in', 'v_q_gain', 'v_k_gain', 'v_conv_w', 'v_conv_b', 'v_dt_bias', 'v_a_log', 'v_d_skip', 'v_attn_out_gain', 'v_ssm_out_gain', 'v_w_out', 'v_norm_ffn', 'v_w_gate', 'v_w_up', 'v_w_down']
TWIN_OUTPUTS = ['loss', 'grad_x', 'grad_norm_mix', 'grad_w_in', 'grad_q_gain', 'grad_k_gain', 'grad_conv_w', 'grad_conv_b', 'grad_dt_bias', 'grad_a_log', 'grad_d_skip', 'grad_attn_out_gain', 'grad_ssm_out_gain', 'grad_w_out', 'grad_norm_ffn', 'grad_w_gate', 'grad_w_up', 'grad_w_down', 'delta_norm_mix', 'delta_w_in', 'delta_q_gain', 'delta_k_gain', 'delta_conv_w', 'delta_conv_b', 'delta_dt_bias', 'delta_a_log', 'delta_d_skip', 'delta_attn_out_gain', 'delta_ssm_out_gain', 'delta_w_out', 'delta_norm_ffn', 'delta_w_gate', 'delta_w_up', 'delta_w_down', 'new_m_norm_mix', 'new_m_w_in', 'new_m_q_gain', 'new_m_k_gain', 'new_m_conv_w', 'new_m_conv_b', 'new_m_dt_bias', 'new_m_a_log', 'new_m_d_skip', 'new_m_attn_out_gain', 'new_m_ssm_out_gain', 'new_m_w_out', 'new_m_norm_ffn', 'new_m_w_gate', 'new_m_w_up', 'new_m_w_down', 'new_v_norm_mix', 'new_v_w_in', 'new_v_q_gain', 'new_v_k_gain', 'new_v_conv_w', 'new_v_conv_b', 'new_v_dt_bias', 'new_v_a_log', 'new_v_d_skip', 'new_v_attn_out_gain', 'new_v_ssm_out_gain', 'new_v_w_out', 'new_v_norm_ffn', 'new_v_w_gate', 'new_v_w_up', 'new_v_w_down']
TWIN_LEAF_KINDS = {'loss': 'loss', 'grad_x': 'grad_x', 'grad_norm_mix': 'grad_w', 'grad_w_in': 'grad_w', 'grad_q_gain': 'grad_w', 'grad_k_gain': 'grad_w', 'grad_conv_w': 'grad_w', 'grad_conv_b': 'grad_w', 'grad_dt_bias': 'grad_w', 'grad_a_log': 'grad_w', 'grad_d_skip': 'grad_w', 'grad_attn_out_gain': 'grad_w', 'grad_ssm_out_gain': 'grad_w', 'grad_w_out': 'grad_w', 'grad_norm_ffn': 'grad_w', 'grad_w_gate': 'grad_w', 'grad_w_up': 'grad_w', 'grad_w_down': 'grad_w', 'delta_norm_mix': 'delta_w', 'delta_w_in': 'delta_w', 'delta_q_gain': 'delta_w', 'delta_k_gain': 'delta_w', 'delta_conv_w': 'delta_w', 'delta_conv_b': 'delta_w', 'delta_dt_bias': 'delta_w', 'delta_a_log': 'delta_w', 'delta_d_skip': 'delta_w', 'delta_attn_out_gain': 'delta_w', 'delta_ssm_out_gain': 'delta_w', 'delta_w_out': 'delta_w', 'delta_norm_ffn': 'delta_w', 'delta_w_gate': 'delta_w', 'delta_w_up': 'delta_w', 'delta_w_down': 'delta_w', 'new_m_norm_mix': 'new_m', 'new_m_w_in': 'new_m', 'new_m_q_gain': 'new_m', 'new_m_k_gain': 'new_m', 'new_m_conv_w': 'new_m', 'new_m_conv_b': 'new_m', 'new_m_dt_bias': 'new_m', 'new_m_a_log': 'new_m', 'new_m_d_skip': 'new_m', 'new_m_attn_out_gain': 'new_m', 'new_m_ssm_out_gain': 'new_m', 'new_m_w_out': 'new_m', 'new_m_norm_ffn': 'new_m', 'new_m_w_gate': 'new_m', 'new_m_w_up': 'new_m', 'new_m_w_down': 'new_m', 'new_v_norm_mix': 'new_v', 'new_v_w_in': 'new_v', 'new_v_q_gain': 'new_v', 'new_v_k_gain': 'new_v', 'new_v_conv_w': 'new_v', 'new_v_conv_b': 'new_v', 'new_v_dt_bias': 'new_v', 'new_v_a_log': 'new_v', 'new_v_d_skip': 'new_v', 'new_v_attn_out_gain': 'new_v', 'new_v_ssm_out_gain': 'new_v', 'new_v_w_out': 'new_v', 'new_v_norm_ffn': 'new_v', 'new_v_w_gate': 'new_v', 'new_v_w_up': 'new_v', 'new_v_w_down': 'new_v'}


def _forward(args):
    return _fwd_reference(*[args[k] for k in FWD_PARAMS])


def _output_shape():
    out = _jax.eval_shape(lambda: _forward(_fwd_setup_inputs(0)))
    return out.shape, out.dtype

N_MICROBATCH = 1
ADAM_LR = 0.001
ADAM_B1 = 0.9
ADAM_B2 = 0.999
ADAM_EPS = 1e-08
ADAM_WD = 0.01
ADAM_STEP = 10
PER_EXAMPLE_BATCH_AXIS = {'x': 0, 'loss_target': 0}
SHARED_INPUTS = []
_WEIGHT_DTYPES = {'norm_mix': _jnp.float32, 'w_in': _jnp.float32, 'q_gain': _jnp.float32, 'k_gain': _jnp.float32, 'conv_w': _jnp.float32, 'conv_b': _jnp.float32, 'dt_bias': _jnp.float32, 'a_log': _jnp.float32, 'd_skip': _jnp.float32, 'attn_out_gain': _jnp.float32, 'ssm_out_gain': _jnp.float32, 'w_out': _jnp.float32, 'norm_ffn': _jnp.float32, 'w_gate': _jnp.float32, 'w_up': _jnp.float32, 'w_down': _jnp.float32}
MOMENT_SCALE = {'norm_mix': 8.009192e-01, 'w_in': 4.710609e-01, 'q_gain': 5.521855e-01, 'k_gain': 5.501767e-01, 'conv_w': 7.356244e-01, 'conv_b': 2.453945e+00, 'dt_bias': 1.143116e+00, 'a_log': 7.850542e+00, 'd_skip': 5.353337e+00, 'attn_out_gain': 1.592762e+01, 'ssm_out_gain': 2.327745e+01, 'w_out': 1.050192e+00, 'norm_ffn': 1.229214e+01, 'w_gate': 2.014523e-01, 'w_up': 1.822202e-01, 'w_down': 2.949428e-01}


def _to_microbatches(a, axis):
    t = _jnp.moveaxis(a, axis, 0)
    t = t.reshape((N_MICROBATCH, t.shape[0] // N_MICROBATCH) + t.shape[1:])
    return _jnp.moveaxis(t, 1, axis + 1)


def setup_inputs(seed: int = 0) -> dict:
    inp = _fwd_setup_inputs(seed)
    key = _jax.random.fold_in(_jax.random.key(seed), 7919)
    shape, _ = _output_shape()
    out = dict(inp)
    out["loss_target"] = _jax.random.normal(_jax.random.fold_in(key, 0), shape, _jnp.float32)
    for i, name in enumerate(TWIN_WEIGHTS):
        w = inp[name].astype(_jnp.float32)
        if MOMENT_SCALE is None:
            s = _jnp.sqrt(_jnp.mean(_jnp.square(w)) + 1e-30)
        else:
            s = MOMENT_SCALE[name]
        km, kv = _jax.random.split(_jax.random.fold_in(key, i + 1))
        out[name] = w
        out["m_" + name] = s * _jax.random.normal(km, w.shape, _jnp.float32)
        out["v_" + name] = (s * s) * _jax.random.uniform(kv, w.shape, _jnp.float32, 0.5, 1.5)
    if N_MICROBATCH > 1:
        for name, axis in PER_EXAMPLE_BATCH_AXIS.items():
            out[name] = _to_microbatches(out[name], axis)
    return {'x': out['x'], 'norm_mix': out['norm_mix'], 'w_in': out['w_in'], 'q_gain': out['q_gain'], 'k_gain': out['k_gain'], 'conv_w': out['conv_w'], 'conv_b': out['conv_b'], 'dt_bias': out['dt_bias'], 'a_log': out['a_log'], 'd_skip': out['d_skip'], 'attn_out_gain': out['attn_out_gain'], 'ssm_out_gain': out['ssm_out_gain'], 'w_out': out['w_out'], 'norm_ffn': out['norm_ffn'], 'w_gate': out['w_gate'], 'w_up': out['w_up'], 'w_down': out['w_down'], 'loss_target': out['loss_target'], 'm_norm_mix': out['m_norm_mix'], 'm_w_in': out['m_w_in'], 'm_q_gain': out['m_q_gain'], 'm_k_gain': out['m_k_gain'], 'm_conv_w': out['m_conv_w'], 'm_conv_b': out['m_conv_b'], 'm_dt_bias': out['m_dt_bias'], 'm_a_log': out['m_a_log'], 'm_d_skip': out['m_d_skip'], 'm_attn_out_gain': out['m_attn_out_gain'], 'm_ssm_out_gain': out['m_ssm_out_gain'], 'm_w_out': out['m_w_out'], 'm_norm_ffn': out['m_norm_ffn'], 'm_w_gate': out['m_w_gate'], 'm_w_up': out['m_w_up'], 'm_w_down': out['m_w_down'], 'v_norm_mix': out['v_norm_mix'], 'v_w_in': out['v_w_in'], 'v_q_gain': out['v_q_gain'], 'v_k_gain': out['v_k_gain'], 'v_conv_w': out['v_conv_w'], 'v_conv_b': out['v_conv_b'], 'v_dt_bias': out['v_dt_bias'], 'v_a_log': out['v_a_log'], 'v_d_skip': out['v_d_skip'], 'v_attn_out_gain': out['v_attn_out_gain'], 'v_ssm_out_gain': out['v_ssm_out_gain'], 'v_w_out': out['v_w_out'], 'v_norm_ffn': out['v_norm_ffn'], 'v_w_gate': out['v_w_gate'], 'v_w_up': out['v_w_up'], 'v_w_down': out['v_w_down']}


def _loss(weights, diff, rest, loss_target):
    with _jax.named_scope("forward"):
        args = {**rest, TWIN_DIFF_INPUT: diff, **{k: w.astype(_WEIGHT_DTYPES[k]) for k, w in weights.items()}}
        y = _forward(args)
    with _jax.named_scope("loss_head"):
        err = _jnp.square(y.astype(_jnp.float32) - loss_target)
        return 0.5 * _jnp.sum(_jnp.mean(err, axis=-1)) if err.ndim else 0.5 * err


def _adamw(w, g, m, v):
    m = ADAM_B1 * m + (1.0 - ADAM_B1) * g
    v = ADAM_B2 * v + (1.0 - ADAM_B2) * _jnp.square(g)
    m_hat = m / (1.0 - ADAM_B1 ** ADAM_STEP)
    v_hat = v / (1.0 - ADAM_B2 ** ADAM_STEP)
    delta = -ADAM_LR * (m_hat / (_jnp.sqrt(v_hat) + ADAM_EPS) + ADAM_WD * w)
    return delta, m, v


def reference(x, norm_mix, w_in, q_gain, k_gain, conv_w, conv_b, dt_bias, a_log, d_skip, attn_out_gain, ssm_out_gain, w_out, norm_ffn, w_gate, w_up, w_down, loss_target, m_norm_mix, m_w_in, m_q_gain, m_k_gain, m_conv_w, m_conv_b, m_dt_bias, m_a_log, m_d_skip, m_attn_out_gain, m_ssm_out_gain, m_w_out, m_norm_ffn, m_w_gate, m_w_up, m_w_down, v_norm_mix, v_w_in, v_q_gain, v_k_gain, v_conv_w, v_conv_b, v_dt_bias, v_a_log, v_d_skip, v_attn_out_gain, v_ssm_out_gain, v_w_out, v_norm_ffn, v_w_gate, v_w_up, v_w_down):
    given = dict(x=x, norm_mix=norm_mix, w_in=w_in, q_gain=q_gain, k_gain=k_gain, conv_w=conv_w, conv_b=conv_b, dt_bias=dt_bias, a_log=a_log, d_skip=d_skip, attn_out_gain=attn_out_gain, ssm_out_gain=ssm_out_gain, w_out=w_out, norm_ffn=norm_ffn, w_gate=w_gate, w_up=w_up, w_down=w_down, loss_target=loss_target, m_norm_mix=m_norm_mix, m_w_in=m_w_in, m_q_gain=m_q_gain, m_k_gain=m_k_gain, m_conv_w=m_conv_w, m_conv_b=m_conv_b, m_dt_bias=m_dt_bias, m_a_log=m_a_log, m_d_skip=m_d_skip, m_attn_out_gain=m_attn_out_gain, m_ssm_out_gain=m_ssm_out_gain, m_w_out=m_w_out, m_norm_ffn=m_norm_ffn, m_w_gate=m_w_gate, m_w_up=m_w_up, m_w_down=m_w_down, v_norm_mix=v_norm_mix, v_w_in=v_w_in, v_q_gain=v_q_gain, v_k_gain=v_k_gain, v_conv_w=v_conv_w, v_conv_b=v_conv_b, v_dt_bias=v_dt_bias, v_a_log=v_a_log, v_d_skip=v_d_skip, v_attn_out_gain=v_attn_out_gain, v_ssm_out_gain=v_ssm_out_gain, v_w_out=v_w_out, v_norm_ffn=v_norm_ffn, v_w_gate=v_w_gate, v_w_up=v_w_up, v_w_down=v_w_down)
    weights = {n: given[n] for n in TWIN_WEIGHTS}
    shared = {n: given[n] for n in SHARED_INPUTS}
    per_example = {n: given[n] for n in ['x']}
    grad_fn = _jax.value_and_grad(_loss, argnums=(0, 1))

    def one_microbatch(ex, loss_target):
        ex = dict(ex)
        diff = ex.pop(TWIN_DIFF_INPUT)
        return grad_fn(weights, diff, {**shared, **ex}, loss_target)

    if N_MICROBATCH == 1:
        loss, (grad_w, grad_x) = one_microbatch(per_example, given["loss_target"])
    else:
        def body(carry, xs):
            loss_sum, grad_sum = carry
            l_k, (gw_k, gx_k) = one_microbatch(xs[0], xs[1])
            with _jax.named_scope("update"):
                return (loss_sum + l_k, _jax.tree.map(_jnp.add, grad_sum, gw_k)), gx_k

        init = (_jnp.zeros((), _jnp.float32), _jax.tree.map(_jnp.zeros_like, weights))
        (loss, grad_w), grad_x = _jax.lax.scan(body, init, (per_example, given["loss_target"]))
    with _jax.named_scope("update"):
        delta_w, new_m, new_v = {}, {}, {}
        for n in TWIN_WEIGHTS:
            delta_w[n], new_m[n], new_v[n] = _adamw(weights[n], grad_w[n], given["m_" + n], given["v_" + n])
    return (loss, grad_x, *[grad_w[n] for n in TWIN_WEIGHTS], *[delta_w[n] for n in TWIN_WEIGHTS],
            *[new_m[n] for n in TWIN_WEIGHTS], *[new_v[n] for n in TWIN_WEIGHTS])
```

```python
import jax
import jax.numpy as jnp
from jax import lax
from jax.experimental import pallas as pl
from jax.experimental.pallas import tpu as pltpu

F32 = jnp.float32
BF16 = jnp.bfloat16
MESH = pl.DeviceIdType.MESH
ANY = pl.BlockSpec(memory_space=pl.ANY)
VMEM = pl.BlockSpec(memory_space=pltpu.VMEM)

EPS = 1e-6
ATT_HEAD_DIM = 128
SSM_HEAD_DIM = 64
SSM_GROUPS = 2
SSM_STATE = 128
SSD_CHUNK = 128
CONV_WIDTH = 4
LANES = 128
SUBLANES = 8
ATT_TILE = 256
N_CHIPS = 4
N_DEV = 8

ADAM_LR = 0.001
ADAM_B1 = 0.9
ADAM_B2 = 0.999
ADAM_EPS = 1e-08
ADAM_WD = 0.01
ADAM_STEP = 10

VMEM_LIMIT = 48 * 1024 * 1024

NN = (((1,), (0,)), ((), ()))
NT = (((1,), (1,)), ((), ()))
TN = (((0,), (0,)), ((), ()))


def _dot(a, b, dims=NN):
    return lax.dot_general(a.astype(BF16), b.astype(BF16), dims, preferred_element_type=F32)


def _dot_exact(x, ones, dims=NN, passes=3, ones_left=False):
    acc = None
    rem = x
    for _ in range(passes):
        piece = rem.astype(BF16)
        rem = rem - piece.astype(F32)
        p = (lax.dot_general(ones, piece, dims, preferred_element_type=F32) if ones_left
             else lax.dot_general(piece, ones, dims, preferred_element_type=F32))
        acc = p if acc is None else acc + p
    return acc


def _iota2(shape, axis):
    return lax.broadcasted_iota(jnp.int32, shape, axis)


def _tri(n, cmp):
    return cmp(_iota2((n, n), 0), _iota2((n, n), 1)).astype(BF16)


def _sum_all(v):
    return jnp.sum(jnp.sum(v, axis=1, keepdims=True), axis=0, keepdims=True)


def _fit(tile, dim, unit=LANES):
    if dim <= tile:
        return dim
    return max(k for k in range(unit, tile + 1, unit) if dim % k == 0)


def _params(sem):
    return pltpu.CompilerParams(dimension_semantics=sem, vmem_limit_bytes=VMEM_LIMIT)


def _softplus(x):
    return jnp.maximum(x, 0.0) + jnp.log(1.0 + jnp.exp(-jnp.abs(x)))


def _sigmoid(x):
    return 1.0 / (1.0 + jnp.exp(-x))


def _rms_fwd(x, g):
    r = lax.rsqrt(jnp.mean(x * x, axis=-1, keepdims=True) + EPS)
    return (x * r) * g


def _rms_bwd(x, g, dh):
    r = lax.rsqrt(jnp.mean(x * x, axis=-1, keepdims=True) + EPS)
    y = x * r
    dy = dh * g
    dx = r * (dy - y * jnp.mean(dy * y, axis=-1, keepdims=True))
    return dx, dh * y


def _matmul(groups, extras, epilogue, out_dtypes, m, n, tm, tn, name):
    tm, tn = _fit(tm, m), _fit(tn, n)
    flat = [t for grp in groups for t in grp]
    n_terms, n_extra = len(flat), len(extras)

    def body(*refs):
        outs = refs[2 * n_terms + n_extra:]
        accs, pos = [], 0
        for grp in groups:
            acc = None
            for (_, _, mode) in grp:
                dims = {"nn": NN, "nt": NT, "tn": TN}[mode]
                p = _dot(refs[2 * pos][...], refs[2 * pos + 1][...], dims)
                acc = p if acc is None else acc + p
                pos += 1
            accs.append(acc)
        ex = [refs[2 * n_terms + i][...] for i in range(n_extra)]
        res = epilogue(accs, ex)
        for o_ref, r in zip(outs, res, strict=True):
            o_ref[...] = r.astype(o_ref.dtype)

    in_specs, args = [], []
    for (a, b, mode) in flat:
        if mode == "nn":
            k = a.shape[1]
            in_specs += [pl.BlockSpec((tm, k), lambda i, j: (i, 0)), pl.BlockSpec((k, tn), lambda i, j: (0, j))]
        elif mode == "nt":
            k = a.shape[1]
            in_specs += [pl.BlockSpec((tm, k), lambda i, j: (i, 0)), pl.BlockSpec((tn, k), lambda i, j: (j, 0))]
        else:
            k = a.shape[0]
            in_specs += [pl.BlockSpec((k, tm), lambda i, j: (0, i)), pl.BlockSpec((k, tn), lambda i, j: (0, j))]
        args += [a, b]
    for e in extras:
        in_specs.append(pl.BlockSpec((tm, tn), lambda i, j: (i, j)))
        args.append(e)
    return pl.pallas_call(
        body, name=name, grid=(m // tm, n // tn), in_specs=in_specs,
        out_specs=[pl.BlockSpec((tm, tn), lambda i, j: (i, j)) for _ in out_dtypes],
        out_shape=[jax.ShapeDtypeStruct((m, n), d) for d in out_dtypes],
        compiler_params=_params(("parallel", "parallel")),
    )(*args)


def _mm(a, b, mode, m, n, out_dtype, name, tm=512, tn=512, res=None):
    extras = [] if res is None else [res]
    epi = (lambda accs, ex: (accs[0],)) if res is None else (lambda accs, ex: (accs[0] + ex[0],))
    return _matmul([[(a, b, mode)]], extras, epi, [out_dtype], m, n, tm, tn, name)[0]


def _swiglu_fwd_epilogue(accs, ex):
    g, u = accs
    return g, u, (g * _sigmoid(g)) * u


def _swiglu_bwd_epilogue(accs, ex):
    dact, (g, u) = accs[0], ex
    sg = _sigmoid(g)
    silu = g * sg
    return dact * u * (sg * (1.0 + g * (1.0 - sg))), dact * silu


def _rmsnorm_fwd(x, g, name, tr=512):
    t, d = x.shape
    tr = min(tr, t)

    def body(x_ref, g_ref, h_ref):
        h_ref[...] = _rms_fwd(x_ref[...], g_ref[...]).astype(BF16)

    return pl.pallas_call(
        body, name=name, grid=(t // tr,),
        in_specs=[pl.BlockSpec((tr, d), lambda i: (i, 0)), pl.BlockSpec((1, d), lambda i: (0, 0))],
        out_specs=pl.BlockSpec((tr, d), lambda i: (i, 0)),
        out_shape=jax.ShapeDtypeStruct((t, d), BF16),
        compiler_params=_params(("parallel",)),
    )(x, g.reshape(1, d))


def _rmsnorm_bwd(x, g, dh, dres, name, tr=256):
    t, d = x.shape
    tr = min(tr, t)

    def body(x_ref, g_ref, dh_ref, dres_ref, dx_ref, dxb_ref, dg_ref):
        dx, dgr = _rms_bwd(x_ref[...], g_ref[...], dh_ref[...])
        dx = dx + dres_ref[...]
        dx_ref[...] = dx
        dxb_ref[...] = dx.astype(BF16)

        @pl.when(pl.program_id(0) == 0)
        def _():
            dg_ref[...] = jnp.zeros_like(dg_ref)

        dg_ref[...] += jnp.sum(dgr, axis=0, keepdims=True)

    row = pl.BlockSpec((tr, d), lambda i: (i, 0))
    vec = pl.BlockSpec((1, d), lambda i: (0, 0))
    dx, dxb, dg = pl.pallas_call(
        body, name=name, grid=(t // tr,),
        in_specs=[row, vec, row, row], out_specs=[row, row, vec],
        out_shape=[jax.ShapeDtypeStruct((t, d), F32), jax.ShapeDtypeStruct((t, d), BF16),
                   jax.ShapeDtypeStruct((1, d), F32)],
        compiler_params=_params(("arbitrary",)),
    )(x, g.reshape(1, d), dh, dres)
    return dx, dxb, dg.reshape(d)


def _merge_fwd(o_att, y, z, ga, gs, name, tr=256):
    t, wa = o_att.shape
    ws = y.shape[1]
    wg = ws // SSM_GROUPS
    tr = min(tr, t)

    def body(o_ref, y_ref, z_ref, ga_ref, gs_ref, m_ref):
        m_ref[:, 0:wa] = _rms_fwd(o_ref[...], ga_ref[...]).astype(BF16)
        for g in range(SSM_GROUPS):
            sl = slice(g * wg, (g + 1) * wg)
            zz = z_ref[:, sl]
            yz = y_ref[:, sl] * (zz * _sigmoid(zz))
            m_ref[:, wa + g * wg:wa + (g + 1) * wg] = _rms_fwd(yz, gs_ref[:, sl]).astype(BF16)

    return pl.pallas_call(
        body, name=name, grid=(t // tr,),
        in_specs=[pl.BlockSpec((tr, wa), lambda i: (i, 0)), pl.BlockSpec((tr, ws), lambda i: (i, 0)),
                  pl.BlockSpec((tr, ws), lambda i: (i, 0)), pl.BlockSpec((1, wa), lambda i: (0, 0)),
                  pl.BlockSpec((1, ws), lambda i: (0, 0))],
        out_specs=pl.BlockSpec((tr, wa + ws), lambda i: (i, 0)),
        out_shape=jax.ShapeDtypeStruct((t, wa + ws), BF16),
        compiler_params=_params(("parallel",)),
    )(o_att, y, z, ga.reshape(1, wa), gs.reshape(1, ws))


def _merge_bwd(o_att, y, z, ga, gs, dmix, name, tr=256):
    t, wa = o_att.shape
    ws = y.shape[1]
    wg = ws // SSM_GROUPS
    tr = min(tr, t)

    def body(o_ref, y_ref, z_ref, ga_ref, gs_ref, dm_ref, do_ref, dy_ref, dz_ref, dga_ref, dgs_ref):
        @pl.when(pl.program_id(0) == 0)
        def _():
            dga_ref[...] = jnp.zeros_like(dga_ref)
            dgs_ref[...] = jnp.zeros_like(dgs_ref)

        do, dgr = _rms_bwd(o_ref[...], ga_ref[...], dm_ref[:, 0:wa])
        do_ref[...] = do
        dga_ref[...] += jnp.sum(dgr, axis=0, keepdims=True)
        for g in range(SSM_GROUPS):
            sl = slice(g * wg, (g + 1) * wg)
            zz, yy = z_ref[:, sl], y_ref[:, sl]
            sg = _sigmoid(zz)
            silu = zz * sg
            dyz, dgr = _rms_bwd(yy * silu, gs_ref[:, sl], dm_ref[:, wa + g * wg:wa + (g + 1) * wg])
            dy_ref[:, sl] = dyz * silu
            dz_ref[:, sl] = (dyz * yy * (sg + silu * (1.0 - sg))).astype(BF16)
            dgs_ref[:, sl] += jnp.sum(dgr, axis=0, keepdims=True)

    rowa = pl.BlockSpec((tr, wa), lambda i: (i, 0))
    rows = pl.BlockSpec((tr, ws), lambda i: (i, 0))
    veca = pl.BlockSpec((1, wa), lambda i: (0, 0))
    vecs = pl.BlockSpec((1, ws), lambda i: (0, 0))
    do, dy, dz, dga, dgs = pl.pallas_call(
        body, name=name, grid=(t // tr,),
        in_specs=[rowa, rows, rows, veca, vecs, pl.BlockSpec((tr, wa + ws), lambda i: (i, 0))],
        out_specs=[rowa, rows, rows, veca, vecs],
        out_shape=[jax.ShapeDtypeStruct((t, wa), F32), jax.ShapeDtypeStruct((t, ws), F32),
                   jax.ShapeDtypeStruct((t, ws), BF16), jax.ShapeDtypeStruct((1, wa), F32),
                   jax.ShapeDtypeStruct((1, ws), F32)],
        compiler_params=_params(("arbitrary",)),
    )(o_att, y, z, ga.reshape(1, wa), gs.reshape(1, ws), dmix)
    return do, dy, dz, dga.reshape(wa), dgs.reshape(ws)


def _loss_head(y, target, name, tr=256):
    t, d = y.shape
    tr = min(tr, t)

    def body(y_ref, t_ref, dy_ref, dyb_ref, l_ref):
        @pl.when(pl.program_id(0) == 0)
        def _():
            l_ref[...] = jnp.zeros_like(l_ref)

        diff = y_ref[...] - t_ref[...]
        dy = diff * (1.0 / d)
        dy_ref[...] = dy
        dyb_ref[...] = dy.astype(BF16)
        part = jnp.sum(diff * diff, axis=0, keepdims=True)
        fold = part[:, 0:LANES]
        for k in range(1, d // LANES):
            fold = fold + part[:, k * LANES:(k + 1) * LANES]
        l_ref[...] += fold * (0.5 / d)

    row = pl.BlockSpec((tr, d), lambda i: (i, 0))
    return pl.pallas_call(
        body, name=name, grid=(t // tr,), in_specs=[row, row],
        out_specs=[row, row, pl.BlockSpec((1, LANES), lambda i: (0, 0))],
        out_shape=[jax.ShapeDtypeStruct((t, d), F32), jax.ShapeDtypeStruct((t, d), BF16),
                   jax.ShapeDtypeStruct((1, LANES), F32)],
        compiler_params=_params(("arbitrary",)),
    )(y, target)


def _adamw(w, g, m, v, name, tr=256):
    r, c = w.shape
    tr = _fit(tr, r, 16)

    def body(w_ref, g_ref, m_ref, v_ref, d_ref, nm_ref, nv_ref):
        gg = g_ref[...]
        nm = ADAM_B1 * m_ref[...] + (1.0 - ADAM_B1) * gg
        nv = ADAM_B2 * v_ref[...] + (1.0 - ADAM_B2) * (gg * gg)
        m_hat = nm / (1.0 - ADAM_B1 ** ADAM_STEP)
        v_hat = nv / (1.0 - ADAM_B2 ** ADAM_STEP)
        d_ref[...] = -ADAM_LR * (m_hat / (jnp.sqrt(v_hat) + ADAM_EPS) + ADAM_WD * w_ref[...])
        nm_ref[...] = nm
        nv_ref[...] = nv

    blk = pl.BlockSpec((tr, c), lambda i: (i, 0))
    return pl.pallas_call(
        body, name=name, grid=(r // tr,), in_specs=[blk] * 4, out_specs=[blk] * 3,
        out_shape=[jax.ShapeDtypeStruct((r, c), F32)] * 3,
        compiler_params=_params(("parallel",)),
    )(w, g, m, v)


def _att_scores(qi, kj, row0, col0, scale):
    z = _dot(qi, kj, NT) * scale
    n, m = z.shape
    mask = (_iota2((n, m), 1) + col0) < (_iota2((n, m), 0) + row0)
    lb = -_softplus(-z)
    lrm = jnp.where(mask, lb - z, 0.0)
    return lb, lrm, mask


def _attention_fwd(qkv, qg, kg, name, tile=None):
    bsz, s, w3 = qkv.shape
    hd = ATT_HEAD_DIM
    heads = w3 // (3 * hd)
    tile = min(tile or ATT_TILE, s)
    nb = s // tile
    scale = hd ** -0.5

    def body(qkv_ref, qg_ref, kg_ref, o_ref, r_ref, qn_s, kn_s, vb_s, acc_s, c_s):
        qn_s[...] = _rms_fwd(qkv_ref[0, :, 0:hd], qg_ref[...]).astype(BF16)
        kn_s[...] = _rms_fwd(qkv_ref[0, :, hd:2 * hd], kg_ref[...]).astype(BF16)
        vb_s[...] = qkv_ref[0, :, 2 * hd:3 * hd].astype(BF16)
        after = _tri(tile, lambda r, c: r > c)

        def q_loop(i, _):
            rows = pl.ds(pl.multiple_of(i * tile, tile), tile)
            qi = qn_s[rows, :]
            acc_s[...] = jnp.zeros_like(acc_s)
            c_s[...] = jnp.zeros_like(c_s)

            def k_loop(jj, _):
                j = i - jj
                cols = pl.ds(pl.multiple_of(j * tile, tile), tile)
                lb, lrm, mask = _att_scores(qi, kn_s[cols, :], i * tile, j * tile, scale)
                later = _dot_exact(lrm, after) + c_s[...]
                w = jnp.where(mask, jnp.exp(lb + later), 0.0)
                acc_s[...] += _dot(w, vb_s[cols, :])
                c_s[...] += jnp.sum(lrm, axis=1, keepdims=True)
                return 0

            lax.fori_loop(0, i + 1, k_loop, 0)
            o_ref[0, rows, :] = acc_s[...]
            r_ref[0, 0, rows, :] = c_s[...]
            return 0

        lax.fori_loop(0, nb, q_loop, 0)

    return pl.pallas_call(
        body, name=name, grid=(bsz, heads),
        in_specs=[pl.BlockSpec((1, s, 3 * hd), lambda b, h: (b, 0, h)),
                  pl.BlockSpec((1, hd), lambda b, h: (0, 0)), pl.BlockSpec((1, hd), lambda b, h: (0, 0))],
        out_specs=[pl.BlockSpec((1, s, hd), lambda b, h: (b, 0, h)),
                   pl.BlockSpec((1, 1, s, 1), lambda b, h: (b, h, 0, 0))],
        out_shape=[jax.ShapeDtypeStruct((bsz, s, heads * hd), F32),
                   jax.ShapeDtypeStruct((bsz, heads, s, 1), F32)],
        scratch_shapes=[pltpu.VMEM((s, hd), BF16), pltpu.VMEM((s, hd), BF16), pltpu.VMEM((s, hd), BF16),
                        pltpu.VMEM((tile, hd), F32), pltpu.VMEM((tile, 1), F32)],
        compiler_params=_params(("parallel", "parallel")),
    )(qkv, qg.reshape(1, hd), kg.reshape(1, hd))


def _attention_bwd(qkv, qg, kg, rtot, do, name, tile=None):
    bsz, s, w3 = qkv.shape
    hd = ATT_HEAD_DIM
    heads = w3 // (3 * hd)
    tile = min(tile or ATT_TILE, s)
    nb = s // tile
    scale = hd ** -0.5

    def body(qkv_ref, qg_ref, kg_ref, r_ref, do_ref, dqkv_ref, dqg_ref, dkg_ref,
             qn_s, kn_s, vb_s, dob_s, dqn_s, dkn_s, dv_s, c1_s, c2_s):
        qn_s[...] = _rms_fwd(qkv_ref[0, :, 0:hd], qg_ref[...]).astype(BF16)
        kn_s[...] = _rms_fwd(qkv_ref[0, :, hd:2 * hd], kg_ref[...]).astype(BF16)
        vb_s[...] = qkv_ref[0, :, 2 * hd:3 * hd].astype(BF16)
        dob_s[...] = do_ref[0].astype(BF16)
        dqn_s[...] = jnp.zeros_like(dqn_s)
        dkn_s[...] = jnp.zeros_like(dkn_s)
        dv_s[...] = jnp.zeros_like(dv_s)
        upto = _tri(tile, lambda r, c: r <= c)
        before = _tri(tile, lambda r, c: r < c)

        def q_loop(i, _):
            rows = pl.ds(pl.multiple_of(i * tile, tile), tile)
            qi, doi = qn_s[rows, :], dob_s[rows, :]
            rt = r_ref[0, 0, rows, :]
            c1_s[...] = jnp.zeros_like(c1_s)
            c2_s[...] = jnp.zeros_like(c2_s)

            def k_loop(j, _):
                cols = pl.ds(pl.multiple_of(j * tile, tile), tile)
                kj, vj = kn_s[cols, :], vb_s[cols, :]
                lb, lrm, mask = _att_scores(qi, kj, i * tile, j * tile, scale)
                later = rt - (_dot_exact(lrm, upto) + c1_s[...])
                w = jnp.where(mask, jnp.exp(lb + later), 0.0)
                e = w * _dot(doi, vj, NT)
                pre = _dot_exact(e, before, passes=2) + c2_s[...]
                beta = jnp.exp(lb)
                dz = jnp.where(mask, e * (1.0 - beta) - pre * beta, 0.0) * scale
                dv_s[cols, :] += _dot(w, doi, TN)
                dkn_s[cols, :] += _dot(dz, qi, TN)
                dqn_s[rows, :] += _dot(dz, kj)
                c1_s[...] += jnp.sum(lrm, axis=1, keepdims=True)
                c2_s[...] += jnp.sum(e, axis=1, keepdims=True)
                return 0

            lax.fori_loop(0, i + 1, k_loop, 0)
            return 0

        lax.fori_loop(0, nb, q_loop, 0)
        dq, dgq = _rms_bwd(qkv_ref[0, :, 0:hd], qg_ref[...], dqn_s[...])
        dk, dgk = _rms_bwd(qkv_ref[0, :, hd:2 * hd], kg_ref[...], dkn_s[...])
        dqkv_ref[0, :, 0:hd] = dq.astype(BF16)
        dqkv_ref[0, :, hd:2 * hd] = dk.astype(BF16)
        dqkv_ref[0, :, 2 * hd:3 * hd] = dv_s[...].astype(BF16)
        dqg_ref[0, 0] = jnp.sum(dgq, axis=0, keepdims=True)
        dkg_ref[0, 0] = jnp.sum(dgk, axis=0, keepdims=True)

    gain = pl.BlockSpec((1, hd), lambda b, h: (0, 0))
    dgain = pl.BlockSpec((1, 1, 1, hd), lambda b, h: (b, h, 0, 0))
    return pl.pallas_call(
        body, name=name, grid=(bsz, heads),
        in_specs=[pl.BlockSpec((1, s, 3 * hd), lambda b, h: (b, 0, h)), gain, gain,
                  pl.BlockSpec((1, 1, s, 1), lambda b, h: (b, h, 0, 0)),
                  pl.BlockSpec((1, s, hd), lambda b, h: (b, 0, h))],
        out_specs=[pl.BlockSpec((1, s, 3 * hd), lambda b, h: (b, 0, h)), dgain, dgain],
        out_shape=[jax.ShapeDtypeStruct((bsz, s, w3), BF16),
                   jax.ShapeDtypeStruct((bsz, heads, 1, hd), F32),
                   jax.ShapeDtypeStruct((bsz, heads, 1, hd), F32)],
        scratch_shapes=[pltpu.VMEM((s, hd), BF16)] * 4 + [pltpu.VMEM((s, hd), F32)] * 3
        + [pltpu.VMEM((tile, 1), F32)] * 2,
        compiler_params=_params(("parallel", "parallel")),
    )(qkv, qg.reshape(1, hd), kg.reshape(1, hd), rtot, do)


def _conv_pre(pad_ref, w_ref, b_ref, s):
    pre = b_ref[...]
    for i in range(CONV_WIDTH):
        off = SUBLANES - (CONV_WIDTH - 1) + i
        pre = pre + pad_ref[off:off + s, :] * w_ref[i:i + 1, :]
    return pre


def _conv_fwd(u, w, b, name, tc=256):
    bsz, s, c = u.shape
    tc = min(tc, c)

    def body(u_ref, w_ref, b_ref, a_ref, pad_s):
        pad_s[0:SUBLANES, :] = jnp.zeros((SUBLANES, tc), F32)
        pad_s[SUBLANES:SUBLANES + s, :] = u_ref[0]
        pre = _conv_pre(pad_s, w_ref, b_ref, s)
        a_ref[0] = pre * _sigmoid(pre)

    return pl.pallas_call(
        body, name=name, grid=(bsz, c // tc),
        in_specs=[pl.BlockSpec((1, s, tc), lambda i, j: (i, 0, j)),
                  pl.BlockSpec((CONV_WIDTH, tc), lambda i, j: (0, j)), pl.BlockSpec((1, tc), lambda i, j: (0, j))],
        out_specs=pl.BlockSpec((1, s, tc), lambda i, j: (i, 0, j)),
        out_shape=jax.ShapeDtypeStruct((bsz, s, c), F32),
        scratch_shapes=[pltpu.VMEM((s + SUBLANES, tc), F32)],
        compiler_params=_params(("parallel", "parallel")),
    )(u, w, b.reshape(1, c))


def _conv_bwd(u, w, b, da, name, tc=256):
    bsz, s, c = u.shape
    tc = min(tc, c)

    def body(u_ref, w_ref, b_ref, da_ref, du_ref, dw_ref, db_ref, pad_s, gpad_s):
        @pl.when(pl.program_id(1) == 0)
        def _():
            dw_ref[...] = jnp.zeros_like(dw_ref)
            db_ref[...] = jnp.zeros_like(db_ref)

        pad_s[0:SUBLANES, :] = jnp.zeros((SUBLANES, tc), F32)
        pad_s[SUBLANES:SUBLANES + s, :] = u_ref[0]
        pre = _conv_pre(pad_s, w_ref, b_ref, s)
        sg = _sigmoid(pre)
        dpre = da_ref[0] * (sg * (1.0 + pre * (1.0 - sg)))
        gpad_s[0:s, :] = dpre
        gpad_s[s:s + SUBLANES, :] = jnp.zeros((SUBLANES, tc), F32)
        du = jnp.zeros((s, tc), F32)
        for i in range(CONV_WIDTH):
            back = CONV_WIDTH - 1 - i
            du = du + gpad_s[back:back + s, :] * w_ref[i:i + 1, :]
            off = SUBLANES - (CONV_WIDTH - 1) + i
            dw_ref[i:i + 1, :] += jnp.sum(dpre * pad_s[off:off + s, :], axis=0, keepdims=True)
        du_ref[0] = du.astype(BF16)
        db_ref[...] += jnp.sum(dpre, axis=0, keepdims=True)

    blk = pl.BlockSpec((1, s, tc), lambda j, i: (i, 0, j))
    du, dw, db = pl.pallas_call(
        body, name=name, grid=(c // tc, bsz),
        in_specs=[blk, pl.BlockSpec((CONV_WIDTH, tc), lambda j, i: (0, j)),
                  pl.BlockSpec((1, tc), lambda j, i: (0, j)), blk],
        out_specs=[blk, pl.BlockSpec((CONV_WIDTH, tc), lambda j, i: (0, j)),
                   pl.BlockSpec((1, tc), lambda j, i: (0, j))],
        out_shape=[jax.ShapeDtypeStruct((bsz, s, c), BF16), jax.ShapeDtypeStruct((CONV_WIDTH, c), F32),
                   jax.ShapeDtypeStruct((1, c), F32)],
        scratch_shapes=[pltpu.VMEM((s + SUBLANES, tc), F32), pltpu.VMEM((s + SUBLANES, tc), F32)],
        compiler_params=_params(("parallel", "arbitrary")),
    )(u, w, b.reshape(1, c), da)
    return du, dw, db.reshape(c)


def _ssd_chunk_common(b_ref, c_ref, dt_ref, dtb_ref, alog_ref):
    bm, cm = b_ref[0], c_ref[0]
    draw = dt_ref[0] + dtb_ref[...]
    dt = _softplus(draw)
    a_row = -jnp.exp(alog_ref[...])
    da = dt * a_row
    n = SSD_CHUNK
    acum = _dot_exact(da, _tri(n, lambda r, c: r >= c), ones_left=True)
    acum_t = _dot_exact(da, _tri(n, lambda r, c: r <= c), dims=TN)
    cb = _dot(cm, bm, NT)
    return bm, cm, draw, dt, a_row, acum, acum_t, cb


def _ssd_head_common(acum, acum_t, dt, cb, x, i):
    n, p = SSD_CHUNK, SSM_HEAD_DIM
    acol = acum[:, i:i + 1]
    arow = acum_t[i:i + 1, :]
    causal = _iota2((n, n), 0) >= _iota2((n, n), 1)
    lm = jnp.where(causal, jnp.exp(jnp.where(causal, acol - arow, 0.0)), 0.0)
    gm = cb * lm
    dtc = dt[:, i:i + 1]
    xh = x[:, i * p:(i + 1) * p]
    xdt = xh * dtc
    alast = acum[n - 1:n, i:i + 1]
    dte = jnp.exp(alast - acol)
    return acol, lm, gm, dtc, xh, xdt, alast, dte


def _ssd_specs(s, wg, hg, rev):
    g, n, cl = SSM_GROUPS, SSM_STATE, SSD_CHUNK
    nc = s // cl
    boff, coff = (g * wg) // n, (g * wg) // n + g
    ci = (lambda c: nc - 1 - c) if rev else (lambda c: c)
    xblk = pl.BlockSpec((1, cl, wg), lambda b, k, c: (b, ci(c), k))
    bblk = pl.BlockSpec((1, cl, n), lambda b, k, c: (b, ci(c), boff + k))
    cblk = pl.BlockSpec((1, cl, n), lambda b, k, c: (b, ci(c), coff + k))
    nblk = pl.BlockSpec((1, cl, n), lambda b, k, c: (b, ci(c), k))
    dtblk = pl.BlockSpec((1, cl, LANES), lambda b, k, c: (b, ci(c), k))
    vec = pl.BlockSpec((1, LANES), lambda b, k, c: (0, k))
    hsblk = pl.BlockSpec((1, 1, 1, wg, n), lambda b, k, c: (b, k, ci(c), 0, 0))
    return nc, xblk, bblk, cblk, nblk, dtblk, vec, hsblk


def _ssd_fwd(xbc, dtraw, dtb, alog, dskip, hg, name):
    bsz, s, _ = xbc.shape
    g, n, p = SSM_GROUPS, SSM_STATE, SSM_HEAD_DIM
    wg = hg * p
    nc, xblk, bblk, cblk, _, dtblk, vec, hsblk = _ssd_specs(s, wg, hg, False)

    def body(x_ref, b_ref, c_ref, dt_ref, dtb_ref, alog_ref, dsk_ref, y_ref, hs_ref, h_s):
        @pl.when(pl.program_id(2) == 0)
        def _():
            h_s[...] = jnp.zeros_like(h_s)

        bm, cm, _, dt, _, acum, acum_t, cb = _ssd_chunk_common(b_ref, c_ref, dt_ref, dtb_ref, alog_ref)
        x = x_ref[0]
        hs_ref[0, 0, 0] = h_s[...]
        for i in range(hg):
            acol, _, gm, _, xh, xdt, alast, dte = _ssd_head_common(acum, acum_t, dt, cb, x, i)
            hprev = h_s[i * p:(i + 1) * p, :]
            y = _dot(gm, xdt) + _dot(cm, hprev, NT) * jnp.exp(acol) + xh * dsk_ref[:, i:i + 1]
            y_ref[0, :, i * p:(i + 1) * p] = y
            h_s[i * p:(i + 1) * p, :] = hprev * jnp.exp(alast) + _dot(xdt * dte, bm, TN)

    return pl.pallas_call(
        body, name=name, grid=(bsz, g, nc),
        in_specs=[xblk, bblk, cblk, dtblk, vec, vec, vec],
        out_specs=[xblk, hsblk],
        out_shape=[jax.ShapeDtypeStruct((bsz, s, g * wg), F32),
                   jax.ShapeDtypeStruct((bsz, g, nc, wg, n), F32)],
        scratch_shapes=[pltpu.VMEM((wg, n), F32)],
        compiler_params=_params(("parallel", "parallel", "arbitrary")),
    )(xbc, xbc, xbc, dtraw, dtb, alog, dskip)


def _ssd_bwd(xbc, dtraw, dtb, alog, dskip, hs, dy, hg, name):
    bsz, s, _ = xbc.shape
    g, n, p, cl = SSM_GROUPS, SSM_STATE, SSM_HEAD_DIM, SSD_CHUNK
    wg = hg * p
    nc, xblk, bblk, cblk, nblk, dtblk, vec, hsblk = _ssd_specs(s, wg, hg, True)

    def body(x_ref, b_ref, c_ref, dt_ref, dtb_ref, alog_ref, dsk_ref, hs_ref, dy_ref,
             dx_ref, db_ref, dc_ref, ddt_ref, dvec_ref, dh_s):
        @pl.when(pl.program_id(2) == 0)
        def _():
            dh_s[...] = jnp.zeros_like(dh_s)
            dvec_ref[...] = jnp.zeros_like(dvec_ref)

        lane = _iota2((cl, LANES), 1)
        sub = _iota2((LANES, cl), 0)
        lane1 = _iota2((1, LANES), 1)
        last_row = _iota2((cl, 1), 0) == cl - 1
        bm, cm, draw, dt, a_row, acum, acum_t, cb = _ssd_chunk_common(b_ref, c_ref, dt_ref, dtb_ref, alog_ref)
        x = x_ref[0]
        dyc = dy_ref[0]
        dcb = jnp.zeros((cl, cl), F32)
        dcm = jnp.zeros((cl, n), F32)
        dbm = jnp.zeros((cl, n), F32)
        da_col = jnp.zeros((cl, LANES), F32)
        da_row = jnp.zeros((LANES, cl), F32)
        ddt = jnp.zeros((cl, LANES), F32)
        dd = jnp.zeros((1, LANES), F32)
        for i in range(hg):
            acol, lm, gm, dtc, xh, xdt, alast, dte = _ssd_head_common(acum, acum_t, dt, cb, x, i)
            dyh = dyc[:, i * p:(i + 1) * p]
            hprev = hs_ref[0, 0, 0, i * p:(i + 1) * p, :]
            dhn = dh_s[i * p:(i + 1) * p, :]
            ea, cd = jnp.exp(acol), jnp.exp(alast)
            dd = dd + jnp.where(lane1 == i, _sum_all(dyh * xh), 0.0)
            y0 = _dot(cm, hprev, NT)
            dy0 = dyh * ea
            dacol = jnp.sum(dyh * y0, axis=1, keepdims=True) * ea
            dcm = dcm + _dot(dy0, hprev)
            dh_s[i * p:(i + 1) * p, :] = _dot(dy0, cm, TN) + dhn * cd
            d_alast = _sum_all(dhn * hprev) * cd
            dxe = _dot(bm, dhn, NT)
            dbm = dbm + _dot(xdt * dte, dhn)
            dxdt = dxe * dte
            t1 = jnp.sum(dxe * xdt, axis=1, keepdims=True) * dte
            dacol = dacol - t1
            d_alast = d_alast + _sum_all(t1)
            dgm = _dot(dyh, xdt, NT)
            dxdt = dxdt + _dot(gm, dyh, TN)
            dcb = dcb + dgm * lm
            ws = dgm * gm
            dacol = dacol + jnp.sum(ws, axis=1, keepdims=True)
            darow = -jnp.sum(ws, axis=0, keepdims=True)
            dacol = dacol + jnp.where(last_row, d_alast, 0.0)
            dx_ref[0, :, i * p:(i + 1) * p] = dxdt * dtc + dyh * dsk_ref[:, i:i + 1]
            da_col = jnp.where(lane == i, dacol, da_col)
            da_row = jnp.where(sub == i, darow, da_row)
            ddt = jnp.where(lane == i, jnp.sum(dxdt * xh, axis=1, keepdims=True), ddt)
        dc_ref[0] = dcm + _dot(dcb, bm)
        db_ref[0] = dbm + _dot(dcb, cm, TN)
        upper = _tri(cl, lambda r, k: r <= k)
        dda = _dot_exact(da_col, upper, ones_left=True) + _dot_exact(da_row, upper, dims=NT, ones_left=True)
        ddt = ddt + dda * a_row
        ddraw = ddt * _sigmoid(draw)
        ddt_ref[0] = ddraw.astype(BF16)
        dvec_ref[0, 0, 0:1, :] += jnp.sum(ddraw, axis=0, keepdims=True)
        dvec_ref[0, 0, 1:2, :] += jnp.sum(dda * dt, axis=0, keepdims=True) * a_row
        dvec_ref[0, 0, 2:3, :] += dd

    return pl.pallas_call(
        body, name=name, grid=(bsz, g, nc),
        in_specs=[xblk, bblk, cblk, dtblk, vec, vec, vec, hsblk, xblk],
        out_specs=[xblk, nblk, nblk, dtblk,
                   pl.BlockSpec((1, 1, SUBLANES, LANES), lambda b, k, c: (b, k, 0, 0))],
        out_shape=[jax.ShapeDtypeStruct((bsz, s, g * wg), F32), jax.ShapeDtypeStruct((bsz, s, g * n), F32),
                   jax.ShapeDtypeStruct((bsz, s, g * n), F32), jax.ShapeDtypeStruct((bsz, s, g * LANES), BF16),
                   jax.ShapeDtypeStruct((bsz, g, SUBLANES, LANES), F32)],
        scratch_shapes=[pltpu.VMEM((wg, n), F32)],
        compiler_params=_params(("parallel", "parallel", "arbitrary")),
    )(xbc, xbc, xbc, dtraw, dtb, alog, dskip, hs, dy)


def _coords():
    return lax.axis_index("x"), lax.axis_index("y"), lax.axis_index("c")


def _other_chips(x, y):
    return [(1 - x, y), (x, 1 - y), (1 - x, 1 - y)]


def _remote(src, dst, send_sems, recv_sems, k, to):
    return pltpu.make_async_remote_copy(src_ref=src, dst_ref=dst, send_sem=send_sems.at[k],
                                        recv_sem=recv_sems.at[k], device_id=to, device_id_type=MESH)


def _allgather_shards(wb, name):
    layers = wb.shape[0]
    half = layers // 2

    def body(w_ref, o_ref, send_sems, recv_sems, local_sem):
        x, y, c = _coords()
        me = 2 * x + y
        sibling = (x, y, 1 - c)
        chips = _other_chips(x, y)

        def region(j, h):
            return o_ref.at[j, pl.ds(h * half, half)]

        mine = pltpu.make_async_copy(w_ref, o_ref.at[me], local_sem)
        mine.start()
        first = [_remote(w_ref.at[pl.ds(c * half, half)], region(me, c), send_sems, recv_sems, k, (px, py, c))
                 for k, (px, py) in enumerate(chips)]
        for cp in first:
            cp.start()
        passed = []
        for k, (px, py) in enumerate(chips):
            j = 2 * px + py
            _remote(region(j, c), region(j, c), send_sems, recv_sems, k, (px, py, c)).wait_recv()
            fw = _remote(region(j, c), region(j, c), send_sems, recv_sems, 3 + k, sibling)
            fw.start()
            passed.append(fw)
        for k, (px, py) in enumerate(chips):
            j = 2 * px + py
            _remote(region(j, 1 - c), region(j, 1 - c), send_sems, recv_sems, 3 + k, sibling).wait_recv()
        for cp in first + passed:
            cp.wait_send()
        mine.wait()

    return pl.pallas_call(
        body, name=name, in_specs=[ANY], out_specs=ANY,
        out_shape=jax.ShapeDtypeStruct((N_CHIPS,) + wb.shape, wb.dtype),
        scratch_shapes=[pltpu.SemaphoreType.DMA((6,)), pltpu.SemaphoreType.DMA((6,)), pltpu.SemaphoreType.DMA],
    )(wb)


def _pair_exchange(gj, name):
    layers = gj.shape[1]
    half = layers // 2

    def body(g_ref, r_ref, send_sems, recv_sems):
        x, y, c = _coords()
        cp = _remote(g_ref.at[:, pl.ds((1 - c) * half, half)], r_ref, send_sems, recv_sems, 0, (x, y, 1 - c))
        cp.start()
        cp.wait()

    return pl.pallas_call(
        body, name=name, in_specs=[ANY], out_specs=ANY,
        out_shape=jax.ShapeDtypeStruct((gj.shape[0], half) + gj.shape[2:], gj.dtype),
        scratch_shapes=[pltpu.SemaphoreType.DMA((1,)), pltpu.SemaphoreType.DMA((1,))],
    )(gj)


def _pair_add(gj, r1, c_idx, name, tr=256):
    nj, layers, r, c = gj.shape
    half = layers // 2
    tr = _fit(tr, r, 16)

    def body(c_ref, g_ref, r_ref, p_ref, pb_ref):
        s = g_ref[...] + r_ref[...]
        p_ref[...] = s
        pb_ref[...] = s.astype(BF16)

    blk_r = pl.BlockSpec((1, 1, tr, c), lambda j, l, i, cr: (j, l, i, 0))
    blk_g = pl.BlockSpec((1, 1, tr, c), lambda j, l, i, cr: (j, cr[0] * half + l, i, 0))
    return pl.pallas_call(
        body, name=name,
        grid_spec=pltpu.PrefetchScalarGridSpec(
            num_scalar_prefetch=1, grid=(nj, half, r // tr),
            in_specs=[blk_g, blk_r], out_specs=[blk_r, blk_r]),
        out_shape=[jax.ShapeDtypeStruct((nj, half, r, c), F32), jax.ShapeDtypeStruct((nj, half, r, c), BF16)],
        compiler_params=_params(("parallel", "parallel", "parallel")),
    )(c_idx, gj, r1)


def _chip_exchange(p, pb, name):
    def body(p_ref, pb_ref, r_ref, own_ref, send_sems, recv_sems, local_sem):
        x, y, c = _coords()
        me = 2 * x + y
        keep = pltpu.make_async_copy(p_ref.at[me], own_ref, local_sem)
        keep.start()
        cps = [_remote(pb_ref.at[2 * px + py], r_ref.at[k], send_sems, recv_sems, k, (px, py, c))
               for k, (px, py) in enumerate(_other_chips(x, y))]
        for cp in cps:
            cp.start()
        for cp in cps:
            cp.wait()
        keep.wait()

    return pl.pallas_call(
        body, name=name, in_specs=[ANY, ANY], out_specs=[ANY, ANY],
        out_shape=[jax.ShapeDtypeStruct((3,) + pb.shape[1:], BF16), jax.ShapeDtypeStruct(p.shape[1:], F32)],
        scratch_shapes=[pltpu.SemaphoreType.DMA((3,)), pltpu.SemaphoreType.DMA((3,)), pltpu.SemaphoreType.DMA],
    )(p, pb)


def _chip_add(own, r2, name, tr=256):
    half, r, c = own.shape
    tr = _fit(tr, r, 16)

    def body(o_ref, r_ref, f_ref):
        f_ref[...] = ((o_ref[...] + r_ref[0].astype(F32)) + r_ref[1].astype(F32)) + r_ref[2].astype(F32)

    blk = pl.BlockSpec((1, tr, c), lambda l, i: (l, i, 0))
    return pl.pallas_call(
        body, name=name, grid=(half, r // tr),
        in_specs=[blk, pl.BlockSpec((3, 1, tr, c), lambda l, i: (0, l, i, 0))], out_specs=blk,
        out_shape=jax.ShapeDtypeStruct((half, r, c), F32),
        compiler_params=_params(("parallel", "parallel")),
    )(own, r2)


def _pair_gather(f, name):
    half = f.shape[0]

    def body(f_ref, o_ref, send_sems, recv_sems, local_sem):
        x, y, c = _coords()
        rows = pl.ds(c * half, half)
        keep = pltpu.make_async_copy(f_ref, o_ref.at[rows], local_sem)
        keep.start()
        cp = _remote(f_ref, o_ref.at[rows], send_sems, recv_sems, 0, (x, y, 1 - c))
        cp.start()
        cp.wait()
        keep.wait()

    return pl.pallas_call(
        body, name=name, in_specs=[ANY], out_specs=ANY,
        out_shape=jax.ShapeDtypeStruct((2 * half,) + f.shape[1:], f.dtype),
        scratch_shapes=[pltpu.SemaphoreType.DMA((1,)), pltpu.SemaphoreType.DMA((1,)), pltpu.SemaphoreType.DMA],
    )(f)


def _reduce_scatter(gj, c_idx, tag):
    r1 = _pair_exchange(gj, f"rs_pair_{tag}")
    p, pb = _pair_add(gj, r1, c_idx, f"rs_pair_add_{tag}")
    r2, own = _chip_exchange(p, pb, f"rs_chip_{tag}")
    f = _chip_add(own, r2, f"rs_chip_add_{tag}")
    return _pair_gather(f, f"rs_gather_{tag}")


def _small_exchange(v, name, reduce):
    rows = v.shape[0]

    def body(v_ref, o_ref, *rest):
        buf = rest[0] if reduce else o_ref
        send_sems, recv_sems = rest[-2], rest[-1]
        x, y, c = _coords()
        me = 4 * x + 2 * y + c
        buf[me] = v_ref[...]
        cps = []
        for r in range(1, N_DEV):
            peer = (lax.bitwise_xor(x, (r >> 2) & 1), lax.bitwise_xor(y, (r >> 1) & 1), lax.bitwise_xor(c, r & 1))
            cps.append(_remote(v_ref, buf.at[me], send_sems, recv_sems, r - 1, peer))
        for cp in cps:
            cp.start()
        for r in range(1, N_DEV):
            src = buf.at[lax.bitwise_xor(me, r)]
            _remote(src, src, send_sems, recv_sems, r - 1, (x, y, c)).wait_recv()
        for cp in cps:
            cp.wait_send()
        if reduce:
            acc = buf[0]
            for d in range(1, N_DEV):
                acc = acc + buf[d]
            o_ref[...] = acc
            o_ref[0:1, :] = jnp.broadcast_to(jnp.sum(acc[0:1, :], axis=1, keepdims=True), (1, LANES))

    scratch = [pltpu.SemaphoreType.DMA((N_DEV - 1,)), pltpu.SemaphoreType.DMA((N_DEV - 1,))]
    if reduce:
        scratch = [pltpu.VMEM((N_DEV, rows, LANES), F32)] + scratch
    out_shape = (rows, LANES) if reduce else (N_DEV, rows, LANES)
    return pl.pallas_call(
        body, name=name, in_specs=[VMEM], out_specs=VMEM,
        out_shape=jax.ShapeDtypeStruct(out_shape, F32), scratch_shapes=scratch,
    )(v)


def _pack(parts):
    flat = []
    for a in parts:
        a = a.reshape(-1)
        flat.append(jnp.pad(a, (0, (-a.shape[0]) % LANES)))
    v = jnp.concatenate(flat)
    v = jnp.pad(v, (0, (-v.shape[0]) % (SUBLANES * LANES)))
    return v.reshape(-1, LANES)


def _unpack(slab, shapes):
    flat = slab.reshape(-1)
    out, off = [], 0
    for shp in shapes:
        size = 1
        for d in shp:
            size *= d
        out.append(flat[off:off + size].reshape(shp))
        off += size + (-size) % LANES
    return out


def _group_slots(a, hg):
    lead = a.shape[:-1]
    a = a.reshape(lead + (SSM_GROUPS, hg))
    a = jnp.pad(a, [(0, 0)] * len(lead) + [(0, 0), (0, LANES - hg)])
    return a.reshape(lead + (SSM_GROUPS * LANES,))


def _ungroup_slots(a, hg):
    lead = a.shape[:-1]
    return a.reshape(lead + (SSM_GROUPS, LANES))[..., :hg].reshape(lead + (SSM_GROUPS * hg,))


def _layer_fwd(x, p, bsz, s):
    t, d = x.shape
    aw, sw, cd, hg = p["aw"], p["sw"], p["cd"], p["hg"]
    dff = p["wg"].shape[1]
    h = _rmsnorm_fwd(x, p["norm_mix"], "norm_mix_fwd")
    qkv = _mm(h, p["wqkv"], "nn", t, 3 * aw, F32, "proj_qkv", tm=1024, tn=512)
    z = _mm(h, p["wz"], "nn", t, sw, F32, "proj_z", tm=1024, tn=512)
    xbc = _mm(h, p["wxbc"], "nn", t, cd, F32, "proj_xbc", tm=1024, tn=512)
    dtraw = _mm(h, p["wdt"], "nn", t, SSM_GROUPS * LANES, F32, "proj_dt", tm=1024, tn=SSM_GROUPS * LANES)
    qkv3 = qkv.reshape(bsz, s, 3 * aw)
    o_att, rtot = _attention_fwd(qkv3, p["q_gain"], p["k_gain"], "attention_fwd")
    xbc3 = xbc.reshape(bsz, s, cd)
    xact = _conv_fwd(xbc3, p["conv_w"], p["conv_b"], "conv_fwd")
    dt3 = dtraw.reshape(bsz, s, SSM_GROUPS * LANES)
    y, hs = _ssd_fwd(xact, dt3, p["dt_bias"], p["a_log"], p["d_skip"], hg, "ssd_fwd")
    o2, y2 = o_att.reshape(t, aw), y.reshape(t, sw)
    mix = _merge_fwd(o2, y2, z, p["attn_out_gain"], p["ssm_out_gain"], "merge_fwd")
    x1 = _mm(mix, p["wout"], "nn", t, d, F32, "proj_out", tm=1024, tn=512, res=x)
    h2 = _rmsnorm_fwd(x1, p["norm_ffn"], "norm_ffn_fwd")
    gate, up, act = _matmul([[(h2, p["wg"], "nn")], [(h2, p["wu"], "nn")]], [], _swiglu_fwd_epilogue,
                            [F32, F32, BF16], t, dff, 512, 512, "ffn_gate_up")
    x2 = _mm(act, p["wd"], "nn", t, d, F32, "ffn_down", tm=512, tn=512, res=x1)
    saved = dict(x=x, h=h, qkv3=qkv3, z=z, xbc3=xbc3, dt3=dt3, o2=o2, rtot=rtot, xact=xact, hs=hs, y2=y2,
                 mix=mix, x1=x1, h2=h2, gate=gate, up=up, act=act)
    return x2, saved


def _layer_bwd(dx2, dx2b, p, sv, bsz, s):
    t, d = dx2.shape
    aw, sw, cd, hg = p["aw"], p["sw"], p["cd"], p["hg"]
    dff = p["wg"].shape[1]
    gr = {}
    dgate, dup = _matmul([[(dx2b, p["wd"], "nt")]], [sv["gate"], sv["up"]], _swiglu_bwd_epilogue,
                         [BF16, BF16], t, dff, 512, 512, "ffn_down_dgrad")
    gr["wd"] = _mm(sv["act"], dx2b, "tn", dff, d, F32, "ffn_down_wgrad")
    dh2 = _matmul([[(dgate, p["wg"], "nt"), (dup, p["wu"], "nt")]], [], lambda accs, ex: (accs[0],),
                  [F32], t, d, 512, 256, "ffn_gate_up_dgrad")[0]
    gr["wg"] = _mm(sv["h2"], dgate, "tn", d, dff, F32, "ffn_gate_wgrad")
    gr["wu"] = _mm(sv["h2"], dup, "tn", d, dff, F32, "ffn_up_wgrad")
    dx1, dx1b, gr["norm_ffn"] = _rmsnorm_bwd(sv["x1"], p["norm_ffn"], dh2, dx2, "norm_ffn_bwd")
    dmix = _mm(dx1b, p["wout"], "nt", t, aw + sw, F32, "proj_out_dgrad")
    gr["wout"] = _mm(sv["mix"], dx1b, "tn", aw + sw, d, F32, "proj_out_wgrad")
    do, dy, dz, gr["attn_out_gain"], gr["ssm_out_gain"] = _merge_bwd(
        sv["o2"], sv["y2"], sv["z"], p["attn_out_gain"], p["ssm_out_gain"], dmix, "merge_bwd")
    dxs, dbm, dcm, ddt, dvec = _ssd_bwd(sv["xact"], sv["dt3"], p["dt_bias"], p["a_log"], p["d_skip"], sv["hs"],
                                        dy.reshape(bsz, s, sw), hg, "ssd_bwd")
    dvec = jnp.sum(dvec, axis=0).reshape(SSM_GROUPS, SUBLANES, LANES)
    gr["dt_bias"], gr["a_log"], gr["d_skip"] = (dvec[:, k, :hg].reshape(-1) for k in range(3))
    dxact = jnp.concatenate([dxs, dbm, dcm], axis=-1)
    dxbc, gr["conv_w"], gr["conv_b"] = _conv_bwd(sv["xbc3"], p["conv_w"], p["conv_b"], dxact, "conv_bwd")
    dqkv, dqg, dkg = _attention_bwd(sv["qkv3"], p["q_gain"], p["k_gain"], sv["rtot"],
                                    do.reshape(bsz, s, aw), "attention_bwd")
    gr["q_gain"] = jnp.sum(dqg, axis=(0, 1, 2))
    gr["k_gain"] = jnp.sum(dkg, axis=(0, 1, 2))
    dqkv, dxbc, ddt = dqkv.reshape(t, 3 * aw), dxbc.reshape(t, cd), ddt.reshape(t, SSM_GROUPS * LANES)
    dh = _matmul([[(dqkv, p["wqkv"], "nt"), (dz, p["wz"], "nt"), (dxbc, p["wxbc"], "nt"), (ddt, p["wdt"], "nt")]],
                 [], lambda accs, ex: (accs[0],), [F32], t, d, 512, 256, "proj_in_dgrad")[0]
    h = sv["h"]
    gr["wqkv"] = _mm(h, dqkv, "tn", d, 3 * aw, F32, "proj_qkv_wgrad")
    gr["wz"] = _mm(h, dz, "tn", d, sw, F32, "proj_z_wgrad")
    gr["wxbc"] = _mm(h, dxbc, "tn", d, cd, F32, "proj_xbc_wgrad")
    gr["wdt"] = _mm(h, ddt, "tn", d, SSM_GROUPS * LANES, F32, "proj_dt_wgrad", tn=SSM_GROUPS * LANES)
    dx, dxb, gr["norm_mix"] = _rmsnorm_bwd(sv["x"], p["norm_mix"], dh, dx1, "norm_mix_bwd")
    return dx, dxb, gr


SMALL = ["norm_mix", "q_gain", "k_gain", "conv_w", "conv_b", "dt_bias", "a_log", "d_skip",
         "attn_out_gain", "ssm_out_gain", "norm_ffn"]
BIG = ["w_in", "w_out", "w_gate", "w_up", "w_down"]
ORDER = ["norm_mix", "w_in", "q_gain", "k_gain", "conv_w", "conv_b", "dt_bias", "a_log", "d_skip",
         "attn_out_gain", "ssm_out_gain", "w_out", "norm_ffn", "w_gate", "w_up", "w_down"]


def kernel(x, norm_mix, w_in, q_gain, k_gain, conv_w, conv_b, dt_bias, a_log, d_skip, attn_out_gain, ssm_out_gain, w_out, norm_ffn, w_gate, w_up, w_down, loss_target, m_norm_mix, m_w_in, m_q_gain, m_k_gain, m_conv_w, m_conv_b, m_dt_bias, m_a_log, m_d_skip, m_attn_out_gain, m_ssm_out_gain, m_w_out, m_norm_ffn, m_w_gate, m_w_up, m_w_down, v_norm_mix, v_w_in, v_q_gain, v_k_gain, v_conv_w, v_conv_b, v_dt_bias, v_a_log, v_d_skip, v_attn_out_gain, v_ssm_out_gain, v_w_out, v_norm_ffn, v_w_gate, v_w_up, v_w_down):
    w = dict(norm_mix=norm_mix, w_in=w_in, q_gain=q_gain, k_gain=k_gain, conv_w=conv_w, conv_b=conv_b,
             dt_bias=dt_bias, a_log=a_log, d_skip=d_skip, attn_out_gain=attn_out_gain, ssm_out_gain=ssm_out_gain,
             w_out=w_out, norm_ffn=norm_ffn, w_gate=w_gate, w_up=w_up, w_down=w_down)
    mom = dict(norm_mix=m_norm_mix, w_in=m_w_in, q_gain=m_q_gain, k_gain=m_k_gain, conv_w=m_conv_w,
               conv_b=m_conv_b, dt_bias=m_dt_bias, a_log=m_a_log, d_skip=m_d_skip,
               attn_out_gain=m_attn_out_gain, ssm_out_gain=m_ssm_out_gain, w_out=m_w_out, norm_ffn=m_norm_ffn,
               w_gate=m_w_gate, w_up=m_w_up, w_down=m_w_down)
    var = dict(norm_mix=v_norm_mix, w_in=v_w_in, q_gain=v_q_gain, k_gain=v_k_gain, conv_w=v_conv_w,
               conv_b=v_conv_b, dt_bias=v_dt_bias, a_log=v_a_log, d_skip=v_d_skip,
               attn_out_gain=v_attn_out_gain, ssm_out_gain=v_ssm_out_gain, w_out=v_w_out, norm_ffn=v_norm_ffn,
               w_gate=v_w_gate, w_up=v_w_up, w_down=v_w_down)

    bsz, s, d = x.shape
    t = bsz * s
    depth = norm_mix.shape[0]
    aw = attn_out_gain.shape[1]
    sw = ssm_out_gain.shape[1]
    cd = conv_b.shape[1]
    hs_n = dt_bias.shape[1]
    hg = hs_n // SSM_GROUPS
    heads = aw // ATT_HEAD_DIM
    in_dim = 3 * aw + sw + cd + hs_n
    dff = w_gate.shape[2] * N_CHIPS
    cs = conv_w.shape[2]
    my_chip = 2 * lax.axis_index("x") + lax.axis_index("y")
    c_idx = lax.axis_index("c").astype(jnp.int32).reshape(1)

    gat = {n: _allgather_shards(w[n].astype(BF16), f"allgather_{n}") for n in BIG}
    win = jnp.transpose(gat["w_in"], (1, 2, 0, 3)).reshape(depth, d, in_dim)
    wqkv = win[:, :, :3 * aw].reshape(depth, d, 3, heads, ATT_HEAD_DIM)
    wqkv = jnp.transpose(wqkv, (0, 1, 3, 2, 4)).reshape(depth, d, 3 * aw)
    wz = win[:, :, 3 * aw:3 * aw + sw]
    wxbc = win[:, :, 3 * aw + sw:3 * aw + sw + cd]
    wdt = _group_slots(win[:, :, 3 * aw + sw + cd:], hg)
    wout = jnp.transpose(gat["w_out"], (1, 0, 2, 3)).reshape(depth, aw + sw, d)
    wgate = jnp.transpose(gat["w_gate"], (1, 2, 0, 3)).reshape(depth, d, dff)
    wup = jnp.transpose(gat["w_up"], (1, 2, 0, 3)).reshape(depth, d, dff)
    wdown = jnp.transpose(gat["w_down"], (1, 0, 2, 3)).reshape(depth, dff, d)
    conv_all = _small_exchange(_pack([conv_w]), "allgather_conv_w", False)
    conv_full = jnp.concatenate(
        [_unpack(conv_all[2 * j], [conv_w.shape])[0] for j in range(N_CHIPS)], axis=-1)

    def layer_params(l):
        return dict(aw=aw, sw=sw, cd=cd, hg=hg, norm_mix=norm_mix[l], wqkv=wqkv[l], wz=wz[l], wxbc=wxbc[l],
                    wdt=wdt[l], q_gain=q_gain[l], k_gain=k_gain[l], conv_w=conv_full[l], conv_b=conv_b[l],
                    dt_bias=_group_slots(dt_bias[l], hg).reshape(1, -1),
                    a_log=_group_slots(a_log[l], hg).reshape(1, -1),
                    d_skip=_group_slots(d_skip[l], hg).reshape(1, -1),
                    attn_out_gain=attn_out_gain[l], ssm_out_gain=ssm_out_gain[l], wout=wout[l],
                    norm_ffn=norm_ffn[l], wg=wgate[l], wu=wup[l], wd=wdown[l])

    xt = x.reshape(t, d)
    saved = []
    for l in range(depth):
        xt, sv = _layer_fwd(xt, layer_params(l), bsz, s)
        saved.append(sv)
    dxt, dxb, loss_lanes = _loss_head(xt, loss_target.reshape(t, d), "loss_head")
    grads = [None] * depth
    for l in reversed(range(depth)):
        dxt, dxb, grads[l] = _layer_bwd(dxt, dxb, layer_params(l), saved[l], bsz, s)
    grad_x = dxt.reshape(bsz, s, d)

    def stack(name):
        return jnp.stack([grads[l][name] for l in range(depth)])

    gqkv = stack("wqkv").reshape(depth, d, heads, 3, ATT_HEAD_DIM)
    gqkv = jnp.transpose(gqkv, (0, 1, 3, 2, 4)).reshape(depth, d, 3 * aw)
    gin = jnp.concatenate([gqkv, stack("wz"), stack("wxbc"), _ungroup_slots(stack("wdt"), hg)], axis=-1)
    per_chip = {
        "w_in": jnp.transpose(gin.reshape(depth, d, N_CHIPS, in_dim // N_CHIPS), (2, 0, 1, 3)),
        "w_out": jnp.transpose(stack("wout").reshape(depth, N_CHIPS, (aw + sw) // N_CHIPS, d), (1, 0, 2, 3)),
        "w_gate": jnp.transpose(stack("wg").reshape(depth, d, N_CHIPS, dff // N_CHIPS), (2, 0, 1, 3)),
        "w_up": jnp.transpose(stack("wu").reshape(depth, d, N_CHIPS, dff // N_CHIPS), (2, 0, 1, 3)),
        "w_down": jnp.transpose(stack("wd").reshape(depth, N_CHIPS, dff // N_CHIPS, d), (1, 0, 2, 3)),
    }
    g = {n: _reduce_scatter(per_chip[n], c_idx, n) for n in BIG}
    small_shapes = [(1, LANES)] + [(depth, CONV_WIDTH, cd) if n == "conv_w" else w[n].shape for n in SMALL]
    small = _small_exchange(_pack([loss_lanes] + [stack(n) for n in SMALL]), "allreduce_small", True)
    small = _unpack(small, small_shapes)
    loss = small[0][0, 0]
    for n, a in zip(SMALL, small[1:], strict=True):
        g[n] = a
    g["conv_w"] = lax.dynamic_slice_in_dim(g["conv_w"], my_chip * cs, cs, axis=2)

    delta, new_m, new_v = {}, {}, {}
    for n in BIG:
        shp = w[n].shape
        two_d = (shp[0] * shp[1], shp[2])
        dl, nm, nv = _adamw(w[n].reshape(two_d), g[n].reshape(two_d), mom[n].reshape(two_d),
                            var[n].reshape(two_d), f"adamw_{n}")
        delta[n], new_m[n], new_v[n] = dl.reshape(shp), nm.reshape(shp), nv.reshape(shp)
    shapes = [w[n].shape for n in SMALL]
    dl, nm, nv = _adamw(_pack([w[n] for n in SMALL]), _pack([g[n] for n in SMALL]),
                        _pack([mom[n] for n in SMALL]), _pack([var[n] for n in SMALL]), "adamw_small")
    for n, a, b, c in zip(SMALL, _unpack(dl, shapes), _unpack(nm, shapes), _unpack(nv, shapes), strict=True):
        delta[n], new_m[n], new_v[n] = a, b, c

    return (loss, grad_x, *[g[n] for n in ORDER], *[delta[n] for n in ORDER],
            *[new_m[n] for n in ORDER], *[new_v[n] for n in ORDER])
```

```python
import jax
import jax.numpy as jnp
from jax import lax
from jax.experimental import pallas as pl
from jax.experimental.pallas import tpu as pltpu

F32 = jnp.float32
BF16 = jnp.bfloat16
MESH = pl.DeviceIdType.MESH
ANY = pl.BlockSpec(memory_space=pl.ANY)
VMEM = pl.BlockSpec(memory_space=pltpu.VMEM)

EPS = 1e-6
ATT_HEAD_DIM = 128
SSM_HEAD_DIM = 64
SSM_GROUPS = 2
SSM_STATE = 128
SSD_CHUNK = 128
CONV_WIDTH = 4
LANES = 128
SUBLANES = 8
ATT_TILE = 256
N_CHIPS = 4
N_DEV = 8

ADAM_LR = 0.001
ADAM_B1 = 0.9
ADAM_B2 = 0.999
ADAM_EPS = 1e-08
ADAM_WD = 0.01
ADAM_STEP = 10

VMEM_LIMIT = 48 * 1024 * 1024

NN = (((1,), (0,)), ((), ()))
NT = (((1,), (1,)), ((), ()))
TN = (((0,), (0,)), ((), ()))


def _dot(a, b, dims=NN):
    return lax.dot_general(a.astype(BF16), b.astype(BF16), dims, preferred_element_type=F32)


def _dot_exact(x, ones, dims=NN, passes=3, ones_left=False):
    acc = None
    rem = x
    for _ in range(passes):
        piece = rem.astype(BF16)
        rem = rem - piece.astype(F32)
        p = (lax.dot_general(ones, piece, dims, preferred_element_type=F32) if ones_left
             else lax.dot_general(piece, ones, dims, preferred_element_type=F32))
        acc = p if acc is None else acc + p
    return acc


def _iota2(shape, axis):
    return lax.broadcasted_iota(jnp.int32, shape, axis)


def _tri(n, cmp):
    return cmp(_iota2((n, n), 0), _iota2((n, n), 1)).astype(BF16)


def _sum_all(v):
    return jnp.sum(jnp.sum(v, axis=1, keepdims=True), axis=0, keepdims=True)


def _fit(tile, dim, unit=LANES):
    if dim <= tile:
        return dim
    return max(k for k in range(unit, tile + 1, unit) if dim % k == 0)


def _params(sem):
    return pltpu.CompilerParams(dimension_semantics=sem, vmem_limit_bytes=VMEM_LIMIT)


def _softplus(x):
    return jnp.maximum(x, 0.0) + jnp.log(1.0 + jnp.exp(-jnp.abs(x)))


def _sigmoid(x):
    return 1.0 / (1.0 + jnp.exp(-x))


def _rms_fwd(x, g):
    r = lax.rsqrt(jnp.mean(x * x, axis=-1, keepdims=True) + EPS)
    return (x * r) * g


def _rms_bwd(x, g, dh):
    r = lax.rsqrt(jnp.mean(x * x, axis=-1, keepdims=True) + EPS)
    y = x * r
    dy = dh * g
    dx = r * (dy - y * jnp.mean(dy * y, axis=-1, keepdims=True))
    return dx, dh * y


def _matmul(groups, extras, epilogue, out_dtypes, m, n, tm, tn, name):
    tm, tn = _fit(tm, m), _fit(tn, n)
    flat = [t for grp in groups for t in grp]
    n_terms, n_extra = len(flat), len(extras)

    def body(*refs):
        outs = refs[2 * n_terms + n_extra:]
        accs, pos = [], 0
        for grp in groups:
            acc = None
            for (_, _, mode) in grp:
                dims = {"nn": NN, "nt": NT, "tn": TN}[mode]
                p = _dot(refs[2 * pos][...], refs[2 * pos + 1][...], dims)
                acc = p if acc is None else acc + p
                pos += 1
            accs.append(acc)
        ex = [refs[2 * n_terms + i][...] for i in range(n_extra)]
        res = epilogue(accs, ex)
        for o_ref, r in zip(outs, res, strict=True):
            o_ref[...] = r.astype(o_ref.dtype)

    in_specs, args = [], []
    for (a, b, mode) in flat:
        if mode == "nn":
            k = a.shape[1]
            in_specs += [pl.BlockSpec((tm, k), lambda i, j: (i, 0)), pl.BlockSpec((k, tn), lambda i, j: (0, j))]
        elif mode == "nt":
            k = a.shape[1]
            in_specs += [pl.BlockSpec((tm, k), lambda i, j: (i, 0)), pl.BlockSpec((tn, k), lambda i, j: (j, 0))]
        else:
            k = a.shape[0]
            in_specs += [pl.BlockSpec((k, tm), lambda i, j: (0, i)), pl.BlockSpec((k, tn), lambda i, j: (0, j))]
        args += [a, b]
    for e in extras:
        in_specs.append(pl.BlockSpec((tm, tn), lambda i, j: (i, j)))
        args.append(e)
    return pl.pallas_call(
        body, name=name, grid=(m // tm, n // tn), in_specs=in_specs,
        out_specs=[pl.BlockSpec((tm, tn), lambda i, j: (i, j)) for _ in out_dtypes],
        out_shape=[jax.ShapeDtypeStruct((m, n), d) for d in out_dtypes],
        compiler_params=_params(("parallel", "parallel")),
    )(*args)


def _mm(a, b, mode, m, n, out_dtype, name, tm=512, tn=512, res=None):
    extras = [] if res is None else [res]
    epi = (lambda accs, ex: (accs[0],)) if res is None else (lambda accs, ex: (accs[0] + ex[0],))
    return _matmul([[(a, b, mode)]], extras, epi, [out_dtype], m, n, tm, tn, name)[0]


def _swiglu_fwd_epilogue(accs, ex):
    g, u = accs
    return g, u, (g * _sigmoid(g)) * u


def _swiglu_bwd_epilogue(accs, ex):
    dact, (g, u) = accs[0], ex
    sg = _sigmoid(g)
    silu = g * sg
    return dact * u * (sg * (1.0 + g * (1.0 - sg))), dact * silu


def _rmsnorm_fwd(x, g, name, tr=512):
    t, d = x.shape
    tr = min(tr, t)

    def body(x_ref, g_ref, h_ref):
        h_ref[...] = _rms_fwd(x_ref[...], g_ref[...]).astype(BF16)

    return pl.pallas_call(
        body, name=name, grid=(t // tr,),
        in_specs=[pl.BlockSpec((tr, d), lambda i: (i, 0)), pl.BlockSpec((1, d), lambda i: (0, 0))],
        out_specs=pl.BlockSpec((tr, d), lambda i: (i, 0)),
        out_shape=jax.ShapeDtypeStruct((t, d), BF16),
        compiler_params=_params(("parallel",)),
    )(x, g.reshape(1, d))


def _rmsnorm_bwd(x, g, dh, dres, name, tr=256):
    t, d = x.shape
    tr = min(tr, t)

    def body(x_ref, g_ref, dh_ref, dres_ref, dx_ref, dxb_ref, dg_ref):
        dx, dgr = _rms_bwd(x_ref[...], g_ref[...], dh_ref[...])
        dx = dx + dres_ref[...]
        dx_ref[...] = dx
        dxb_ref[...] = dx.astype(BF16)

        @pl.when(pl.program_id(0) == 0)
        def _():
            dg_ref[...] = jnp.zeros_like(dg_ref)

        dg_ref[...] += jnp.sum(dgr, axis=0, keepdims=True)

    row = pl.BlockSpec((tr, d), lambda i: (i, 0))
    vec = pl.BlockSpec((1, d), lambda i: (0, 0))
    dx, dxb, dg = pl.pallas_call(
        body, name=name, grid=(t // tr,),
        in_specs=[row, vec, row, row], out_specs=[row, row, vec],
        out_shape=[jax.ShapeDtypeStruct((t, d), F32), jax.ShapeDtypeStruct((t, d), BF16),
                   jax.ShapeDtypeStruct((1, d), F32)],
        compiler_params=_params(("arbitrary",)),
    )(x, g.reshape(1, d), dh, dres)
    return dx, dxb, dg.reshape(d)


def _merge_fwd(o_att, y, z, ga, gs, name, tr=256):
    t, wa = o_att.shape
    ws = y.shape[1]
    wg = ws // SSM_GROUPS
    tr = min(tr, t)

    def body(o_ref, y_ref, z_ref, ga_ref, gs_ref, m_ref):
        m_ref[:, 0:wa] = _rms_fwd(o_ref[...], ga_ref[...]).astype(BF16)
        for g in range(SSM_GROUPS):
            sl = slice(g * wg, (g + 1) * wg)
            zz = z_ref[:, sl]
            yz = y_ref[:, sl] * (zz * _sigmoid(zz))
            m_ref[:, wa + g * wg:wa + (g + 1) * wg] = _rms_fwd(yz, gs_ref[:, sl]).astype(BF16)

    return pl.pallas_call(
        body, name=name, grid=(t // tr,),
        in_specs=[pl.BlockSpec((tr, wa), lambda i: (i, 0)), pl.BlockSpec((tr, ws), lambda i: (i, 0)),
                  pl.BlockSpec((tr, ws), lambda i: (i, 0)), pl.BlockSpec((1, wa), lambda i: (0, 0)),
                  pl.BlockSpec((1, ws), lambda i: (0, 0))],
        out_specs=pl.BlockSpec((tr, wa + ws), lambda i: (i, 0)),
        out_shape=jax.ShapeDtypeStruct((t, wa + ws), BF16),
        compiler_params=_params(("parallel",)),
    )(o_att, y, z, ga.reshape(1, wa), gs.reshape(1, ws))


def _merge_bwd(o_att, y, z, ga, gs, dmix, name, tr=256):
    t, wa = o_att.shape
    ws = y.shape[1]
    wg = ws // SSM_GROUPS
    tr = min(tr, t)

    def body(o_ref, y_ref, z_ref, ga_ref, gs_ref, dm_ref, do_ref, dy_ref, dz_ref, dga_ref, dgs_ref):
        @pl.when(pl.program_id(0) == 0)
        def _():
            dga_ref[...] = jnp.zeros_like(dga_ref)
            dgs_ref[...] = jnp.zeros_like(dgs_ref)

        do, dgr = _rms_bwd(o_ref[...], ga_ref[...], dm_ref[:, 0:wa])
        do_ref[...] = do
        dga_ref[...] += jnp.sum(dgr, axis=0, keepdims=True)
        for g in range(SSM_GROUPS):
            sl = slice(g * wg, (g + 1) * wg)
            zz, yy = z_ref[:, sl], y_ref[:, sl]
            sg = _sigmoid(zz)
            silu = zz * sg
            dyz, dgr = _rms_bwd(yy * silu, gs_ref[:, sl], dm_ref[:, wa + g * wg:wa + (g + 1) * wg])
            dy_ref[:, sl] = dyz * silu
            dz_ref[:, sl] = (dyz * yy * (sg + silu * (1.0 - sg))).astype(BF16)
            dgs_ref[:, sl] += jnp.sum(dgr, axis=0, keepdims=True)

    rowa = pl.BlockSpec((tr, wa), lambda i: (i, 0))
    rows = pl.BlockSpec((tr, ws), lambda i: (i, 0))
    veca = pl.BlockSpec((1, wa), lambda i: (0, 0))
    vecs = pl.BlockSpec((1, ws), lambda i: (0, 0))
    do, dy, dz, dga, dgs = pl.pallas_call(
        body, name=name, grid=(t // tr,),
        in_specs=[rowa, rows, rows, veca, vecs, pl.BlockSpec((tr, wa + ws), lambda i: (i, 0))],
        out_specs=[rowa, rows, rows, veca, vecs],
        out_shape=[jax.ShapeDtypeStruct((t, wa), F32), jax.ShapeDtypeStruct((t, ws), F32),
                   jax.ShapeDtypeStruct((t, ws), BF16), jax.ShapeDtypeStruct((1, wa), F32),
                   jax.ShapeDtypeStruct((1, ws), F32)],
        compiler_params=_params(("arbitrary",)),
    )(o_att, y, z, ga.reshape(1, wa), gs.reshape(1, ws), dmix)
    return do, dy, dz, dga.reshape(wa), dgs.reshape(ws)


def _loss_head(y, target, name, tr=256):
    t, d = y.shape
    tr = min(tr, t)

    def body(y_ref, t_ref, dy_ref, dyb_ref, l_ref):
        @pl.when(pl.program_id(0) == 0)
        def _():
            l_ref[...] = jnp.zeros_like(l_ref)

        diff = y_ref[...] - t_ref[...]
        dy = diff * (1.0 / d)
        dy_ref[...] = dy
        dyb_ref[...] = dy.astype(BF16)
        part = jnp.sum(diff * diff, axis=0, keepdims=True)
        fold = part[:, 0:LANES]
        for k in range(1, d // LANES):
            fold = fold + part[:, k * LANES:(k + 1) * LANES]
        l_ref[...] += fold * (0.5 / d)

    row = pl.BlockSpec((tr, d), lambda i: (i, 0))
    return pl.pallas_call(
        body, name=name, grid=(t // tr,), in_specs=[row, row],
        out_specs=[row, row, pl.BlockSpec((1, LANES), lambda i: (0, 0))],
        out_shape=[jax.ShapeDtypeStruct((t, d), F32), jax.ShapeDtypeStruct((t, d), BF16),
                   jax.ShapeDtypeStruct((1, LANES), F32)],
        compiler_params=_params(("arbitrary",)),
    )(y, target)


def _adamw(w, g, m, v, name, tr=256):
    r, c = w.shape
    tr = _fit(tr, r, 16)

    def body(w_ref, g_ref, m_ref, v_ref, d_ref, nm_ref, nv_ref):
        gg = g_ref[...]
        nm = ADAM_B1 * m_ref[...] + (1.0 - ADAM_B1) * gg
        nv = ADAM_B2 * v_ref[...] + (1.0 - ADAM_B2) * (gg * gg)
        m_hat = nm / (1.0 - ADAM_B1 ** ADAM_STEP)
        v_hat = nv / (1.0 - ADAM_B2 ** ADAM_STEP)
        d_ref[...] = -ADAM_LR * (m_hat / (jnp.sqrt(v_hat) + ADAM_EPS) + ADAM_WD * w_ref[...])
        nm_ref[...] = nm
        nv_ref[...] = nv

    blk = pl.BlockSpec((tr, c), lambda i: (i, 0))
    return pl.pallas_call(
        body, name=name, grid=(r // tr,), in_specs=[blk] * 4, out_specs=[blk] * 3,
        out_shape=[jax.ShapeDtypeStruct((r, c), F32)] * 3,
        compiler_params=_params(("parallel",)),
    )(w, g, m, v)


def _att_scores(qi, kj, scale, diagonal):
    z = _dot(qi, kj, NT) * scale
    lb = -_softplus(-z)
    lrm = lb - z
    if not diagonal:
        return lb, lrm, None
    mask = _iota2(z.shape, 1) < _iota2(z.shape, 0)
    return lb, jnp.where(mask, lrm, 0.0), mask


def _masked(mask, v):
    return v if mask is None else jnp.where(mask, v, 0.0)


def _attention_fwd(qkv, qg, kg, name, tile=None):
    bsz, s, w3 = qkv.shape
    hd = ATT_HEAD_DIM
    heads = w3 // (3 * hd)
    tile = min(tile or ATT_TILE, s)
    nb = s // tile
    scale = hd ** -0.5

    def body(qkv_ref, qg_ref, kg_ref, o_ref, r_ref, qn_s, kn_s, vb_s, acc_s, c_s):
        qn_s[...] = _rms_fwd(qkv_ref[0, :, 0:hd], qg_ref[...]).astype(BF16)
        kn_s[...] = _rms_fwd(qkv_ref[0, :, hd:2 * hd], kg_ref[...]).astype(BF16)
        vb_s[...] = qkv_ref[0, :, 2 * hd:3 * hd].astype(BF16)
        after = _tri(tile, lambda r, c: r > c)

        def q_loop(i, _):
            rows = pl.ds(pl.multiple_of(i * tile, tile), tile)
            qi = qn_s[rows, :]
            acc_s[...] = jnp.zeros_like(acc_s)
            c_s[...] = jnp.zeros_like(c_s)

            def key_tile(j, diagonal):
                cols = pl.ds(pl.multiple_of(j * tile, tile), tile)
                lb, lrm, mask = _att_scores(qi, kn_s[cols, :], scale, diagonal)
                later = _dot_exact(lrm, after, passes=2) + c_s[...]
                w = _masked(mask, jnp.exp(lb + later))
                acc_s[...] += _dot(w, vb_s[cols, :])
                c_s[...] += jnp.sum(lrm, axis=1, keepdims=True)

            def k_loop(jj, _):
                key_tile(i - jj, False)
                return 0

            key_tile(i, True)
            lax.fori_loop(1, i + 1, k_loop, 0)
            o_ref[0, rows, :] = acc_s[...]
            r_ref[0, 0, rows, :] = c_s[...]
            return 0

        lax.fori_loop(0, nb, q_loop, 0)

    return pl.pallas_call(
        body, name=name, grid=(bsz, heads),
        in_specs=[pl.BlockSpec((1, s, 3 * hd), lambda b, h: (b, 0, h)),
                  pl.BlockSpec((1, hd), lambda b, h: (0, 0)), pl.BlockSpec((1, hd), lambda b, h: (0, 0))],
        out_specs=[pl.BlockSpec((1, s, hd), lambda b, h: (b, 0, h)),
                   pl.BlockSpec((1, 1, s, 1), lambda b, h: (b, h, 0, 0))],
        out_shape=[jax.ShapeDtypeStruct((bsz, s, heads * hd), F32),
                   jax.ShapeDtypeStruct((bsz, heads, s, 1), F32)],
        scratch_shapes=[pltpu.VMEM((s, hd), BF16), pltpu.VMEM((s, hd), BF16), pltpu.VMEM((s, hd), BF16),
                        pltpu.VMEM((tile, hd), F32), pltpu.VMEM((tile, 1), F32)],
        compiler_params=_params(("parallel", "parallel")),
    )(qkv, qg.reshape(1, hd), kg.reshape(1, hd))


def _attention_bwd(qkv, qg, kg, rtot, do, name, tile=None):
    bsz, s, w3 = qkv.shape
    hd = ATT_HEAD_DIM
    heads = w3 // (3 * hd)
    tile = min(tile or ATT_TILE, s)
    nb = s // tile
    scale = hd ** -0.5

    def body(qkv_ref, qg_ref, kg_ref, r_ref, do_ref, dqkv_ref, dqg_ref, dkg_ref,
             qn_s, kn_s, vb_s, dob_s, dqn_s, dkn_s, dv_s, c1_s, c2_s):
        qn_s[...] = _rms_fwd(qkv_ref[0, :, 0:hd], qg_ref[...]).astype(BF16)
        kn_s[...] = _rms_fwd(qkv_ref[0, :, hd:2 * hd], kg_ref[...]).astype(BF16)
        vb_s[...] = qkv_ref[0, :, 2 * hd:3 * hd].astype(BF16)
        dob_s[...] = do_ref[0].astype(BF16)
        dqn_s[...] = jnp.zeros_like(dqn_s)
        dkn_s[...] = jnp.zeros_like(dkn_s)
        dv_s[...] = jnp.zeros_like(dv_s)
        upto = _tri(tile, lambda r, c: r <= c)
        before = _tri(tile, lambda r, c: r < c)

        def q_loop(i, _):
            rows = pl.ds(pl.multiple_of(i * tile, tile), tile)
            qi, doi = qn_s[rows, :], dob_s[rows, :]
            rt = r_ref[0, 0, rows, :]
            c1_s[...] = jnp.zeros_like(c1_s)
            c2_s[...] = jnp.zeros_like(c2_s)

            def key_tile(j, diagonal):
                cols = pl.ds(pl.multiple_of(j * tile, tile), tile)
                kj, vj = kn_s[cols, :], vb_s[cols, :]
                lb, lrm, mask = _att_scores(qi, kj, scale, diagonal)
                later = rt - (_dot_exact(lrm, upto, passes=2) + c1_s[...])
                w = _masked(mask, jnp.exp(lb + later))
                e = w * _dot(doi, vj, NT)
                pre = _dot_exact(e, before, passes=2) + c2_s[...]
                dz = _masked(mask, (e - jnp.exp(lb) * (e + pre)) * scale)
                dv_s[cols, :] += _dot(w, doi, TN)
                dkn_s[cols, :] += _dot(dz, qi, TN)
                dqn_s[rows, :] += _dot(dz, kj)
                c1_s[...] += jnp.sum(lrm, axis=1, keepdims=True)
                c2_s[...] += jnp.sum(e, axis=1, keepdims=True)

            def k_loop(j, _):
                key_tile(j, False)
                return 0

            lax.fori_loop(0, i, k_loop, 0)
            key_tile(i, True)
            return 0

        lax.fori_loop(0, nb, q_loop, 0)
        dq, dgq = _rms_bwd(qkv_ref[0, :, 0:hd], qg_ref[...], dqn_s[...])
        dk, dgk = _rms_bwd(qkv_ref[0, :, hd:2 * hd], kg_ref[...], dkn_s[...])
        dqkv_ref[0, :, 0:hd] = dq.astype(BF16)
        dqkv_ref[0, :, hd:2 * hd] = dk.astype(BF16)
        dqkv_ref[0, :, 2 * hd:3 * hd] = dv_s[...].astype(BF16)
        dqg_ref[0, 0] = jnp.sum(dgq, axis=0, keepdims=True)
        dkg_ref[0, 0] = jnp.sum(dgk, axis=0, keepdims=True)

    gain = pl.BlockSpec((1, hd), lambda b, h: (0, 0))
    dgain = pl.BlockSpec((1, 1, 1, hd), lambda b, h: (b, h, 0, 0))
    return pl.pallas_call(
        body, name=name, grid=(bsz, heads),
        in_specs=[pl.BlockSpec((1, s, 3 * hd), lambda b, h: (b, 0, h)), gain, gain,
                  pl.BlockSpec((1, 1, s, 1), lambda b, h: (b, h, 0, 0)),
                  pl.BlockSpec((1, s, hd), lambda b, h: (b, 0, h))],
        out_specs=[pl.BlockSpec((1, s, 3 * hd), lambda b, h: (b, 0, h)), dgain, dgain],
        out_shape=[jax.ShapeDtypeStruct((bsz, s, w3), BF16),
                   jax.ShapeDtypeStruct((bsz, heads, 1, hd), F32),
                   jax.ShapeDtypeStruct((bsz, heads, 1, hd), F32)],
        scratch_shapes=[pltpu.VMEM((s, hd), BF16)] * 4 + [pltpu.VMEM((s, hd), F32)] * 3
        + [pltpu.VMEM((tile, 1), F32)] * 2,
        compiler_params=_params(("parallel", "parallel")),
    )(qkv, qg.reshape(1, hd), kg.reshape(1, hd), rtot, do)


def _conv_pre(pad_ref, w_ref, b_ref, s):
    pre = b_ref[...]
    for i in range(CONV_WIDTH):
        off = SUBLANES - (CONV_WIDTH - 1) + i
        pre = pre + pad_ref[off:off + s, :] * w_ref[i:i + 1, :]
    return pre


def _conv_fwd(u, w, b, name, tc=256):
    bsz, s, c = u.shape
    tc = min(tc, c)

    def body(u_ref, w_ref, b_ref, a_ref, pad_s):
        pad_s[0:SUBLANES, :] = jnp.zeros((SUBLANES, tc), F32)
        pad_s[SUBLANES:SUBLANES + s, :] = u_ref[0]
        pre = _conv_pre(pad_s, w_ref, b_ref, s)
        a_ref[0] = pre * _sigmoid(pre)

    return pl.pallas_call(
        body, name=name, grid=(bsz, c // tc),
        in_specs=[pl.BlockSpec((1, s, tc), lambda i, j: (i, 0, j)),
                  pl.BlockSpec((CONV_WIDTH, tc), lambda i, j: (0, j)), pl.BlockSpec((1, tc), lambda i, j: (0, j))],
        out_specs=pl.BlockSpec((1, s, tc), lambda i, j: (i, 0, j)),
        out_shape=jax.ShapeDtypeStruct((bsz, s, c), F32),
        scratch_shapes=[pltpu.VMEM((s + SUBLANES, tc), F32)],
        compiler_params=_params(("parallel", "parallel")),
    )(u, w, b.reshape(1, c))


def _conv_bwd(u, w, b, da, name, tc=256):
    bsz, s, c = u.shape
    tc = min(tc, c)

    def body(u_ref, w_ref, b_ref, da_ref, du_ref, dw_ref, db_ref, pad_s, gpad_s):
        @pl.when(pl.program_id(1) == 0)
        def _():
            dw_ref[...] = jnp.zeros_like(dw_ref)
            db_ref[...] = jnp.zeros_like(db_ref)

        pad_s[0:SUBLANES, :] = jnp.zeros((SUBLANES, tc), F32)
        pad_s[SUBLANES:SUBLANES + s, :] = u_ref[0]
        pre = _conv_pre(pad_s, w_ref, b_ref, s)
        sg = _sigmoid(pre)
        dpre = da_ref[0] * (sg * (1.0 + pre * (1.0 - sg)))
        gpad_s[0:s, :] = dpre
        gpad_s[s:s + SUBLANES, :] = jnp.zeros((SUBLANES, tc), F32)
        du = jnp.zeros((s, tc), F32)
        for i in range(CONV_WIDTH):
            back = CONV_WIDTH - 1 - i
            du = du + gpad_s[back:back + s, :] * w_ref[i:i + 1, :]
            off = SUBLANES - (CONV_WIDTH - 1) + i
            dw_ref[i:i + 1, :] += jnp.sum(dpre * pad_s[off:off + s, :], axis=0, keepdims=True)
        du_ref[0] = du.astype(BF16)
        db_ref[...] += jnp.sum(dpre, axis=0, keepdims=True)

    blk = pl.BlockSpec((1, s, tc), lambda j, i: (i, 0, j))
    du, dw, db = pl.pallas_call(
        body, name=name, grid=(c // tc, bsz),
        in_specs=[blk, pl.BlockSpec((CONV_WIDTH, tc), lambda j, i: (0, j)),
                  pl.BlockSpec((1, tc), lambda j, i: (0, j)), blk],
        out_specs=[blk, pl.BlockSpec((CONV_WIDTH, tc), lambda j, i: (0, j)),
                   pl.BlockSpec((1, tc), lambda j, i: (0, j))],
        out_shape=[jax.ShapeDtypeStruct((bsz, s, c), BF16), jax.ShapeDtypeStruct((CONV_WIDTH, c), F32),
                   jax.ShapeDtypeStruct((1, c), F32)],
        scratch_shapes=[pltpu.VMEM((s + SUBLANES, tc), F32), pltpu.VMEM((s + SUBLANES, tc), F32)],
        compiler_params=_params(("parallel", "arbitrary")),
    )(u, w, b.reshape(1, c), da)
    return du, dw, db.reshape(c)


def _ssd_chunk_common(b_ref, c_ref, dt_ref, dtb_ref, alog_ref):
    bm, cm = b_ref[0], c_ref[0]
    draw = dt_ref[0] + dtb_ref[...]
    dt = _softplus(draw)
    a_row = -jnp.exp(alog_ref[...])
    da = dt * a_row
    n = SSD_CHUNK
    acum = _dot_exact(da, _tri(n, lambda r, c: r >= c), ones_left=True)
    acum_t = _dot_exact(da, _tri(n, lambda r, c: r <= c), dims=TN)
    cb = _dot(cm, bm, NT)
    return bm, cm, draw, dt, a_row, acum, acum_t, cb


def _ssd_head_common(acum, acum_t, dt, cb, x, i):
    n, p = SSD_CHUNK, SSM_HEAD_DIM
    acol = acum[:, i:i + 1]
    arow = acum_t[i:i + 1, :]
    causal = _iota2((n, n), 0) >= _iota2((n, n), 1)
    lm = jnp.where(causal, jnp.exp(jnp.where(causal, acol - arow, 0.0)), 0.0)
    gm = cb * lm
    dtc = dt[:, i:i + 1]
    xh = x[:, i * p:(i + 1) * p]
    xdt = xh * dtc
    alast = acum[n - 1:n, i:i + 1]
    dte = jnp.exp(alast - acol)
    return acol, lm, gm, dtc, xh, xdt, alast, dte


def _ssd_specs(s, wg, hg, rev):
    g, n, cl = SSM_GROUPS, SSM_STATE, SSD_CHUNK
    nc = s // cl
    boff, coff = (g * wg) // n, (g * wg) // n + g
    ci = (lambda c: nc - 1 - c) if rev else (lambda c: c)
    xblk = pl.BlockSpec((1, cl, wg), lambda b, k, c: (b, ci(c), k))
    bblk = pl.BlockSpec((1, cl, n), lambda b, k, c: (b, ci(c), boff + k))
    cblk = pl.BlockSpec((1, cl, n), lambda b, k, c: (b, ci(c), coff + k))
    nblk = pl.BlockSpec((1, cl, n), lambda b, k, c: (b, ci(c), k))
    dtblk = pl.BlockSpec((1, cl, LANES), lambda b, k, c: (b, ci(c), k))
    vec = pl.BlockSpec((1, LANES), lambda b, k, c: (0, k))
    hsblk = pl.BlockSpec((1, 1, 1, wg, n), lambda b, k, c: (b, k, ci(c), 0, 0))
    return nc, xblk, bblk, cblk, nblk, dtblk, vec, hsblk


def _ssd_fwd(xbc, dtraw, dtb, alog, dskip, hg, name):
    bsz, s, _ = xbc.shape
    g, n, p = SSM_GROUPS, SSM_STATE, SSM_HEAD_DIM
    wg = hg * p
    nc, xblk, bblk, cblk, _, dtblk, vec, hsblk = _ssd_specs(s, wg, hg, False)

    def body(x_ref, b_ref, c_ref, dt_ref, dtb_ref, alog_ref, dsk_ref, y_ref, hs_ref, h_s):
        @pl.when(pl.program_id(2) == 0)
        def _():
            h_s[...] = jnp.zeros_like(h_s)

        bm, cm, _, dt, _, acum, acum_t, cb = _ssd_chunk_common(b_ref, c_ref, dt_ref, dtb_ref, alog_ref)
        x = x_ref[0]
        hs_ref[0, 0, 0] = h_s[...]
        for i in range(hg):
            acol, _, gm, _, xh, xdt, alast, dte = _ssd_head_common(acum, acum_t, dt, cb, x, i)
            hprev = h_s[i * p:(i + 1) * p, :]
            y = _dot(gm, xdt) + _dot(cm, hprev, NT) * jnp.exp(acol) + xh * dsk_ref[:, i:i + 1]
            y_ref[0, :, i * p:(i + 1) * p] = y
            h_s[i * p:(i + 1) * p, :] = hprev * jnp.exp(alast) + _dot(xdt * dte, bm, TN)

    return pl.pallas_call(
        body, name=name, grid=(bsz, g, nc),
        in_specs=[xblk, bblk, cblk, dtblk, vec, vec, vec],
        out_specs=[xblk, hsblk],
        out_shape=[jax.ShapeDtypeStruct((bsz, s, g * wg), F32),
                   jax.ShapeDtypeStruct((bsz, g, nc, wg, n), F32)],
        scratch_shapes=[pltpu.VMEM((wg, n), F32)],
        compiler_params=_params(("parallel", "parallel", "arbitrary")),
    )(xbc, xbc, xbc, dtraw, dtb, alog, dskip)


def _ssd_bwd(xbc, dtraw, dtb, alog, dskip, hs, dy, hg, name):
    bsz, s, _ = xbc.shape
    g, n, p, cl = SSM_GROUPS, SSM_STATE, SSM_HEAD_DIM, SSD_CHUNK
    wg = hg * p
    nc, xblk, bblk, cblk, nblk, dtblk, vec, hsblk = _ssd_specs(s, wg, hg, True)

    def body(x_ref, b_ref, c_ref, dt_ref, dtb_ref, alog_ref, dsk_ref, hs_ref, dy_ref,
             dx_ref, db_ref, dc_ref, ddt_ref, dvec_ref, dh_s):
        @pl.when(pl.program_id(2) == 0)
        def _():
            dh_s[...] = jnp.zeros_like(dh_s)
            dvec_ref[...] = jnp.zeros_like(dvec_ref)

        lane = _iota2((cl, LANES), 1)
        sub = _iota2((LANES, cl), 0)
        lane1 = _iota2((1, LANES), 1)
        last_row = _iota2((cl, 1), 0) == cl - 1
        bm, cm, draw, dt, a_row, acum, acum_t, cb = _ssd_chunk_common(b_ref, c_ref, dt_ref, dtb_ref, alog_ref)
        x = x_ref[0]
        dyc = dy_ref[0]
        dcb = jnp.zeros((cl, cl), F32)
        dcm = jnp.zeros((cl, n), F32)
        dbm = jnp.zeros((cl, n), F32)
        da_col = jnp.zeros((cl, LANES), F32)
        da_row = jnp.zeros((LANES, cl), F32)
        ddt = jnp.zeros((cl, LANES), F32)
        dd = jnp.zeros((1, LANES), F32)
        for i in range(hg):
            acol, lm, gm, dtc, xh, xdt, alast, dte = _ssd_head_common(acum, acum_t, dt, cb, x, i)
            dyh = dyc[:, i * p:(i + 1) * p]
            hprev = hs_ref[0, 0, 0, i * p:(i + 1) * p, :]
            dhn = dh_s[i * p:(i + 1) * p, :]
            ea, cd = jnp.exp(acol), jnp.exp(alast)
            dd = dd + jnp.where(lane1 == i, _sum_all(dyh * xh), 0.0)
            y0 = _dot(cm, hprev, NT)
            dy0 = dyh * ea
            dacol = jnp.sum(dyh * y0, axis=1, keepdims=True) * ea
            dcm = dcm + _dot(dy0, hprev)
            dh_s[i * p:(i + 1) * p, :] = _dot(dy0, cm, TN) + dhn * cd
            d_alast = _sum_all(dhn * hprev) * cd
            dxe = _dot(bm, dhn, NT)
            dbm = dbm + _dot(xdt * dte, dhn)
            dxdt = dxe * dte
            t1 = jnp.sum(dxe * xdt, axis=1, keepdims=True) * dte
            dacol = dacol - t1
            d_alast = d_alast + _sum_all(t1)
            dgm = _dot(dyh, xdt, NT)
            dxdt = dxdt + _dot(gm, dyh, TN)
            dcb = dcb + dgm * lm
            ws = dgm * gm
            dacol = dacol + jnp.sum(ws, axis=1, keepdims=True)
            darow = -jnp.sum(ws, axis=0, keepdims=True)
            dacol = dacol + jnp.where(last_row, d_alast, 0.0)
            dx_ref[0, :, i * p:(i + 1) * p] = dxdt * dtc + dyh * dsk_ref[:, i:i + 1]
            da_col = jnp.where(lane == i, dacol, da_col)
            da_row = jnp.where(sub == i, darow, da_row)
            ddt = jnp.where(lane == i, jnp.sum(dxdt * xh, axis=1, keepdims=True), ddt)
        dc_ref[0] = dcm + _dot(dcb, bm)
        db_ref[0] = dbm + _dot(dcb, cm, TN)
        upper = _tri(cl, lambda r, k: r <= k)
        dda = _dot_exact(da_col, upper, ones_left=True) + _dot_exact(da_row, upper, dims=NT, ones_left=True)
        ddt = ddt + dda * a_row
        ddraw = ddt * _sigmoid(draw)
        ddt_ref[0] = ddraw.astype(BF16)
        dvec_ref[0, 0, 0:1, :] += jnp.sum(ddraw, axis=0, keepdims=True)
        dvec_ref[0, 0, 1:2, :] += jnp.sum(dda * dt, axis=0, keepdims=True) * a_row
        dvec_ref[0, 0, 2:3, :] += dd

    return pl.pallas_call(
        body, name=name, grid=(bsz, g, nc),
        in_specs=[xblk, bblk, cblk, dtblk, vec, vec, vec, hsblk, xblk],
        out_specs=[xblk, nblk, nblk, dtblk,
                   pl.BlockSpec((1, 1, SUBLANES, LANES), lambda b, k, c: (b, k, 0, 0))],
        out_shape=[jax.ShapeDtypeStruct((bsz, s, g * wg), F32), jax.ShapeDtypeStruct((bsz, s, g * n), F32),
                   jax.ShapeDtypeStruct((bsz, s, g * n), F32), jax.ShapeDtypeStruct((bsz, s, g * LANES), BF16),
                   jax.ShapeDtypeStruct((bsz, g, SUBLANES, LANES), F32)],
        scratch_shapes=[pltpu.VMEM((wg, n), F32)],
        compiler_params=_params(("parallel", "parallel", "arbitrary")),
    )(xbc, xbc, xbc, dtraw, dtb, alog, dskip, hs, dy)


def _coords():
    return lax.axis_index("x"), lax.axis_index("y"), lax.axis_index("c")


def _other_chips(x, y):
    return [(1 - x, y), (x, 1 - y), (1 - x, 1 - y)]


def _remote(src, dst, send_sems, recv_sems, k, to):
    return pltpu.make_async_remote_copy(src_ref=src, dst_ref=dst, send_sem=send_sems.at[k],
                                        recv_sem=recv_sems.at[k], device_id=to, device_id_type=MESH)


def _allgather_shards(wb, name):
    layers = wb.shape[0]
    half = layers // 2

    def body(w_ref, o_ref, send_sems, recv_sems):
        x, y, c = _coords()
        me = 2 * x + y
        sibling = (x, y, 1 - c)
        chips = _other_chips(x, y)

        def region(j, h):
            return o_ref.at[j, pl.ds(h * half, half)]

        first = [_remote(w_ref.at[pl.ds(c * half, half)], region(me, c), send_sems, recv_sems, k, (px, py, c))
                 for k, (px, py) in enumerate(chips)]
        for cp in first:
            cp.start()
        passed = []
        for k, (px, py) in enumerate(chips):
            j = 2 * px + py
            _remote(region(j, c), region(j, c), send_sems, recv_sems, k, (px, py, c)).wait_recv()
            fw = _remote(region(j, c), region(j, c), send_sems, recv_sems, 3 + k, sibling)
            fw.start()
            passed.append(fw)
        for k, (px, py) in enumerate(chips):
            j = 2 * px + py
            _remote(region(j, 1 - c), region(j, 1 - c), send_sems, recv_sems, 3 + k, sibling).wait_recv()
        for cp in first + passed:
            cp.wait_send()

    out = pl.pallas_call(
        body, name=name, in_specs=[ANY], out_specs=ANY,
        out_shape=jax.ShapeDtypeStruct((N_CHIPS,) + wb.shape, wb.dtype),
        scratch_shapes=[pltpu.SemaphoreType.DMA((6,)), pltpu.SemaphoreType.DMA((6,))],
    )(wb)
    my_chip = 2 * lax.axis_index("x") + lax.axis_index("y")
    return lax.dynamic_update_slice(out, wb[None], (my_chip,) + (0,) * wb.ndim)


def _pair_exchange(gj, name):
    layers = gj.shape[1]
    half = layers // 2

    def body(g_ref, r_ref, send_sems, recv_sems):
        x, y, c = _coords()
        cp = _remote(g_ref.at[:, pl.ds((1 - c) * half, half)], r_ref, send_sems, recv_sems, 0, (x, y, 1 - c))
        cp.start()
        cp.wait()

    return pl.pallas_call(
        body, name=name, in_specs=[ANY], out_specs=ANY,
        out_shape=jax.ShapeDtypeStruct((gj.shape[0], half) + gj.shape[2:], gj.dtype),
        scratch_shapes=[pltpu.SemaphoreType.DMA((1,)), pltpu.SemaphoreType.DMA((1,))],
    )(gj)


def _pair_add(gj, r1, c_idx, name, tr=256):
    nj, layers, r, c = gj.shape
    half = layers // 2
    tr = _fit(tr, r, 16)

    def body(c_ref, g_ref, r_ref, p_ref, pb_ref):
        s = g_ref[...] + r_ref[...]
        p_ref[...] = s
        pb_ref[...] = s.astype(BF16)

    blk_r = pl.BlockSpec((1, 1, tr, c), lambda j, l, i, cr: (j, l, i, 0))
    blk_g = pl.BlockSpec((1, 1, tr, c), lambda j, l, i, cr: (j, cr[0] * half + l, i, 0))
    return pl.pallas_call(
        body, name=name,
        grid_spec=pltpu.PrefetchScalarGridSpec(
            num_scalar_prefetch=1, grid=(nj, half, r // tr),
            in_specs=[blk_g, blk_r], out_specs=[blk_r, blk_r]),
        out_shape=[jax.ShapeDtypeStruct((nj, half, r, c), F32), jax.ShapeDtypeStruct((nj, half, r, c), BF16)],
        compiler_params=_params(("parallel", "parallel", "parallel")),
    )(c_idx, gj, r1)


def _chip_exchange(pb, name):
    def body(pb_ref, r_ref, send_sems, recv_sems):
        x, y, c = _coords()
        cps = [_remote(pb_ref.at[2 * px + py], r_ref.at[k], send_sems, recv_sems, k, (px, py, c))
               for k, (px, py) in enumerate(_other_chips(x, y))]
        for cp in cps:
            cp.start()
        for cp in cps:
            cp.wait()

    return pl.pallas_call(
        body, name=name, in_specs=[ANY], out_specs=ANY,
        out_shape=jax.ShapeDtypeStruct((3,) + pb.shape[1:], BF16),
        scratch_shapes=[pltpu.SemaphoreType.DMA((3,)), pltpu.SemaphoreType.DMA((3,))],
    )(pb)


def _chip_add(p, r2, chip_idx, name, tr=256):
    _, half, r, c = p.shape
    tr = _fit(tr, r, 16)

    def body(j_ref, o_ref, r_ref, f_ref):
        f_ref[...] = ((o_ref[0] + r_ref[0].astype(F32)) + r_ref[1].astype(F32)) + r_ref[2].astype(F32)

    blk = pl.BlockSpec((1, tr, c), lambda l, i, jr: (l, i, 0))
    return pl.pallas_call(
        body, name=name,
        grid_spec=pltpu.PrefetchScalarGridSpec(
            num_scalar_prefetch=1, grid=(half, r // tr),
            in_specs=[pl.BlockSpec((1, 1, tr, c), lambda l, i, jr: (jr[0], l, i, 0)),
                      pl.BlockSpec((3, 1, tr, c), lambda l, i, jr: (0, l, i, 0))],
            out_specs=blk),
        out_shape=jax.ShapeDtypeStruct((half, r, c), F32),
        compiler_params=_params(("parallel", "parallel")),
    )(chip_idx, p, r2)


def _pair_gather(f, name):
    half = f.shape[0]

    def body(f_ref, o_ref, send_sems, recv_sems):
        x, y, c = _coords()
        cp = _remote(f_ref, o_ref.at[pl.ds(c * half, half)], send_sems, recv_sems, 0, (x, y, 1 - c))
        cp.start()
        cp.wait()

    out = pl.pallas_call(
        body, name=name, in_specs=[ANY], out_specs=ANY,
        out_shape=jax.ShapeDtypeStruct((2 * half,) + f.shape[1:], f.dtype),
        scratch_shapes=[pltpu.SemaphoreType.DMA((1,)), pltpu.SemaphoreType.DMA((1,))],
    )(f)
    return lax.dynamic_update_slice(out, f, (lax.axis_index("c") * half,) + (0,) * (f.ndim - 1))


def _reduce_scatter(gj, c_idx, chip_idx, tag):
    r1 = _pair_exchange(gj, f"rs_pair_{tag}")
    p, pb = _pair_add(gj, r1, c_idx, f"rs_pair_add_{tag}")
    r2 = _chip_exchange(pb, f"rs_chip_{tag}")
    f = _chip_add(p, r2, chip_idx, f"rs_chip_add_{tag}")
    return _pair_gather(f, f"rs_gather_{tag}")


def _small_exchange(v, name, reduce):
    rows = v.shape[0]

    def body(v_ref, o_ref, *rest):
        buf = rest[0] if reduce else o_ref
        send_sems, recv_sems = rest[-2], rest[-1]
        x, y, c = _coords()
        me = 4 * x + 2 * y + c
        buf[me] = v_ref[...]
        cps = []
        for r in range(1, N_DEV):
            peer = (lax.bitwise_xor(x, (r >> 2) & 1), lax.bitwise_xor(y, (r >> 1) & 1), lax.bitwise_xor(c, r & 1))
            cps.append(_remote(v_ref, buf.at[me], send_sems, recv_sems, r - 1, peer))
        for cp in cps:
            cp.start()
        for r in range(1, N_DEV):
            src = buf.at[lax.bitwise_xor(me, r)]
            _remote(src, src, send_sems, recv_sems, r - 1, (x, y, c)).wait_recv()
        for cp in cps:
            cp.wait_send()
        if reduce:
            acc = buf[0]
            for d in range(1, N_DEV):
                acc = acc + buf[d]
            o_ref[...] = acc
            o_ref[0:1, :] = jnp.broadcast_to(jnp.sum(acc[0:1, :], axis=1, keepdims=True), (1, LANES))

    scratch = [pltpu.SemaphoreType.DMA((N_DEV - 1,)), pltpu.SemaphoreType.DMA((N_DEV - 1,))]
    if reduce:
        scratch = [pltpu.VMEM((N_DEV, rows, LANES), F32)] + scratch
    out_shape = (rows, LANES) if reduce else (N_DEV, rows, LANES)
    return pl.pallas_call(
        body, name=name, in_specs=[VMEM], out_specs=VMEM,
        out_shape=jax.ShapeDtypeStruct(out_shape, F32), scratch_shapes=scratch,
    )(v)


def _pack(parts):
    flat = []
    for a in parts:
        a = a.reshape(-1)
        flat.append(jnp.pad(a, (0, (-a.shape[0]) % LANES)))
    v = jnp.concatenate(flat)
    v = jnp.pad(v, (0, (-v.shape[0]) % (SUBLANES * LANES)))
    return v.reshape(-1, LANES)


def _unpack(slab, shapes):
    flat = slab.reshape(-1)
    out, off = [], 0
    for shp in shapes:
        size = 1
        for d in shp:
            size *= d
        out.append(flat[off:off + size].reshape(shp))
        off += size + (-size) % LANES
    return out


def _group_slots(a, hg):
    lead = a.shape[:-1]
    a = a.reshape(lead + (SSM_GROUPS, hg))
    a = jnp.pad(a, [(0, 0)] * len(lead) + [(0, 0), (0, LANES - hg)])
    return a.reshape(lead + (SSM_GROUPS * LANES,))


def _ungroup_slots(a, hg):
    lead = a.shape[:-1]
    return a.reshape(lead + (SSM_GROUPS, LANES))[..., :hg].reshape(lead + (SSM_GROUPS * hg,))


def _layer_fwd(x, p, bsz, s):
    t, d = x.shape
    aw, sw, cd, hg = p["aw"], p["sw"], p["cd"], p["hg"]
    dff = p["wg"].shape[1]
    h = _rmsnorm_fwd(x, p["norm_mix"], "norm_mix_fwd")
    qkv = _mm(h, p["wqkv"], "nn", t, 3 * aw, F32, "proj_qkv", tm=1024, tn=512)
    z = _mm(h, p["wz"], "nn", t, sw, F32, "proj_z", tm=1024, tn=512)
    xbc = _mm(h, p["wxbc"], "nn", t, cd, F32, "proj_xbc", tm=1024, tn=512)
    dtraw = _mm(h, p["wdt"], "nn", t, SSM_GROUPS * LANES, F32, "proj_dt", tm=1024, tn=SSM_GROUPS * LANES)
    qkv3 = qkv.reshape(bsz, s, 3 * aw)
    o_att, rtot = _attention_fwd(qkv3, p["q_gain"], p["k_gain"], "attention_fwd")
    xbc3 = xbc.reshape(bsz, s, cd)
    xact = _conv_fwd(xbc3, p["conv_w"], p["conv_b"], "conv_fwd")
    dt3 = dtraw.reshape(bsz, s, SSM_GROUPS * LANES)
    y, hs = _ssd_fwd(xact, dt3, p["dt_bias"], p["a_log"], p["d_skip"], hg, "ssd_fwd")
    o2, y2 = o_att.reshape(t, aw), y.reshape(t, sw)
    mix = _merge_fwd(o2, y2, z, p["attn_out_gain"], p["ssm_out_gain"], "merge_fwd")
    x1 = _mm(mix, p["wout"], "nn", t, d, F32, "proj_out", tm=1024, tn=512, res=x)
    h2 = _rmsnorm_fwd(x1, p["norm_ffn"], "norm_ffn_fwd")
    gate, up, act = _matmul([[(h2, p["wg"], "nn")], [(h2, p["wu"], "nn")]], [], _swiglu_fwd_epilogue,
                            [F32, F32, BF16], t, dff, 512, 512, "ffn_gate_up")
    x2 = _mm(act, p["wd"], "nn", t, d, F32, "ffn_down", tm=512, tn=512, res=x1)
    saved = dict(x=x, h=h, qkv3=qkv3, z=z, xbc3=xbc3, dt3=dt3, o2=o2, rtot=rtot, xact=xact, hs=hs, y2=y2,
                 mix=mix, x1=x1, h2=h2, gate=gate, up=up, act=act)
    return x2, saved


def _layer_bwd(dx2, dx2b, p, sv, bsz, s):
    t, d = dx2.shape
    aw, sw, cd, hg = p["aw"], p["sw"], p["cd"], p["hg"]
    dff = p["wg"].shape[1]
    gr = {}
    dgate, dup = _matmul([[(dx2b, p["wd"], "nt")]], [sv["gate"], sv["up"]], _swiglu_bwd_epilogue,
                         [BF16, BF16], t, dff, 512, 512, "ffn_down_dgrad")
    gr["wd"] = _mm(sv["act"], dx2b, "tn", dff, d, F32, "ffn_down_wgrad")
    dh2 = _matmul([[(dgate, p["wg"], "nt"), (dup, p["wu"], "nt")]], [], lambda accs, ex: (accs[0],),
                  [F32], t, d, 512, 256, "ffn_gate_up_dgrad")[0]
    gr["wg"] = _mm(sv["h2"], dgate, "tn", d, dff, F32, "ffn_gate_wgrad")
    gr["wu"] = _mm(sv["h2"], dup, "tn", d, dff, F32, "ffn_up_wgrad")
    dx1, dx1b, gr["norm_ffn"] = _rmsnorm_bwd(sv["x1"], p["norm_ffn"], dh2, dx2, "norm_ffn_bwd")
    dmix = _mm(dx1b, p["wout"], "nt", t, aw + sw, F32, "proj_out_dgrad")
    gr["wout"] = _mm(sv["mix"], dx1b, "tn", aw + sw, d, F32, "proj_out_wgrad")
    do, dy, dz, gr["attn_out_gain"], gr["ssm_out_gain"] = _merge_bwd(
        sv["o2"], sv["y2"], sv["z"], p["attn_out_gain"], p["ssm_out_gain"], dmix, "merge_bwd")
    dxs, dbm, dcm, ddt, dvec = _ssd_bwd(sv["xact"], sv["dt3"], p["dt_bias"], p["a_log"], p["d_skip"], sv["hs"],
                                        dy.reshape(bsz, s, sw), hg, "ssd_bwd")
    dvec = jnp.sum(dvec, axis=0).reshape(SSM_GROUPS, SUBLANES, LANES)
    gr["dt_bias"], gr["a_log"], gr["d_skip"] = (dvec[:, k, :hg].reshape(-1) for k in range(3))
    dxact = jnp.concatenate([dxs, dbm, dcm], axis=-1)
    dxbc, gr["conv_w"], gr["conv_b"] = _conv_bwd(sv["xbc3"], p["conv_w"], p["conv_b"], dxact, "conv_bwd")
    dqkv, dqg, dkg = _attention_bwd(sv["qkv3"], p["q_gain"], p["k_gain"], sv["rtot"],
                                    do.reshape(bsz, s, aw), "attention_bwd")
    gr["q_gain"] = jnp.sum(dqg, axis=(0, 1, 2))
    gr["k_gain"] = jnp.sum(dkg, axis=(0, 1, 2))
    dqkv, dxbc, ddt = dqkv.reshape(t, 3 * aw), dxbc.reshape(t, cd), ddt.reshape(t, SSM_GROUPS * LANES)
    dh = _matmul([[(dqkv, p["wqkv"], "nt"), (dz, p["wz"], "nt"), (dxbc, p["wxbc"], "nt"), (ddt, p["wdt"], "nt")]],
                 [], lambda accs, ex: (accs[0],), [F32], t, d, 512, 256, "proj_in_dgrad")[0]
    h = sv["h"]
    gr["wqkv"] = _mm(h, dqkv, "tn", d, 3 * aw, F32, "proj_qkv_wgrad")
    gr["wz"] = _mm(h, dz, "tn", d, sw, F32, "proj_z_wgrad")
    gr["wxbc"] = _mm(h, dxbc, "tn", d, cd, F32, "proj_xbc_wgrad")
    gr["wdt"] = _mm(h, ddt, "tn", d, SSM_GROUPS * LANES, F32, "proj_dt_wgrad", tn=SSM_GROUPS * LANES)
    dx, dxb, gr["norm_mix"] = _rmsnorm_bwd(sv["x"], p["norm_mix"], dh, dx1, "norm_mix_bwd")
    return dx, dxb, gr


SMALL = ["norm_mix", "q_gain", "k_gain", "conv_w", "conv_b", "dt_bias", "a_log", "d_skip",
         "attn_out_gain", "ssm_out_gain", "norm_ffn"]
BIG = ["w_in", "w_out", "w_gate", "w_up", "w_down"]
ORDER = ["norm_mix", "w_in", "q_gain", "k_gain", "conv_w", "conv_b", "dt_bias", "a_log", "d_skip",
         "attn_out_gain", "ssm_out_gain", "w_out", "norm_ffn", "w_gate", "w_up", "w_down"]


def kernel(x, norm_mix, w_in, q_gain, k_gain, conv_w, conv_b, dt_bias, a_log, d_skip, attn_out_gain, ssm_out_gain, w_out, norm_ffn, w_gate, w_up, w_down, loss_target, m_norm_mix, m_w_in, m_q_gain, m_k_gain, m_conv_w, m_conv_b, m_dt_bias, m_a_log, m_d_skip, m_attn_out_gain, m_ssm_out_gain, m_w_out, m_norm_ffn, m_w_gate, m_w_up, m_w_down, v_norm_mix, v_w_in, v_q_gain, v_k_gain, v_conv_w, v_conv_b, v_dt_bias, v_a_log, v_d_skip, v_attn_out_gain, v_ssm_out_gain, v_w_out, v_norm_ffn, v_w_gate, v_w_up, v_w_down):
    w = dict(norm_mix=norm_mix, w_in=w_in, q_gain=q_gain, k_gain=k_gain, conv_w=conv_w, conv_b=conv_b,
             dt_bias=dt_bias, a_log=a_log, d_skip=d_skip, attn_out_gain=attn_out_gain, ssm_out_gain=ssm_out_gain,
             w_out=w_out, norm_ffn=norm_ffn, w_gate=w_gate, w_up=w_up, w_down=w_down)
    mom = dict(norm_mix=m_norm_mix, w_in=m_w_in, q_gain=m_q_gain, k_gain=m_k_gain, conv_w=m_conv_w,
               conv_b=m_conv_b, dt_bias=m_dt_bias, a_log=m_a_log, d_skip=m_d_skip,
               attn_out_gain=m_attn_out_gain, ssm_out_gain=m_ssm_out_gain, w_out=m_w_out, norm_ffn=m_norm_ffn,
               w_gate=m_w_gate, w_up=m_w_up, w_down=m_w_down)
    var = dict(norm_mix=v_norm_mix, w_in=v_w_in, q_gain=v_q_gain, k_gain=v_k_gain, conv_w=v_conv_w,
               conv_b=v_conv_b, dt_bias=v_dt_bias, a_log=v_a_log, d_skip=v_d_skip,
               attn_out_gain=v_attn_out_gain, ssm_out_gain=v_ssm_out_gain, w_out=v_w_out, norm_ffn=v_norm_ffn,
               w_gate=v_w_gate, w_up=v_w_up, w_down=v_w_down)

    bsz, s, d = x.shape
    t = bsz * s
    depth = norm_mix.shape[0]
    aw = attn_out_gain.shape[1]
    sw = ssm_out_gain.shape[1]
    cd = conv_b.shape[1]
    hs_n = dt_bias.shape[1]
    hg = hs_n // SSM_GROUPS
    heads = aw // ATT_HEAD_DIM
    in_dim = 3 * aw + sw + cd + hs_n
    dff = w_gate.shape[2] * N_CHIPS
    cs = conv_w.shape[2]
    my_chip = 2 * lax.axis_index("x") + lax.axis_index("y")
    c_idx = lax.axis_index("c").astype(jnp.int32).reshape(1)

    gat = {n: _allgather_shards(w[n].astype(BF16), f"allgather_{n}") for n in BIG}
    win = jnp.transpose(gat["w_in"], (1, 2, 0, 3)).reshape(depth, d, in_dim)
    wqkv = win[:, :, :3 * aw].reshape(depth, d, 3, heads, ATT_HEAD_DIM)
    wqkv = jnp.transpose(wqkv, (0, 1, 3, 2, 4)).reshape(depth, d, 3 * aw)
    wz = win[:, :, 3 * aw:3 * aw + sw]
    wxbc = win[:, :, 3 * aw + sw:3 * aw + sw + cd]
    wdt = _group_slots(win[:, :, 3 * aw + sw + cd:], hg)
    wout = jnp.transpose(gat["w_out"], (1, 0, 2, 3)).reshape(depth, aw + sw, d)
    wgate = jnp.transpose(gat["w_gate"], (1, 2, 0, 3)).reshape(depth, d, dff)
    wup = jnp.transpose(gat["w_up"], (1, 2, 0, 3)).reshape(depth, d, dff)
    wdown = jnp.transpose(gat["w_down"], (1, 0, 2, 3)).reshape(depth, dff, d)
    conv_all = _small_exchange(_pack([conv_w]), "allgather_conv_w", False)
    conv_full = jnp.concatenate(
        [_unpack(conv_all[2 * j], [conv_w.shape])[0] for j in range(N_CHIPS)], axis=-1)

    def layer_params(l):
        return dict(aw=aw, sw=sw, cd=cd, hg=hg, norm_mix=norm_mix[l], wqkv=wqkv[l], wz=wz[l], wxbc=wxbc[l],
                    wdt=wdt[l], q_gain=q_gain[l], k_gain=k_gain[l], conv_w=conv_full[l], conv_b=conv_b[l],
                    dt_bias=_group_slots(dt_bias[l], hg).reshape(1, -1),
                    a_log=_group_slots(a_log[l], hg).reshape(1, -1),
                    d_skip=_group_slots(d_skip[l], hg).reshape(1, -1),
                    attn_out_gain=attn_out_gain[l], ssm_out_gain=ssm_out_gain[l], wout=wout[l],
                    norm_ffn=norm_ffn[l], wg=wgate[l], wu=wup[l], wd=wdown[l])

    xt = x.reshape(t, d)
    saved = []
    for l in range(depth):
        xt, sv = _layer_fwd(xt, layer_params(l), bsz, s)
        saved.append(sv)
    dxt, dxb, loss_lanes = _loss_head(xt, loss_target.reshape(t, d), "loss_head")
    grads = [None] * depth
    for l in reversed(range(depth)):
        dxt, dxb, grads[l] = _layer_bwd(dxt, dxb, layer_params(l), saved[l], bsz, s)
    grad_x = dxt.reshape(bsz, s, d)

    def stack(name):
        return jnp.stack([grads[l][name] for l in range(depth)])

    gqkv = stack("wqkv").reshape(depth, d, heads, 3, ATT_HEAD_DIM)
    gqkv = jnp.transpose(gqkv, (0, 1, 3, 2, 4)).reshape(depth, d, 3 * aw)
    gin = jnp.concatenate([gqkv, stack("wz"), stack("wxbc"), _ungroup_slots(stack("wdt"), hg)], axis=-1)
    per_chip = {
        "w_in": jnp.transpose(gin.reshape(depth, d, N_CHIPS, in_dim // N_CHIPS), (2, 0, 1, 3)),
        "w_out": jnp.transpose(stack("wout").reshape(depth, N_CHIPS, (aw + sw) // N_CHIPS, d), (1, 0, 2, 3)),
        "w_gate": jnp.transpose(stack("wg").reshape(depth, d, N_CHIPS, dff // N_CHIPS), (2, 0, 1, 3)),
        "w_up": jnp.transpose(stack("wu").reshape(depth, d, N_CHIPS, dff // N_CHIPS), (2, 0, 1, 3)),
        "w_down": jnp.transpose(stack("wd").reshape(depth, N_CHIPS, dff // N_CHIPS, d), (1, 0, 2, 3)),
    }
    chip_idx = my_chip.astype(jnp.int32).reshape(1)
    g = {n: _reduce_scatter(per_chip[n], c_idx, chip_idx, n) for n in BIG}
    small_shapes = [(1, LANES)] + [(depth, CONV_WIDTH, cd) if n == "conv_w" else w[n].shape for n in SMALL]
    small = _small_exchange(_pack([loss_lanes] + [stack(n) for n in SMALL]), "allreduce_small", True)
    small = _unpack(small, small_shapes)
    loss = small[0][0, 0]
    for n, a in zip(SMALL, small[1:], strict=True):
        g[n] = a
    g["conv_w"] = lax.dynamic_slice_in_dim(g["conv_w"], my_chip * cs, cs, axis=2)

    delta, new_m, new_v = {}, {}, {}
    for n in BIG:
        shp = w[n].shape
        two_d = (shp[0] * shp[1], shp[2])
        dl, nm, nv = _adamw(w[n].reshape(two_d), g[n].reshape(two_d), mom[n].reshape(two_d),
                            var[n].reshape(two_d), f"adamw_{n}")
        delta[n], new_m[n], new_v[n] = dl.reshape(shp), nm.reshape(shp), nv.reshape(shp)
    shapes = [w[n].shape for n in SMALL]
    dl, nm, nv = _adamw(_pack([w[n] for n in SMALL]), _pack([g[n] for n in SMALL]),
                        _pack([mom[n] for n in SMALL]), _pack([var[n] for n in SMALL]), "adamw_small")
    for n, a, b, c in zip(SMALL, _unpack(dl, shapes), _unpack(nm, shapes), _unpack(nv, shapes), strict=True):
        delta[n], new_m[n], new_v[n] = a, b, c

    return (loss, grad_x, *[g[n] for n in ORDER], *[delta[n] for n in ORDER],
            *[new_m[n] for n in ORDER], *[new_v[n] for n in ORDER])
```

```python
import jax
import jax.numpy as jnp
from jax import lax
from jax.experimental import pallas as pl
from jax.experimental.pallas import tpu as pltpu

F32 = jnp.float32
BF16 = jnp.bfloat16
MESH = pl.DeviceIdType.MESH
ANY = pl.BlockSpec(memory_space=pl.ANY)
VMEM = pl.BlockSpec(memory_space=pltpu.VMEM)

EPS = 1e-6
ATT_HEAD_DIM = 128
SSM_HEAD_DIM = 64
SSM_GROUPS = 2
SSM_STATE = 128
SSD_CHUNK = 128
CONV_WIDTH = 4
LANES = 128
SUBLANES = 8
ATT_TILE = 256
N_CHIPS = 4
N_DEV = 8

ADAM_LR = 0.001
ADAM_B1 = 0.9
ADAM_B2 = 0.999
ADAM_EPS = 1e-08
ADAM_WD = 0.01
ADAM_STEP = 10

VMEM_LIMIT = 48 * 1024 * 1024

NN = (((1,), (0,)), ((), ()))
NT = (((1,), (1,)), ((), ()))
TN = (((0,), (0,)), ((), ()))


def _dot(a, b, dims=NN):
    return lax.dot_general(a.astype(BF16), b.astype(BF16), dims, preferred_element_type=F32)


def _dot_exact(x, ones, dims=NN, passes=3, ones_left=False):
    acc = None
    rem = x
    for _ in range(passes):
        piece = rem.astype(BF16)
        rem = rem - piece.astype(F32)
        p = (lax.dot_general(ones, piece, dims, preferred_element_type=F32) if ones_left
             else lax.dot_general(piece, ones, dims, preferred_element_type=F32))
        acc = p if acc is None else acc + p
    return acc


def _iota2(shape, axis):
    return lax.broadcasted_iota(jnp.int32, shape, axis)


def _tri(n, cmp):
    return cmp(_iota2((n, n), 0), _iota2((n, n), 1)).astype(BF16)


def _sum_all(v):
    return jnp.sum(jnp.sum(v, axis=1, keepdims=True), axis=0, keepdims=True)


def _fit(tile, dim, unit=LANES):
    if dim <= tile:
        return dim
    return max(k for k in range(unit, tile + 1, unit) if dim % k == 0)


def _params(sem):
    return pltpu.CompilerParams(dimension_semantics=sem, vmem_limit_bytes=VMEM_LIMIT)


def _call(body, *, name, grid, in_specs, out_specs, out_shape, sem, args, scratch_shapes=(), rider=None):
    in_specs, out_specs, out_shape = list(in_specs), list(out_specs), list(out_shape)
    scratch_shapes = list(scratch_shapes)
    if rider is None:
        res = pl.pallas_call(body, name=name, grid=grid, in_specs=in_specs, out_specs=out_specs,
                             out_shape=out_shape, scratch_shapes=scratch_shapes,
                             compiler_params=_params(sem))(*args)
        return list(res), []
    n_in, n_out, n_scr = len(in_specs), len(out_specs), len(scratch_shapes)
    r_in, r_out, n_sems = len(rider["ins"]), len(rider["outs"]), rider["n_sems"]

    def hosted(*refs):
        ins, rest = refs[:n_in], refs[n_in:]
        rins, rest = rest[:r_in], rest[r_in:]
        outs, rest = rest[:n_out], rest[n_out:]
        routs, rest = rest[:r_out], rest[r_out:]
        scr, (send_sems, recv_sems) = rest[:n_scr], rest[n_scr:]
        first, last = None, None
        for d, size in enumerate(grid):
            f, e = pl.program_id(d) == 0, pl.program_id(d) == size - 1
            first = f if first is None else jnp.logical_and(first, f)
            last = e if last is None else jnp.logical_and(last, e)

        @pl.when(first)
        def _():
            for cp in rider["copies"](rins, routs, send_sems, recv_sems):
                cp.start()

        body(*ins, *outs, *scr)

        @pl.when(last)
        def _():
            for cp in rider["copies"](rins, routs, send_sems, recv_sems):
                cp.wait()

    res = pl.pallas_call(
        hosted, name=name, grid=grid, in_specs=in_specs + [ANY] * r_in, out_specs=out_specs + [ANY] * r_out,
        out_shape=out_shape + list(rider["outs"]),
        scratch_shapes=scratch_shapes + [pltpu.SemaphoreType.DMA((n_sems,)), pltpu.SemaphoreType.DMA((n_sems,))],
        input_output_aliases={n_in + i: n_out + o for i, o in rider["aliases"].items()},
        compiler_params=_params(("arbitrary",) * len(grid)),
    )(*args, *rider["ins"])
    return list(res[:n_out]), list(res[n_out:])


def _softplus(x):
    return jnp.maximum(x, 0.0) + jnp.log(1.0 + jnp.exp(-jnp.abs(x)))


def _sigmoid(x):
    return 1.0 / (1.0 + jnp.exp(-x))


def _rms_fwd(x, g):
    r = lax.rsqrt(jnp.mean(x * x, axis=-1, keepdims=True) + EPS)
    return (x * r) * g


def _rms_bwd(x, g, dh):
    r = lax.rsqrt(jnp.mean(x * x, axis=-1, keepdims=True) + EPS)
    y = x * r
    dy = dh * g
    dx = r * (dy - y * jnp.mean(dy * y, axis=-1, keepdims=True))
    return dx, dh * y


def _matmul(groups, extras, epilogue, out_dtypes, m, n, tm, tn, name, rider=None):
    tm, tn = _fit(tm, m), _fit(tn, n)
    flat = [t for grp in groups for t in grp]
    n_terms, n_extra = len(flat), len(extras)

    def body(*refs):
        outs = refs[2 * n_terms + n_extra:]
        accs, pos = [], 0
        for grp in groups:
            acc = None
            for (_, _, mode) in grp:
                dims = {"nn": NN, "nt": NT, "tn": TN}[mode]
                p = _dot(refs[2 * pos][...], refs[2 * pos + 1][...], dims)
                acc = p if acc is None else acc + p
                pos += 1
            accs.append(acc)
        ex = [refs[2 * n_terms + i][...] for i in range(n_extra)]
        res = epilogue(accs, ex)
        for o_ref, r in zip(outs, res, strict=True):
            o_ref[...] = r.astype(o_ref.dtype)

    in_specs, args = [], []
    for (a, b, mode) in flat:
        if mode == "nn":
            k = a.shape[1]
            in_specs += [pl.BlockSpec((tm, k), lambda i, j: (i, 0)), pl.BlockSpec((k, tn), lambda i, j: (0, j))]
        elif mode == "nt":
            k = a.shape[1]
            in_specs += [pl.BlockSpec((tm, k), lambda i, j: (i, 0)), pl.BlockSpec((tn, k), lambda i, j: (j, 0))]
        else:
            k = a.shape[0]
            in_specs += [pl.BlockSpec((k, tm), lambda i, j: (0, i)), pl.BlockSpec((k, tn), lambda i, j: (0, j))]
        args += [a, b]
    for e in extras:
        in_specs.append(pl.BlockSpec((tm, tn), lambda i, j: (i, j)))
        args.append(e)
    outs, routs = _call(
        body, name=name, grid=(m // tm, n // tn), in_specs=in_specs,
        out_specs=[pl.BlockSpec((tm, tn), lambda i, j: (i, j)) for _ in out_dtypes],
        out_shape=[jax.ShapeDtypeStruct((m, n), d) for d in out_dtypes],
        sem=("parallel", "parallel"), args=args, rider=rider)
    return outs if rider is None else (outs, routs)


def _mm(a, b, mode, m, n, out_dtype, name, tm=512, tn=512, res=None):
    extras = [] if res is None else [res]
    epi = (lambda accs, ex: (accs[0],)) if res is None else (lambda accs, ex: (accs[0] + ex[0],))
    return _matmul([[(a, b, mode)]], extras, epi, [out_dtype], m, n, tm, tn, name)[0]


def _swiglu_fwd_epilogue(accs, ex):
    g, u = accs
    return g, u, (g * _sigmoid(g)) * u


def _swiglu_bwd_epilogue(accs, ex):
    dact, (g, u) = accs[0], ex
    sg = _sigmoid(g)
    silu = g * sg
    return dact * u * (sg * (1.0 + g * (1.0 - sg))), dact * silu


def _rmsnorm_fwd(x, g, name, tr=512):
    t, d = x.shape
    tr = min(tr, t)

    def body(x_ref, g_ref, h_ref):
        h_ref[...] = _rms_fwd(x_ref[...], g_ref[...]).astype(BF16)

    return pl.pallas_call(
        body, name=name, grid=(t // tr,),
        in_specs=[pl.BlockSpec((tr, d), lambda i: (i, 0)), pl.BlockSpec((1, d), lambda i: (0, 0))],
        out_specs=pl.BlockSpec((tr, d), lambda i: (i, 0)),
        out_shape=jax.ShapeDtypeStruct((t, d), BF16),
        compiler_params=_params(("parallel",)),
    )(x, g.reshape(1, d))


def _rmsnorm_bwd(x, g, dh, dres, name, tr=256):
    t, d = x.shape
    tr = min(tr, t)

    def body(x_ref, g_ref, dh_ref, dres_ref, dx_ref, dxb_ref, dg_ref):
        dx, dgr = _rms_bwd(x_ref[...], g_ref[...], dh_ref[...])
        dx = dx + dres_ref[...]
        dx_ref[...] = dx
        dxb_ref[...] = dx.astype(BF16)

        @pl.when(pl.program_id(0) == 0)
        def _():
            dg_ref[...] = jnp.zeros_like(dg_ref)

        dg_ref[...] += jnp.sum(dgr, axis=0, keepdims=True)

    row = pl.BlockSpec((tr, d), lambda i: (i, 0))
    vec = pl.BlockSpec((1, d), lambda i: (0, 0))
    dx, dxb, dg = pl.pallas_call(
        body, name=name, grid=(t // tr,),
        in_specs=[row, vec, row, row], out_specs=[row, row, vec],
        out_shape=[jax.ShapeDtypeStruct((t, d), F32), jax.ShapeDtypeStruct((t, d), BF16),
                   jax.ShapeDtypeStruct((1, d), F32)],
        compiler_params=_params(("arbitrary",)),
    )(x, g.reshape(1, d), dh, dres)
    return dx, dxb, dg.reshape(d)


def _merge_fwd(o_att, y, z, ga, gs, name, tr=256):
    t, wa = o_att.shape
    ws = y.shape[1]
    wg = ws // SSM_GROUPS
    tr = min(tr, t)

    def body(o_ref, y_ref, z_ref, ga_ref, gs_ref, m_ref):
        m_ref[:, 0:wa] = _rms_fwd(o_ref[...], ga_ref[...]).astype(BF16)
        for g in range(SSM_GROUPS):
            sl = slice(g * wg, (g + 1) * wg)
            zz = z_ref[:, sl]
            yz = y_ref[:, sl] * (zz * _sigmoid(zz))
            m_ref[:, wa + g * wg:wa + (g + 1) * wg] = _rms_fwd(yz, gs_ref[:, sl]).astype(BF16)

    return pl.pallas_call(
        body, name=name, grid=(t // tr,),
        in_specs=[pl.BlockSpec((tr, wa), lambda i: (i, 0)), pl.BlockSpec((tr, ws), lambda i: (i, 0)),
                  pl.BlockSpec((tr, ws), lambda i: (i, 0)), pl.BlockSpec((1, wa), lambda i: (0, 0)),
                  pl.BlockSpec((1, ws), lambda i: (0, 0))],
        out_specs=pl.BlockSpec((tr, wa + ws), lambda i: (i, 0)),
        out_shape=jax.ShapeDtypeStruct((t, wa + ws), BF16),
        compiler_params=_params(("parallel",)),
    )(o_att, y, z, ga.reshape(1, wa), gs.reshape(1, ws))


def _merge_bwd(o_att, y, z, ga, gs, dmix, name, tr=256):
    t, wa = o_att.shape
    ws = y.shape[1]
    wg = ws // SSM_GROUPS
    tr = min(tr, t)

    def body(o_ref, y_ref, z_ref, ga_ref, gs_ref, dm_ref, do_ref, dy_ref, dz_ref, dga_ref, dgs_ref):
        @pl.when(pl.program_id(0) == 0)
        def _():
            dga_ref[...] = jnp.zeros_like(dga_ref)
            dgs_ref[...] = jnp.zeros_like(dgs_ref)

        do, dgr = _rms_bwd(o_ref[...], ga_ref[...], dm_ref[:, 0:wa])
        do_ref[...] = do
        dga_ref[...] += jnp.sum(dgr, axis=0, keepdims=True)
        for g in range(SSM_GROUPS):
            sl = slice(g * wg, (g + 1) * wg)
            zz, yy = z_ref[:, sl], y_ref[:, sl]
            sg = _sigmoid(zz)
            silu = zz * sg
            dyz, dgr = _rms_bwd(yy * silu, gs_ref[:, sl], dm_ref[:, wa + g * wg:wa + (g + 1) * wg])
            dy_ref[:, sl] = dyz * silu
            dz_ref[:, sl] = (dyz * yy * (sg + silu * (1.0 - sg))).astype(BF16)
            dgs_ref[:, sl] += jnp.sum(dgr, axis=0, keepdims=True)

    rowa = pl.BlockSpec((tr, wa), lambda i: (i, 0))
    rows = pl.BlockSpec((tr, ws), lambda i: (i, 0))
    veca = pl.BlockSpec((1, wa), lambda i: (0, 0))
    vecs = pl.BlockSpec((1, ws), lambda i: (0, 0))
    do, dy, dz, dga, dgs = pl.pallas_call(
        body, name=name, grid=(t // tr,),
        in_specs=[rowa, rows, rows, veca, vecs, pl.BlockSpec((tr, wa + ws), lambda i: (i, 0))],
        out_specs=[rowa, rows, rows, veca, vecs],
        out_shape=[jax.ShapeDtypeStruct((t, wa), F32), jax.ShapeDtypeStruct((t, ws), F32),
                   jax.ShapeDtypeStruct((t, ws), BF16), jax.ShapeDtypeStruct((1, wa), F32),
                   jax.ShapeDtypeStruct((1, ws), F32)],
        compiler_params=_params(("arbitrary",)),
    )(o_att, y, z, ga.reshape(1, wa), gs.reshape(1, ws), dmix)
    return do, dy, dz, dga.reshape(wa), dgs.reshape(ws)


def _loss_head(y, target, name, tr=256):
    t, d = y.shape
    tr = min(tr, t)

    def body(y_ref, t_ref, dy_ref, dyb_ref, l_ref):
        @pl.when(pl.program_id(0) == 0)
        def _():
            l_ref[...] = jnp.zeros_like(l_ref)

        diff = y_ref[...] - t_ref[...]
        dy = diff * (1.0 / d)
        dy_ref[...] = dy
        dyb_ref[...] = dy.astype(BF16)
        part = jnp.sum(diff * diff, axis=0, keepdims=True)
        fold = part[:, 0:LANES]
        for k in range(1, d // LANES):
            fold = fold + part[:, k * LANES:(k + 1) * LANES]
        l_ref[...] += fold * (0.5 / d)

    row = pl.BlockSpec((tr, d), lambda i: (i, 0))
    return pl.pallas_call(
        body, name=name, grid=(t // tr,), in_specs=[row, row],
        out_specs=[row, row, pl.BlockSpec((1, LANES), lambda i: (0, 0))],
        out_shape=[jax.ShapeDtypeStruct((t, d), F32), jax.ShapeDtypeStruct((t, d), BF16),
                   jax.ShapeDtypeStruct((1, LANES), F32)],
        compiler_params=_params(("arbitrary",)),
    )(y, target)


def _adamw(w, g, m, v, name, tr=256):
    r, c = w.shape
    tr = _fit(tr, r, 16)

    def body(w_ref, g_ref, m_ref, v_ref, d_ref, nm_ref, nv_ref):
        gg = g_ref[...]
        nm = ADAM_B1 * m_ref[...] + (1.0 - ADAM_B1) * gg
        nv = ADAM_B2 * v_ref[...] + (1.0 - ADAM_B2) * (gg * gg)
        m_hat = nm / (1.0 - ADAM_B1 ** ADAM_STEP)
        v_hat = nv / (1.0 - ADAM_B2 ** ADAM_STEP)
        d_ref[...] = -ADAM_LR * (m_hat / (jnp.sqrt(v_hat) + ADAM_EPS) + ADAM_WD * w_ref[...])
        nm_ref[...] = nm
        nv_ref[...] = nv

    blk = pl.BlockSpec((tr, c), lambda i: (i, 0))
    return pl.pallas_call(
        body, name=name, grid=(r // tr,), in_specs=[blk] * 4, out_specs=[blk] * 3,
        out_shape=[jax.ShapeDtypeStruct((r, c), F32)] * 3,
        compiler_params=_params(("parallel",)),
    )(w, g, m, v)


def _att_scores(qi, kj, scale, diagonal):
    z = _dot(qi, kj, NT) * scale
    lb = -_softplus(-z)
    lrm = lb - z
    if not diagonal:
        return lb, lrm, None
    mask = _iota2(z.shape, 1) < _iota2(z.shape, 0)
    return lb, jnp.where(mask, lrm, 0.0), mask


def _masked(mask, v):
    return v if mask is None else jnp.where(mask, v, 0.0)


def _attention_fwd(qkv, qg, kg, name, tile=None, rider=None):
    bsz, s, w3 = qkv.shape
    hd = ATT_HEAD_DIM
    heads = w3 // (3 * hd)
    tile = min(tile or ATT_TILE, s)
    nb = s // tile
    scale = hd ** -0.5

    def body(qkv_ref, qg_ref, kg_ref, o_ref, r_ref, qn_s, kn_s, vb_s, acc_s, c_s):
        qn_s[...] = _rms_fwd(qkv_ref[0, :, 0:hd], qg_ref[...]).astype(BF16)
        kn_s[...] = _rms_fwd(qkv_ref[0, :, hd:2 * hd], kg_ref[...]).astype(BF16)
        vb_s[...] = qkv_ref[0, :, 2 * hd:3 * hd].astype(BF16)
        after = _tri(tile, lambda r, c: r > c)

        def q_loop(i, _):
            rows = pl.ds(pl.multiple_of(i * tile, tile), tile)
            qi = qn_s[rows, :]
            acc_s[...] = jnp.zeros_like(acc_s)
            c_s[...] = jnp.zeros_like(c_s)

            def key_tile(j, diagonal):
                cols = pl.ds(pl.multiple_of(j * tile, tile), tile)
                lb, lrm, mask = _att_scores(qi, kn_s[cols, :], scale, diagonal)
                later = _dot_exact(lrm, after, passes=2) + c_s[...]
                w = _masked(mask, jnp.exp(lb + later))
                acc_s[...] += _dot(w, vb_s[cols, :])
                c_s[...] += jnp.sum(lrm, axis=1, keepdims=True)

            def k_loop(jj, _):
                key_tile(i - jj, False)
                return 0

            key_tile(i, True)
            lax.fori_loop(1, i + 1, k_loop, 0)
            o_ref[0, rows, :] = acc_s[...]
            r_ref[0, 0, rows, :] = c_s[...]
            return 0

        lax.fori_loop(0, nb, q_loop, 0)

    return _call(
        body, name=name, grid=(bsz, heads),
        in_specs=[pl.BlockSpec((1, s, 3 * hd), lambda b, h: (b, 0, h)),
                  pl.BlockSpec((1, hd), lambda b, h: (0, 0)), pl.BlockSpec((1, hd), lambda b, h: (0, 0))],
        out_specs=[pl.BlockSpec((1, s, hd), lambda b, h: (b, 0, h)),
                   pl.BlockSpec((1, 1, s, 1), lambda b, h: (b, h, 0, 0))],
        out_shape=[jax.ShapeDtypeStruct((bsz, s, heads * hd), F32),
                   jax.ShapeDtypeStruct((bsz, heads, s, 1), F32)],
        scratch_shapes=[pltpu.VMEM((s, hd), BF16), pltpu.VMEM((s, hd), BF16), pltpu.VMEM((s, hd), BF16),
                        pltpu.VMEM((tile, hd), F32), pltpu.VMEM((tile, 1), F32)],
        sem=("parallel", "parallel"), args=(qkv, qg.reshape(1, hd), kg.reshape(1, hd)), rider=rider)


def _attention_bwd(qkv, qg, kg, rtot, do, name, tile=None, rider=None):
    bsz, s, w3 = qkv.shape
    hd = ATT_HEAD_DIM
    heads = w3 // (3 * hd)
    tile = min(tile or ATT_TILE, s)
    nb = s // tile
    scale = hd ** -0.5

    def body(qkv_ref, qg_ref, kg_ref, r_ref, do_ref, dqkv_ref, dqg_ref, dkg_ref,
             qn_s, kn_s, vb_s, dob_s, dqn_s, dkn_s, dv_s, c1_s, c2_s):
        qn_s[...] = _rms_fwd(qkv_ref[0, :, 0:hd], qg_ref[...]).astype(BF16)
        kn_s[...] = _rms_fwd(qkv_ref[0, :, hd:2 * hd], kg_ref[...]).astype(BF16)
        vb_s[...] = qkv_ref[0, :, 2 * hd:3 * hd].astype(BF16)
        dob_s[...] = do_ref[0].astype(BF16)
        dqn_s[...] = jnp.zeros_like(dqn_s)
        dkn_s[...] = jnp.zeros_like(dkn_s)
        dv_s[...] = jnp.zeros_like(dv_s)
        upto = _tri(tile, lambda r, c: r <= c)
        before = _tri(tile, lambda r, c: r < c)

        def q_loop(i, _):
            rows = pl.ds(pl.multiple_of(i * tile, tile), tile)
            qi, doi = qn_s[rows, :], dob_s[rows, :]
            rt = r_ref[0, 0, rows, :]
            c1_s[...] = jnp.zeros_like(c1_s)
            c2_s[...] = jnp.zeros_like(c2_s)

            def key_tile(j, diagonal):
                cols = pl.ds(pl.multiple_of(j * tile, tile), tile)
                kj, vj = kn_s[cols, :], vb_s[cols, :]
                lb, lrm, mask = _att_scores(qi, kj, scale, diagonal)
                later = rt - (_dot_exact(lrm, upto, passes=2) + c1_s[...])
                w = _masked(mask, jnp.exp(lb + later))
                e = w * _dot(doi, vj, NT)
                pre = _dot_exact(e, before, passes=2) + c2_s[...]
                dz = _masked(mask, (e - jnp.exp(lb) * (e + pre)) * scale)
                dv_s[cols, :] += _dot(w, doi, TN)
                dkn_s[cols, :] += _dot(dz, qi, TN)
                dqn_s[rows, :] += _dot(dz, kj)
                c1_s[...] += jnp.sum(lrm, axis=1, keepdims=True)
                c2_s[...] += jnp.sum(e, axis=1, keepdims=True)

            def k_loop(j, _):
                key_tile(j, False)
                return 0

            lax.fori_loop(0, i, k_loop, 0)
            key_tile(i, True)
            return 0

        lax.fori_loop(0, nb, q_loop, 0)
        dq, dgq = _rms_bwd(qkv_ref[0, :, 0:hd], qg_ref[...], dqn_s[...])
        dk, dgk = _rms_bwd(qkv_ref[0, :, hd:2 * hd], kg_ref[...], dkn_s[...])
        dqkv_ref[0, :, 0:hd] = dq.astype(BF16)
        dqkv_ref[0, :, hd:2 * hd] = dk.astype(BF16)
        dqkv_ref[0, :, 2 * hd:3 * hd] = dv_s[...].astype(BF16)
        dqg_ref[0, 0] = jnp.sum(dgq, axis=0, keepdims=True)
        dkg_ref[0, 0] = jnp.sum(dgk, axis=0, keepdims=True)

    gain = pl.BlockSpec((1, hd), lambda b, h: (0, 0))
    dgain = pl.BlockSpec((1, 1, 1, hd), lambda b, h: (b, h, 0, 0))
    return _call(
        body, name=name, grid=(bsz, heads),
        in_specs=[pl.BlockSpec((1, s, 3 * hd), lambda b, h: (b, 0, h)), gain, gain,
                  pl.BlockSpec((1, 1, s, 1), lambda b, h: (b, h, 0, 0)),
                  pl.BlockSpec((1, s, hd), lambda b, h: (b, 0, h))],
        out_specs=[pl.BlockSpec((1, s, 3 * hd), lambda b, h: (b, 0, h)), dgain, dgain],
        out_shape=[jax.ShapeDtypeStruct((bsz, s, w3), BF16),
                   jax.ShapeDtypeStruct((bsz, heads, 1, hd), F32),
                   jax.ShapeDtypeStruct((bsz, heads, 1, hd), F32)],
        scratch_shapes=[pltpu.VMEM((s, hd), BF16)] * 4 + [pltpu.VMEM((s, hd), F32)] * 3
        + [pltpu.VMEM((tile, 1), F32)] * 2,
        sem=("parallel", "parallel"), args=(qkv, qg.reshape(1, hd), kg.reshape(1, hd), rtot, do), rider=rider)


def _conv_pre(pad_ref, w_ref, b_ref, s):
    pre = b_ref[...]
    for i in range(CONV_WIDTH):
        off = SUBLANES - (CONV_WIDTH - 1) + i
        pre = pre + pad_ref[off:off + s, :] * w_ref[i:i + 1, :]
    return pre


def _conv_fwd(u, w, b, name, tc=256):
    bsz, s, c = u.shape
    tc = min(tc, c)

    def body(u_ref, w_ref, b_ref, a_ref, pad_s):
        pad_s[0:SUBLANES, :] = jnp.zeros((SUBLANES, tc), F32)
        pad_s[SUBLANES:SUBLANES + s, :] = u_ref[0]
        pre = _conv_pre(pad_s, w_ref, b_ref, s)
        a_ref[0] = pre * _sigmoid(pre)

    return pl.pallas_call(
        body, name=name, grid=(bsz, c // tc),
        in_specs=[pl.BlockSpec((1, s, tc), lambda i, j: (i, 0, j)),
                  pl.BlockSpec((CONV_WIDTH, tc), lambda i, j: (0, j)), pl.BlockSpec((1, tc), lambda i, j: (0, j))],
        out_specs=pl.BlockSpec((1, s, tc), lambda i, j: (i, 0, j)),
        out_shape=jax.ShapeDtypeStruct((bsz, s, c), F32),
        scratch_shapes=[pltpu.VMEM((s + SUBLANES, tc), F32)],
        compiler_params=_params(("parallel", "parallel")),
    )(u, w, b.reshape(1, c))


def _conv_bwd(u, w, b, da, name, tc=256):
    bsz, s, c = u.shape
    tc = min(tc, c)

    def body(u_ref, w_ref, b_ref, da_ref, du_ref, dw_ref, db_ref, pad_s, gpad_s):
        @pl.when(pl.program_id(1) == 0)
        def _():
            dw_ref[...] = jnp.zeros_like(dw_ref)
            db_ref[...] = jnp.zeros_like(db_ref)

        pad_s[0:SUBLANES, :] = jnp.zeros((SUBLANES, tc), F32)
        pad_s[SUBLANES:SUBLANES + s, :] = u_ref[0]
        pre = _conv_pre(pad_s, w_ref, b_ref, s)
        sg = _sigmoid(pre)
        dpre = da_ref[0] * (sg * (1.0 + pre * (1.0 - sg)))
        gpad_s[0:s, :] = dpre
        gpad_s[s:s + SUBLANES, :] = jnp.zeros((SUBLANES, tc), F32)
        du = jnp.zeros((s, tc), F32)
        for i in range(CONV_WIDTH):
            back = CONV_WIDTH - 1 - i
            du = du + gpad_s[back:back + s, :] * w_ref[i:i + 1, :]
            off = SUBLANES - (CONV_WIDTH - 1) + i
            dw_ref[i:i + 1, :] += jnp.sum(dpre * pad_s[off:off + s, :], axis=0, keepdims=True)
        du_ref[0] = du.astype(BF16)
        db_ref[...] += jnp.sum(dpre, axis=0, keepdims=True)

    blk = pl.BlockSpec((1, s, tc), lambda j, i: (i, 0, j))
    du, dw, db = pl.pallas_call(
        body, name=name, grid=(c // tc, bsz),
        in_specs=[blk, pl.BlockSpec((CONV_WIDTH, tc), lambda j, i: (0, j)),
                  pl.BlockSpec((1, tc), lambda j, i: (0, j)), blk],
        out_specs=[blk, pl.BlockSpec((CONV_WIDTH, tc), lambda j, i: (0, j)),
                   pl.BlockSpec((1, tc), lambda j, i: (0, j))],
        out_shape=[jax.ShapeDtypeStruct((bsz, s, c), BF16), jax.ShapeDtypeStruct((CONV_WIDTH, c), F32),
                   jax.ShapeDtypeStruct((1, c), F32)],
        scratch_shapes=[pltpu.VMEM((s + SUBLANES, tc), F32), pltpu.VMEM((s + SUBLANES, tc), F32)],
        compiler_params=_params(("parallel", "arbitrary")),
    )(u, w, b.reshape(1, c), da)
    return du, dw, db.reshape(c)


def _ssd_chunk_common(b_ref, c_ref, dt_ref, dtb_ref, alog_ref):
    bm, cm = b_ref[0], c_ref[0]
    draw = dt_ref[0] + dtb_ref[...]
    dt = _softplus(draw)
    a_row = -jnp.exp(alog_ref[...])
    da = dt * a_row
    n = SSD_CHUNK
    acum = _dot_exact(da, _tri(n, lambda r, c: r >= c), ones_left=True)
    acum_t = _dot_exact(da, _tri(n, lambda r, c: r <= c), dims=TN)
    cb = _dot(cm, bm, NT)
    return bm, cm, draw, dt, a_row, acum, acum_t, cb


def _ssd_head_common(acum, acum_t, dt, cb, x, i):
    n, p = SSD_CHUNK, SSM_HEAD_DIM
    acol = acum[:, i:i + 1]
    arow = acum_t[i:i + 1, :]
    causal = _iota2((n, n), 0) >= _iota2((n, n), 1)
    lm = jnp.where(causal, jnp.exp(jnp.where(causal, acol - arow, 0.0)), 0.0)
    gm = cb * lm
    dtc = dt[:, i:i + 1]
    xh = x[:, i * p:(i + 1) * p]
    xdt = xh * dtc
    alast = acum[n - 1:n, i:i + 1]
    dte = jnp.exp(alast - acol)
    return acol, lm, gm, dtc, xh, xdt, alast, dte


def _ssd_specs(s, wg, hg, rev):
    g, n, cl = SSM_GROUPS, SSM_STATE, SSD_CHUNK
    nc = s // cl
    boff, coff = (g * wg) // n, (g * wg) // n + g
    ci = (lambda c: nc - 1 - c) if rev else (lambda c: c)
    xblk = pl.BlockSpec((1, cl, wg), lambda b, k, c: (b, ci(c), k))
    bblk = pl.BlockSpec((1, cl, n), lambda b, k, c: (b, ci(c), boff + k))
    cblk = pl.BlockSpec((1, cl, n), lambda b, k, c: (b, ci(c), coff + k))
    nblk = pl.BlockSpec((1, cl, n), lambda b, k, c: (b, ci(c), k))
    dtblk = pl.BlockSpec((1, cl, LANES), lambda b, k, c: (b, ci(c), k))
    vec = pl.BlockSpec((1, LANES), lambda b, k, c: (0, k))
    hsblk = pl.BlockSpec((1, 1, 1, wg, n), lambda b, k, c: (b, k, ci(c), 0, 0))
    return nc, xblk, bblk, cblk, nblk, dtblk, vec, hsblk


def _ssd_fwd(xbc, dtraw, dtb, alog, dskip, hg, name, rider=None):
    bsz, s, _ = xbc.shape
    g, n, p = SSM_GROUPS, SSM_STATE, SSM_HEAD_DIM
    wg = hg * p
    nc, xblk, bblk, cblk, _, dtblk, vec, hsblk = _ssd_specs(s, wg, hg, False)

    def body(x_ref, b_ref, c_ref, dt_ref, dtb_ref, alog_ref, dsk_ref, y_ref, hs_ref, h_s):
        @pl.when(pl.program_id(2) == 0)
        def _():
            h_s[...] = jnp.zeros_like(h_s)

        bm, cm, _, dt, _, acum, acum_t, cb = _ssd_chunk_common(b_ref, c_ref, dt_ref, dtb_ref, alog_ref)
        x = x_ref[0]
        hs_ref[0, 0, 0] = h_s[...]
        for i in range(hg):
            acol, _, gm, _, xh, xdt, alast, dte = _ssd_head_common(acum, acum_t, dt, cb, x, i)
            hprev = h_s[i * p:(i + 1) * p, :]
            y = _dot(gm, xdt) + _dot(cm, hprev, NT) * jnp.exp(acol) + xh * dsk_ref[:, i:i + 1]
            y_ref[0, :, i * p:(i + 1) * p] = y
            h_s[i * p:(i + 1) * p, :] = hprev * jnp.exp(alast) + _dot(xdt * dte, bm, TN)

    return _call(
        body, name=name, grid=(bsz, g, nc),
        in_specs=[xblk, bblk, cblk, dtblk, vec, vec, vec],
        out_specs=[xblk, hsblk],
        out_shape=[jax.ShapeDtypeStruct((bsz, s, g * wg), F32),
                   jax.ShapeDtypeStruct((bsz, g, nc, wg, n), F32)],
        scratch_shapes=[pltpu.VMEM((wg, n), F32)],
        sem=("parallel", "parallel", "arbitrary"), args=(xbc, xbc, xbc, dtraw, dtb, alog, dskip), rider=rider)


def _ssd_bwd(xbc, dtraw, dtb, alog, dskip, hs, dy, hg, name, rider=None):
    bsz, s, _ = xbc.shape
    g, n, p, cl = SSM_GROUPS, SSM_STATE, SSM_HEAD_DIM, SSD_CHUNK
    wg = hg * p
    nc, xblk, bblk, cblk, nblk, dtblk, vec, hsblk = _ssd_specs(s, wg, hg, True)

    def body(x_ref, b_ref, c_ref, dt_ref, dtb_ref, alog_ref, dsk_ref, hs_ref, dy_ref,
             dx_ref, db_ref, dc_ref, ddt_ref, dvec_ref, dh_s):
        @pl.when(pl.program_id(2) == 0)
        def _():
            dh_s[...] = jnp.zeros_like(dh_s)
            dvec_ref[...] = jnp.zeros_like(dvec_ref)

        lane = _iota2((cl, LANES), 1)
        sub = _iota2((LANES, cl), 0)
        lane1 = _iota2((1, LANES), 1)
        last_row = _iota2((cl, 1), 0) == cl - 1
        bm, cm, draw, dt, a_row, acum, acum_t, cb = _ssd_chunk_common(b_ref, c_ref, dt_ref, dtb_ref, alog_ref)
        x = x_ref[0]
        dyc = dy_ref[0]
        dcb = jnp.zeros((cl, cl), F32)
        dcm = jnp.zeros((cl, n), F32)
        dbm = jnp.zeros((cl, n), F32)
        da_col = jnp.zeros((cl, LANES), F32)
        da_row = jnp.zeros((LANES, cl), F32)
        ddt = jnp.zeros((cl, LANES), F32)
        dd = jnp.zeros((1, LANES), F32)
        for i in range(hg):
            acol, lm, gm, dtc, xh, xdt, alast, dte = _ssd_head_common(acum, acum_t, dt, cb, x, i)
            dyh = dyc[:, i * p:(i + 1) * p]
            hprev = hs_ref[0, 0, 0, i * p:(i + 1) * p, :]
            dhn = dh_s[i * p:(i + 1) * p, :]
            ea, cd = jnp.exp(acol), jnp.exp(alast)
            dd = dd + jnp.where(lane1 == i, _sum_all(dyh * xh), 0.0)
            y0 = _dot(cm, hprev, NT)
            dy0 = dyh * ea
            dacol = jnp.sum(dyh * y0, axis=1, keepdims=True) * ea
            dcm = dcm + _dot(dy0, hprev)
            dh_s[i * p:(i + 1) * p, :] = _dot(dy0, cm, TN) + dhn * cd
            d_alast = _sum_all(dhn * hprev) * cd
            dxe = _dot(bm, dhn, NT)
            dbm = dbm + _dot(xdt * dte, dhn)
            dxdt = dxe * dte
            t1 = jnp.sum(dxe * xdt, axis=1, keepdims=True) * dte
            dacol = dacol - t1
            d_alast = d_alast + _sum_all(t1)
            dgm = _dot(dyh, xdt, NT)
            dxdt = dxdt + _dot(gm, dyh, TN)
            dcb = dcb + dgm * lm
            ws = dgm * gm
            dacol = dacol + jnp.sum(ws, axis=1, keepdims=True)
            darow = -jnp.sum(ws, axis=0, keepdims=True)
            dacol = dacol + jnp.where(last_row, d_alast, 0.0)
            dx_ref[0, :, i * p:(i + 1) * p] = dxdt * dtc + dyh * dsk_ref[:, i:i + 1]
            da_col = jnp.where(lane == i, dacol, da_col)
            da_row = jnp.where(sub == i, darow, da_row)
            ddt = jnp.where(lane == i, jnp.sum(dxdt * xh, axis=1, keepdims=True), ddt)
        dc_ref[0] = dcm + _dot(dcb, bm)
        db_ref[0] = dbm + _dot(dcb, cm, TN)
        upper = _tri(cl, lambda r, k: r <= k)
        dda = _dot_exact(da_col, upper, ones_left=True) + _dot_exact(da_row, upper, dims=NT, ones_left=True)
        ddt = ddt + dda * a_row
        ddraw = ddt * _sigmoid(draw)
        ddt_ref[0] = ddraw.astype(BF16)
        dvec_ref[0, 0, 0:1, :] += jnp.sum(ddraw, axis=0, keepdims=True)
        dvec_ref[0, 0, 1:2, :] += jnp.sum(dda * dt, axis=0, keepdims=True) * a_row
        dvec_ref[0, 0, 2:3, :] += dd

    return _call(
        body, name=name, grid=(bsz, g, nc),
        in_specs=[xblk, bblk, cblk, dtblk, vec, vec, vec, hsblk, xblk],
        out_specs=[xblk, nblk, nblk, dtblk,
                   pl.BlockSpec((1, 1, SUBLANES, LANES), lambda b, k, c: (b, k, 0, 0))],
        out_shape=[jax.ShapeDtypeStruct((bsz, s, g * wg), F32), jax.ShapeDtypeStruct((bsz, s, g * n), F32),
                   jax.ShapeDtypeStruct((bsz, s, g * n), F32), jax.ShapeDtypeStruct((bsz, s, g * LANES), BF16),
                   jax.ShapeDtypeStruct((bsz, g, SUBLANES, LANES), F32)],
        scratch_shapes=[pltpu.VMEM((wg, n), F32)],
        sem=("parallel", "parallel", "arbitrary"), args=(xbc, xbc, xbc, dtraw, dtb, alog, dskip, hs, dy),
        rider=rider)


def _coords():
    return lax.axis_index("x"), lax.axis_index("y"), lax.axis_index("c")


def _other_chips(x, y):
    return [(1 - x, y), (x, 1 - y), (1 - x, 1 - y)]


def _remote(src, dst, send_sems, recv_sems, k, to):
    return pltpu.make_async_remote_copy(src_ref=src, dst_ref=dst, send_sem=send_sems.at[k],
                                        recv_sem=recv_sems.at[k], device_id=to, device_id_type=MESH)


def _standalone(rider, name):
    r_in, r_out, n_sems = len(rider["ins"]), len(rider["outs"]), rider["n_sems"]

    def body(*refs):
        rins, routs, (send_sems, recv_sems) = refs[:r_in], refs[r_in:r_in + r_out], refs[r_in + r_out:]
        cps = rider["copies"](rins, routs, send_sems, recv_sems)
        for cp in cps:
            cp.start()
        for cp in cps:
            cp.wait()

    res = pl.pallas_call(
        body, name=name, in_specs=[ANY] * r_in, out_specs=[ANY] * r_out, out_shape=list(rider["outs"]),
        scratch_shapes=[pltpu.SemaphoreType.DMA((n_sems,)), pltpu.SemaphoreType.DMA((n_sems,))],
        input_output_aliases=dict(rider["aliases"]),
    )(*rider["ins"])
    return list(res)


def _gather_chips_rider(shards):
    def copies(rins, routs, send_sems, recv_sems):
        x, y, c = _coords()
        me = 2 * x + y
        cps = []
        for q, (w_ref, o_ref) in enumerate(zip(rins, routs, strict=True)):
            rh = w_ref.shape[0] // 2
            rows = pl.ds(c * rh, rh)
            for k, (px, py) in enumerate(_other_chips(x, y)):
                cps.append(_remote(w_ref.at[rows], o_ref.at[me, rows], send_sems, recv_sems, 3 * q + k, (px, py, c)))
        return cps

    return dict(ins=list(shards), outs=[jax.ShapeDtypeStruct((N_CHIPS,) + w.shape, w.dtype) for w in shards],
                aliases={}, n_sems=3 * len(shards), copies=copies)


def _gather_pair_rider(gathered):
    def copies(rins, routs, send_sems, recv_sems):
        x, y, c = _coords()
        cps = []
        for q, o_ref in enumerate(routs):
            rh = o_ref.shape[1] // 2
            for k, (px, py) in enumerate(_other_chips(x, y)):
                part = o_ref.at[2 * px + py, pl.ds(c * rh, rh)]
                cps.append(_remote(part, part, send_sems, recv_sems, 3 * q + k, (x, y, 1 - c)))
        return cps

    return dict(ins=list(gathered), outs=[jax.ShapeDtypeStruct(g.shape, g.dtype) for g in gathered],
                aliases={i: i for i in range(len(gathered))}, n_sems=3 * len(gathered), copies=copies)


def _reduce_pair_rider(grads):
    def copies(rins, routs, send_sems, recv_sems):
        x, y, c = _coords()
        cps = []
        for q, (g_ref, r_ref) in enumerate(zip(rins, routs, strict=True)):
            rh = g_ref.shape[1] // 2
            cps.append(_remote(g_ref.at[:, pl.ds((1 - c) * rh, rh)], r_ref, send_sems, recv_sems, q, (x, y, 1 - c)))
        return cps

    return dict(ins=list(grads),
                outs=[jax.ShapeDtypeStruct((g.shape[0], g.shape[1] // 2, g.shape[2]), g.dtype) for g in grads],
                aliases={}, n_sems=len(grads), copies=copies)


def _reduce_chips_rider(pair_sums):
    def copies(rins, routs, send_sems, recv_sems):
        x, y, c = _coords()
        cps = []
        for q, (p_ref, r_ref) in enumerate(zip(rins, routs, strict=True)):
            for k, (px, py) in enumerate(_other_chips(x, y)):
                cps.append(_remote(p_ref.at[2 * px + py], r_ref.at[k], send_sems, recv_sems, 3 * q + k, (px, py, c)))
        return cps

    return dict(ins=list(pair_sums), outs=[jax.ShapeDtypeStruct((3,) + p.shape[1:], p.dtype) for p in pair_sums],
                aliases={}, n_sems=3 * len(pair_sums), copies=copies)


def _reduce_finish_rider(sums, totals, layer, depth):
    fresh = totals is None

    def copies(rins, routs, send_sems, recv_sems):
        x, y, c = _coords()
        cps = []
        for q, (f_ref, o_ref) in enumerate(zip(rins[:len(sums)], routs, strict=True)):
            rh = f_ref.shape[0]
            cps.append(_remote(f_ref, o_ref.at[layer, pl.ds(c * rh, rh)], send_sems, recv_sems, q, (x, y, 1 - c)))
        return cps

    ins = list(sums) + ([] if fresh else list(totals))
    outs = [jax.ShapeDtypeStruct((depth, 2 * f.shape[0], f.shape[1]), F32) for f in sums]
    return dict(ins=ins, outs=outs, aliases={} if fresh else {len(sums) + i: i for i in range(len(sums))},
                n_sems=len(sums), copies=copies)


def _pair_add(gj, r1, c_idx, name, tr=256):
    nj, r, c = gj.shape
    rh = r // 2
    tr = _fit(tr, rh, 16)
    nt = rh // tr

    def body(c_ref, g_ref, r_ref, p_ref, pb_ref):
        s = g_ref[...] + r_ref[...]
        p_ref[...] = s
        pb_ref[...] = s.astype(BF16)

    blk_r = pl.BlockSpec((1, tr, c), lambda j, i, cr: (j, i, 0))
    blk_g = pl.BlockSpec((1, tr, c), lambda j, i, cr: (j, cr[0] * nt + i, 0))
    return pl.pallas_call(
        body, name=name,
        grid_spec=pltpu.PrefetchScalarGridSpec(
            num_scalar_prefetch=1, grid=(nj, nt), in_specs=[blk_g, blk_r], out_specs=[blk_r, blk_r]),
        out_shape=[jax.ShapeDtypeStruct((nj, rh, c), F32), jax.ShapeDtypeStruct((nj, rh, c), BF16)],
        compiler_params=_params(("parallel", "parallel")),
    )(c_idx, gj, r1)


def _chip_add(p, r2, chip_idx, name, tr=256):
    _, rh, c = p.shape
    tr = _fit(tr, rh, 16)

    def body(j_ref, o_ref, r_ref, f_ref):
        f_ref[...] = ((o_ref[0] + r_ref[0].astype(F32)) + r_ref[1].astype(F32)) + r_ref[2].astype(F32)

    return pl.pallas_call(
        body, name=name,
        grid_spec=pltpu.PrefetchScalarGridSpec(
            num_scalar_prefetch=1, grid=(rh // tr,),
            in_specs=[pl.BlockSpec((1, tr, c), lambda i, jr: (jr[0], i, 0)),
                      pl.BlockSpec((3, tr, c), lambda i, jr: (0, i, 0))],
            out_specs=pl.BlockSpec((tr, c), lambda i, jr: (i, 0))),
        out_shape=jax.ShapeDtypeStruct((rh, c), F32),
        compiler_params=_params(("parallel",)),
    )(chip_idx, p, r2)


BIG = ["w_in", "w_out", "w_gate", "w_up", "w_down"]
WITH_ATTENTION = ["w_in", "w_gate", "w_up"]
WITH_SSD = ["w_out", "w_down"]


def _set_own_slot(gathered, shard):
    my_chip = 2 * lax.axis_index("x") + lax.axis_index("y")
    return lax.dynamic_update_slice(gathered, shard[None], (my_chip, 0, 0))


class _GatherPlan:
    def __init__(self, shards):
        self.shards, self.parts = shards, {}

    def rider(self, host):
        if self.shards is None:
            return None
        if host == "ffn_gate_up":
            return _gather_pair_rider([self.parts[n] for n in BIG])
        names = WITH_ATTENTION if host == "attention_fwd" else WITH_SSD
        return _gather_chips_rider([self.shards[n] for n in names])

    def collect(self, host, outs):
        if self.shards is None:
            return
        names = BIG if host == "ffn_gate_up" else WITH_ATTENTION if host == "attention_fwd" else WITH_SSD
        self.parts.update(zip(names, outs, strict=True))

    def gathered(self):
        return {n: _set_own_slot(self.parts[n], self.shards[n]) for n in BIG}


def _gather_now(shards, tag):
    plan = _GatherPlan(shards)
    parts = _standalone(_gather_chips_rider([shards[n] for n in BIG]), f"allgather_chips_{tag}")
    plan.parts = dict(zip(BIG, _standalone(_gather_pair_rider(parts), f"allgather_pair_{tag}"), strict=True))
    return plan.gathered()


class _ReducePlan:
    def __init__(self, grads, layer, totals, depth, c_idx, chip_idx):
        self.grads, self.layer, self.totals, self.depth = grads, layer, totals, depth
        self.c_idx, self.chip_idx = c_idx, chip_idx
        self.p, self.pb, self.r2, self.f = {}, {}, {}, {}

    def rider(self, host):
        if self.grads is None:
            return None
        if host == "ffn_down_dgrad":
            return _reduce_pair_rider([self.grads[n] for n in BIG])
        if host == "proj_in_dgrad":
            tot = None if self.totals is None else [self.totals[n] for n in BIG]
            return _reduce_finish_rider([self.f[n] for n in BIG], tot, self.layer, self.depth)
        names = WITH_SSD if host == "ssd_bwd" else WITH_ATTENTION
        return _reduce_chips_rider([self.pb[n] for n in names])

    def collect(self, host, outs):
        if self.grads is None:
            return
        tag = f"layer{self.layer}"
        if host == "ffn_down_dgrad":
            for n, r1 in zip(BIG, outs, strict=True):
                self.p[n], self.pb[n] = _pair_add(self.grads[n], r1, self.c_idx, f"rs_pair_add_{n}_{tag}")
        elif host == "proj_in_dgrad":
            c = lax.axis_index("c")
            self.totals = {n: lax.dynamic_update_slice(t, self.f[n][None], (self.layer, c * self.f[n].shape[0], 0))
                           for n, t in zip(BIG, outs, strict=True)}
        else:
            names = WITH_SSD if host == "ssd_bwd" else WITH_ATTENTION
            for n, r2 in zip(names, outs, strict=True):
                self.f[n] = _chip_add(self.p[n], r2, self.chip_idx, f"rs_chip_add_{n}_{tag}")

    def run_now(self):
        tag = f"layer{self.layer}"
        self.collect("ffn_down_dgrad", _standalone(self.rider("ffn_down_dgrad"), f"rs_pair_{tag}"))
        self.collect("ssd_bwd", _standalone(self.rider("ssd_bwd"), f"rs_chips_a_{tag}"))
        self.collect("attention_bwd", _standalone(self.rider("attention_bwd"), f"rs_chips_b_{tag}"))
        self.collect("proj_in_dgrad", _standalone(self.rider("proj_in_dgrad"), f"rs_finish_{tag}"))
        return self.totals


def _small_exchange(v, name, reduce):
    rows = v.shape[0]

    def body(v_ref, o_ref, *rest):
        buf = rest[0] if reduce else o_ref
        send_sems, recv_sems = rest[-2], rest[-1]
        x, y, c = _coords()
        me = 4 * x + 2 * y + c
        buf[me] = v_ref[...]
        cps = []
        for r in range(1, N_DEV):
            peer = (lax.bitwise_xor(x, (r >> 2) & 1), lax.bitwise_xor(y, (r >> 1) & 1), lax.bitwise_xor(c, r & 1))
            cps.append(_remote(v_ref, buf.at[me], send_sems, recv_sems, r - 1, peer))
        for cp in cps:
            cp.start()
        for r in range(1, N_DEV):
            src = buf.at[lax.bitwise_xor(me, r)]
            _remote(src, src, send_sems, recv_sems, r - 1, (x, y, c)).wait_recv()
        for cp in cps:
            cp.wait_send()
        if reduce:
            acc = buf[0]
            for d in range(1, N_DEV):
                acc = acc + buf[d]
            o_ref[...] = acc
            o_ref[0:1, :] = jnp.broadcast_to(jnp.sum(acc[0:1, :], axis=1, keepdims=True), (1, LANES))

    scratch = [pltpu.SemaphoreType.DMA((N_DEV - 1,)), pltpu.SemaphoreType.DMA((N_DEV - 1,))]
    if reduce:
        scratch = [pltpu.VMEM((N_DEV, rows, LANES), F32)] + scratch
    out_shape = (rows, LANES) if reduce else (N_DEV, rows, LANES)
    return pl.pallas_call(
        body, name=name, in_specs=[VMEM], out_specs=VMEM,
        out_shape=jax.ShapeDtypeStruct(out_shape, F32), scratch_shapes=scratch,
    )(v)


def _pack(parts):
    flat = []
    for a in parts:
        a = a.reshape(-1)
        flat.append(jnp.pad(a, (0, (-a.shape[0]) % LANES)))
    v = jnp.concatenate(flat)
    v = jnp.pad(v, (0, (-v.shape[0]) % (SUBLANES * LANES)))
    return v.reshape(-1, LANES)


def _unpack(slab, shapes):
    flat = slab.reshape(-1)
    out, off = [], 0
    for shp in shapes:
        size = 1
        for d in shp:
            size *= d
        out.append(flat[off:off + size].reshape(shp))
        off += size + (-size) % LANES
    return out


def _group_slots(a, hg):
    lead = a.shape[:-1]
    a = a.reshape(lead + (SSM_GROUPS, hg))
    a = jnp.pad(a, [(0, 0)] * len(lead) + [(0, 0), (0, LANES - hg)])
    return a.reshape(lead + (SSM_GROUPS * LANES,))


def _ungroup_slots(a, hg):
    lead = a.shape[:-1]
    return a.reshape(lead + (SSM_GROUPS, LANES))[..., :hg].reshape(lead + (SSM_GROUPS * hg,))


def _layer_fwd(x, p, bsz, s, plan):
    t, d = x.shape
    aw, sw, cd, hg = p["aw"], p["sw"], p["cd"], p["hg"]
    dff = p["wg"].shape[1]
    h = _rmsnorm_fwd(x, p["norm_mix"], "norm_mix_fwd")
    qkv = _mm(h, p["wqkv"], "nn", t, 3 * aw, F32, "proj_qkv", tm=1024, tn=512)
    z = _mm(h, p["wz"], "nn", t, sw, F32, "proj_z", tm=1024, tn=512)
    xbc = _mm(h, p["wxbc"], "nn", t, cd, F32, "proj_xbc", tm=1024, tn=512)
    dtraw = _mm(h, p["wdt"], "nn", t, SSM_GROUPS * LANES, F32, "proj_dt", tm=1024, tn=SSM_GROUPS * LANES)
    qkv3 = qkv.reshape(bsz, s, 3 * aw)
    (o_att, rtot), sent = _attention_fwd(qkv3, p["q_gain"], p["k_gain"], "attention_fwd",
                                         rider=plan.rider("attention_fwd"))
    plan.collect("attention_fwd", sent)
    xbc3 = xbc.reshape(bsz, s, cd)
    xact = _conv_fwd(xbc3, p["conv_w"], p["conv_b"], "conv_fwd")
    dt3 = dtraw.reshape(bsz, s, SSM_GROUPS * LANES)
    (y, hs), sent = _ssd_fwd(xact, dt3, p["dt_bias"], p["a_log"], p["d_skip"], hg, "ssd_fwd",
                             rider=plan.rider("ssd_fwd"))
    plan.collect("ssd_fwd", sent)
    o2, y2 = o_att.reshape(t, aw), y.reshape(t, sw)
    mix = _merge_fwd(o2, y2, z, p["attn_out_gain"], p["ssm_out_gain"], "merge_fwd")
    x1 = _mm(mix, p["wout"], "nn", t, d, F32, "proj_out", tm=1024, tn=512, res=x)
    h2 = _rmsnorm_fwd(x1, p["norm_ffn"], "norm_ffn_fwd")
    (gate, up, act), sent = _hosted_matmul(
        [[(h2, p["wg"], "nn")], [(h2, p["wu"], "nn")]], [], _swiglu_fwd_epilogue,
        [F32, F32, BF16], t, dff, 512, 512, "ffn_gate_up", plan)
    x2 = _mm(act, p["wd"], "nn", t, d, F32, "ffn_down", tm=512, tn=512, res=x1)
    saved = dict(x=x, h=h, qkv3=qkv3, z=z, xbc3=xbc3, dt3=dt3, o2=o2, rtot=rtot, xact=xact, hs=hs, y2=y2,
                 mix=mix, x1=x1, h2=h2, gate=gate, up=up, act=act)
    return x2, saved


def _hosted_matmul(groups, extras, epilogue, out_dtypes, m, n, tm, tn, name, plan):
    rider = plan.rider(name)
    if rider is None:
        return _matmul(groups, extras, epilogue, out_dtypes, m, n, tm, tn, name), []
    outs, sent = _matmul(groups, extras, epilogue, out_dtypes, m, n, tm, tn, name, rider=rider)
    plan.collect(name, sent)
    return outs, sent


def _layer_bwd(dx2, dx2b, p, sv, bsz, s, plan):
    t, d = dx2.shape
    aw, sw, cd, hg = p["aw"], p["sw"], p["cd"], p["hg"]
    dff = p["wg"].shape[1]
    gr = {}
    (dgate, dup), _ = _hosted_matmul([[(dx2b, p["wd"], "nt")]], [sv["gate"], sv["up"]], _swiglu_bwd_epilogue,
                                     [BF16, BF16], t, dff, 512, 512, "ffn_down_dgrad", plan)
    gr["wd"] = _mm(sv["act"], dx2b, "tn", dff, d, F32, "ffn_down_wgrad")
    dh2 = _matmul([[(dgate, p["wg"], "nt"), (dup, p["wu"], "nt")]], [], lambda accs, ex: (accs[0],),
                  [F32], t, d, 512, 256, "ffn_gate_up_dgrad")[0]
    gr["wg"] = _mm(sv["h2"], dgate, "tn", d, dff, F32, "ffn_gate_wgrad")
    gr["wu"] = _mm(sv["h2"], dup, "tn", d, dff, F32, "ffn_up_wgrad")
    dx1, dx1b, gr["norm_ffn"] = _rmsnorm_bwd(sv["x1"], p["norm_ffn"], dh2, dx2, "norm_ffn_bwd")
    dmix = _mm(dx1b, p["wout"], "nt", t, aw + sw, F32, "proj_out_dgrad")
    gr["wout"] = _mm(sv["mix"], dx1b, "tn", aw + sw, d, F32, "proj_out_wgrad")
    do, dy, dz, gr["attn_out_gain"], gr["ssm_out_gain"] = _merge_bwd(
        sv["o2"], sv["y2"], sv["z"], p["attn_out_gain"], p["ssm_out_gain"], dmix, "merge_bwd")
    (dxs, dbm, dcm, ddt, dvec), sent = _ssd_bwd(sv["xact"], sv["dt3"], p["dt_bias"], p["a_log"], p["d_skip"],
                                                sv["hs"], dy.reshape(bsz, s, sw), hg, "ssd_bwd",
                                                rider=plan.rider("ssd_bwd"))
    plan.collect("ssd_bwd", sent)
    dvec = jnp.sum(dvec, axis=0).reshape(SSM_GROUPS, SUBLANES, LANES)
    gr["dt_bias"], gr["a_log"], gr["d_skip"] = (dvec[:, k, :hg].reshape(-1) for k in range(3))
    dxact = jnp.concatenate([dxs, dbm, dcm], axis=-1)
    dxbc, gr["conv_w"], gr["conv_b"] = _conv_bwd(sv["xbc3"], p["conv_w"], p["conv_b"], dxact, "conv_bwd")
    (dqkv, dqg, dkg), sent = _attention_bwd(sv["qkv3"], p["q_gain"], p["k_gain"], sv["rtot"],
                                            do.reshape(bsz, s, aw), "attention_bwd",
                                            rider=plan.rider("attention_bwd"))
    plan.collect("attention_bwd", sent)
    gr["q_gain"] = jnp.sum(dqg, axis=(0, 1, 2))
    gr["k_gain"] = jnp.sum(dkg, axis=(0, 1, 2))
    dqkv, dxbc, ddt = dqkv.reshape(t, 3 * aw), dxbc.reshape(t, cd), ddt.reshape(t, SSM_GROUPS * LANES)
    (dh,), _ = _hosted_matmul(
        [[(dqkv, p["wqkv"], "nt"), (dz, p["wz"], "nt"), (dxbc, p["wxbc"], "nt"), (ddt, p["wdt"], "nt")]],
        [], lambda accs, ex: (accs[0],), [F32], t, d, 512, 256, "proj_in_dgrad", plan)
    h = sv["h"]
    gr["wqkv"] = _mm(h, dqkv, "tn", d, 3 * aw, F32, "proj_qkv_wgrad")
    gr["wz"] = _mm(h, dz, "tn", d, sw, F32, "proj_z_wgrad")
    gr["wxbc"] = _mm(h, dxbc, "tn", d, cd, F32, "proj_xbc_wgrad")
    gr["wdt"] = _mm(h, ddt, "tn", d, SSM_GROUPS * LANES, F32, "proj_dt_wgrad", tn=SSM_GROUPS * LANES)
    dx, dxb, gr["norm_mix"] = _rmsnorm_bwd(sv["x"], p["norm_mix"], dh, dx1, "norm_mix_bwd")
    return dx, dxb, gr


SMALL = ["norm_mix", "q_gain", "k_gain", "conv_w", "conv_b", "dt_bias", "a_log", "d_skip",
         "attn_out_gain", "ssm_out_gain", "norm_ffn"]
ORDER = ["norm_mix", "w_in", "q_gain", "k_gain", "conv_w", "conv_b", "dt_bias", "a_log", "d_skip",
         "attn_out_gain", "ssm_out_gain", "w_out", "norm_ffn", "w_gate", "w_up", "w_down"]


def kernel(x, norm_mix, w_in, q_gain, k_gain, conv_w, conv_b, dt_bias, a_log, d_skip, attn_out_gain, ssm_out_gain, w_out, norm_ffn, w_gate, w_up, w_down, loss_target, m_norm_mix, m_w_in, m_q_gain, m_k_gain, m_conv_w, m_conv_b, m_dt_bias, m_a_log, m_d_skip, m_attn_out_gain, m_ssm_out_gain, m_w_out, m_norm_ffn, m_w_gate, m_w_up, m_w_down, v_norm_mix, v_w_in, v_q_gain, v_k_gain, v_conv_w, v_conv_b, v_dt_bias, v_a_log, v_d_skip, v_attn_out_gain, v_ssm_out_gain, v_w_out, v_norm_ffn, v_w_gate, v_w_up, v_w_down):
    w = dict(norm_mix=norm_mix, w_in=w_in, q_gain=q_gain, k_gain=k_gain, conv_w=conv_w, conv_b=conv_b,
             dt_bias=dt_bias, a_log=a_log, d_skip=d_skip, attn_out_gain=attn_out_gain, ssm_out_gain=ssm_out_gain,
             w_out=w_out, norm_ffn=norm_ffn, w_gate=w_gate, w_up=w_up, w_down=w_down)
    mom = dict(norm_mix=m_norm_mix, w_in=m_w_in, q_gain=m_q_gain, k_gain=m_k_gain, conv_w=m_conv_w,
               conv_b=m_conv_b, dt_bias=m_dt_bias, a_log=m_a_log, d_skip=m_d_skip,
               attn_out_gain=m_attn_out_gain, ssm_out_gain=m_ssm_out_gain, w_out=m_w_out, norm_ffn=m_norm_ffn,
               w_gate=m_w_gate, w_up=m_w_up, w_down=m_w_down)
    var = dict(norm_mix=v_norm_mix, w_in=v_w_in, q_gain=v_q_gain, k_gain=v_k_gain, conv_w=v_conv_w,
               conv_b=v_conv_b, dt_bias=v_dt_bias, a_log=v_a_log, d_skip=v_d_skip,
               attn_out_gain=v_attn_out_gain, ssm_out_gain=v_ssm_out_gain, w_out=v_w_out, norm_ffn=v_norm_ffn,
               w_gate=v_w_gate, w_up=v_w_up, w_down=v_w_down)

    bsz, s, d = x.shape
    t = bsz * s
    depth = norm_mix.shape[0]
    aw = attn_out_gain.shape[1]
    sw = ssm_out_gain.shape[1]
    cd = conv_b.shape[1]
    hs_n = dt_bias.shape[1]
    hg = hs_n // SSM_GROUPS
    heads = aw // ATT_HEAD_DIM
    in_dim = 3 * aw + sw + cd + hs_n
    dff = w_gate.shape[2] * N_CHIPS
    cs = conv_w.shape[2]
    my_chip = 2 * lax.axis_index("x") + lax.axis_index("y")
    c_idx = lax.axis_index("c").astype(jnp.int32).reshape(1)

    chip_idx = my_chip.astype(jnp.int32).reshape(1)
    wb = {n: w[n].astype(BF16) for n in BIG}
    conv_all = _small_exchange(_pack([conv_w]), "allgather_conv_w", False)
    conv_full = jnp.concatenate(
        [_unpack(conv_all[2 * j], [conv_w.shape])[0] for j in range(N_CHIPS)], axis=-1)

    def layer_params(l, gat):
        win = jnp.transpose(gat["w_in"], (1, 0, 2)).reshape(d, in_dim)
        wqkv = win[:, :3 * aw].reshape(d, 3, heads, ATT_HEAD_DIM)
        wqkv = jnp.transpose(wqkv, (0, 2, 1, 3)).reshape(d, 3 * aw)
        return dict(aw=aw, sw=sw, cd=cd, hg=hg, norm_mix=norm_mix[l], wqkv=wqkv, wz=win[:, 3 * aw:3 * aw + sw],
                    wxbc=win[:, 3 * aw + sw:3 * aw + sw + cd], wdt=_group_slots(win[:, 3 * aw + sw + cd:], hg),
                    q_gain=q_gain[l], k_gain=k_gain[l], conv_w=conv_full[l], conv_b=conv_b[l],
                    dt_bias=_group_slots(dt_bias[l], hg).reshape(1, -1),
                    a_log=_group_slots(a_log[l], hg).reshape(1, -1),
                    d_skip=_group_slots(d_skip[l], hg).reshape(1, -1),
                    attn_out_gain=attn_out_gain[l], ssm_out_gain=ssm_out_gain[l],
                    wout=gat["w_out"].reshape(aw + sw, d), norm_ffn=norm_ffn[l],
                    wg=jnp.transpose(gat["w_gate"], (1, 0, 2)).reshape(d, dff),
                    wu=jnp.transpose(gat["w_up"], (1, 0, 2)).reshape(d, dff), wd=gat["w_down"].reshape(dff, d))

    def per_chip(gr):
        gqkv = gr["wqkv"].reshape(d, heads, 3, ATT_HEAD_DIM)
        gqkv = jnp.transpose(gqkv, (0, 2, 1, 3)).reshape(d, 3 * aw)
        gin = jnp.concatenate([gqkv, gr["wz"], gr["wxbc"], _ungroup_slots(gr["wdt"], hg)], axis=-1)
        return {"w_in": jnp.transpose(gin.reshape(d, N_CHIPS, in_dim // N_CHIPS), (1, 0, 2)),
                "w_out": gr["wout"].reshape(N_CHIPS, (aw + sw) // N_CHIPS, d),
                "w_gate": jnp.transpose(gr["wg"].reshape(d, N_CHIPS, dff // N_CHIPS), (1, 0, 2)),
                "w_up": jnp.transpose(gr["wu"].reshape(d, N_CHIPS, dff // N_CHIPS), (1, 0, 2)),
                "w_down": gr["wd"].reshape(N_CHIPS, dff // N_CHIPS, d)}

    xt = x.reshape(t, d)
    saved, params = [], []
    gat = _gather_now({n: wb[n][0] for n in BIG}, "first")
    for l in range(depth):
        params.append(layer_params(l, gat))
        plan = _GatherPlan({n: wb[n][l + 1] for n in BIG} if l + 1 < depth else None)
        xt, sv = _layer_fwd(xt, params[l], bsz, s, plan)
        saved.append(sv)
        if l + 1 < depth:
            gat = plan.gathered()
    dxt, dxb, loss_lanes = _loss_head(xt, loss_target.reshape(t, d), "loss_head")

    grads = [None] * depth
    pending, totals = None, None
    for l in reversed(range(depth)):
        plan = _ReducePlan(pending, l + 1, totals, depth, c_idx, chip_idx)
        dxt, dxb, grads[l] = _layer_bwd(dxt, dxb, params[l], saved[l], bsz, s, plan)
        totals = plan.totals
        pending = per_chip(grads[l])
    g = _ReducePlan(pending, 0, totals, depth, c_idx, chip_idx).run_now()
    grad_x = dxt.reshape(bsz, s, d)

    def stack(name):
        return jnp.stack([grads[l][name] for l in range(depth)])

    small_shapes = [(1, LANES)] + [(depth, CONV_WIDTH, cd) if n == "conv_w" else w[n].shape for n in SMALL]
    small = _small_exchange(_pack([loss_lanes] + [stack(n) for n in SMALL]), "allreduce_small", True)
    small = _unpack(small, small_shapes)
    loss = small[0][0, 0]
    for n, a in zip(SMALL, small[1:], strict=True):
        g[n] = a
    g["conv_w"] = lax.dynamic_slice_in_dim(g["conv_w"], my_chip * cs, cs, axis=2)

    delta, new_m, new_v = {}, {}, {}
    for n in BIG:
        shp = w[n].shape
        two_d = (shp[0] * shp[1], shp[2])
        dl, nm, nv = _adamw(w[n].reshape(two_d), g[n].reshape(two_d), mom[n].reshape(two_d),
                            var[n].reshape(two_d), f"adamw_{n}")
        delta[n], new_m[n], new_v[n] = dl.reshape(shp), nm.reshape(shp), nv.reshape(shp)
    shapes = [w[n].shape for n in SMALL]
    dl, nm, nv = _adamw(_pack([w[n] for n in SMALL]), _pack([g[n] for n in SMALL]),
                        _pack([mom[n] for n in SMALL]), _pack([var[n] for n in SMALL]), "adamw_small")
    for n, a, b, c in zip(SMALL, _unpack(dl, shapes), _unpack(nm, shapes), _unpack(nv, shapes), strict=True):
        delta[n], new_m[n], new_v[n] = a, b, c

    return (loss, grad_x, *[g[n] for n in ORDER], *[delta[n] for n in ORDER],
            *[new_m[n] for n in ORDER], *[new_v[n] for n in ORDER])
```

```python
import jax
import jax.numpy as jnp
from jax import lax
from jax.experimental import pallas as pl
from jax.experimental.pallas import tpu as pltpu

F32 = jnp.float32
BF16 = jnp.bfloat16
MESH = pl.DeviceIdType.MESH
ANY = pl.BlockSpec(memory_space=pl.ANY)
VMEM = pl.BlockSpec(memory_space=pltpu.VMEM)

EPS = 1e-6
ATT_HEAD_DIM = 128
SSM_HEAD_DIM = 64
SSM_GROUPS = 2
SSM_STATE = 128
SSD_CHUNK = 128
CONV_WIDTH = 4
LANES = 128
SUBLANES = 8
ATT_TILE = 256
ATT_STRIP = 128
N_CHIPS = 4
N_DEV = 8

ADAM_LR = 0.001
ADAM_B1 = 0.9
ADAM_B2 = 0.999
ADAM_EPS = 1e-08
ADAM_WD = 0.01
ADAM_STEP = 10

VMEM_LIMIT = 48 * 1024 * 1024

NN = (((1,), (0,)), ((), ()))
NT = (((1,), (1,)), ((), ()))
TN = (((0,), (0,)), ((), ()))


def _dot(a, b, dims=NN):
    return lax.dot_general(a.astype(BF16), b.astype(BF16), dims, preferred_element_type=F32)


def _dot_exact(x, ones, dims=NN, passes=3, ones_left=False):
    acc = None
    rem = x
    for _ in range(passes):
        piece = rem.astype(BF16)
        rem = rem - piece.astype(F32)
        p = (lax.dot_general(ones, piece, dims, preferred_element_type=F32) if ones_left
             else lax.dot_general(piece, ones, dims, preferred_element_type=F32))
        acc = p if acc is None else acc + p
    return acc


def _scan_lanes(x, tri, passes, reverse=False):
    nblk = x.shape[1] // LANES
    blocks = [x[:, k * LANES:(k + 1) * LANES] for k in range(nblk)]
    out, carry = [None] * nblk, None
    for k in (reversed(range(nblk)) if reverse else range(nblk)):
        p = _dot_exact(blocks[k], tri, passes=passes)
        out[k] = p if carry is None else p + carry
        tot = jnp.sum(blocks[k], axis=1, keepdims=True)
        carry = tot if carry is None else carry + tot
    return (out[0] if nblk == 1 else jnp.concatenate(out, axis=1)), carry


def _iota2(shape, axis):
    return lax.broadcasted_iota(jnp.int32, shape, axis)


def _tri(n, cmp):
    return cmp(_iota2((n, n), 0), _iota2((n, n), 1)).astype(BF16)


def _sum_all(v):
    return jnp.sum(jnp.sum(v, axis=1, keepdims=True), axis=0, keepdims=True)


def _fit(tile, dim, unit=LANES):
    if dim <= tile:
        return dim
    return max(k for k in range(unit, tile + 1, unit) if dim % k == 0)


def _params(sem):
    return pltpu.CompilerParams(dimension_semantics=sem, vmem_limit_bytes=VMEM_LIMIT)


def _call(body, *, name, grid, in_specs, out_specs, out_shape, sem, args, scratch_shapes=(), rider=None):
    in_specs, out_specs, out_shape = list(in_specs), list(out_specs), list(out_shape)
    scratch_shapes = list(scratch_shapes)
    if rider is None:
        res = pl.pallas_call(body, name=name, grid=grid, in_specs=in_specs, out_specs=out_specs,
                             out_shape=out_shape, scratch_shapes=scratch_shapes,
                             compiler_params=_params(sem))(*args)
        return list(res), []
    n_in, n_out, n_scr = len(in_specs), len(out_specs), len(scratch_shapes)
    r_in, r_out, n_sems = len(rider["ins"]), len(rider["outs"]), rider["n_sems"]

    def hosted(*refs):
        ins, rest = refs[:n_in], refs[n_in:]
        rins, rest = rest[:r_in], rest[r_in:]
        outs, rest = rest[:n_out], rest[n_out:]
        routs, rest = rest[:r_out], rest[r_out:]
        scr, (send_sems, recv_sems) = rest[:n_scr], rest[n_scr:]
        first, last = None, None
        for d, size in enumerate(grid):
            f, e = pl.program_id(d) == 0, pl.program_id(d) == size - 1
            first = f if first is None else jnp.logical_and(first, f)
            last = e if last is None else jnp.logical_and(last, e)

        @pl.when(first)
        def _():
            for cp in rider["copies"](rins, routs, send_sems, recv_sems):
                cp.start()

        body(*ins, *outs, *scr)

        @pl.when(last)
        def _():
            for cp in rider["copies"](rins, routs, send_sems, recv_sems):
                cp.wait()

    res = pl.pallas_call(
        hosted, name=name, grid=grid, in_specs=in_specs + [ANY] * r_in, out_specs=out_specs + [ANY] * r_out,
        out_shape=out_shape + list(rider["outs"]),
        scratch_shapes=scratch_shapes + [pltpu.SemaphoreType.DMA((n_sems,)), pltpu.SemaphoreType.DMA((n_sems,))],
        input_output_aliases={n_in + i: n_out + o for i, o in rider["aliases"].items()},
        compiler_params=_params(("arbitrary",) * len(grid)),
    )(*args, *rider["ins"])
    return list(res[:n_out]), list(res[n_out:])


def _softplus(x):
    return jnp.maximum(x, 0.0) + jnp.log(1.0 + jnp.exp(-jnp.abs(x)))


def _sigmoid(x):
    return 1.0 / (1.0 + jnp.exp(-x))


def _rms_fwd(x, g):
    r = lax.rsqrt(jnp.mean(x * x, axis=-1, keepdims=True) + EPS)
    return (x * r) * g


def _rms_bwd(x, g, dh):
    r = lax.rsqrt(jnp.mean(x * x, axis=-1, keepdims=True) + EPS)
    y = x * r
    dy = dh * g
    dx = r * (dy - y * jnp.mean(dy * y, axis=-1, keepdims=True))
    return dx, dh * y


def _matmul(groups, extras, epilogue, out_dtypes, m, n, tm, tn, name, rider=None):
    tm, tn = _fit(tm, m), _fit(tn, n)
    flat = [t for grp in groups for t in grp]
    n_terms, n_extra = len(flat), len(extras)

    def body(*refs):
        outs = refs[2 * n_terms + n_extra:]
        accs, pos = [], 0
        for grp in groups:
            acc = None
            for (_, _, mode) in grp:
                dims = {"nn": NN, "nt": NT, "tn": TN}[mode]
                p = _dot(refs[2 * pos][...], refs[2 * pos + 1][...], dims)
                acc = p if acc is None else acc + p
                pos += 1
            accs.append(acc)
        ex = [refs[2 * n_terms + i][...] for i in range(n_extra)]
        res = epilogue(accs, ex)
        for o_ref, r in zip(outs, res, strict=True):
            o_ref[...] = r.astype(o_ref.dtype)

    in_specs, args = [], []
    for (a, b, mode) in flat:
        if mode == "nn":
            k = a.shape[1]
            in_specs += [pl.BlockSpec((tm, k), lambda i, j: (i, 0)), pl.BlockSpec((k, tn), lambda i, j: (0, j))]
        elif mode == "nt":
            k = a.shape[1]
            in_specs += [pl.BlockSpec((tm, k), lambda i, j: (i, 0)), pl.BlockSpec((tn, k), lambda i, j: (j, 0))]
        else:
            k = a.shape[0]
            in_specs += [pl.BlockSpec((k, tm), lambda i, j: (0, i)), pl.BlockSpec((k, tn), lambda i, j: (0, j))]
        args += [a, b]
    for e in extras:
        in_specs.append(pl.BlockSpec((tm, tn), lambda i, j: (i, j)))
        args.append(e)
    outs, routs = _call(
        body, name=name, grid=(m // tm, n // tn), in_specs=in_specs,
        out_specs=[pl.BlockSpec((tm, tn), lambda i, j: (i, j)) for _ in out_dtypes],
        out_shape=[jax.ShapeDtypeStruct((m, n), d) for d in out_dtypes],
        sem=("parallel", "parallel"), args=args, rider=rider)
    return outs if rider is None else (outs, routs)


def _mm(a, b, mode, m, n, out_dtype, name, tm=512, tn=512, res=None):
    extras = [] if res is None else [res]
    epi = (lambda accs, ex: (accs[0],)) if res is None else (lambda accs, ex: (accs[0] + ex[0],))
    return _matmul([[(a, b, mode)]], extras, epi, [out_dtype], m, n, tm, tn, name)[0]


def _swiglu_fwd_epilogue(accs, ex):
    g, u = accs
    return g, u, (g * _sigmoid(g)) * u


def _swiglu_bwd_epilogue(accs, ex):
    dact, (g, u) = accs[0], ex
    sg = _sigmoid(g)
    silu = g * sg
    return dact * u * (sg * (1.0 + g * (1.0 - sg))), dact * silu


def _rmsnorm_fwd(x, g, name, tr=512):
    t, d = x.shape
    tr = min(tr, t)

    def body(x_ref, g_ref, h_ref):
        h_ref[...] = _rms_fwd(x_ref[...], g_ref[...]).astype(BF16)

    return pl.pallas_call(
        body, name=name, grid=(t // tr,),
        in_specs=[pl.BlockSpec((tr, d), lambda i: (i, 0)), pl.BlockSpec((1, d), lambda i: (0, 0))],
        out_specs=pl.BlockSpec((tr, d), lambda i: (i, 0)),
        out_shape=jax.ShapeDtypeStruct((t, d), BF16),
        compiler_params=_params(("parallel",)),
    )(x, g.reshape(1, d))


def _rmsnorm_bwd(x, g, dh, dres, name, tr=256):
    t, d = x.shape
    tr = min(tr, t)

    def body(x_ref, g_ref, dh_ref, dres_ref, dx_ref, dxb_ref, dg_ref):
        dx, dgr = _rms_bwd(x_ref[...], g_ref[...], dh_ref[...])
        dx = dx + dres_ref[...]
        dx_ref[...] = dx
        dxb_ref[...] = dx.astype(BF16)

        @pl.when(pl.program_id(0) == 0)
        def _():
            dg_ref[...] = jnp.zeros_like(dg_ref)

        dg_ref[...] += jnp.sum(dgr, axis=0, keepdims=True)

    row = pl.BlockSpec((tr, d), lambda i: (i, 0))
    vec = pl.BlockSpec((1, d), lambda i: (0, 0))
    dx, dxb, dg = pl.pallas_call(
        body, name=name, grid=(t // tr,),
        in_specs=[row, vec, row, row], out_specs=[row, row, vec],
        out_shape=[jax.ShapeDtypeStruct((t, d), F32), jax.ShapeDtypeStruct((t, d), BF16),
                   jax.ShapeDtypeStruct((1, d), F32)],
        compiler_params=_params(("arbitrary",)),
    )(x, g.reshape(1, d), dh, dres)
    return dx, dxb, dg.reshape(d)


def _merge_fwd(o_att, y, z, ga, gs, name, tr=256):
    t, wa = o_att.shape
    ws = y.shape[1]
    wg = ws // SSM_GROUPS
    tr = min(tr, t)

    def body(o_ref, y_ref, z_ref, ga_ref, gs_ref, m_ref):
        m_ref[:, 0:wa] = _rms_fwd(o_ref[...], ga_ref[...]).astype(BF16)
        for g in range(SSM_GROUPS):
            sl = slice(g * wg, (g + 1) * wg)
            zz = z_ref[:, sl]
            yz = y_ref[:, sl] * (zz * _sigmoid(zz))
            m_ref[:, wa + g * wg:wa + (g + 1) * wg] = _rms_fwd(yz, gs_ref[:, sl]).astype(BF16)

    return pl.pallas_call(
        body, name=name, grid=(t // tr,),
        in_specs=[pl.BlockSpec((tr, wa), lambda i: (i, 0)), pl.BlockSpec((tr, ws), lambda i: (i, 0)),
                  pl.BlockSpec((tr, ws), lambda i: (i, 0)), pl.BlockSpec((1, wa), lambda i: (0, 0)),
                  pl.BlockSpec((1, ws), lambda i: (0, 0))],
        out_specs=pl.BlockSpec((tr, wa + ws), lambda i: (i, 0)),
        out_shape=jax.ShapeDtypeStruct((t, wa + ws), BF16),
        compiler_params=_params(("parallel",)),
    )(o_att, y, z, ga.reshape(1, wa), gs.reshape(1, ws))


def _merge_bwd(o_att, y, z, ga, gs, dmix, name, tr=256):
    t, wa = o_att.shape
    ws = y.shape[1]
    wg = ws // SSM_GROUPS
    tr = min(tr, t)

    def body(o_ref, y_ref, z_ref, ga_ref, gs_ref, dm_ref, do_ref, dy_ref, dz_ref, dga_ref, dgs_ref):
        @pl.when(pl.program_id(0) == 0)
        def _():
            dga_ref[...] = jnp.zeros_like(dga_ref)
            dgs_ref[...] = jnp.zeros_like(dgs_ref)

        do, dgr = _rms_bwd(o_ref[...], ga_ref[...], dm_ref[:, 0:wa])
        do_ref[...] = do
        dga_ref[...] += jnp.sum(dgr, axis=0, keepdims=True)
        for g in range(SSM_GROUPS):
            sl = slice(g * wg, (g + 1) * wg)
            zz, yy = z_ref[:, sl], y_ref[:, sl]
            sg = _sigmoid(zz)
            silu = zz * sg
            dyz, dgr = _rms_bwd(yy * silu, gs_ref[:, sl], dm_ref[:, wa + g * wg:wa + (g + 1) * wg])
            dy_ref[:, sl] = dyz * silu
            dz_ref[:, sl] = (dyz * yy * (sg + silu * (1.0 - sg))).astype(BF16)
            dgs_ref[:, sl] += jnp.sum(dgr, axis=0, keepdims=True)

    rowa = pl.BlockSpec((tr, wa), lambda i: (i, 0))
    rows = pl.BlockSpec((tr, ws), lambda i: (i, 0))
    veca = pl.BlockSpec((1, wa), lambda i: (0, 0))
    vecs = pl.BlockSpec((1, ws), lambda i: (0, 0))
    do, dy, dz, dga, dgs = pl.pallas_call(
        body, name=name, grid=(t // tr,),
        in_specs=[rowa, rows, rows, veca, vecs, pl.BlockSpec((tr, wa + ws), lambda i: (i, 0))],
        out_specs=[rowa, rows, rows, veca, vecs],
        out_shape=[jax.ShapeDtypeStruct((t, wa), F32), jax.ShapeDtypeStruct((t, ws), F32),
                   jax.ShapeDtypeStruct((t, ws), BF16), jax.ShapeDtypeStruct((1, wa), F32),
                   jax.ShapeDtypeStruct((1, ws), F32)],
        compiler_params=_params(("arbitrary",)),
    )(o_att, y, z, ga.reshape(1, wa), gs.reshape(1, ws), dmix)
    return do, dy, dz, dga.reshape(wa), dgs.reshape(ws)


def _loss_head(y, target, name, tr=256):
    t, d = y.shape
    tr = min(tr, t)

    def body(y_ref, t_ref, dy_ref, dyb_ref, l_ref):
        @pl.when(pl.program_id(0) == 0)
        def _():
            l_ref[...] = jnp.zeros_like(l_ref)

        diff = y_ref[...] - t_ref[...]
        dy = diff * (1.0 / d)
        dy_ref[...] = dy
        dyb_ref[...] = dy.astype(BF16)
        part = jnp.sum(diff * diff, axis=0, keepdims=True)
        fold = part[:, 0:LANES]
        for k in range(1, d // LANES):
            fold = fold + part[:, k * LANES:(k + 1) * LANES]
        l_ref[...] += fold * (0.5 / d)

    row = pl.BlockSpec((tr, d), lambda i: (i, 0))
    return pl.pallas_call(
        body, name=name, grid=(t // tr,), in_specs=[row, row],
        out_specs=[row, row, pl.BlockSpec((1, LANES), lambda i: (0, 0))],
        out_shape=[jax.ShapeDtypeStruct((t, d), F32), jax.ShapeDtypeStruct((t, d), BF16),
                   jax.ShapeDtypeStruct((1, LANES), F32)],
        compiler_params=_params(("arbitrary",)),
    )(y, target)


def _adamw(w, g, m, v, name, tr=256):
    r, c = w.shape
    tr = _fit(tr, r, 16)

    def body(w_ref, g_ref, m_ref, v_ref, d_ref, nm_ref, nv_ref):
        gg = g_ref[...]
        nm = ADAM_B1 * m_ref[...] + (1.0 - ADAM_B1) * gg
        nv = ADAM_B2 * v_ref[...] + (1.0 - ADAM_B2) * (gg * gg)
        m_hat = nm / (1.0 - ADAM_B1 ** ADAM_STEP)
        v_hat = nv / (1.0 - ADAM_B2 ** ADAM_STEP)
        d_ref[...] = -ADAM_LR * (m_hat / (jnp.sqrt(v_hat) + ADAM_EPS) + ADAM_WD * w_ref[...])
        nm_ref[...] = nm
        nv_ref[...] = nv

    blk = pl.BlockSpec((tr, c), lambda i: (i, 0))
    return pl.pallas_call(
        body, name=name, grid=(r // tr,), in_specs=[blk] * 4, out_specs=[blk] * 3,
        out_shape=[jax.ShapeDtypeStruct((r, c), F32)] * 3,
        compiler_params=_params(("parallel",)),
    )(w, g, m, v)


def _att_scores(qi, kj, scale, row0):
    z = _dot(qi, kj, NT) * scale
    lb = -_softplus(-z)
    lrm = lb - z
    if row0 is None:
        return lb, lrm, None
    mask = _iota2(z.shape, 1) < _iota2(z.shape, 0) + row0
    return lb, jnp.where(mask, lrm, 0.0), mask


def _masked(mask, v):
    return v if mask is None else jnp.where(mask, v, 0.0)


def _attention_fwd(qkv, qg, kg, name, tile=None, rider=None):
    bsz, s, w3 = qkv.shape
    hd = ATT_HEAD_DIM
    heads = w3 // (3 * hd)
    tile = min(tile or ATT_TILE, s)
    strip = min(ATT_STRIP, tile)
    nb = s // tile
    scale = hd ** -0.5

    def body(qkv_ref, qg_ref, kg_ref, o_ref, r_ref, qn_s, kn_s, vb_s, acc_s, c_s):
        qn_s[...] = _rms_fwd(qkv_ref[0, :, 0:hd], qg_ref[...]).astype(BF16)
        kn_s[...] = _rms_fwd(qkv_ref[0, :, hd:2 * hd], kg_ref[...]).astype(BF16)
        vb_s[...] = qkv_ref[0, :, 2 * hd:3 * hd].astype(BF16)
        after = _tri(LANES, lambda r, c: r > c)

        def q_loop(i, _):
            rows = pl.ds(pl.multiple_of(i * tile, tile), tile)
            acc_s[...] = jnp.zeros_like(acc_s)
            c_s[...] = jnp.zeros_like(c_s)

            def key_tile(j, diagonal):
                cols = pl.ds(pl.multiple_of(j * tile, tile), tile)
                kj, vj = kn_s[cols, :], vb_s[cols, :]
                strips = range(tile // strip)
                subs = [slice(r * strip, (r + 1) * strip) for r in strips]
                srows = [pl.ds(pl.multiple_of(i * tile + r * strip, strip), strip) for r in strips]
                sc = [_att_scores(qn_s[srows[r], :], kj, scale, r * strip if diagonal else None) for r in strips]
                later = [_scan_lanes(sc[r][1], after, 2, reverse=True) for r in strips]
                for r in strips:
                    w = _masked(sc[r][2], jnp.exp(sc[r][0] + (later[r][0] + c_s[subs[r], :])))
                    acc_s[subs[r], :] += _dot(w, vj)
                    c_s[subs[r], :] += later[r][1]

            def k_loop(jj, _):
                key_tile(i - jj, False)
                return 0

            key_tile(i, True)
            lax.fori_loop(1, i + 1, k_loop, 0)
            o_ref[0, rows, :] = acc_s[...]
            r_ref[0, 0, rows, :] = c_s[...]
            return 0

        lax.fori_loop(0, nb, q_loop, 0)

    return _call(
        body, name=name, grid=(bsz, heads),
        in_specs=[pl.BlockSpec((1, s, 3 * hd), lambda b, h: (b, 0, h)),
                  pl.BlockSpec((1, hd), lambda b, h: (0, 0)), pl.BlockSpec((1, hd), lambda b, h: (0, 0))],
        out_specs=[pl.BlockSpec((1, s, hd), lambda b, h: (b, 0, h)),
                   pl.BlockSpec((1, 1, s, 1), lambda b, h: (b, h, 0, 0))],
        out_shape=[jax.ShapeDtypeStruct((bsz, s, heads * hd), F32),
                   jax.ShapeDtypeStruct((bsz, heads, s, 1), F32)],
        scratch_shapes=[pltpu.VMEM((s, hd), BF16), pltpu.VMEM((s, hd), BF16), pltpu.VMEM((s, hd), BF16),
                        pltpu.VMEM((tile, hd), F32), pltpu.VMEM((tile, 1), F32)],
        sem=("parallel", "parallel"), args=(qkv, qg.reshape(1, hd), kg.reshape(1, hd)), rider=rider)


def _attention_bwd(qkv, qg, kg, rtot, do, name, tile=None, rider=None):
    bsz, s, w3 = qkv.shape
    hd = ATT_HEAD_DIM
    heads = w3 // (3 * hd)
    tile = min(tile or ATT_TILE, s)
    strip = min(ATT_STRIP, tile)
    nb = s // tile
    scale = hd ** -0.5

    def body(qkv_ref, qg_ref, kg_ref, r_ref, do_ref, dqkv_ref, dqg_ref, dkg_ref,
             qn_s, kn_s, vb_s, dob_s, dqn_s, dkn_s, dv_s, c1_s, c2_s, wb_s, dzb_s):
        qn_s[...] = _rms_fwd(qkv_ref[0, :, 0:hd], qg_ref[...]).astype(BF16)
        kn_s[...] = _rms_fwd(qkv_ref[0, :, hd:2 * hd], kg_ref[...]).astype(BF16)
        vb_s[...] = qkv_ref[0, :, 2 * hd:3 * hd].astype(BF16)
        dob_s[...] = do_ref[0].astype(BF16)
        dqn_s[...] = jnp.zeros_like(dqn_s)
        dkn_s[...] = jnp.zeros_like(dkn_s)
        dv_s[...] = jnp.zeros_like(dv_s)
        upto = _tri(LANES, lambda r, c: r <= c)
        before = _tri(LANES, lambda r, c: r < c)

        def q_loop(i, _):
            rows = pl.ds(pl.multiple_of(i * tile, tile), tile)
            c1_s[...] = jnp.zeros_like(c1_s)
            c2_s[...] = jnp.zeros_like(c2_s)

            def key_tile(j, diagonal):
                cols = pl.ds(pl.multiple_of(j * tile, tile), tile)
                kj, vj = kn_s[cols, :], vb_s[cols, :]
                strips = range(tile // strip)
                subs = [slice(r * strip, (r + 1) * strip) for r in strips]
                srows = [pl.ds(pl.multiple_of(i * tile + r * strip, strip), strip) for r in strips]
                sc = [_att_scores(qn_s[srows[r], :], kj, scale, r * strip if diagonal else None) for r in strips]
                dw = [_dot(dob_s[srows[r], :], vj, NT) for r in strips]
                upto_lr = [_scan_lanes(sc[r][1], upto, 2) for r in strips]
                w = [_masked(sc[r][2], jnp.exp(sc[r][0] + (r_ref[0, 0, srows[r], :] - (upto_lr[r][0] + c1_s[subs[r], :]))))
                     for r in strips]
                e = [w[r] * dw[r] for r in strips]
                pre = [_scan_lanes(e[r], before, 1) for r in strips]
                dz = [_masked(sc[r][2], (e[r] - jnp.exp(sc[r][0]) * (e[r] + (pre[r][0] + c2_s[subs[r], :]))) * scale)
                      for r in strips]
                for r in strips:
                    wb_s[subs[r], :] = w[r].astype(BF16)
                    dzb_s[subs[r], :] = dz[r].astype(BF16)
                    c1_s[subs[r], :] += upto_lr[r][1]
                    c2_s[subs[r], :] += pre[r][1]
                dqn_s[rows, :] += _dot(dzb_s[...], kj)
                dv_s[cols, :] += _dot(wb_s[...], dob_s[rows, :], TN)
                dkn_s[cols, :] += _dot(dzb_s[...], qn_s[rows, :], TN)

            def k_loop(j, _):
                key_tile(j, False)
                return 0

            lax.fori_loop(0, i, k_loop, 0)
            key_tile(i, True)
            return 0

        lax.fori_loop(0, nb, q_loop, 0)
        dq, dgq = _rms_bwd(qkv_ref[0, :, 0:hd], qg_ref[...], dqn_s[...])
        dk, dgk = _rms_bwd(qkv_ref[0, :, hd:2 * hd], kg_ref[...], dkn_s[...])
        dqkv_ref[0, :, 0:hd] = dq.astype(BF16)
        dqkv_ref[0, :, hd:2 * hd] = dk.astype(BF16)
        dqkv_ref[0, :, 2 * hd:3 * hd] = dv_s[...].astype(BF16)
        dqg_ref[0, 0] = jnp.sum(dgq, axis=0, keepdims=True)
        dkg_ref[0, 0] = jnp.sum(dgk, axis=0, keepdims=True)

    gain = pl.BlockSpec((1, hd), lambda b, h: (0, 0))
    dgain = pl.BlockSpec((1, 1, 1, hd), lambda b, h: (b, h, 0, 0))
    return _call(
        body, name=name, grid=(bsz, heads),
        in_specs=[pl.BlockSpec((1, s, 3 * hd), lambda b, h: (b, 0, h)), gain, gain,
                  pl.BlockSpec((1, 1, s, 1), lambda b, h: (b, h, 0, 0)),
                  pl.BlockSpec((1, s, hd), lambda b, h: (b, 0, h))],
        out_specs=[pl.BlockSpec((1, s, 3 * hd), lambda b, h: (b, 0, h)), dgain, dgain],
        out_shape=[jax.ShapeDtypeStruct((bsz, s, w3), BF16),
                   jax.ShapeDtypeStruct((bsz, heads, 1, hd), F32),
                   jax.ShapeDtypeStruct((bsz, heads, 1, hd), F32)],
        scratch_shapes=[pltpu.VMEM((s, hd), BF16)] * 4 + [pltpu.VMEM((s, hd), F32)] * 3
        + [pltpu.VMEM((tile, 1), F32)] * 2 + [pltpu.VMEM((tile, tile), BF16)] * 2,
        sem=("parallel", "parallel"), args=(qkv, qg.reshape(1, hd), kg.reshape(1, hd), rtot, do), rider=rider)


def _conv_pre(pad_ref, w_ref, b_ref, s):
    pre = b_ref[...]
    for i in range(CONV_WIDTH):
        off = SUBLANES - (CONV_WIDTH - 1) + i
        pre = pre + pad_ref[off:off + s, :] * w_ref[i:i + 1, :]
    return pre


def _conv_fwd(u, w, b, name, tc=256):
    bsz, s, c = u.shape
    tc = min(tc, c)

    def body(u_ref, w_ref, b_ref, a_ref, pad_s):
        pad_s[0:SUBLANES, :] = jnp.zeros((SUBLANES, tc), F32)
        pad_s[SUBLANES:SUBLANES + s, :] = u_ref[0]
        pre = _conv_pre(pad_s, w_ref, b_ref, s)
        a_ref[0] = pre * _sigmoid(pre)

    return pl.pallas_call(
        body, name=name, grid=(bsz, c // tc),
        in_specs=[pl.BlockSpec((1, s, tc), lambda i, j: (i, 0, j)),
                  pl.BlockSpec((CONV_WIDTH, tc), lambda i, j: (0, j)), pl.BlockSpec((1, tc), lambda i, j: (0, j))],
        out_specs=pl.BlockSpec((1, s, tc), lambda i, j: (i, 0, j)),
        out_shape=jax.ShapeDtypeStruct((bsz, s, c), F32),
        scratch_shapes=[pltpu.VMEM((s + SUBLANES, tc), F32)],
        compiler_params=_params(("parallel", "parallel")),
    )(u, w, b.reshape(1, c))


def _conv_bwd(u, w, b, da, name, tc=256):
    bsz, s, c = u.shape
    tc = min(tc, c)

    def body(u_ref, w_ref, b_ref, da_ref, du_ref, dw_ref, db_ref, pad_s, gpad_s):
        @pl.when(pl.program_id(1) == 0)
        def _():
            dw_ref[...] = jnp.zeros_like(dw_ref)
            db_ref[...] = jnp.zeros_like(db_ref)

        pad_s[0:SUBLANES, :] = jnp.zeros((SUBLANES, tc), F32)
        pad_s[SUBLANES:SUBLANES + s, :] = u_ref[0]
        pre = _conv_pre(pad_s, w_ref, b_ref, s)
        sg = _sigmoid(pre)
        dpre = da_ref[0] * (sg * (1.0 + pre * (1.0 - sg)))
        gpad_s[0:s, :] = dpre
        gpad_s[s:s + SUBLANES, :] = jnp.zeros((SUBLANES, tc), F32)
        du = jnp.zeros((s, tc), F32)
        for i in range(CONV_WIDTH):
            back = CONV_WIDTH - 1 - i
            du = du + gpad_s[back:back + s, :] * w_ref[i:i + 1, :]
            off = SUBLANES - (CONV_WIDTH - 1) + i
            dw_ref[i:i + 1, :] += jnp.sum(dpre * pad_s[off:off + s, :], axis=0, keepdims=True)
        du_ref[0] = du.astype(BF16)
        db_ref[...] += jnp.sum(dpre, axis=0, keepdims=True)

    blk = pl.BlockSpec((1, s, tc), lambda j, i: (i, 0, j))
    du, dw, db = pl.pallas_call(
        body, name=name, grid=(c // tc, bsz),
        in_specs=[blk, pl.BlockSpec((CONV_WIDTH, tc), lambda j, i: (0, j)),
                  pl.BlockSpec((1, tc), lambda j, i: (0, j)), blk],
        out_specs=[blk, pl.BlockSpec((CONV_WIDTH, tc), lambda j, i: (0, j)),
                   pl.BlockSpec((1, tc), lambda j, i: (0, j))],
        out_shape=[jax.ShapeDtypeStruct((bsz, s, c), BF16), jax.ShapeDtypeStruct((CONV_WIDTH, c), F32),
                   jax.ShapeDtypeStruct((1, c), F32)],
        scratch_shapes=[pltpu.VMEM((s + SUBLANES, tc), F32), pltpu.VMEM((s + SUBLANES, tc), F32)],
        compiler_params=_params(("parallel", "arbitrary")),
    )(u, w, b.reshape(1, c), da)
    return du, dw, db.reshape(c)


def _ssd_chunk_common(b_ref, c_ref, dt_ref, dtb_ref, alog_ref):
    bm, cm = b_ref[0], c_ref[0]
    draw = dt_ref[0] + dtb_ref[...]
    dt = _softplus(draw)
    a_row = -jnp.exp(alog_ref[...])
    da = dt * a_row
    n = SSD_CHUNK
    acum = _dot_exact(da, _tri(n, lambda r, c: r >= c), ones_left=True)
    acum_t = _dot_exact(da, _tri(n, lambda r, c: r <= c), dims=TN)
    cb = _dot(cm, bm, NT)
    return bm, cm, draw, dt, a_row, acum, acum_t, cb


def _row_totals(v):
    return _dot_exact(v, jnp.ones((v.shape[1], LANES), BF16), passes=2)


def _ssd_head_common(acum, acum_t, dt, cb, x, i):
    n, p = SSD_CHUNK, SSM_HEAD_DIM
    pick = (_iota2((LANES, LANES), 0) == i).astype(BF16)
    acol = _dot_exact(acum, pick)
    dtc = _dot_exact(dt, pick)[:, :p]
    arow = acum_t[i:i + 1, :]
    causal = _iota2((n, n), 0) >= _iota2((n, n), 1)
    lm = jnp.where(causal, jnp.exp(jnp.where(causal, acol - arow, 0.0)), 0.0)
    gm = cb * lm
    xh = x[:, i * p:(i + 1) * p]
    xdt = xh * dtc
    alast = acol[n - 1:n, :]
    dte = jnp.exp(alast - acol)
    return acol, lm, gm, dtc, xh, xdt, alast, dte


def _ssd_specs(s, wg, hg, rev):
    g, n, cl = SSM_GROUPS, SSM_STATE, SSD_CHUNK
    nc = s // cl
    boff, coff = (g * wg) // n, (g * wg) // n + g
    ci = (lambda c: nc - 1 - c) if rev else (lambda c: c)
    xblk = pl.BlockSpec((1, cl, wg), lambda b, k, c: (b, ci(c), k))
    bblk = pl.BlockSpec((1, cl, n), lambda b, k, c: (b, ci(c), boff + k))
    cblk = pl.BlockSpec((1, cl, n), lambda b, k, c: (b, ci(c), coff + k))
    nblk = pl.BlockSpec((1, cl, n), lambda b, k, c: (b, ci(c), k))
    dtblk = pl.BlockSpec((1, cl, LANES), lambda b, k, c: (b, ci(c), k))
    vec = pl.BlockSpec((1, LANES), lambda b, k, c: (0, k))
    hsblk = pl.BlockSpec((1, 1, 1, wg, n), lambda b, k, c: (b, k, ci(c), 0, 0))
    return nc, xblk, bblk, cblk, nblk, dtblk, vec, hsblk


def _ssd_fwd(xbc, dtraw, dtb, alog, dskip, hg, name, rider=None):
    bsz, s, _ = xbc.shape
    g, n, p = SSM_GROUPS, SSM_STATE, SSM_HEAD_DIM
    wg = hg * p
    nc, xblk, bblk, cblk, _, dtblk, vec, hsblk = _ssd_specs(s, wg, hg, False)

    def body(x_ref, b_ref, c_ref, dt_ref, dtb_ref, alog_ref, dsk_ref, y_ref, hs_ref, h_s):
        @pl.when(pl.program_id(2) == 0)
        def _():
            h_s[...] = jnp.zeros_like(h_s)

        bm, cm, _, dt, _, acum, acum_t, cb = _ssd_chunk_common(b_ref, c_ref, dt_ref, dtb_ref, alog_ref)
        x = x_ref[0]
        hs_ref[0, 0, 0] = h_s[...]
        hd_ = range(hg)
        hc = [_ssd_head_common(acum, acum_t, dt, cb, x, i) for i in hd_]
        hprev = [h_s[i * p:(i + 1) * p, :] for i in hd_]
        ydiag = [_dot(hc[i][2], hc[i][5]) for i in hd_]
        yoff = [_dot(cm, hprev[i], NT) for i in hd_]
        st = [_dot(hc[i][5] * hc[i][7][:, :p], bm, TN) for i in hd_]
        for i in hd_:
            acol, _, _, _, xh, _, alast, _ = hc[i]
            y_ref[0, :, i * p:(i + 1) * p] = ydiag[i] + yoff[i] * jnp.exp(acol[:, :p]) + xh * dsk_ref[:, i:i + 1]
            h_s[i * p:(i + 1) * p, :] = hprev[i] * jnp.exp(alast) + st[i]

    return _call(
        body, name=name, grid=(bsz, g, nc),
        in_specs=[xblk, bblk, cblk, dtblk, vec, vec, vec],
        out_specs=[xblk, hsblk],
        out_shape=[jax.ShapeDtypeStruct((bsz, s, g * wg), F32),
                   jax.ShapeDtypeStruct((bsz, g, nc, wg, n), F32)],
        scratch_shapes=[pltpu.VMEM((wg, n), F32)],
        sem=("parallel", "parallel", "arbitrary"), args=(xbc, xbc, xbc, dtraw, dtb, alog, dskip), rider=rider)


def _ssd_bwd(xbc, dtraw, dtb, alog, dskip, hs, dy, hg, name, rider=None):
    bsz, s, _ = xbc.shape
    g, n, p, cl = SSM_GROUPS, SSM_STATE, SSM_HEAD_DIM, SSD_CHUNK
    wg = hg * p
    nc, xblk, bblk, cblk, nblk, dtblk, vec, hsblk = _ssd_specs(s, wg, hg, True)

    def body(x_ref, b_ref, c_ref, dt_ref, dtb_ref, alog_ref, dsk_ref, hs_ref, dy_ref,
             dx_ref, db_ref, dc_ref, ddt_ref, dvec_ref, dh_s):
        @pl.when(pl.program_id(2) == 0)
        def _():
            dh_s[...] = jnp.zeros_like(dh_s)
            dvec_ref[...] = jnp.zeros_like(dvec_ref)

        lane = _iota2((cl, LANES), 1)
        sub = _iota2((LANES, cl), 0)
        lane1 = _iota2((1, LANES), 1)
        last_row = _iota2((cl, 1), 0) == cl - 1
        bm, cm, draw, dt, a_row, acum, acum_t, cb = _ssd_chunk_common(b_ref, c_ref, dt_ref, dtb_ref, alog_ref)
        x = x_ref[0]
        dyc = dy_ref[0]
        hd_ = range(hg)
        hc = [_ssd_head_common(acum, acum_t, dt, cb, x, i) for i in hd_]
        dyh = [dyc[:, i * p:(i + 1) * p] for i in hd_]
        hprev = [hs_ref[0, 0, 0, i * p:(i + 1) * p, :] for i in hd_]
        dhn = [dh_s[i * p:(i + 1) * p, :] for i in hd_]
        ea = [jnp.exp(hc[i][0]) for i in hd_]
        cd = [jnp.exp(hc[i][6]) for i in hd_]
        y0 = [_dot(cm, hprev[i], NT) for i in hd_]
        dxe = [_dot(bm, dhn[i], NT) for i in hd_]
        dgm = [_dot(dyh[i], hc[i][5], NT) for i in hd_]
        gdy = [_dot(hc[i][2], dyh[i], TN) for i in hd_]
        dy0 = [dyh[i] * ea[i][:, :p] for i in hd_]
        dcm_h = [_dot(dy0[i], hprev[i]) for i in hd_]
        dh_new = [_dot(dy0[i], cm, TN) + dhn[i] * cd[i] for i in hd_]
        dbm_h = [_dot(hc[i][5] * hc[i][7][:, :p], dhn[i]) for i in hd_]
        ws = [dgm[i] * hc[i][2] for i in hd_]
        dxdt = [dxe[i] * hc[i][7][:, :p] + gdy[i] for i in hd_]
        s_y0 = [_row_totals(dyh[i] * y0[i]) for i in hd_]
        s_xe = [_row_totals(dxe[i] * hc[i][5]) for i in hd_]
        s_ws = [_row_totals(ws[i]) for i in hd_]
        s_dt = [_row_totals(dxdt[i] * hc[i][4]) for i in hd_]
        s_dd = [_row_totals(dyh[i] * hc[i][4]) for i in hd_]
        s_hh = [_row_totals(dhn[i] * hprev[i]) for i in hd_]
        dcb = jnp.zeros((cl, cl), F32)
        dcm = jnp.zeros((cl, n), F32)
        dbm = jnp.zeros((cl, n), F32)
        da_col = jnp.zeros((cl, LANES), F32)
        da_row = jnp.zeros((LANES, cl), F32)
        ddt = jnp.zeros((cl, LANES), F32)
        dd = jnp.zeros((1, LANES), F32)
        for i in hd_:
            _, lm, _, dtc, _, _, _, dte = hc[i]
            dh_s[i * p:(i + 1) * p, :] = dh_new[i]
            dd = dd + jnp.where(lane1 == i, jnp.sum(s_dd[i], axis=0, keepdims=True), 0.0)
            t1 = s_xe[i] * dte
            d_alast = jnp.sum(s_hh[i], axis=0, keepdims=True) * cd[i] + jnp.sum(t1, axis=0, keepdims=True)
            dacol = s_y0[i] * ea[i] - t1 + s_ws[i] + jnp.where(last_row, d_alast, 0.0)
            dcb = dcb + dgm[i] * lm
            dcm = dcm + dcm_h[i]
            dbm = dbm + dbm_h[i]
            dx_ref[0, :, i * p:(i + 1) * p] = dxdt[i] * dtc + dyh[i] * dsk_ref[:, i:i + 1]
            da_col = jnp.where(lane == i, dacol, da_col)
            da_row = jnp.where(sub == i, -jnp.sum(ws[i], axis=0, keepdims=True), da_row)
            ddt = jnp.where(lane == i, s_dt[i], ddt)
        dc_ref[0] = dcm + _dot(dcb, bm)
        db_ref[0] = dbm + _dot(dcb, cm, TN)
        upper = _tri(cl, lambda r, k: r <= k)
        dda = _dot_exact(da_col, upper, ones_left=True) + _dot_exact(da_row, upper, dims=NT, ones_left=True)
        ddt = ddt + dda * a_row
        ddraw = ddt * _sigmoid(draw)
        ddt_ref[0] = ddraw.astype(BF16)
        dvec_ref[0, 0, 0:1, :] += jnp.sum(ddraw, axis=0, keepdims=True)
        dvec_ref[0, 0, 1:2, :] += jnp.sum(dda * dt, axis=0, keepdims=True) * a_row
        dvec_ref[0, 0, 2:3, :] += dd

    return _call(
        body, name=name, grid=(bsz, g, nc),
        in_specs=[xblk, bblk, cblk, dtblk, vec, vec, vec, hsblk, xblk],
        out_specs=[xblk, nblk, nblk, dtblk,
                   pl.BlockSpec((1, 1, SUBLANES, LANES), lambda b, k, c: (b, k, 0, 0))],
        out_shape=[jax.ShapeDtypeStruct((bsz, s, g * wg), F32), jax.ShapeDtypeStruct((bsz, s, g * n), F32),
                   jax.ShapeDtypeStruct((bsz, s, g * n), F32), jax.ShapeDtypeStruct((bsz, s, g * LANES), BF16),
                   jax.ShapeDtypeStruct((bsz, g, SUBLANES, LANES), F32)],
        scratch_shapes=[pltpu.VMEM((wg, n), F32)],
        sem=("parallel", "parallel", "arbitrary"), args=(xbc, xbc, xbc, dtraw, dtb, alog, dskip, hs, dy),
        rider=rider)


def _coords():
    return lax.axis_index("x"), lax.axis_index("y"), lax.axis_index("c")


def _other_chips(x, y):
    return [(1 - x, y), (x, 1 - y), (1 - x, 1 - y)]


def _remote(src, dst, send_sems, recv_sems, k, to):
    return pltpu.make_async_remote_copy(src_ref=src, dst_ref=dst, send_sem=send_sems.at[k],
                                        recv_sem=recv_sems.at[k], device_id=to, device_id_type=MESH)


def _standalone(rider, name):
    r_in, r_out, n_sems = len(rider["ins"]), len(rider["outs"]), rider["n_sems"]

    def body(*refs):
        rins, routs, (send_sems, recv_sems) = refs[:r_in], refs[r_in:r_in + r_out], refs[r_in + r_out:]
        cps = rider["copies"](rins, routs, send_sems, recv_sems)
        for cp in cps:
            cp.start()
        for cp in cps:
            cp.wait()

    res = pl.pallas_call(
        body, name=name, in_specs=[ANY] * r_in, out_specs=[ANY] * r_out, out_shape=list(rider["outs"]),
        scratch_shapes=[pltpu.SemaphoreType.DMA((n_sems,)), pltpu.SemaphoreType.DMA((n_sems,))],
        input_output_aliases=dict(rider["aliases"]),
    )(*rider["ins"])
    return list(res)


def _gather_chips_rider(shards):
    def copies(rins, routs, send_sems, recv_sems):
        x, y, c = _coords()
        me = 2 * x + y
        cps = []
        for q, (w_ref, o_ref) in enumerate(zip(rins, routs, strict=True)):
            rh = w_ref.shape[0] // 2
            rows = pl.ds(c * rh, rh)
            for k, (px, py) in enumerate(_other_chips(x, y)):
                cps.append(_remote(w_ref.at[rows], o_ref.at[me, rows], send_sems, recv_sems, 3 * q + k, (px, py, c)))
        return cps

    return dict(ins=list(shards), outs=[jax.ShapeDtypeStruct((N_CHIPS,) + w.shape, w.dtype) for w in shards],
                aliases={}, n_sems=3 * len(shards), copies=copies)


def _gather_pair_rider(gathered):
    def copies(rins, routs, send_sems, recv_sems):
        x, y, c = _coords()
        cps = []
        for q, o_ref in enumerate(routs):
            rh = o_ref.shape[1] // 2
            for k, (px, py) in enumerate(_other_chips(x, y)):
                part = o_ref.at[2 * px + py, pl.ds(c * rh, rh)]
                cps.append(_remote(part, part, send_sems, recv_sems, 3 * q + k, (x, y, 1 - c)))
        return cps

    return dict(ins=list(gathered), outs=[jax.ShapeDtypeStruct(g.shape, g.dtype) for g in gathered],
                aliases={i: i for i in range(len(gathered))}, n_sems=3 * len(gathered), copies=copies)


def _reduce_pair_rider(grads):
    def copies(rins, routs, send_sems, recv_sems):
        x, y, c = _coords()
        cps = []
        for q, (g_ref, r_ref) in enumerate(zip(rins, routs, strict=True)):
            rh = g_ref.shape[1] // 2
            cps.append(_remote(g_ref.at[:, pl.ds((1 - c) * rh, rh)], r_ref, send_sems, recv_sems, q, (x, y, 1 - c)))
        return cps

    return dict(ins=list(grads),
                outs=[jax.ShapeDtypeStruct((g.shape[0], g.shape[1] // 2, g.shape[2]), g.dtype) for g in grads],
                aliases={}, n_sems=len(grads), copies=copies)


def _reduce_chips_rider(pair_sums):
    def copies(rins, routs, send_sems, recv_sems):
        x, y, c = _coords()
        cps = []
        for q, (p_ref, r_ref) in enumerate(zip(rins, routs, strict=True)):
            for k, (px, py) in enumerate(_other_chips(x, y)):
                cps.append(_remote(p_ref.at[2 * px + py], r_ref.at[k], send_sems, recv_sems, 3 * q + k, (px, py, c)))
        return cps

    return dict(ins=list(pair_sums), outs=[jax.ShapeDtypeStruct((3,) + p.shape[1:], p.dtype) for p in pair_sums],
                aliases={}, n_sems=3 * len(pair_sums), copies=copies)


def _reduce_finish_rider(sums, totals, layer, depth):
    fresh = totals is None

    def copies(rins, routs, send_sems, recv_sems):
        x, y, c = _coords()
        cps = []
        for q, (f_ref, o_ref) in enumerate(zip(rins[:len(sums)], routs, strict=True)):
            rh = f_ref.shape[0]
            cps.append(_remote(f_ref, o_ref.at[layer, pl.ds(c * rh, rh)], send_sems, recv_sems, q, (x, y, 1 - c)))
        return cps

    ins = list(sums) + ([] if fresh else list(totals))
    outs = [jax.ShapeDtypeStruct((depth, 2 * f.shape[0], f.shape[1]), F32) for f in sums]
    return dict(ins=ins, outs=outs, aliases={} if fresh else {len(sums) + i: i for i in range(len(sums))},
                n_sems=len(sums), copies=copies)


def _pair_add(gj, r1, c_idx, name, tr=256):
    nj, r, c = gj.shape
    rh = r // 2
    tr = _fit(tr, rh, 16)
    nt = rh // tr

    def body(c_ref, g_ref, r_ref, p_ref, pb_ref):
        s = g_ref[...] + r_ref[...]
        p_ref[...] = s
        pb_ref[...] = s.astype(BF16)

    blk_r = pl.BlockSpec((1, tr, c), lambda j, i, cr: (j, i, 0))
    blk_g = pl.BlockSpec((1, tr, c), lambda j, i, cr: (j, cr[0] * nt + i, 0))
    return pl.pallas_call(
        body, name=name,
        grid_spec=pltpu.PrefetchScalarGridSpec(
            num_scalar_prefetch=1, grid=(nj, nt), in_specs=[blk_g, blk_r], out_specs=[blk_r, blk_r]),
        out_shape=[jax.ShapeDtypeStruct((nj, rh, c), F32), jax.ShapeDtypeStruct((nj, rh, c), BF16)],
        compiler_params=_params(("parallel", "parallel")),
    )(c_idx, gj, r1)


def _chip_add(p, r2, chip_idx, name, tr=256):
    _, rh, c = p.shape
    tr = _fit(tr, rh, 16)

    def body(j_ref, o_ref, r_ref, f_ref):
        f_ref[...] = ((o_ref[0] + r_ref[0].astype(F32)) + r_ref[1].astype(F32)) + r_ref[2].astype(F32)

    return pl.pallas_call(
        body, name=name,
        grid_spec=pltpu.PrefetchScalarGridSpec(
            num_scalar_prefetch=1, grid=(rh // tr,),
            in_specs=[pl.BlockSpec((1, tr, c), lambda i, jr: (jr[0], i, 0)),
                      pl.BlockSpec((3, tr, c), lambda i, jr: (0, i, 0))],
            out_specs=pl.BlockSpec((tr, c), lambda i, jr: (i, 0))),
        out_shape=jax.ShapeDtypeStruct((rh, c), F32),
        compiler_params=_params(("parallel",)),
    )(chip_idx, p, r2)


BIG = ["w_in", "w_out", "w_gate", "w_up", "w_down"]
WITH_ATTENTION = ["w_in", "w_gate", "w_up"]
WITH_SSD = ["w_out", "w_down"]


def _set_own_slot(gathered, shard):
    my_chip = 2 * lax.axis_index("x") + lax.axis_index("y")
    return lax.dynamic_update_slice(gathered, shard[None], (my_chip, 0, 0))


class _GatherPlan:
    def __init__(self, shards):
        self.shards, self.parts = shards, {}

    def rider(self, host):
        if self.shards is None:
            return None
        if host == "ffn_gate_up":
            return _gather_pair_rider([self.parts[n] for n in BIG])
        names = WITH_ATTENTION if host == "attention_fwd" else WITH_SSD
        return _gather_chips_rider([self.shards[n] for n in names])

    def collect(self, host, outs):
        if self.shards is None:
            return
        names = BIG if host == "ffn_gate_up" else WITH_ATTENTION if host == "attention_fwd" else WITH_SSD
        self.parts.update(zip(names, outs, strict=True))

    def gathered(self):
        return {n: _set_own_slot(self.parts[n], self.shards[n]) for n in BIG}


def _gather_now(shards, tag):
    plan = _GatherPlan(shards)
    parts = _standalone(_gather_chips_rider([shards[n] for n in BIG]), f"allgather_chips_{tag}")
    plan.parts = dict(zip(BIG, _standalone(_gather_pair_rider(parts), f"allgather_pair_{tag}"), strict=True))
    return plan.gathered()


class _ReducePlan:
    def __init__(self, grads, layer, totals, depth, c_idx, chip_idx):
        self.grads, self.layer, self.totals, self.depth = grads, layer, totals, depth
        self.c_idx, self.chip_idx = c_idx, chip_idx
        self.p, self.pb, self.r2, self.f = {}, {}, {}, {}

    def rider(self, host):
        if self.grads is None:
            return None
        if host == "ffn_down_dgrad":
            return _reduce_pair_rider([self.grads[n] for n in BIG])
        if host == "proj_in_dgrad":
            tot = None if self.totals is None else [self.totals[n] for n in BIG]
            return _reduce_finish_rider([self.f[n] for n in BIG], tot, self.layer, self.depth)
        names = WITH_SSD if host == "ssd_bwd" else WITH_ATTENTION
        return _reduce_chips_rider([self.pb[n] for n in names])

    def collect(self, host, outs):
        if self.grads is None:
            return
        tag = f"layer{self.layer}"
        if host == "ffn_down_dgrad":
            for n, r1 in zip(BIG, outs, strict=True):
                self.p[n], self.pb[n] = _pair_add(self.grads[n], r1, self.c_idx, f"rs_pair_add_{n}_{tag}")
        elif host == "proj_in_dgrad":
            c = lax.axis_index("c")
            self.totals = {n: lax.dynamic_update_slice(t, self.f[n][None], (self.layer, c * self.f[n].shape[0], 0))
                           for n, t in zip(BIG, outs, strict=True)}
        else:
            names = WITH_SSD if host == "ssd_bwd" else WITH_ATTENTION
            for n, r2 in zip(names, outs, strict=True):
                self.f[n] = _chip_add(self.p[n], r2, self.chip_idx, f"rs_chip_add_{n}_{tag}")

    def run_now(self):
        tag = f"layer{self.layer}"
        self.collect("ffn_down_dgrad", _standalone(self.rider("ffn_down_dgrad"), f"rs_pair_{tag}"))
        self.collect("ssd_bwd", _standalone(self.rider("ssd_bwd"), f"rs_chips_a_{tag}"))
        self.collect("attention_bwd", _standalone(self.rider("attention_bwd"), f"rs_chips_b_{tag}"))
        self.collect("proj_in_dgrad", _standalone(self.rider("proj_in_dgrad"), f"rs_finish_{tag}"))
        return self.totals


def _small_exchange(v, name, reduce):
    rows = v.shape[0]

    def body(v_ref, o_ref, *rest):
        buf = rest[0] if reduce else o_ref
        send_sems, recv_sems = rest[-2], rest[-1]
        x, y, c = _coords()
        me = 4 * x + 2 * y + c
        buf[me] = v_ref[...]
        cps = []
        for r in range(1, N_DEV):
            peer = (lax.bitwise_xor(x, (r >> 2) & 1), lax.bitwise_xor(y, (r >> 1) & 1), lax.bitwise_xor(c, r & 1))
            cps.append(_remote(v_ref, buf.at[me], send_sems, recv_sems, r - 1, peer))
        for cp in cps:
            cp.start()
        for r in range(1, N_DEV):
            src = buf.at[lax.bitwise_xor(me, r)]
            _remote(src, src, send_sems, recv_sems, r - 1, (x, y, c)).wait_recv()
        for cp in cps:
            cp.wait_send()
        if reduce:
            acc = buf[0]
            for d in range(1, N_DEV):
                acc = acc + buf[d]
            o_ref[...] = acc
            o_ref[0:1, :] = jnp.broadcast_to(jnp.sum(acc[0:1, :], axis=1, keepdims=True), (1, LANES))

    scratch = [pltpu.SemaphoreType.DMA((N_DEV - 1,)), pltpu.SemaphoreType.DMA((N_DEV - 1,))]
    if reduce:
        scratch = [pltpu.VMEM((N_DEV, rows, LANES), F32)] + scratch
    out_shape = (rows, LANES) if reduce else (N_DEV, rows, LANES)
    return pl.pallas_call(
        body, name=name, in_specs=[VMEM], out_specs=VMEM,
        out_shape=jax.ShapeDtypeStruct(out_shape, F32), scratch_shapes=scratch,
    )(v)


def _pack(parts):
    flat = []
    for a in parts:
        a = a.reshape(-1)
        flat.append(jnp.pad(a, (0, (-a.shape[0]) % LANES)))
    v = jnp.concatenate(flat)
    v = jnp.pad(v, (0, (-v.shape[0]) % (SUBLANES * LANES)))
    return v.reshape(-1, LANES)


def _unpack(slab, shapes):
    flat = slab.reshape(-1)
    out, off = [], 0
    for shp in shapes:
        size = 1
        for d in shp:
            size *= d
        out.append(flat[off:off + size].reshape(shp))
        off += size + (-size) % LANES
    return out


def _group_slots(a, hg):
    lead = a.shape[:-1]
    a = a.reshape(lead + (SSM_GROUPS, hg))
    a = jnp.pad(a, [(0, 0)] * len(lead) + [(0, 0), (0, LANES - hg)])
    return a.reshape(lead + (SSM_GROUPS * LANES,))


def _ungroup_slots(a, hg):
    lead = a.shape[:-1]
    return a.reshape(lead + (SSM_GROUPS, LANES))[..., :hg].reshape(lead + (SSM_GROUPS * hg,))


def _layer_fwd(x, p, bsz, s, plan):
    t, d = x.shape
    aw, sw, cd, hg = p["aw"], p["sw"], p["cd"], p["hg"]
    dff = p["wg"].shape[1]
    h = _rmsnorm_fwd(x, p["norm_mix"], "norm_mix_fwd")
    qkv = _mm(h, p["wqkv"], "nn", t, 3 * aw, F32, "proj_qkv", tm=1024, tn=512)
    z = _mm(h, p["wz"], "nn", t, sw, F32, "proj_z", tm=1024, tn=512)
    xbc = _mm(h, p["wxbc"], "nn", t, cd, F32, "proj_xbc", tm=1024, tn=512)
    dtraw = _mm(h, p["wdt"], "nn", t, SSM_GROUPS * LANES, F32, "proj_dt", tm=1024, tn=SSM_GROUPS * LANES)
    qkv3 = qkv.reshape(bsz, s, 3 * aw)
    (o_att, rtot), sent = _attention_fwd(qkv3, p["q_gain"], p["k_gain"], "attention_fwd",
                                         rider=plan.rider("attention_fwd"))
    plan.collect("attention_fwd", sent)
    xbc3 = xbc.reshape(bsz, s, cd)
    xact = _conv_fwd(xbc3, p["conv_w"], p["conv_b"], "conv_fwd")
    dt3 = dtraw.reshape(bsz, s, SSM_GROUPS * LANES)
    (y, hs), sent = _ssd_fwd(xact, dt3, p["dt_bias"], p["a_log"], p["d_skip"], hg, "ssd_fwd",
                             rider=plan.rider("ssd_fwd"))
    plan.collect("ssd_fwd", sent)
    o2, y2 = o_att.reshape(t, aw), y.reshape(t, sw)
    mix = _merge_fwd(o2, y2, z, p["attn_out_gain"], p["ssm_out_gain"], "merge_fwd")
    x1 = _mm(mix, p["wout"], "nn", t, d, F32, "proj_out", tm=1024, tn=512, res=x)
    h2 = _rmsnorm_fwd(x1, p["norm_ffn"], "norm_ffn_fwd")
    (gate, up, act), sent = _hosted_matmul(
        [[(h2, p["wg"], "nn")], [(h2, p["wu"], "nn")]], [], _swiglu_fwd_epilogue,
        [F32, F32, BF16], t, dff, 512, 512, "ffn_gate_up", plan)
    x2 = _mm(act, p["wd"], "nn", t, d, F32, "ffn_down", tm=512, tn=512, res=x1)
    saved = dict(x=x, h=h, qkv3=qkv3, z=z, xbc3=xbc3, dt3=dt3, o2=o2, rtot=rtot, xact=xact, hs=hs, y2=y2,
                 mix=mix, x1=x1, h2=h2, gate=gate, up=up, act=act)
    return x2, saved


def _hosted_matmul(groups, extras, epilogue, out_dtypes, m, n, tm, tn, name, plan):
    rider = plan.rider(name)
    if rider is None:
        return _matmul(groups, extras, epilogue, out_dtypes, m, n, tm, tn, name), []
    outs, sent = _matmul(groups, extras, epilogue, out_dtypes, m, n, tm, tn, name, rider=rider)
    plan.collect(name, sent)
    return outs, sent


def _layer_bwd(dx2, dx2b, p, sv, bsz, s, plan):
    t, d = dx2.shape
    aw, sw, cd, hg = p["aw"], p["sw"], p["cd"], p["hg"]
    dff = p["wg"].shape[1]
    gr = {}
    (dgate, dup), _ = _hosted_matmul([[(dx2b, p["wd"], "nt")]], [sv["gate"], sv["up"]], _swiglu_bwd_epilogue,
                                     [BF16, BF16], t, dff, 512, 512, "ffn_down_dgrad", plan)
    gr["wd"] = _mm(sv["act"], dx2b, "tn", dff, d, F32, "ffn_down_wgrad")
    dh2 = _matmul([[(dgate, p["wg"], "nt"), (dup, p["wu"], "nt")]], [], lambda accs, ex: (accs[0],),
                  [F32], t, d, 512, 256, "ffn_gate_up_dgrad")[0]
    gr["wg"] = _mm(sv["h2"], dgate, "tn", d, dff, F32, "ffn_gate_wgrad")
    gr["wu"] = _mm(sv["h2"], dup, "tn", d, dff, F32, "ffn_up_wgrad")
    dx1, dx1b, gr["norm_ffn"] = _rmsnorm_bwd(sv["x1"], p["norm_ffn"], dh2, dx2, "norm_ffn_bwd")
    dmix = _mm(dx1b, p["wout"], "nt", t, aw + sw, F32, "proj_out_dgrad")
    gr["wout"] = _mm(sv["mix"], dx1b, "tn", aw + sw, d, F32, "proj_out_wgrad")
    do, dy, dz, gr["attn_out_gain"], gr["ssm_out_gain"] = _merge_bwd(
        sv["o2"], sv["y2"], sv["z"], p["attn_out_gain"], p["ssm_out_gain"], dmix, "merge_bwd")
    (dxs, dbm, dcm, ddt, dvec), sent = _ssd_bwd(sv["xact"], sv["dt3"], p["dt_bias"], p["a_log"], p["d_skip"],
                                                sv["hs"], dy.reshape(bsz, s, sw), hg, "ssd_bwd",
                                                rider=plan.rider("ssd_bwd"))
    plan.collect("ssd_bwd", sent)
    dvec = jnp.sum(dvec, axis=0).reshape(SSM_GROUPS, SUBLANES, LANES)
    gr["dt_bias"], gr["a_log"], gr["d_skip"] = (dvec[:, k, :hg].reshape(-1) for k in range(3))
    dxact = jnp.concatenate([dxs, dbm, dcm], axis=-1)
    dxbc, gr["conv_w"], gr["conv_b"] = _conv_bwd(sv["xbc3"], p["conv_w"], p["conv_b"], dxact, "conv_bwd")
    (dqkv, dqg, dkg), sent = _attention_bwd(sv["qkv3"], p["q_gain"], p["k_gain"], sv["rtot"],
                                            do.reshape(bsz, s, aw), "attention_bwd",
                                            rider=plan.rider("attention_bwd"))
    plan.collect("attention_bwd", sent)
    gr["q_gain"] = jnp.sum(dqg, axis=(0, 1, 2))
    gr["k_gain"] = jnp.sum(dkg, axis=(0, 1, 2))
    dqkv, dxbc, ddt = dqkv.reshape(t, 3 * aw), dxbc.reshape(t, cd), ddt.reshape(t, SSM_GROUPS * LANES)
    (dh,), _ = _hosted_matmul(
        [[(dqkv, p["wqkv"], "nt"), (dz, p["wz"], "nt"), (dxbc, p["wxbc"], "nt"), (ddt, p["wdt"], "nt")]],
        [], lambda accs, ex: (accs[0],), [F32], t, d, 512, 256, "proj_in_dgrad", plan)
    h = sv["h"]
    gr["wqkv"] = _mm(h, dqkv, "tn", d, 3 * aw, F32, "proj_qkv_wgrad")
    gr["wz"] = _mm(h, dz, "tn", d, sw, F32, "proj_z_wgrad")
    gr["wxbc"] = _mm(h, dxbc, "tn", d, cd, F32, "proj_xbc_wgrad")
    gr["wdt"] = _mm(h, ddt, "tn", d, SSM_GROUPS * LANES, F32, "proj_dt_wgrad", tn=SSM_GROUPS * LANES)
    dx, dxb, gr["norm_mix"] = _rmsnorm_bwd(sv["x"], p["norm_mix"], dh, dx1, "norm_mix_bwd")
    return dx, dxb, gr


SMALL = ["norm_mix", "q_gain", "k_gain", "conv_w", "conv_b", "dt_bias", "a_log", "d_skip",
         "attn_out_gain", "ssm_out_gain", "norm_ffn"]
ORDER = ["norm_mix", "w_in", "q_gain", "k_gain", "conv_w", "conv_b", "dt_bias", "a_log", "d_skip",
         "attn_out_gain", "ssm_out_gain", "w_out", "norm_ffn", "w_gate", "w_up", "w_down"]


def kernel(x, norm_mix, w_in, q_gain, k_gain, conv_w, conv_b, dt_bias, a_log, d_skip, attn_out_gain, ssm_out_gain, w_out, norm_ffn, w_gate, w_up, w_down, loss_target, m_norm_mix, m_w_in, m_q_gain, m_k_gain, m_conv_w, m_conv_b, m_dt_bias, m_a_log, m_d_skip, m_attn_out_gain, m_ssm_out_gain, m_w_out, m_norm_ffn, m_w_gate, m_w_up, m_w_down, v_norm_mix, v_w_in, v_q_gain, v_k_gain, v_conv_w, v_conv_b, v_dt_bias, v_a_log, v_d_skip, v_attn_out_gain, v_ssm_out_gain, v_w_out, v_norm_ffn, v_w_gate, v_w_up, v_w_down):
    w = dict(norm_mix=norm_mix, w_in=w_in, q_gain=q_gain, k_gain=k_gain, conv_w=conv_w, conv_b=conv_b,
             dt_bias=dt_bias, a_log=a_log, d_skip=d_skip, attn_out_gain=attn_out_gain, ssm_out_gain=ssm_out_gain,
             w_out=w_out, norm_ffn=norm_ffn, w_gate=w_gate, w_up=w_up, w_down=w_down)
    mom = dict(norm_mix=m_norm_mix, w_in=m_w_in, q_gain=m_q_gain, k_gain=m_k_gain, conv_w=m_conv_w,
               conv_b=m_conv_b, dt_bias=m_dt_bias, a_log=m_a_log, d_skip=m_d_skip,
               attn_out_gain=m_attn_out_gain, ssm_out_gain=m_ssm_out_gain, w_out=m_w_out, norm_ffn=m_norm_ffn,
               w_gate=m_w_gate, w_up=m_w_up, w_down=m_w_down)
    var = dict(norm_mix=v_norm_mix, w_in=v_w_in, q_gain=v_q_gain, k_gain=v_k_gain, conv_w=v_conv_w,
               conv_b=v_conv_b, dt_bias=v_dt_bias, a_log=v_a_log, d_skip=v_d_skip,
               attn_out_gain=v_attn_out_gain, ssm_out_gain=v_ssm_out_gain, w_out=v_w_out, norm_ffn=v_norm_ffn,
               w_gate=v_w_gate, w_up=v_w_up, w_down=v_w_down)

    bsz, s, d = x.shape
    t = bsz * s
    depth = norm_mix.shape[0]
    aw = attn_out_gain.shape[1]
    sw = ssm_out_gain.shape[1]
    cd = conv_b.shape[1]
    hs_n = dt_bias.shape[1]
    hg = hs_n // SSM_GROUPS
    heads = aw // ATT_HEAD_DIM
    in_dim = 3 * aw + sw + cd + hs_n
    dff = w_gate.shape[2] * N_CHIPS
    cs = conv_w.shape[2]
    my_chip = 2 * lax.axis_index("x") + lax.axis_index("y")
    c_idx = lax.axis_index("c").astype(jnp.int32).reshape(1)

    chip_idx = my_chip.astype(jnp.int32).reshape(1)
    wb = {n: w[n].astype(BF16) for n in BIG}
    conv_all = _small_exchange(_pack([conv_w]), "allgather_conv_w", False)
    conv_full = jnp.concatenate(
        [_unpack(conv_all[2 * j], [conv_w.shape])[0] for j in range(N_CHIPS)], axis=-1)

    def layer_params(l, gat):
        win = jnp.transpose(gat["w_in"], (1, 0, 2)).reshape(d, in_dim)
        wqkv = win[:, :3 * aw].reshape(d, 3, heads, ATT_HEAD_DIM)
        wqkv = jnp.transpose(wqkv, (0, 2, 1, 3)).reshape(d, 3 * aw)
        return dict(aw=aw, sw=sw, cd=cd, hg=hg, norm_mix=norm_mix[l], wqkv=wqkv, wz=win[:, 3 * aw:3 * aw + sw],
                    wxbc=win[:, 3 * aw + sw:3 * aw + sw + cd], wdt=_group_slots(win[:, 3 * aw + sw + cd:], hg),
                    q_gain=q_gain[l], k_gain=k_gain[l], conv_w=conv_full[l], conv_b=conv_b[l],
                    dt_bias=_group_slots(dt_bias[l], hg).reshape(1, -1),
                    a_log=_group_slots(a_log[l], hg).reshape(1, -1),
                    d_skip=_group_slots(d_skip[l], hg).reshape(1, -1),
                    attn_out_gain=attn_out_gain[l], ssm_out_gain=ssm_out_gain[l],
                    wout=gat["w_out"].reshape(aw + sw, d), norm_ffn=norm_ffn[l],
                    wg=jnp.transpose(gat["w_gate"], (1, 0, 2)).reshape(d, dff),
                    wu=jnp.transpose(gat["w_up"], (1, 0, 2)).reshape(d, dff), wd=gat["w_down"].reshape(dff, d))

    def per_chip(gr):
        gqkv = gr["wqkv"].reshape(d, heads, 3, ATT_HEAD_DIM)
        gqkv = jnp.transpose(gqkv, (0, 2, 1, 3)).reshape(d, 3 * aw)
        gin = jnp.concatenate([gqkv, gr["wz"], gr["wxbc"], _ungroup_slots(gr["wdt"], hg)], axis=-1)
        return {"w_in": jnp.transpose(gin.reshape(d, N_CHIPS, in_dim // N_CHIPS), (1, 0, 2)),
                "w_out": gr["wout"].reshape(N_CHIPS, (aw + sw) // N_CHIPS, d),
                "w_gate": jnp.transpose(gr["wg"].reshape(d, N_CHIPS, dff // N_CHIPS), (1, 0, 2)),
                "w_up": jnp.transpose(gr["wu"].reshape(d, N_CHIPS, dff // N_CHIPS), (1, 0, 2)),
                "w_down": gr["wd"].reshape(N_CHIPS, dff // N_CHIPS, d)}

    xt = x.reshape(t, d)
    saved, params = [], []
    gat = _gather_now({n: wb[n][0] for n in BIG}, "first")
    for l in range(depth):
        params.append(layer_params(l, gat))
        plan = _GatherPlan({n: wb[n][l + 1] for n in BIG} if l + 1 < depth else None)
        xt, sv = _layer_fwd(xt, params[l], bsz, s, plan)
        saved.append(sv)
        if l + 1 < depth:
            gat = plan.gathered()
    dxt, dxb, loss_lanes = _loss_head(xt, loss_target.reshape(t, d), "loss_head")

    grads = [None] * depth
    pending, totals = None, None
    for l in reversed(range(depth)):
        plan = _ReducePlan(pending, l + 1, totals, depth, c_idx, chip_idx)
        dxt, dxb, grads[l] = _layer_bwd(dxt, dxb, params[l], saved[l], bsz, s, plan)
        totals = plan.totals
        pending = per_chip(grads[l])
    g = _ReducePlan(pending, 0, totals, depth, c_idx, chip_idx).run_now()
    grad_x = dxt.reshape(bsz, s, d)

    def stack(name):
        return jnp.stack([grads[l][name] for l in range(depth)])

    small_shapes = [(1, LANES)] + [(depth, CONV_WIDTH, cd) if n == "conv_w" else w[n].shape for n in SMALL]
    small = _small_exchange(_pack([loss_lanes] + [stack(n) for n in SMALL]), "allreduce_small", True)
    small = _unpack(small, small_shapes)
    loss = small[0][0, 0]
    for n, a in zip(SMALL, small[1:], strict=True):
        g[n] = a
    g["conv_w"] = lax.dynamic_slice_in_dim(g["conv_w"], my_chip * cs, cs, axis=2)

    delta, new_m, new_v = {}, {}, {}
    for n in BIG:
        shp = w[n].shape
        two_d = (shp[0] * shp[1], shp[2])
        dl, nm, nv = _adamw(w[n].reshape(two_d), g[n].reshape(two_d), mom[n].reshape(two_d),
                            var[n].reshape(two_d), f"adamw_{n}")
        delta[n], new_m[n], new_v[n] = dl.reshape(shp), nm.reshape(shp), nv.reshape(shp)
    shapes = [w[n].shape for n in SMALL]
    dl, nm, nv = _adamw(_pack([w[n] for n in SMALL]), _pack([g[n] for n in SMALL]),
                        _pack([mom[n] for n in SMALL]), _pack([var[n] for n in SMALL]), "adamw_small")
    for n, a, b, c in zip(SMALL, _unpack(dl, shapes), _unpack(nm, shapes), _unpack(nv, shapes), strict=True):
        delta[n], new_m[n], new_v[n] = a, b, c

    return (loss, grad_x, *[g[n] for n in ORDER], *[delta[n] for n in ORDER],
            *[new_m[n] for n in ORDER], *[new_v[n] for n in ORDER])
```

```python
import jax
import jax.numpy as jnp
from jax import lax
from jax.experimental import pallas as pl
from jax.experimental.pallas import tpu as pltpu

F32 = jnp.float32
BF16 = jnp.bfloat16
MESH = pl.DeviceIdType.MESH
ANY = pl.BlockSpec(memory_space=pl.ANY)
VMEM = pl.BlockSpec(memory_space=pltpu.VMEM)

EPS = 1e-6
ATT_HEAD_DIM = 128
SSM_HEAD_DIM = 64
SSM_GROUPS = 2
SSM_STATE = 128
SSD_CHUNK = 128
CONV_WIDTH = 4
LANES = 128
SUBLANES = 8
ATT_TILE = 256
ATT_STRIP = 128
N_CHIPS = 4
N_DEV = 8

ADAM_LR = 0.001
ADAM_B1 = 0.9
ADAM_B2 = 0.999
ADAM_EPS = 1e-08
ADAM_WD = 0.01
ADAM_STEP = 10

VMEM_LIMIT = 48 * 1024 * 1024

NN = (((1,), (0,)), ((), ()))
NT = (((1,), (1,)), ((), ()))
TN = (((0,), (0,)), ((), ()))


def _dot(a, b, dims=NN):
    return lax.dot_general(a.astype(BF16), b.astype(BF16), dims, preferred_element_type=F32)


def _dot_exact(x, ones, dims=NN, passes=3, ones_left=False):
    acc = None
    rem = x
    for _ in range(passes):
        piece = rem.astype(BF16)
        rem = rem - piece.astype(F32)
        p = (lax.dot_general(ones, piece, dims, preferred_element_type=F32) if ones_left
             else lax.dot_general(piece, ones, dims, preferred_element_type=F32))
        acc = p if acc is None else acc + p
    return acc


def _scan_lanes(x, tri, passes, reverse=False):
    nblk = x.shape[1] // LANES
    blocks = [x[:, k * LANES:(k + 1) * LANES] for k in range(nblk)]
    out, carry = [None] * nblk, None
    for k in (reversed(range(nblk)) if reverse else range(nblk)):
        p = _dot_exact(blocks[k], tri, passes=passes)
        out[k] = p if carry is None else p + carry
        tot = jnp.sum(blocks[k], axis=1, keepdims=True)
        carry = tot if carry is None else carry + tot
    return (out[0] if nblk == 1 else jnp.concatenate(out, axis=1)), carry


def _iota2(shape, axis):
    return lax.broadcasted_iota(jnp.int32, shape, axis)


def _tri(n, cmp):
    return cmp(_iota2((n, n), 0), _iota2((n, n), 1)).astype(BF16)


def _sum_all(v):
    return jnp.sum(jnp.sum(v, axis=1, keepdims=True), axis=0, keepdims=True)


def _fit(tile, dim, unit=LANES):
    if dim <= tile:
        return dim
    return max(k for k in range(unit, tile + 1, unit) if dim % k == 0)


def _params(sem):
    return pltpu.CompilerParams(dimension_semantics=sem, vmem_limit_bytes=VMEM_LIMIT)


def _call(body, *, name, grid, in_specs, out_specs, out_shape, sem, args, scratch_shapes=(), rider=None):
    in_specs, out_specs, out_shape = list(in_specs), list(out_specs), list(out_shape)
    scratch_shapes = list(scratch_shapes)
    if rider is None:
        res = pl.pallas_call(body, name=name, grid=grid, in_specs=in_specs, out_specs=out_specs,
                             out_shape=out_shape, scratch_shapes=scratch_shapes,
                             compiler_params=_params(sem))(*args)
        return list(res), []
    n_in, n_out, n_scr = len(in_specs), len(out_specs), len(scratch_shapes)
    r_in, r_out, n_sems = len(rider["ins"]), len(rider["outs"]), rider["n_sems"]

    def hosted(*refs):
        ins, rest = refs[:n_in], refs[n_in:]
        rins, rest = rest[:r_in], rest[r_in:]
        outs, rest = rest[:n_out], rest[n_out:]
        routs, rest = rest[:r_out], rest[r_out:]
        scr, (send_sems, recv_sems) = rest[:n_scr], rest[n_scr:]
        first, last = None, None
        for d, size in enumerate(grid):
            f, e = pl.program_id(d) == 0, pl.program_id(d) == size - 1
            first = f if first is None else jnp.logical_and(first, f)
            last = e if last is None else jnp.logical_and(last, e)

        @pl.when(first)
        def _():
            for cp in rider["copies"](rins, routs, send_sems, recv_sems):
                cp.start()

        body(*ins, *outs, *scr)

        @pl.when(last)
        def _():
            for cp in rider["copies"](rins, routs, send_sems, recv_sems):
                cp.wait()

    res = pl.pallas_call(
        hosted, name=name, grid=grid, in_specs=in_specs + [ANY] * r_in, out_specs=out_specs + [ANY] * r_out,
        out_shape=out_shape + list(rider["outs"]),
        scratch_shapes=scratch_shapes + [pltpu.SemaphoreType.DMA((n_sems,)), pltpu.SemaphoreType.DMA((n_sems,))],
        input_output_aliases={n_in + i: n_out + o for i, o in rider["aliases"].items()},
        compiler_params=_params(("arbitrary",) * len(grid)),
    )(*args, *rider["ins"])
    return list(res[:n_out]), list(res[n_out:])


def _softplus(x):
    return jnp.maximum(x, 0.0) + jnp.log(1.0 + jnp.exp(-jnp.abs(x)))


def _sigmoid(x):
    return 1.0 / (1.0 + jnp.exp(-x))


def _rms_fwd(x, g):
    r = lax.rsqrt(jnp.mean(x * x, axis=-1, keepdims=True) + EPS)
    return (x * r) * g


def _rms_bwd(x, g, dh):
    r = lax.rsqrt(jnp.mean(x * x, axis=-1, keepdims=True) + EPS)
    y = x * r
    dy = dh * g
    dx = r * (dy - y * jnp.mean(dy * y, axis=-1, keepdims=True))
    return dx, dh * y


def _matmul(groups, extras, epilogue, out_dtypes, m, n, tm, tn, name, rider=None):
    tm, tn = _fit(tm, m), _fit(tn, n)
    flat = [t for grp in groups for t in grp]
    n_terms, n_extra = len(flat), len(extras)

    def body(*refs):
        outs = refs[2 * n_terms + n_extra:]
        accs, pos = [], 0
        for grp in groups:
            acc = None
            for (_, _, mode) in grp:
                dims = {"nn": NN, "nt": NT, "tn": TN}[mode]
                p = _dot(refs[2 * pos][...], refs[2 * pos + 1][...], dims)
                acc = p if acc is None else acc + p
                pos += 1
            accs.append(acc)
        ex = [refs[2 * n_terms + i][...] for i in range(n_extra)]
        res = epilogue(accs, ex)
        for o_ref, r in zip(outs, res, strict=True):
            o_ref[...] = r.astype(o_ref.dtype)

    in_specs, args = [], []
    for (a, b, mode) in flat:
        if mode == "nn":
            k = a.shape[1]
            in_specs += [pl.BlockSpec((tm, k), lambda i, j: (i, 0)), pl.BlockSpec((k, tn), lambda i, j: (0, j))]
        elif mode == "nt":
            k = a.shape[1]
            in_specs += [pl.BlockSpec((tm, k), lambda i, j: (i, 0)), pl.BlockSpec((tn, k), lambda i, j: (j, 0))]
        else:
            k = a.shape[0]
            in_specs += [pl.BlockSpec((k, tm), lambda i, j: (0, i)), pl.BlockSpec((k, tn), lambda i, j: (0, j))]
        args += [a, b]
    for e in extras:
        in_specs.append(pl.BlockSpec((tm, tn), lambda i, j: (i, j)))
        args.append(e)
    outs, routs = _call(
        body, name=name, grid=(m // tm, n // tn), in_specs=in_specs,
        out_specs=[pl.BlockSpec((tm, tn), lambda i, j: (i, j)) for _ in out_dtypes],
        out_shape=[jax.ShapeDtypeStruct((m, n), d) for d in out_dtypes],
        sem=("parallel", "parallel"), args=args, rider=rider)
    return outs if rider is None else (outs, routs)


def _mm(a, b, mode, m, n, out_dtype, name, tm=512, tn=512, res=None):
    extras = [] if res is None else [res]
    epi = (lambda accs, ex: (accs[0],)) if res is None else (lambda accs, ex: (accs[0] + ex[0],))
    return _matmul([[(a, b, mode)]], extras, epi, [out_dtype], m, n, tm, tn, name)[0]


def _swiglu_fwd_epilogue(accs, ex):
    g, u = accs
    return g, u, (g * _sigmoid(g)) * u


def _swiglu_bwd_epilogue(accs, ex):
    dact, (g, u) = accs[0], ex
    sg = _sigmoid(g)
    silu = g * sg
    return dact * u * (sg * (1.0 + g * (1.0 - sg))), dact * silu


def _rmsnorm_fwd(x, g, name, tr=512):
    t, d = x.shape
    tr = min(tr, t)

    def body(x_ref, g_ref, h_ref):
        h_ref[...] = _rms_fwd(x_ref[...], g_ref[...]).astype(BF16)

    return pl.pallas_call(
        body, name=name, grid=(t // tr,),
        in_specs=[pl.BlockSpec((tr, d), lambda i: (i, 0)), pl.BlockSpec((1, d), lambda i: (0, 0))],
        out_specs=pl.BlockSpec((tr, d), lambda i: (i, 0)),
        out_shape=jax.ShapeDtypeStruct((t, d), BF16),
        compiler_params=_params(("parallel",)),
    )(x, g.reshape(1, d))


def _rmsnorm_bwd(x, g, dh, dres, name, tr=256):
    t, d = x.shape
    tr = min(tr, t)

    def body(x_ref, g_ref, dh_ref, dres_ref, dx_ref, dxb_ref, dg_ref):
        dx, dgr = _rms_bwd(x_ref[...], g_ref[...], dh_ref[...])
        dx = dx + dres_ref[...]
        dx_ref[...] = dx
        dxb_ref[...] = dx.astype(BF16)

        @pl.when(pl.program_id(0) == 0)
        def _():
            dg_ref[...] = jnp.zeros_like(dg_ref)

        dg_ref[...] += jnp.sum(dgr, axis=0, keepdims=True)

    row = pl.BlockSpec((tr, d), lambda i: (i, 0))
    vec = pl.BlockSpec((1, d), lambda i: (0, 0))
    dx, dxb, dg = pl.pallas_call(
        body, name=name, grid=(t // tr,),
        in_specs=[row, vec, row, row], out_specs=[row, row, vec],
        out_shape=[jax.ShapeDtypeStruct((t, d), F32), jax.ShapeDtypeStruct((t, d), BF16),
                   jax.ShapeDtypeStruct((1, d), F32)],
        compiler_params=_params(("arbitrary",)),
    )(x, g.reshape(1, d), dh, dres)
    return dx, dxb, dg.reshape(d)


def _merge_fwd(o_att, y, z, ga, gs, name, tr=256):
    t, wa = o_att.shape
    ws = y.shape[1]
    wg = ws // SSM_GROUPS
    tr = min(tr, t)

    def body(o_ref, y_ref, z_ref, ga_ref, gs_ref, m_ref):
        m_ref[:, 0:wa] = _rms_fwd(o_ref[...], ga_ref[...]).astype(BF16)
        for g in range(SSM_GROUPS):
            sl = slice(g * wg, (g + 1) * wg)
            zz = z_ref[:, sl]
            yz = y_ref[:, sl] * (zz * _sigmoid(zz))
            m_ref[:, wa + g * wg:wa + (g + 1) * wg] = _rms_fwd(yz, gs_ref[:, sl]).astype(BF16)

    return pl.pallas_call(
        body, name=name, grid=(t // tr,),
        in_specs=[pl.BlockSpec((tr, wa), lambda i: (i, 0)), pl.BlockSpec((tr, ws), lambda i: (i, 0)),
                  pl.BlockSpec((tr, ws), lambda i: (i, 0)), pl.BlockSpec((1, wa), lambda i: (0, 0)),
                  pl.BlockSpec((1, ws), lambda i: (0, 0))],
        out_specs=pl.BlockSpec((tr, wa + ws), lambda i: (i, 0)),
        out_shape=jax.ShapeDtypeStruct((t, wa + ws), BF16),
        compiler_params=_params(("parallel",)),
    )(o_att, y, z, ga.reshape(1, wa), gs.reshape(1, ws))


def _merge_bwd(o_att, y, z, ga, gs, dmix, name, tr=256):
    t, wa = o_att.shape
    ws = y.shape[1]
    wg = ws // SSM_GROUPS
    tr = min(tr, t)

    def body(o_ref, y_ref, z_ref, ga_ref, gs_ref, dm_ref, do_ref, dy_ref, dz_ref, dga_ref, dgs_ref):
        @pl.when(pl.program_id(0) == 0)
        def _():
            dga_ref[...] = jnp.zeros_like(dga_ref)
            dgs_ref[...] = jnp.zeros_like(dgs_ref)

        do, dgr = _rms_bwd(o_ref[...], ga_ref[...], dm_ref[:, 0:wa])
        do_ref[...] = do
        dga_ref[...] += jnp.sum(dgr, axis=0, keepdims=True)
        for g in range(SSM_GROUPS):
            sl = slice(g * wg, (g + 1) * wg)
            zz, yy = z_ref[:, sl], y_ref[:, sl]
            sg = _sigmoid(zz)
            silu = zz * sg
            dyz, dgr = _rms_bwd(yy * silu, gs_ref[:, sl], dm_ref[:, wa + g * wg:wa + (g + 1) * wg])
            dy_ref[:, sl] = dyz * silu
            dz_ref[:, sl] = (dyz * yy * (sg + silu * (1.0 - sg))).astype(BF16)
            dgs_ref[:, sl] += jnp.sum(dgr, axis=0, keepdims=True)

    rowa = pl.BlockSpec((tr, wa), lambda i: (i, 0))
    rows = pl.BlockSpec((tr, ws), lambda i: (i, 0))
    veca = pl.BlockSpec((1, wa), lambda i: (0, 0))
    vecs = pl.BlockSpec((1, ws), lambda i: (0, 0))
    do, dy, dz, dga, dgs = pl.pallas_call(
        body, name=name, grid=(t // tr,),
        in_specs=[rowa, rows, rows, veca, vecs, pl.BlockSpec((tr, wa + ws), lambda i: (i, 0))],
        out_specs=[rowa, rows, rows, veca, vecs],
        out_shape=[jax.ShapeDtypeStruct((t, wa), F32), jax.ShapeDtypeStruct((t, ws), F32),
                   jax.ShapeDtypeStruct((t, ws), BF16), jax.ShapeDtypeStruct((1, wa), F32),
                   jax.ShapeDtypeStruct((1, ws), F32)],
        compiler_params=_params(("arbitrary",)),
    )(o_att, y, z, ga.reshape(1, wa), gs.reshape(1, ws), dmix)
    return do, dy, dz, dga.reshape(wa), dgs.reshape(ws)


def _loss_head(y, target, name, tr=256):
    t, d = y.shape
    tr = min(tr, t)

    def body(y_ref, t_ref, dy_ref, dyb_ref, l_ref):
        @pl.when(pl.program_id(0) == 0)
        def _():
            l_ref[...] = jnp.zeros_like(l_ref)

        diff = y_ref[...] - t_ref[...]
        dy = diff * (1.0 / d)
        dy_ref[...] = dy
        dyb_ref[...] = dy.astype(BF16)
        part = jnp.sum(diff * diff, axis=0, keepdims=True)
        fold = part[:, 0:LANES]
        for k in range(1, d // LANES):
            fold = fold + part[:, k * LANES:(k + 1) * LANES]
        l_ref[...] += fold * (0.5 / d)

    row = pl.BlockSpec((tr, d), lambda i: (i, 0))
    return pl.pallas_call(
        body, name=name, grid=(t // tr,), in_specs=[row, row],
        out_specs=[row, row, pl.BlockSpec((1, LANES), lambda i: (0, 0))],
        out_shape=[jax.ShapeDtypeStruct((t, d), F32), jax.ShapeDtypeStruct((t, d), BF16),
                   jax.ShapeDtypeStruct((1, LANES), F32)],
        compiler_params=_params(("arbitrary",)),
    )(y, target)


def _adamw(w, g, m, v, name, tr=256):
    r, c = w.shape
    tr = _fit(tr, r, 16)

    def body(w_ref, g_ref, m_ref, v_ref, d_ref, nm_ref, nv_ref):
        gg = g_ref[...]
        nm = ADAM_B1 * m_ref[...] + (1.0 - ADAM_B1) * gg
        nv = ADAM_B2 * v_ref[...] + (1.0 - ADAM_B2) * (gg * gg)
        m_hat = nm / (1.0 - ADAM_B1 ** ADAM_STEP)
        v_hat = nv / (1.0 - ADAM_B2 ** ADAM_STEP)
        d_ref[...] = -ADAM_LR * (m_hat / (jnp.sqrt(v_hat) + ADAM_EPS) + ADAM_WD * w_ref[...])
        nm_ref[...] = nm
        nv_ref[...] = nv

    blk = pl.BlockSpec((tr, c), lambda i: (i, 0))
    return pl.pallas_call(
        body, name=name, grid=(r // tr,), in_specs=[blk] * 4, out_specs=[blk] * 3,
        out_shape=[jax.ShapeDtypeStruct((r, c), F32)] * 3,
        compiler_params=_params(("parallel",)),
    )(w, g, m, v)


def _att_scores(qi, kj, scale, row0):
    z = _dot(qi, kj, NT) * scale
    lb = -_softplus(-z)
    lrm = lb - z
    if row0 is None:
        return lb, lrm, None
    mask = _iota2(z.shape, 1) < _iota2(z.shape, 0) + row0
    return lb, jnp.where(mask, lrm, 0.0), mask


def _masked(mask, v):
    return v if mask is None else jnp.where(mask, v, 0.0)


def _attention_fwd(qkv, qg, kg, name, tile=None, rider=None):
    bsz, s, w3 = qkv.shape
    hd = ATT_HEAD_DIM
    heads = w3 // (3 * hd)
    tile = min(tile or ATT_TILE, s)
    strip = min(ATT_STRIP, tile)
    nb = s // tile
    scale = hd ** -0.5

    def body(qkv_ref, qg_ref, kg_ref, o_ref, r_ref, qn_s, kn_s, vb_s, acc_s, c_s):
        qn_s[...] = _rms_fwd(qkv_ref[0, :, 0:hd], qg_ref[...]).astype(BF16)
        kn_s[...] = _rms_fwd(qkv_ref[0, :, hd:2 * hd], kg_ref[...]).astype(BF16)
        vb_s[...] = qkv_ref[0, :, 2 * hd:3 * hd].astype(BF16)
        after = _tri(LANES, lambda r, c: r > c)

        def q_loop(i, _):
            rows = pl.ds(pl.multiple_of(i * tile, tile), tile)
            acc_s[...] = jnp.zeros_like(acc_s)
            c_s[...] = jnp.zeros_like(c_s)

            def key_tile(j, diagonal):
                cols = pl.ds(pl.multiple_of(j * tile, tile), tile)
                kj, vj = kn_s[cols, :], vb_s[cols, :]
                strips = range(tile // strip)
                subs = [slice(r * strip, (r + 1) * strip) for r in strips]
                srows = [pl.ds(pl.multiple_of(i * tile + r * strip, strip), strip) for r in strips]
                sc = [_att_scores(qn_s[srows[r], :], kj, scale, r * strip if diagonal else None) for r in strips]
                later = [_scan_lanes(sc[r][1], after, 2, reverse=True) for r in strips]
                for r in strips:
                    w = _masked(sc[r][2], jnp.exp(sc[r][0] + (later[r][0] + c_s[subs[r], :])))
                    acc_s[subs[r], :] += _dot(w, vj)
                    c_s[subs[r], :] += later[r][1]

            def k_loop(jj, _):
                key_tile(i - jj, False)
                return 0

            key_tile(i, True)
            lax.fori_loop(1, i + 1, k_loop, 0)
            o_ref[0, rows, :] = acc_s[...]
            r_ref[0, 0, rows, :] = c_s[...]
            return 0

        lax.fori_loop(0, nb, q_loop, 0)

    return _call(
        body, name=name, grid=(bsz, heads),
        in_specs=[pl.BlockSpec((1, s, 3 * hd), lambda b, h: (b, 0, h)),
                  pl.BlockSpec((1, hd), lambda b, h: (0, 0)), pl.BlockSpec((1, hd), lambda b, h: (0, 0))],
        out_specs=[pl.BlockSpec((1, s, hd), lambda b, h: (b, 0, h)),
                   pl.BlockSpec((1, 1, s, 1), lambda b, h: (b, h, 0, 0))],
        out_shape=[jax.ShapeDtypeStruct((bsz, s, heads * hd), F32),
                   jax.ShapeDtypeStruct((bsz, heads, s, 1), F32)],
        scratch_shapes=[pltpu.VMEM((s, hd), BF16), pltpu.VMEM((s, hd), BF16), pltpu.VMEM((s, hd), BF16),
                        pltpu.VMEM((tile, hd), F32), pltpu.VMEM((tile, 1), F32)],
        sem=("parallel", "parallel"), args=(qkv, qg.reshape(1, hd), kg.reshape(1, hd)), rider=rider)


def _attention_bwd(qkv, qg, kg, rtot, do, name, tile=None, rider=None):
    bsz, s, w3 = qkv.shape
    hd = ATT_HEAD_DIM
    heads = w3 // (3 * hd)
    tile = min(tile or ATT_TILE, s)
    strip = min(ATT_STRIP, tile)
    nb = s // tile
    scale = hd ** -0.5

    def body(qkv_ref, qg_ref, kg_ref, r_ref, do_ref, dqkv_ref, dqg_ref, dkg_ref,
             qn_s, kn_s, vb_s, dob_s, dqn_s, dkn_s, dv_s, c1_s, c2_s, wb_s, dzb_s):
        qn_s[...] = _rms_fwd(qkv_ref[0, :, 0:hd], qg_ref[...]).astype(BF16)
        kn_s[...] = _rms_fwd(qkv_ref[0, :, hd:2 * hd], kg_ref[...]).astype(BF16)
        vb_s[...] = qkv_ref[0, :, 2 * hd:3 * hd].astype(BF16)
        dob_s[...] = do_ref[0].astype(BF16)
        dqn_s[...] = jnp.zeros_like(dqn_s)
        dkn_s[...] = jnp.zeros_like(dkn_s)
        dv_s[...] = jnp.zeros_like(dv_s)
        upto = _tri(LANES, lambda r, c: r <= c)
        before = _tri(LANES, lambda r, c: r < c)

        def q_loop(i, _):
            rows = pl.ds(pl.multiple_of(i * tile, tile), tile)
            c1_s[...] = jnp.zeros_like(c1_s)
            c2_s[...] = jnp.zeros_like(c2_s)

            def key_tile(j, diagonal):
                cols = pl.ds(pl.multiple_of(j * tile, tile), tile)
                kj, vj = kn_s[cols, :], vb_s[cols, :]
                strips = range(tile // strip)
                subs = [slice(r * strip, (r + 1) * strip) for r in strips]
                srows = [pl.ds(pl.multiple_of(i * tile + r * strip, strip), strip) for r in strips]
                sc = [_att_scores(qn_s[srows[r], :], kj, scale, r * strip if diagonal else None) for r in strips]
                dw = [_dot(dob_s[srows[r], :], vj, NT) for r in strips]
                upto_lr = [_scan_lanes(sc[r][1], upto, 2) for r in strips]
                w = [_masked(sc[r][2], jnp.exp(sc[r][0] + (r_ref[0, 0, srows[r], :] - (upto_lr[r][0] + c1_s[subs[r], :]))))
                     for r in strips]
                e = [w[r] * dw[r] for r in strips]
                pre = [_scan_lanes(e[r], before, 1) for r in strips]
                dz = [_masked(sc[r][2], (e[r] - jnp.exp(sc[r][0]) * (e[r] + (pre[r][0] + c2_s[subs[r], :]))) * scale)
                      for r in strips]
                for r in strips:
                    wb_s[subs[r], :] = w[r].astype(BF16)
                    dzb_s[subs[r], :] = dz[r].astype(BF16)
                    c1_s[subs[r], :] += upto_lr[r][1]
                    c2_s[subs[r], :] += pre[r][1]
                dqn_s[rows, :] += _dot(dzb_s[...], kj)
                dv_s[cols, :] += _dot(wb_s[...], dob_s[rows, :], TN)
                dkn_s[cols, :] += _dot(dzb_s[...], qn_s[rows, :], TN)

            def k_loop(j, _):
                key_tile(j, False)
                return 0

            lax.fori_loop(0, i, k_loop, 0)
            key_tile(i, True)
            return 0

        lax.fori_loop(0, nb, q_loop, 0)
        dq, dgq = _rms_bwd(qkv_ref[0, :, 0:hd], qg_ref[...], dqn_s[...])
        dk, dgk = _rms_bwd(qkv_ref[0, :, hd:2 * hd], kg_ref[...], dkn_s[...])
        dqkv_ref[0, :, 0:hd] = dq.astype(BF16)
        dqkv_ref[0, :, hd:2 * hd] = dk.astype(BF16)
        dqkv_ref[0, :, 2 * hd:3 * hd] = dv_s[...].astype(BF16)
        dqg_ref[0, 0] = jnp.sum(dgq, axis=0, keepdims=True)
        dkg_ref[0, 0] = jnp.sum(dgk, axis=0, keepdims=True)

    gain = pl.BlockSpec((1, hd), lambda b, h: (0, 0))
    dgain = pl.BlockSpec((1, 1, 1, hd), lambda b, h: (b, h, 0, 0))
    return _call(
        body, name=name, grid=(bsz, heads),
        in_specs=[pl.BlockSpec((1, s, 3 * hd), lambda b, h: (b, 0, h)), gain, gain,
                  pl.BlockSpec((1, 1, s, 1), lambda b, h: (b, h, 0, 0)),
                  pl.BlockSpec((1, s, hd), lambda b, h: (b, 0, h))],
        out_specs=[pl.BlockSpec((1, s, 3 * hd), lambda b, h: (b, 0, h)), dgain, dgain],
        out_shape=[jax.ShapeDtypeStruct((bsz, s, w3), BF16),
                   jax.ShapeDtypeStruct((bsz, heads, 1, hd), F32),
                   jax.ShapeDtypeStruct((bsz, heads, 1, hd), F32)],
        scratch_shapes=[pltpu.VMEM((s, hd), BF16)] * 4 + [pltpu.VMEM((s, hd), F32)] * 3
        + [pltpu.VMEM((tile, 1), F32)] * 2 + [pltpu.VMEM((tile, tile), BF16)] * 2,
        sem=("parallel", "parallel"), args=(qkv, qg.reshape(1, hd), kg.reshape(1, hd), rtot, do), rider=rider)


def _conv_pre(pad_ref, w_ref, b_ref, s):
    pre = b_ref[...]
    for i in range(CONV_WIDTH):
        off = SUBLANES - (CONV_WIDTH - 1) + i
        pre = pre + pad_ref[off:off + s, :] * w_ref[i:i + 1, :]
    return pre


def _conv_fwd(u, w, b, name, tc=256):
    bsz, s, c = u.shape
    tc = min(tc, c)

    def body(u_ref, w_ref, b_ref, a_ref, pad_s):
        pad_s[0:SUBLANES, :] = jnp.zeros((SUBLANES, tc), F32)
        pad_s[SUBLANES:SUBLANES + s, :] = u_ref[0]
        pre = _conv_pre(pad_s, w_ref, b_ref, s)
        a_ref[0] = pre * _sigmoid(pre)

    return pl.pallas_call(
        body, name=name, grid=(bsz, c // tc),
        in_specs=[pl.BlockSpec((1, s, tc), lambda i, j: (i, 0, j)),
                  pl.BlockSpec((CONV_WIDTH, tc), lambda i, j: (0, j)), pl.BlockSpec((1, tc), lambda i, j: (0, j))],
        out_specs=pl.BlockSpec((1, s, tc), lambda i, j: (i, 0, j)),
        out_shape=jax.ShapeDtypeStruct((bsz, s, c), F32),
        scratch_shapes=[pltpu.VMEM((s + SUBLANES, tc), F32)],
        compiler_params=_params(("parallel", "parallel")),
    )(u, w, b.reshape(1, c))


def _conv_bwd(u, w, b, da, name, tc=256):
    bsz, s, c = u.shape
    tc = min(tc, c)

    def body(u_ref, w_ref, b_ref, da_ref, du_ref, dw_ref, db_ref, pad_s, gpad_s):
        @pl.when(pl.program_id(1) == 0)
        def _():
            dw_ref[...] = jnp.zeros_like(dw_ref)
            db_ref[...] = jnp.zeros_like(db_ref)

        pad_s[0:SUBLANES, :] = jnp.zeros((SUBLANES, tc), F32)
        pad_s[SUBLANES:SUBLANES + s, :] = u_ref[0]
        pre = _conv_pre(pad_s, w_ref, b_ref, s)
        sg = _sigmoid(pre)
        dpre = da_ref[0] * (sg * (1.0 + pre * (1.0 - sg)))
        gpad_s[0:s, :] = dpre
        gpad_s[s:s + SUBLANES, :] = jnp.zeros((SUBLANES, tc), F32)
        du = jnp.zeros((s, tc), F32)
        for i in range(CONV_WIDTH):
            back = CONV_WIDTH - 1 - i
            du = du + gpad_s[back:back + s, :] * w_ref[i:i + 1, :]
            off = SUBLANES - (CONV_WIDTH - 1) + i
            dw_ref[i:i + 1, :] += jnp.sum(dpre * pad_s[off:off + s, :], axis=0, keepdims=True)
        du_ref[0] = du.astype(BF16)
        db_ref[...] += jnp.sum(dpre, axis=0, keepdims=True)

    blk = pl.BlockSpec((1, s, tc), lambda j, i: (i, 0, j))
    du, dw, db = pl.pallas_call(
        body, name=name, grid=(c // tc, bsz),
        in_specs=[blk, pl.BlockSpec((CONV_WIDTH, tc), lambda j, i: (0, j)),
                  pl.BlockSpec((1, tc), lambda j, i: (0, j)), blk],
        out_specs=[blk, pl.BlockSpec((CONV_WIDTH, tc), lambda j, i: (0, j)),
                   pl.BlockSpec((1, tc), lambda j, i: (0, j))],
        out_shape=[jax.ShapeDtypeStruct((bsz, s, c), BF16), jax.ShapeDtypeStruct((CONV_WIDTH, c), F32),
                   jax.ShapeDtypeStruct((1, c), F32)],
        scratch_shapes=[pltpu.VMEM((s + SUBLANES, tc), F32), pltpu.VMEM((s + SUBLANES, tc), F32)],
        compiler_params=_params(("parallel", "arbitrary")),
    )(u, w, b.reshape(1, c), da)
    return du, dw, db.reshape(c)


def _ssd_chunk_common(b_ref, c_ref, dt_ref, dtb_ref, alog_ref):
    bm, cm = b_ref[0], c_ref[0]
    draw = dt_ref[0] + dtb_ref[...]
    dt = _softplus(draw)
    a_row = -jnp.exp(alog_ref[...])
    da = dt * a_row
    n = SSD_CHUNK
    acum = _dot_exact(da, _tri(n, lambda r, c: r >= c), ones_left=True)
    acum_t = _dot_exact(da, _tri(n, lambda r, c: r <= c), dims=TN)
    cb = _dot(cm, bm, NT)
    return bm, cm, draw, dt, a_row, acum, acum_t, cb


def _row_totals(v):
    return _dot_exact(v, jnp.ones((v.shape[1], LANES), BF16), passes=2)


def _ssd_head_common(acum, acum_t, dt, cb, x, i):
    n, p = SSD_CHUNK, SSM_HEAD_DIM
    pick = (_iota2((LANES, LANES), 0) == i).astype(BF16)
    acol = _dot_exact(acum, pick)
    dtc = _dot_exact(dt, pick)[:, :p]
    arow = acum_t[i:i + 1, :]
    causal = _iota2((n, n), 0) >= _iota2((n, n), 1)
    lm = jnp.where(causal, jnp.exp(jnp.where(causal, acol - arow, 0.0)), 0.0)
    gm = cb * lm
    xh = x[:, i * p:(i + 1) * p]
    xdt = xh * dtc
    alast = acol[n - 1:n, :]
    dte = jnp.exp(alast - acol)
    return acol, lm, gm, dtc, xh, xdt, alast, dte


def _ssd_specs(s, wg, hg, rev):
    g, n, cl = SSM_GROUPS, SSM_STATE, SSD_CHUNK
    nc = s // cl
    boff, coff = (g * wg) // n, (g * wg) // n + g
    ci = (lambda c: nc - 1 - c) if rev else (lambda c: c)
    xblk = pl.BlockSpec((1, cl, wg), lambda b, k, c: (b, ci(c), k))
    bblk = pl.BlockSpec((1, cl, n), lambda b, k, c: (b, ci(c), boff + k))
    cblk = pl.BlockSpec((1, cl, n), lambda b, k, c: (b, ci(c), coff + k))
    nblk = pl.BlockSpec((1, cl, n), lambda b, k, c: (b, ci(c), k))
    dtblk = pl.BlockSpec((1, cl, LANES), lambda b, k, c: (b, ci(c), k))
    vec = pl.BlockSpec((1, LANES), lambda b, k, c: (0, k))
    hsblk = pl.BlockSpec((1, 1, 1, wg, n), lambda b, k, c: (b, k, ci(c), 0, 0))
    return nc, xblk, bblk, cblk, nblk, dtblk, vec, hsblk


def _ssd_fwd(xbc, dtraw, dtb, alog, dskip, hg, name, rider=None):
    bsz, s, _ = xbc.shape
    g, n, p = SSM_GROUPS, SSM_STATE, SSM_HEAD_DIM
    wg = hg * p
    nc, xblk, bblk, cblk, _, dtblk, vec, hsblk = _ssd_specs(s, wg, hg, False)

    def body(x_ref, b_ref, c_ref, dt_ref, dtb_ref, alog_ref, dsk_ref, y_ref, hs_ref, h_s):
        @pl.when(pl.program_id(2) == 0)
        def _():
            h_s[...] = jnp.zeros_like(h_s)

        bm, cm, _, dt, _, acum, acum_t, cb = _ssd_chunk_common(b_ref, c_ref, dt_ref, dtb_ref, alog_ref)
        x = x_ref[0]
        hs_ref[0, 0, 0] = h_s[...]
        hd_ = range(hg)
        hc = [_ssd_head_common(acum, acum_t, dt, cb, x, i) for i in hd_]
        hprev = [h_s[i * p:(i + 1) * p, :] for i in hd_]
        ydiag = [_dot(hc[i][2], hc[i][5]) for i in hd_]
        yoff = [_dot(cm, hprev[i], NT) for i in hd_]
        st = [_dot(hc[i][5] * hc[i][7][:, :p], bm, TN) for i in hd_]
        for i in hd_:
            acol, _, _, _, xh, _, alast, _ = hc[i]
            y_ref[0, :, i * p:(i + 1) * p] = ydiag[i] + yoff[i] * jnp.exp(acol[:, :p]) + xh * dsk_ref[:, i:i + 1]
            h_s[i * p:(i + 1) * p, :] = hprev[i] * jnp.exp(alast) + st[i]

    return _call(
        body, name=name, grid=(bsz, g, nc),
        in_specs=[xblk, bblk, cblk, dtblk, vec, vec, vec],
        out_specs=[xblk, hsblk],
        out_shape=[jax.ShapeDtypeStruct((bsz, s, g * wg), F32),
                   jax.ShapeDtypeStruct((bsz, g, nc, wg, n), F32)],
        scratch_shapes=[pltpu.VMEM((wg, n), F32)],
        sem=("parallel", "parallel", "arbitrary"), args=(xbc, xbc, xbc, dtraw, dtb, alog, dskip), rider=rider)


def _ssd_bwd(xbc, dtraw, dtb, alog, dskip, hs, dy, hg, name, rider=None):
    bsz, s, _ = xbc.shape
    g, n, p, cl = SSM_GROUPS, SSM_STATE, SSM_HEAD_DIM, SSD_CHUNK
    wg = hg * p
    nc, xblk, bblk, cblk, nblk, dtblk, vec, hsblk = _ssd_specs(s, wg, hg, True)

    def body(x_ref, b_ref, c_ref, dt_ref, dtb_ref, alog_ref, dsk_ref, hs_ref, dy_ref,
             dx_ref, db_ref, dc_ref, ddt_ref, dvec_ref, dh_s):
        @pl.when(pl.program_id(2) == 0)
        def _():
            dh_s[...] = jnp.zeros_like(dh_s)
            dvec_ref[...] = jnp.zeros_like(dvec_ref)

        lane = _iota2((cl, LANES), 1)
        sub = _iota2((LANES, cl), 0)
        lane1 = _iota2((1, LANES), 1)
        last_row = _iota2((cl, 1), 0) == cl - 1
        bm, cm, draw, dt, a_row, acum, acum_t, cb = _ssd_chunk_common(b_ref, c_ref, dt_ref, dtb_ref, alog_ref)
        x = x_ref[0]
        dyc = dy_ref[0]
        hd_ = range(hg)
        hc = [_ssd_head_common(acum, acum_t, dt, cb, x, i) for i in hd_]
        dyh = [dyc[:, i * p:(i + 1) * p] for i in hd_]
        hprev = [hs_ref[0, 0, 0, i * p:(i + 1) * p, :] for i in hd_]
        dhn = [dh_s[i * p:(i + 1) * p, :] for i in hd_]
        ea = [jnp.exp(hc[i][0]) for i in hd_]
        cd = [jnp.exp(hc[i][6]) for i in hd_]
        y0 = [_dot(cm, hprev[i], NT) for i in hd_]
        dxe = [_dot(bm, dhn[i], NT) for i in hd_]
        dgm = [_dot(dyh[i], hc[i][5], NT) for i in hd_]
        gdy = [_dot(hc[i][2], dyh[i], TN) for i in hd_]
        dy0 = [dyh[i] * ea[i][:, :p] for i in hd_]
        dcm_h = [_dot(dy0[i], hprev[i]) for i in hd_]
        dh_new = [_dot(dy0[i], cm, TN) + dhn[i] * cd[i] for i in hd_]
        dbm_h = [_dot(hc[i][5] * hc[i][7][:, :p], dhn[i]) for i in hd_]
        ws = [dgm[i] * hc[i][2] for i in hd_]
        dxdt = [dxe[i] * hc[i][7][:, :p] + gdy[i] for i in hd_]
        s_y0 = [_row_totals(dyh[i] * y0[i]) for i in hd_]
        s_xe = [_row_totals(dxe[i] * hc[i][5]) for i in hd_]
        s_ws = [_row_totals(ws[i]) for i in hd_]
        s_dt = [_row_totals(dxdt[i] * hc[i][4]) for i in hd_]
        s_dd = [_row_totals(dyh[i] * hc[i][4]) for i in hd_]
        s_hh = [_row_totals(dhn[i] * hprev[i]) for i in hd_]
        dcb = jnp.zeros((cl, cl), F32)
        dcm = jnp.zeros((cl, n), F32)
        dbm = jnp.zeros((cl, n), F32)
        da_col = jnp.zeros((cl, LANES), F32)
        da_row = jnp.zeros((LANES, cl), F32)
        ddt = jnp.zeros((cl, LANES), F32)
        dd = jnp.zeros((1, LANES), F32)
        for i in hd_:
            _, lm, _, dtc, _, _, _, dte = hc[i]
            dh_s[i * p:(i + 1) * p, :] = dh_new[i]
            dd = dd + jnp.where(lane1 == i, jnp.sum(s_dd[i], axis=0, keepdims=True), 0.0)
            t1 = s_xe[i] * dte
            d_alast = jnp.sum(s_hh[i], axis=0, keepdims=True) * cd[i] + jnp.sum(t1, axis=0, keepdims=True)
            dacol = s_y0[i] * ea[i] - t1 + s_ws[i] + jnp.where(last_row, d_alast, 0.0)
            dcb = dcb + dgm[i] * lm
            dcm = dcm + dcm_h[i]
            dbm = dbm + dbm_h[i]
            dx_ref[0, :, i * p:(i + 1) * p] = dxdt[i] * dtc + dyh[i] * dsk_ref[:, i:i + 1]
            da_col = jnp.where(lane == i, dacol, da_col)
            da_row = jnp.where(sub == i, -jnp.sum(ws[i], axis=0, keepdims=True), da_row)
            ddt = jnp.where(lane == i, s_dt[i], ddt)
        dc_ref[0] = dcm + _dot(dcb, bm)
        db_ref[0] = dbm + _dot(dcb, cm, TN)
        upper = _tri(cl, lambda r, k: r <= k)
        dda = _dot_exact(da_col, upper, ones_left=True) + _dot_exact(da_row, upper, dims=NT, ones_left=True)
        ddt = ddt + dda * a_row
        ddraw = ddt * _sigmoid(draw)
        ddt_ref[0] = ddraw.astype(BF16)
        dvec_ref[0, 0, 0:1, :] += jnp.sum(ddraw, axis=0, keepdims=True)
        dvec_ref[0, 0, 1:2, :] += jnp.sum(dda * dt, axis=0, keepdims=True) * a_row
        dvec_ref[0, 0, 2:3, :] += dd

    return _call(
        body, name=name, grid=(bsz, g, nc),
        in_specs=[xblk, bblk, cblk, dtblk, vec, vec, vec, hsblk, xblk],
        out_specs=[xblk, nblk, nblk, dtblk,
                   pl.BlockSpec((1, 1, SUBLANES, LANES), lambda b, k, c: (b, k, 0, 0))],
        out_shape=[jax.ShapeDtypeStruct((bsz, s, g * wg), F32), jax.ShapeDtypeStruct((bsz, s, g * n), F32),
                   jax.ShapeDtypeStruct((bsz, s, g * n), F32), jax.ShapeDtypeStruct((bsz, s, g * LANES), BF16),
                   jax.ShapeDtypeStruct((bsz, g, SUBLANES, LANES), F32)],
        scratch_shapes=[pltpu.VMEM((wg, n), F32)],
        sem=("parallel", "parallel", "arbitrary"), args=(xbc, xbc, xbc, dtraw, dtb, alog, dskip, hs, dy),
        rider=rider)


def _coords():
    return lax.axis_index("x"), lax.axis_index("y"), lax.axis_index("c")


def _other_chips(x, y):
    return [(1 - x, y), (x, 1 - y), (1 - x, 1 - y)]


def _remote(src, dst, send_sems, recv_sems, k, to):
    return pltpu.make_async_remote_copy(src_ref=src, dst_ref=dst, send_sem=send_sems.at[k],
                                        recv_sem=recv_sems.at[k], device_id=to, device_id_type=MESH)


def _standalone(rider, name):
    r_in, r_out, n_sems = len(rider["ins"]), len(rider["outs"]), rider["n_sems"]

    def body(*refs):
        rins, routs, (send_sems, recv_sems) = refs[:r_in], refs[r_in:r_in + r_out], refs[r_in + r_out:]
        cps = rider["copies"](rins, routs, send_sems, recv_sems)
        for cp in cps:
            cp.start()
        for cp in cps:
            cp.wait()

    res = pl.pallas_call(
        body, name=name, in_specs=[ANY] * r_in, out_specs=[ANY] * r_out, out_shape=list(rider["outs"]),
        scratch_shapes=[pltpu.SemaphoreType.DMA((n_sems,)), pltpu.SemaphoreType.DMA((n_sems,))],
        input_output_aliases=dict(rider["aliases"]),
    )(*rider["ins"])
    return list(res)


def _gather_chips_rider(shards):
    def copies(rins, routs, send_sems, recv_sems):
        x, y, c = _coords()
        me = 2 * x + y
        cps = []
        for q, (w_ref, o_ref) in enumerate(zip(rins, routs, strict=True)):
            rh = w_ref.shape[0] // 2
            rows = pl.ds(c * rh, rh)
            for k, (px, py) in enumerate(_other_chips(x, y)):
                cps.append(_remote(w_ref.at[rows], o_ref.at[me, rows], send_sems, recv_sems, 3 * q + k, (px, py, c)))
        return cps

    return dict(ins=list(shards), outs=[jax.ShapeDtypeStruct((N_CHIPS,) + w.shape, w.dtype) for w in shards],
                aliases={}, n_sems=3 * len(shards), copies=copies)


def _gather_pair_rider(gathered):
    def copies(rins, routs, send_sems, recv_sems):
        x, y, c = _coords()
        cps = []
        for q, o_ref in enumerate(routs):
            rh = o_ref.shape[1] // 2
            for k, (px, py) in enumerate(_other_chips(x, y)):
                part = o_ref.at[2 * px + py, pl.ds(c * rh, rh)]
                cps.append(_remote(part, part, send_sems, recv_sems, 3 * q + k, (x, y, 1 - c)))
        return cps

    return dict(ins=list(gathered), outs=[jax.ShapeDtypeStruct(g.shape, g.dtype) for g in gathered],
                aliases={i: i for i in range(len(gathered))}, n_sems=3 * len(gathered), copies=copies)


def _reduce_pair_rider(grads):
    def copies(rins, routs, send_sems, recv_sems):
        x, y, c = _coords()
        cps = []
        for q, (g_ref, r_ref) in enumerate(zip(rins, routs, strict=True)):
            rh = g_ref.shape[1] // 2
            cps.append(_remote(g_ref.at[:, pl.ds((1 - c) * rh, rh)], r_ref, send_sems, recv_sems, q, (x, y, 1 - c)))
        return cps

    return dict(ins=list(grads),
                outs=[jax.ShapeDtypeStruct((g.shape[0], g.shape[1] // 2, g.shape[2]), g.dtype) for g in grads],
                aliases={}, n_sems=len(grads), copies=copies)


def _reduce_chips_rider(pair_sums):
    def copies(rins, routs, send_sems, recv_sems):
        x, y, c = _coords()
        cps = []
        for q, (p_ref, r_ref) in enumerate(zip(rins, routs, strict=True)):
            for k, (px, py) in enumerate(_other_chips(x, y)):
                cps.append(_remote(p_ref.at[2 * px + py], r_ref.at[k], send_sems, recv_sems, 3 * q + k, (px, py, c)))
        return cps

    return dict(ins=list(pair_sums), outs=[jax.ShapeDtypeStruct((3,) + p.shape[1:], p.dtype) for p in pair_sums],
                aliases={}, n_sems=3 * len(pair_sums), copies=copies)


def _reduce_finish_rider(sums, totals, layers, depth):
    def copies(rins, routs, send_sems, recv_sems):
        x, y, c = _coords()
        cps = []
        for q, (f_ref, o_ref) in enumerate(zip(rins[:len(sums)], routs, strict=True)):
            rh = f_ref.shape[0]
            cps.append(_remote(f_ref, o_ref.at[layers[q], pl.ds(c * rh, rh)], send_sems, recv_sems, q, (x, y, 1 - c)))
        return cps

    kept = [q for q, t in enumerate(totals) if t is not None]
    outs = [jax.ShapeDtypeStruct((depth, 2 * f.shape[0], f.shape[1]), F32) for f in sums]
    return dict(ins=list(sums) + [totals[q] for q in kept], outs=outs,
                aliases={len(sums) + k: q for k, q in enumerate(kept)}, n_sems=len(sums), copies=copies)


def _pair_add(gj, r1, c_idx, name, tr=256):
    nj, r, c = gj.shape
    rh = r // 2
    tr = _fit(tr, rh, 16)
    nt = rh // tr

    def body(c_ref, g_ref, r_ref, p_ref, pb_ref):
        s = g_ref[...] + r_ref[...]
        p_ref[...] = s
        pb_ref[...] = s.astype(BF16)

    blk_r = pl.BlockSpec((1, tr, c), lambda j, i, cr: (j, i, 0))
    blk_g = pl.BlockSpec((1, tr, c), lambda j, i, cr: (j, cr[0] * nt + i, 0))
    return pl.pallas_call(
        body, name=name,
        grid_spec=pltpu.PrefetchScalarGridSpec(
            num_scalar_prefetch=1, grid=(nj, nt), in_specs=[blk_g, blk_r], out_specs=[blk_r, blk_r]),
        out_shape=[jax.ShapeDtypeStruct((nj, rh, c), F32), jax.ShapeDtypeStruct((nj, rh, c), BF16)],
        compiler_params=_params(("parallel", "parallel")),
    )(c_idx, gj, r1)


def _chip_add(p, r2, chip_idx, name, tr=256):
    _, rh, c = p.shape
    tr = _fit(tr, rh, 16)

    def body(j_ref, o_ref, r_ref, f_ref):
        f_ref[...] = ((o_ref[0] + r_ref[0].astype(F32)) + r_ref[1].astype(F32)) + r_ref[2].astype(F32)

    return pl.pallas_call(
        body, name=name,
        grid_spec=pltpu.PrefetchScalarGridSpec(
            num_scalar_prefetch=1, grid=(rh // tr,),
            in_specs=[pl.BlockSpec((1, tr, c), lambda i, jr: (jr[0], i, 0)),
                      pl.BlockSpec((3, tr, c), lambda i, jr: (0, i, 0))],
            out_specs=pl.BlockSpec((tr, c), lambda i, jr: (i, 0))),
        out_shape=jax.ShapeDtypeStruct((rh, c), F32),
        compiler_params=_params(("parallel",)),
    )(chip_idx, p, r2)


BIG = ["w_in", "w_out", "w_gate", "w_up", "w_down"]
LATE = ["w_out", "w_gate", "w_up", "w_down"]
FFN = ["w_gate", "w_up", "w_down"]
MIX = ["w_in", "w_out"]


def _set_own_slot(gathered, shard):
    my_chip = 2 * lax.axis_index("x") + lax.axis_index("y")
    return lax.dynamic_update_slice(gathered, shard[None], (my_chip, 0, 0))


class _GatherPlan:
    def __init__(self, late, next_in):
        self.late, self.next_in, self.parts = late, next_in, {}

    def rider(self, host):
        if self.late is None:
            return None
        if host == "attention_fwd":
            return _gather_chips_rider([self.late[n] for n in LATE])
        if host == "ssd_fwd":
            return _gather_pair_rider([self.parts[n] for n in LATE])
        if self.next_in is None:
            return None
        return _gather_chips_rider([self.next_in]) if host == "ffn_gate_up" else _gather_pair_rider([self.parts["w_in"]])

    def collect(self, host, outs):
        if outs:
            self.parts.update(zip(LATE if host in ("attention_fwd", "ssd_fwd") else ["w_in"], outs, strict=True))

    def late_gathered(self):
        return {n: _set_own_slot(self.parts[n], self.late[n]) for n in LATE}

    def next_gathered(self):
        return _set_own_slot(self.parts["w_in"], self.next_in)


def _gather_now(shard, tag):
    part = _standalone(_gather_chips_rider([shard]), f"allgather_chips_{tag}")
    return _set_own_slot(_standalone(_gather_pair_rider(part), f"allgather_pair_{tag}")[0], shard)


class _ReducePlan:
    PAIR = {"ffn_down_dgrad": "mix", "ffn_gate_up_dgrad": "ffn"}
    CHIPS = {"ssd_bwd": "mix", "attention_bwd": "ffn"}

    def __init__(self, mix, mix_layer, ffn_layer, totals, depth, ids):
        self.groups = {} if mix is None else {"mix": (MIX, mix, mix_layer)}
        self.ffn_layer, self.totals, self.depth, self.ids = ffn_layer, dict(totals), depth, ids
        self.p, self.pb, self.f = {}, {}, {}

    def add_ffn(self, grads):
        if self.ids is not None:
            self.groups["ffn"] = (FFN, grads, self.ffn_layer)

    def _present(self):
        return [(n, layer) for names, _, layer in self.groups.values() for n in names if n in self.f]

    def rider(self, host):
        if host == "proj_in_dgrad":
            done = self._present()
            if not done:
                return None
            return _reduce_finish_rider([self.f[n] for n, _ in done], [self.totals.get(n) for n, _ in done],
                                        [layer for _, layer in done], self.depth)
        group = self.groups.get(self.PAIR.get(host) or self.CHIPS.get(host))
        if group is None:
            return None
        names, grads, _ = group
        return (_reduce_pair_rider([grads[n] for n in names]) if host in self.PAIR
                else _reduce_chips_rider([self.pb[n] for n in names]))

    def collect(self, host, outs):
        if not outs:
            return
        c_idx, chip_idx = self.ids
        if host == "proj_in_dgrad":
            c = lax.axis_index("c")
            for (n, layer), t in zip(self._present(), outs, strict=True):
                self.totals[n] = lax.dynamic_update_slice(t, self.f[n][None], (layer, c * self.f[n].shape[0], 0))
            return
        names, grads, layer = self.groups[self.PAIR.get(host) or self.CHIPS.get(host)]
        for n, got in zip(names, outs, strict=True):
            if host in self.PAIR:
                self.p[n], self.pb[n] = _pair_add(grads[n], got, c_idx, f"rs_pair_add_{n}_layer{layer}")
            else:
                self.f[n] = _chip_add(self.p[n], got, chip_idx, f"rs_chip_add_{n}_layer{layer}")

    def run_now(self, tag):
        self.collect("ffn_down_dgrad", _standalone(self.rider("ffn_down_dgrad"), f"rs_pair_{tag}"))
        self.collect("ssd_bwd", _standalone(self.rider("ssd_bwd"), f"rs_chips_{tag}"))
        self.collect("proj_in_dgrad", _standalone(self.rider("proj_in_dgrad"), f"rs_finish_{tag}"))
        return self.totals


def _small_exchange(v, name, reduce):
    rows = v.shape[0]

    def body(v_ref, o_ref, *rest):
        buf = rest[0] if reduce else o_ref
        send_sems, recv_sems = rest[-2], rest[-1]
        x, y, c = _coords()
        me = 4 * x + 2 * y + c
        buf[me] = v_ref[...]
        cps = []
        for r in range(1, N_DEV):
            peer = (lax.bitwise_xor(x, (r >> 2) & 1), lax.bitwise_xor(y, (r >> 1) & 1), lax.bitwise_xor(c, r & 1))
            cps.append(_remote(v_ref, buf.at[me], send_sems, recv_sems, r - 1, peer))
        for cp in cps:
            cp.start()
        for r in range(1, N_DEV):
            src = buf.at[lax.bitwise_xor(me, r)]
            _remote(src, src, send_sems, recv_sems, r - 1, (x, y, c)).wait_recv()
        for cp in cps:
            cp.wait_send()
        if reduce:
            acc = buf[0]
            for d in range(1, N_DEV):
                acc = acc + buf[d]
            o_ref[...] = acc
            o_ref[0:1, :] = jnp.broadcast_to(jnp.sum(acc[0:1, :], axis=1, keepdims=True), (1, LANES))

    scratch = [pltpu.SemaphoreType.DMA((N_DEV - 1,)), pltpu.SemaphoreType.DMA((N_DEV - 1,))]
    if reduce:
        scratch = [pltpu.VMEM((N_DEV, rows, LANES), F32)] + scratch
    out_shape = (rows, LANES) if reduce else (N_DEV, rows, LANES)
    return pl.pallas_call(
        body, name=name, in_specs=[VMEM], out_specs=VMEM,
        out_shape=jax.ShapeDtypeStruct(out_shape, F32), scratch_shapes=scratch,
    )(v)


def _pack(parts):
    flat = []
    for a in parts:
        a = a.reshape(-1)
        flat.append(jnp.pad(a, (0, (-a.shape[0]) % LANES)))
    v = jnp.concatenate(flat)
    v = jnp.pad(v, (0, (-v.shape[0]) % (SUBLANES * LANES)))
    return v.reshape(-1, LANES)


def _unpack(slab, shapes):
    flat = slab.reshape(-1)
    out, off = [], 0
    for shp in shapes:
        size = 1
        for d in shp:
            size *= d
        out.append(flat[off:off + size].reshape(shp))
        off += size + (-size) % LANES
    return out


def _group_slots(a, hg):
    lead = a.shape[:-1]
    a = a.reshape(lead + (SSM_GROUPS, hg))
    a = jnp.pad(a, [(0, 0)] * len(lead) + [(0, 0), (0, LANES - hg)])
    return a.reshape(lead + (SSM_GROUPS * LANES,))


def _ungroup_slots(a, hg):
    lead = a.shape[:-1]
    return a.reshape(lead + (SSM_GROUPS, LANES))[..., :hg].reshape(lead + (SSM_GROUPS * hg,))


def _layer_fwd(x, p, bsz, s, plan, late_params=None):
    t, d = x.shape
    aw, sw, cd, hg = p["aw"], p["sw"], p["cd"], p["hg"]
    h = _rmsnorm_fwd(x, p["norm_mix"], "norm_mix_fwd")
    qkv = _mm(h, p["wqkv"], "nn", t, 3 * aw, F32, "proj_qkv", tm=1024, tn=512)
    z = _mm(h, p["wz"], "nn", t, sw, F32, "proj_z", tm=1024, tn=512)
    xbc = _mm(h, p["wxbc"], "nn", t, cd, F32, "proj_xbc", tm=1024, tn=512)
    dtraw = _mm(h, p["wdt"], "nn", t, SSM_GROUPS * LANES, F32, "proj_dt", tm=1024, tn=SSM_GROUPS * LANES)
    qkv3 = qkv.reshape(bsz, s, 3 * aw)
    (o_att, rtot), sent = _attention_fwd(qkv3, p["q_gain"], p["k_gain"], "attention_fwd",
                                         rider=plan.rider("attention_fwd"))
    plan.collect("attention_fwd", sent)
    xbc3 = xbc.reshape(bsz, s, cd)
    xact = _conv_fwd(xbc3, p["conv_w"], p["conv_b"], "conv_fwd")
    dt3 = dtraw.reshape(bsz, s, SSM_GROUPS * LANES)
    (y, hs), sent = _ssd_fwd(xact, dt3, p["dt_bias"], p["a_log"], p["d_skip"], hg, "ssd_fwd",
                             rider=plan.rider("ssd_fwd"))
    plan.collect("ssd_fwd", sent)
    if plan.late is not None:
        p = {**p, **late_params(plan.late_gathered())}
    dff = p["wg"].shape[1]
    o2, y2 = o_att.reshape(t, aw), y.reshape(t, sw)
    mix = _merge_fwd(o2, y2, z, p["attn_out_gain"], p["ssm_out_gain"], "merge_fwd")
    x1 = _mm(mix, p["wout"], "nn", t, d, F32, "proj_out", tm=1024, tn=512, res=x)
    h2 = _rmsnorm_fwd(x1, p["norm_ffn"], "norm_ffn_fwd")
    (gate, up, act), _ = _hosted_matmul(
        [[(h2, p["wg"], "nn")], [(h2, p["wu"], "nn")]], [], _swiglu_fwd_epilogue,
        [F32, F32, BF16], t, dff, 512, 512, "ffn_gate_up", plan)
    (x2,), _ = _hosted_matmul([[(act, p["wd"], "nn")]], [x1], lambda accs, ex: (accs[0] + ex[0],),
                              [F32], t, d, 512, 512, "ffn_down", plan)
    saved = dict(x=x, h=h, qkv3=qkv3, z=z, xbc3=xbc3, dt3=dt3, o2=o2, rtot=rtot, xact=xact, hs=hs, y2=y2,
                 mix=mix, x1=x1, h2=h2, gate=gate, up=up, act=act)
    return x2, saved, p


def _hosted_matmul(groups, extras, epilogue, out_dtypes, m, n, tm, tn, name, plan):
    rider = plan.rider(name)
    if rider is None:
        return _matmul(groups, extras, epilogue, out_dtypes, m, n, tm, tn, name), []
    outs, sent = _matmul(groups, extras, epilogue, out_dtypes, m, n, tm, tn, name, rider=rider)
    plan.collect(name, sent)
    return outs, sent


def _layer_bwd(dx2, dx2b, p, sv, bsz, s, plan, ffn_to_chips=None):
    t, d = dx2.shape
    aw, sw, cd, hg = p["aw"], p["sw"], p["cd"], p["hg"]
    dff = p["wg"].shape[1]
    gr = {}
    (dgate, dup), _ = _hosted_matmul([[(dx2b, p["wd"], "nt")]], [sv["gate"], sv["up"]], _swiglu_bwd_epilogue,
                                     [BF16, BF16], t, dff, 512, 512, "ffn_down_dgrad", plan)
    gr["wd"] = _mm(sv["act"], dx2b, "tn", dff, d, F32, "ffn_down_wgrad")
    gr["wg"] = _mm(sv["h2"], dgate, "tn", d, dff, F32, "ffn_gate_wgrad")
    gr["wu"] = _mm(sv["h2"], dup, "tn", d, dff, F32, "ffn_up_wgrad")
    if ffn_to_chips is not None:
        plan.add_ffn(ffn_to_chips(gr))
    (dh2,), _ = _hosted_matmul([[(dgate, p["wg"], "nt"), (dup, p["wu"], "nt")]], [], lambda accs, ex: (accs[0],),
                               [F32], t, d, 512, 256, "ffn_gate_up_dgrad", plan)
    dx1, dx1b, gr["norm_ffn"] = _rmsnorm_bwd(sv["x1"], p["norm_ffn"], dh2, dx2, "norm_ffn_bwd")
    dmix = _mm(dx1b, p["wout"], "nt", t, aw + sw, F32, "proj_out_dgrad")
    gr["wout"] = _mm(sv["mix"], dx1b, "tn", aw + sw, d, F32, "proj_out_wgrad")
    do, dy, dz, gr["attn_out_gain"], gr["ssm_out_gain"] = _merge_bwd(
        sv["o2"], sv["y2"], sv["z"], p["attn_out_gain"], p["ssm_out_gain"], dmix, "merge_bwd")
    (dxs, dbm, dcm, ddt, dvec), sent = _ssd_bwd(sv["xact"], sv["dt3"], p["dt_bias"], p["a_log"], p["d_skip"],
                                                sv["hs"], dy.reshape(bsz, s, sw), hg, "ssd_bwd",
                                                rider=plan.rider("ssd_bwd"))
    plan.collect("ssd_bwd", sent)
    dvec = jnp.sum(dvec, axis=0).reshape(SSM_GROUPS, SUBLANES, LANES)
    gr["dt_bias"], gr["a_log"], gr["d_skip"] = (dvec[:, k, :hg].reshape(-1) for k in range(3))
    dxact = jnp.concatenate([dxs, dbm, dcm], axis=-1)
    dxbc, gr["conv_w"], gr["conv_b"] = _conv_bwd(sv["xbc3"], p["conv_w"], p["conv_b"], dxact, "conv_bwd")
    (dqkv, dqg, dkg), sent = _attention_bwd(sv["qkv3"], p["q_gain"], p["k_gain"], sv["rtot"],
                                            do.reshape(bsz, s, aw), "attention_bwd",
                                            rider=plan.rider("attention_bwd"))
    plan.collect("attention_bwd", sent)
    gr["q_gain"] = jnp.sum(dqg, axis=(0, 1, 2))
    gr["k_gain"] = jnp.sum(dkg, axis=(0, 1, 2))
    dqkv, dxbc, ddt = dqkv.reshape(t, 3 * aw), dxbc.reshape(t, cd), ddt.reshape(t, SSM_GROUPS * LANES)
    (dh,), _ = _hosted_matmul(
        [[(dqkv, p["wqkv"], "nt"), (dz, p["wz"], "nt"), (dxbc, p["wxbc"], "nt"), (ddt, p["wdt"], "nt")]],
        [], lambda accs, ex: (accs[0],), [F32], t, d, 512, 256, "proj_in_dgrad", plan)
    h = sv["h"]
    gr["wqkv"] = _mm(h, dqkv, "tn", d, 3 * aw, F32, "proj_qkv_wgrad")
    gr["wz"] = _mm(h, dz, "tn", d, sw, F32, "proj_z_wgrad")
    gr["wxbc"] = _mm(h, dxbc, "tn", d, cd, F32, "proj_xbc_wgrad")
    gr["wdt"] = _mm(h, ddt, "tn", d, SSM_GROUPS * LANES, F32, "proj_dt_wgrad", tn=SSM_GROUPS * LANES)
    dx, dxb, gr["norm_mix"] = _rmsnorm_bwd(sv["x"], p["norm_mix"], dh, dx1, "norm_mix_bwd")
    return dx, dxb, gr


SMALL = ["norm_mix", "q_gain", "k_gain", "conv_w", "conv_b", "dt_bias", "a_log", "d_skip",
         "attn_out_gain", "ssm_out_gain", "norm_ffn"]
ORDER = ["norm_mix", "w_in", "q_gain", "k_gain", "conv_w", "conv_b", "dt_bias", "a_log", "d_skip",
         "attn_out_gain", "ssm_out_gain", "w_out", "norm_ffn", "w_gate", "w_up", "w_down"]


def kernel(x, norm_mix, w_in, q_gain, k_gain, conv_w, conv_b, dt_bias, a_log, d_skip, attn_out_gain, ssm_out_gain, w_out, norm_ffn, w_gate, w_up, w_down, loss_target, m_norm_mix, m_w_in, m_q_gain, m_k_gain, m_conv_w, m_conv_b, m_dt_bias, m_a_log, m_d_skip, m_attn_out_gain, m_ssm_out_gain, m_w_out, m_norm_ffn, m_w_gate, m_w_up, m_w_down, v_norm_mix, v_w_in, v_q_gain, v_k_gain, v_conv_w, v_conv_b, v_dt_bias, v_a_log, v_d_skip, v_attn_out_gain, v_ssm_out_gain, v_w_out, v_norm_ffn, v_w_gate, v_w_up, v_w_down):
    w = dict(norm_mix=norm_mix, w_in=w_in, q_gain=q_gain, k_gain=k_gain, conv_w=conv_w, conv_b=conv_b,
             dt_bias=dt_bias, a_log=a_log, d_skip=d_skip, attn_out_gain=attn_out_gain, ssm_out_gain=ssm_out_gain,
             w_out=w_out, norm_ffn=norm_ffn, w_gate=w_gate, w_up=w_up, w_down=w_down)
    mom = dict(norm_mix=m_norm_mix, w_in=m_w_in, q_gain=m_q_gain, k_gain=m_k_gain, conv_w=m_conv_w,
               conv_b=m_conv_b, dt_bias=m_dt_bias, a_log=m_a_log, d_skip=m_d_skip,
               attn_out_gain=m_attn_out_gain, ssm_out_gain=m_ssm_out_gain, w_out=m_w_out, norm_ffn=m_norm_ffn,
               w_gate=m_w_gate, w_up=m_w_up, w_down=m_w_down)
    var = dict(norm_mix=v_norm_mix, w_in=v_w_in, q_gain=v_q_gain, k_gain=v_k_gain, conv_w=v_conv_w,
               conv_b=v_conv_b, dt_bias=v_dt_bias, a_log=v_a_log, d_skip=v_d_skip,
               attn_out_gain=v_attn_out_gain, ssm_out_gain=v_ssm_out_gain, w_out=v_w_out, norm_ffn=v_norm_ffn,
               w_gate=v_w_gate, w_up=v_w_up, w_down=v_w_down)

    bsz, s, d = x.shape
    t = bsz * s
    depth = norm_mix.shape[0]
    aw = attn_out_gain.shape[1]
    sw = ssm_out_gain.shape[1]
    cd = conv_b.shape[1]
    hs_n = dt_bias.shape[1]
    hg = hs_n // SSM_GROUPS
    heads = aw // ATT_HEAD_DIM
    in_dim = 3 * aw + sw + cd + hs_n
    dff = w_gate.shape[2] * N_CHIPS
    cs = conv_w.shape[2]
    my_chip = 2 * lax.axis_index("x") + lax.axis_index("y")
    c_idx = lax.axis_index("c").astype(jnp.int32).reshape(1)

    chip_idx = my_chip.astype(jnp.int32).reshape(1)
    wb = {n: w[n].astype(BF16) for n in BIG}
    conv_all = _small_exchange(_pack([conv_w]), "allgather_conv_w", False)
    conv_full = jnp.concatenate(
        [_unpack(conv_all[2 * j], [conv_w.shape])[0] for j in range(N_CHIPS)], axis=-1)

    def in_params(l, gat_in):
        win = jnp.transpose(gat_in, (1, 0, 2)).reshape(d, in_dim)
        wqkv = win[:, :3 * aw].reshape(d, 3, heads, ATT_HEAD_DIM)
        wqkv = jnp.transpose(wqkv, (0, 2, 1, 3)).reshape(d, 3 * aw)
        return dict(aw=aw, sw=sw, cd=cd, hg=hg, norm_mix=norm_mix[l], wqkv=wqkv, wz=win[:, 3 * aw:3 * aw + sw],
                    wxbc=win[:, 3 * aw + sw:3 * aw + sw + cd], wdt=_group_slots(win[:, 3 * aw + sw + cd:], hg),
                    q_gain=q_gain[l], k_gain=k_gain[l], conv_w=conv_full[l], conv_b=conv_b[l],
                    dt_bias=_group_slots(dt_bias[l], hg).reshape(1, -1),
                    a_log=_group_slots(a_log[l], hg).reshape(1, -1),
                    d_skip=_group_slots(d_skip[l], hg).reshape(1, -1),
                    attn_out_gain=attn_out_gain[l], ssm_out_gain=ssm_out_gain[l], norm_ffn=norm_ffn[l])

    def late_params(gat):
        return dict(wout=gat["w_out"].reshape(aw + sw, d), wd=gat["w_down"].reshape(dff, d),
                    wg=jnp.transpose(gat["w_gate"], (1, 0, 2)).reshape(d, dff),
                    wu=jnp.transpose(gat["w_up"], (1, 0, 2)).reshape(d, dff))

    def ffn_to_chips(gr):
        return {"w_gate": jnp.transpose(gr["wg"].reshape(d, N_CHIPS, dff // N_CHIPS), (1, 0, 2)),
                "w_up": jnp.transpose(gr["wu"].reshape(d, N_CHIPS, dff // N_CHIPS), (1, 0, 2)),
                "w_down": gr["wd"].reshape(N_CHIPS, dff // N_CHIPS, d)}

    def mix_to_chips(gr):
        gqkv = gr["wqkv"].reshape(d, heads, 3, ATT_HEAD_DIM)
        gqkv = jnp.transpose(gqkv, (0, 2, 1, 3)).reshape(d, 3 * aw)
        gin = jnp.concatenate([gqkv, gr["wz"], gr["wxbc"], _ungroup_slots(gr["wdt"], hg)], axis=-1)
        return {"w_in": jnp.transpose(gin.reshape(d, N_CHIPS, in_dim // N_CHIPS), (1, 0, 2)),
                "w_out": gr["wout"].reshape(N_CHIPS, (aw + sw) // N_CHIPS, d)}

    xt = x.reshape(t, d)
    saved, params = [], []
    gat_in = _gather_now(wb["w_in"][0], "first")
    for l in range(depth):
        plan = _GatherPlan({n: wb[n][l] for n in LATE}, wb["w_in"][l + 1] if l + 1 < depth else None)
        xt, sv, p = _layer_fwd(xt, in_params(l, gat_in), bsz, s, plan, late_params)
        saved.append(sv)
        params.append(p)
        if l + 1 < depth:
            gat_in = plan.next_gathered()
    dxt, dxb, loss_lanes = _loss_head(xt, loss_target.reshape(t, d), "loss_head")

    grads = [None] * depth
    pending, totals = None, {}
    for l in reversed(range(depth)):
        plan = _ReducePlan(pending, l + 1, l, totals, depth, (c_idx, chip_idx))
        dxt, dxb, grads[l] = _layer_bwd(dxt, dxb, params[l], saved[l], bsz, s, plan, ffn_to_chips)
        totals = plan.totals
        pending = mix_to_chips(grads[l])
    g = _ReducePlan(pending, 0, None, totals, depth, (c_idx, chip_idx)).run_now("first_layer")
    grad_x = dxt.reshape(bsz, s, d)

    def stack(name):
        return jnp.stack([grads[l][name] for l in range(depth)])

    small_shapes = [(1, LANES)] + [(depth, CONV_WIDTH, cd) if n == "conv_w" else w[n].shape for n in SMALL]
    small = _small_exchange(_pack([loss_lanes] + [stack(n) for n in SMALL]), "allreduce_small", True)
    small = _unpack(small, small_shapes)
    loss = small[0][0, 0]
    for n, a in zip(SMALL, small[1:], strict=True):
        g[n] = a
    g["conv_w"] = lax.dynamic_slice_in_dim(g["conv_w"], my_chip * cs, cs, axis=2)

    delta, new_m, new_v = {}, {}, {}
    for n in BIG:
        shp = w[n].shape
        two_d = (shp[0] * shp[1], shp[2])
        dl, nm, nv = _adamw(w[n].reshape(two_d), g[n].reshape(two_d), mom[n].reshape(two_d),
                            var[n].reshape(two_d), f"adamw_{n}")
        delta[n], new_m[n], new_v[n] = dl.reshape(shp), nm.reshape(shp), nv.reshape(shp)
    shapes = [w[n].shape for n in SMALL]
    dl, nm, nv = _adamw(_pack([w[n] for n in SMALL]), _pack([g[n] for n in SMALL]),
                        _pack([mom[n] for n in SMALL]), _pack([var[n] for n in SMALL]), "adamw_small")
    for n, a, b, c in zip(SMALL, _unpack(dl, shapes), _unpack(nm, shapes), _unpack(nv, shapes), strict=True):
        delta[n], new_m[n], new_v[n] = a, b, c

    return (loss, grad_x, *[g[n] for n in ORDER], *[delta[n] for n in ORDER],
            *[new_m[n] for n in ORDER], *[new_v[n] for n in ORDER])
```

```python
import jax
import jax.numpy as jnp
from jax import lax
from jax.experimental import pallas as pl
from jax.experimental.pallas import tpu as pltpu

F32 = jnp.float32
BF16 = jnp.bfloat16
MESH = pl.DeviceIdType.MESH
ANY = pl.BlockSpec(memory_space=pl.ANY)
VMEM = pl.BlockSpec(memory_space=pltpu.VMEM)

EPS = 1e-6
ATT_HEAD_DIM = 128
SSM_HEAD_DIM = 64
SSM_GROUPS = 2
SSM_STATE = 128
SSD_CHUNK = 128
CONV_WIDTH = 4
LANES = 128
SUBLANES = 8
ATT_TILE = 256
ATT_STRIP = 128
N_CHIPS = 4
N_DEV = 8

ADAM_LR = 0.001
ADAM_B1 = 0.9
ADAM_B2 = 0.999
ADAM_EPS = 1e-08
ADAM_WD = 0.01
ADAM_STEP = 10

VMEM_LIMIT = 48 * 1024 * 1024

NN = (((1,), (0,)), ((), ()))
NT = (((1,), (1,)), ((), ()))
TN = (((0,), (0,)), ((), ()))


def _dot(a, b, dims=NN):
    return lax.dot_general(a.astype(BF16), b.astype(BF16), dims, preferred_element_type=F32)


def _dot_exact(x, ones, dims=NN, passes=3, ones_left=False):
    acc = None
    rem = x
    for _ in range(passes):
        piece = rem.astype(BF16)
        rem = rem - piece.astype(F32)
        p = (lax.dot_general(ones, piece, dims, preferred_element_type=F32) if ones_left
             else lax.dot_general(piece, ones, dims, preferred_element_type=F32))
        acc = p if acc is None else acc + p
    return acc


def _scan_lanes(x, tri, passes, reverse=False):
    nblk = x.shape[1] // LANES
    blocks = [x[:, k * LANES:(k + 1) * LANES] for k in range(nblk)]
    out, carry = [None] * nblk, None
    for k in (reversed(range(nblk)) if reverse else range(nblk)):
        p = _dot_exact(blocks[k], tri, passes=passes)
        out[k] = p if carry is None else p + carry
        tot = jnp.sum(blocks[k], axis=1, keepdims=True)
        carry = tot if carry is None else carry + tot
    return (out[0] if nblk == 1 else jnp.concatenate(out, axis=1)), carry


def _iota2(shape, axis):
    return lax.broadcasted_iota(jnp.int32, shape, axis)


def _tri(n, cmp):
    return cmp(_iota2((n, n), 0), _iota2((n, n), 1)).astype(BF16)


def _sum_all(v):
    return jnp.sum(jnp.sum(v, axis=1, keepdims=True), axis=0, keepdims=True)


def _fit(tile, dim, unit=LANES):
    if dim <= tile:
        return dim
    return max(k for k in range(unit, tile + 1, unit) if dim % k == 0)


def _params(sem):
    return pltpu.CompilerParams(dimension_semantics=sem, vmem_limit_bytes=VMEM_LIMIT)


def _call(body, *, name, grid, in_specs, out_specs, out_shape, sem, args, scratch_shapes=(), rider=None):
    in_specs, out_specs, out_shape = list(in_specs), list(out_specs), list(out_shape)
    scratch_shapes = list(scratch_shapes)
    if rider is None:
        res = pl.pallas_call(body, name=name, grid=grid, in_specs=in_specs, out_specs=out_specs,
                             out_shape=out_shape, scratch_shapes=scratch_shapes,
                             compiler_params=_params(sem))(*args)
        return list(res), []
    n_in, n_out, n_scr = len(in_specs), len(out_specs), len(scratch_shapes)
    r_in, r_out, n_sems = len(rider["ins"]), len(rider["outs"]), rider["n_sems"]

    def hosted(*refs):
        ins, rest = refs[:n_in], refs[n_in:]
        rins, rest = rest[:r_in], rest[r_in:]
        outs, rest = rest[:n_out], rest[n_out:]
        routs, rest = rest[:r_out], rest[r_out:]
        scr, (send_sems, recv_sems) = rest[:n_scr], rest[n_scr:]
        first, last = None, None
        for d, size in enumerate(grid):
            f, e = pl.program_id(d) == 0, pl.program_id(d) == size - 1
            first = f if first is None else jnp.logical_and(first, f)
            last = e if last is None else jnp.logical_and(last, e)

        @pl.when(first)
        def _():
            for cp in rider["copies"](rins, routs, send_sems, recv_sems):
                cp.start()

        body(*ins, *outs, *scr)

        @pl.when(last)
        def _():
            for cp in rider["copies"](rins, routs, send_sems, recv_sems):
                cp.wait()

    res = pl.pallas_call(
        hosted, name=name, grid=grid, in_specs=in_specs + [ANY] * r_in, out_specs=out_specs + [ANY] * r_out,
        out_shape=out_shape + list(rider["outs"]),
        scratch_shapes=scratch_shapes + [pltpu.SemaphoreType.DMA((n_sems,)), pltpu.SemaphoreType.DMA((n_sems,))],
        input_output_aliases={n_in + i: n_out + o for i, o in rider["aliases"].items()},
        compiler_params=_params(("arbitrary",) * len(grid)),
    )(*args, *rider["ins"])
    return list(res[:n_out]), list(res[n_out:])


def _softplus(x):
    return jnp.maximum(x, 0.0) + jnp.log(1.0 + jnp.exp(-jnp.abs(x)))


def _sigmoid(x):
    return 1.0 / (1.0 + jnp.exp(-x))


def _rms_fwd(x, g):
    r = lax.rsqrt(jnp.mean(x * x, axis=-1, keepdims=True) + EPS)
    return (x * r) * g


def _rms_bwd(x, g, dh):
    r = lax.rsqrt(jnp.mean(x * x, axis=-1, keepdims=True) + EPS)
    y = x * r
    dy = dh * g
    dx = r * (dy - y * jnp.mean(dy * y, axis=-1, keepdims=True))
    return dx, dh * y


def _matmul(groups, extras, epilogue, out_dtypes, m, n, tm, tn, name, rider=None):
    tm, tn = _fit(tm, m), _fit(tn, n)
    flat = [t for grp in groups for t in grp]
    n_terms, n_extra = len(flat), len(extras)

    def body(*refs):
        outs = refs[2 * n_terms + n_extra:]
        accs, pos = [], 0
        for grp in groups:
            acc = None
            for (_, _, mode) in grp:
                dims = {"nn": NN, "nt": NT, "tn": TN}[mode]
                p = _dot(refs[2 * pos][...], refs[2 * pos + 1][...], dims)
                acc = p if acc is None else acc + p
                pos += 1
            accs.append(acc)
        ex = [refs[2 * n_terms + i][...] for i in range(n_extra)]
        res = epilogue(accs, ex)
        for o_ref, r in zip(outs, res, strict=True):
            o_ref[...] = r.astype(o_ref.dtype)

    in_specs, args = [], []
    for (a, b, mode) in flat:
        if mode == "nn":
            k = a.shape[1]
            in_specs += [pl.BlockSpec((tm, k), lambda i, j: (i, 0)), pl.BlockSpec((k, tn), lambda i, j: (0, j))]
        elif mode == "nt":
            k = a.shape[1]
            in_specs += [pl.BlockSpec((tm, k), lambda i, j: (i, 0)), pl.BlockSpec((tn, k), lambda i, j: (j, 0))]
        else:
            k = a.shape[0]
            in_specs += [pl.BlockSpec((k, tm), lambda i, j: (0, i)), pl.BlockSpec((k, tn), lambda i, j: (0, j))]
        args += [a, b]
    for e in extras:
        in_specs.append(pl.BlockSpec((tm, tn), lambda i, j: (i, j)))
        args.append(e)
    outs, routs = _call(
        body, name=name, grid=(m // tm, n // tn), in_specs=in_specs,
        out_specs=[pl.BlockSpec((tm, tn), lambda i, j: (i, j)) for _ in out_dtypes],
        out_shape=[jax.ShapeDtypeStruct((m, n), d) for d in out_dtypes],
        sem=("parallel", "parallel"), args=args, rider=rider)
    return outs if rider is None else (outs, routs)


def _mm(a, b, mode, m, n, out_dtype, name, tm=512, tn=512, res=None):
    extras = [] if res is None else [res]
    epi = (lambda accs, ex: (accs[0],)) if res is None else (lambda accs, ex: (accs[0] + ex[0],))
    return _matmul([[(a, b, mode)]], extras, epi, [out_dtype], m, n, tm, tn, name)[0]


def _swiglu_fwd_epilogue(accs, ex):
    g, u = accs
    return g, u, (g * _sigmoid(g)) * u


def _swiglu_bwd_epilogue(accs, ex):
    dact, (g, u) = accs[0], ex
    sg = _sigmoid(g)
    silu = g * sg
    return dact * u * (sg * (1.0 + g * (1.0 - sg))), dact * silu


def _rmsnorm_fwd(x, g, name, tr=512):
    t, d = x.shape
    tr = min(tr, t)

    def body(x_ref, g_ref, h_ref):
        h_ref[...] = _rms_fwd(x_ref[...], g_ref[...]).astype(BF16)

    return pl.pallas_call(
        body, name=name, grid=(t // tr,),
        in_specs=[pl.BlockSpec((tr, d), lambda i: (i, 0)), pl.BlockSpec((1, d), lambda i: (0, 0))],
        out_specs=pl.BlockSpec((tr, d), lambda i: (i, 0)),
        out_shape=jax.ShapeDtypeStruct((t, d), BF16),
        compiler_params=_params(("parallel",)),
    )(x, g.reshape(1, d))


def _rmsnorm_bwd(x, g, dh, dres, name, tr=256):
    t, d = x.shape
    tr = min(tr, t)

    def body(x_ref, g_ref, dh_ref, dres_ref, dx_ref, dxb_ref, dg_ref):
        dx, dgr = _rms_bwd(x_ref[...], g_ref[...], dh_ref[...])
        dx = dx + dres_ref[...]
        dx_ref[...] = dx
        dxb_ref[...] = dx.astype(BF16)

        @pl.when(pl.program_id(0) == 0)
        def _():
            dg_ref[...] = jnp.zeros_like(dg_ref)

        dg_ref[...] += jnp.sum(dgr, axis=0, keepdims=True)

    row = pl.BlockSpec((tr, d), lambda i: (i, 0))
    vec = pl.BlockSpec((1, d), lambda i: (0, 0))
    dx, dxb, dg = pl.pallas_call(
        body, name=name, grid=(t // tr,),
        in_specs=[row, vec, row, row], out_specs=[row, row, vec],
        out_shape=[jax.ShapeDtypeStruct((t, d), F32), jax.ShapeDtypeStruct((t, d), BF16),
                   jax.ShapeDtypeStruct((1, d), F32)],
        compiler_params=_params(("arbitrary",)),
    )(x, g.reshape(1, d), dh, dres)
    return dx, dxb, dg.reshape(d)


def _merge_fwd(o_att, y, z, ga, gs, name, tr=256):
    t, wa = o_att.shape
    ws = y.shape[1]
    wg = ws // SSM_GROUPS
    tr = min(tr, t)

    def body(o_ref, y_ref, z_ref, ga_ref, gs_ref, m_ref):
        m_ref[:, 0:wa] = _rms_fwd(o_ref[...], ga_ref[...]).astype(BF16)
        for g in range(SSM_GROUPS):
            sl = slice(g * wg, (g + 1) * wg)
            zz = z_ref[:, sl]
            yz = y_ref[:, sl] * (zz * _sigmoid(zz))
            m_ref[:, wa + g * wg:wa + (g + 1) * wg] = _rms_fwd(yz, gs_ref[:, sl]).astype(BF16)

    return pl.pallas_call(
        body, name=name, grid=(t // tr,),
        in_specs=[pl.BlockSpec((tr, wa), lambda i: (i, 0)), pl.BlockSpec((tr, ws), lambda i: (i, 0)),
                  pl.BlockSpec((tr, ws), lambda i: (i, 0)), pl.BlockSpec((1, wa), lambda i: (0, 0)),
                  pl.BlockSpec((1, ws), lambda i: (0, 0))],
        out_specs=pl.BlockSpec((tr, wa + ws), lambda i: (i, 0)),
        out_shape=jax.ShapeDtypeStruct((t, wa + ws), BF16),
        compiler_params=_params(("parallel",)),
    )(o_att, y, z, ga.reshape(1, wa), gs.reshape(1, ws))


def _merge_bwd(o_att, y, z, ga, gs, dmix, name, tr=256):
    t, wa = o_att.shape
    ws = y.shape[1]
    wg = ws // SSM_GROUPS
    tr = min(tr, t)

    def body(o_ref, y_ref, z_ref, ga_ref, gs_ref, dm_ref, do_ref, dy_ref, dz_ref, dga_ref, dgs_ref):
        @pl.when(pl.program_id(0) == 0)
        def _():
            dga_ref[...] = jnp.zeros_like(dga_ref)
            dgs_ref[...] = jnp.zeros_like(dgs_ref)

        do, dgr = _rms_bwd(o_ref[...], ga_ref[...], dm_ref[:, 0:wa])
        do_ref[...] = do
        dga_ref[...] += jnp.sum(dgr, axis=0, keepdims=True)
        for g in range(SSM_GROUPS):
            sl = slice(g * wg, (g + 1) * wg)
            zz, yy = z_ref[:, sl], y_ref[:, sl]
            sg = _sigmoid(zz)
            silu = zz * sg
            dyz, dgr = _rms_bwd(yy * silu, gs_ref[:, sl], dm_ref[:, wa + g * wg:wa + (g + 1) * wg])
            dy_ref[:, sl] = dyz * silu
            dz_ref[:, sl] = (dyz * yy * (sg + silu * (1.0 - sg))).astype(BF16)
            dgs_ref[:, sl] += jnp.sum(dgr, axis=0, keepdims=True)

    rowa = pl.BlockSpec((tr, wa), lambda i: (i, 0))
    rows = pl.BlockSpec((tr, ws), lambda i: (i, 0))
    veca = pl.BlockSpec((1, wa), lambda i: (0, 0))
    vecs = pl.BlockSpec((1, ws), lambda i: (0, 0))
    do, dy, dz, dga, dgs = pl.pallas_call(
        body, name=name, grid=(t // tr,),
        in_specs=[rowa, rows, rows, veca, vecs, pl.BlockSpec((tr, wa + ws), lambda i: (i, 0))],
        out_specs=[rowa, rows, rows, veca, vecs],
        out_shape=[jax.ShapeDtypeStruct((t, wa), F32), jax.ShapeDtypeStruct((t, ws), F32),
                   jax.ShapeDtypeStruct((t, ws), BF16), jax.ShapeDtypeStruct((1, wa), F32),
                   jax.ShapeDtypeStruct((1, ws), F32)],
        compiler_params=_params(("arbitrary",)),
    )(o_att, y, z, ga.reshape(1, wa), gs.reshape(1, ws), dmix)
    return do, dy, dz, dga.reshape(wa), dgs.reshape(ws)


def _loss_head(y, target, name, tr=256):
    t, d = y.shape
    tr = min(tr, t)

    def body(y_ref, t_ref, dy_ref, dyb_ref, l_ref):
        @pl.when(pl.program_id(0) == 0)
        def _():
            l_ref[...] = jnp.zeros_like(l_ref)

        diff = y_ref[...] - t_ref[...]
        dy = diff * (1.0 / d)
        dy_ref[...] = dy
        dyb_ref[...] = dy.astype(BF16)
        part = jnp.sum(diff * diff, axis=0, keepdims=True)
        fold = part[:, 0:LANES]
        for k in range(1, d // LANES):
            fold = fold + part[:, k * LANES:(k + 1) * LANES]
        l_ref[...] += fold * (0.5 / d)

    row = pl.BlockSpec((tr, d), lambda i: (i, 0))
    return pl.pallas_call(
        body, name=name, grid=(t // tr,), in_specs=[row, row],
        out_specs=[row, row, pl.BlockSpec((1, LANES), lambda i: (0, 0))],
        out_shape=[jax.ShapeDtypeStruct((t, d), F32), jax.ShapeDtypeStruct((t, d), BF16),
                   jax.ShapeDtypeStruct((1, LANES), F32)],
        compiler_params=_params(("arbitrary",)),
    )(y, target)


def _adamw(w, g, m, v, name, tr=256):
    r, c = w.shape
    tr = _fit(tr, r, 16)

    def body(w_ref, g_ref, m_ref, v_ref, d_ref, nm_ref, nv_ref):
        gg = g_ref[...]
        nm = ADAM_B1 * m_ref[...] + (1.0 - ADAM_B1) * gg
        nv = ADAM_B2 * v_ref[...] + (1.0 - ADAM_B2) * (gg * gg)
        m_hat = nm / (1.0 - ADAM_B1 ** ADAM_STEP)
        v_hat = nv / (1.0 - ADAM_B2 ** ADAM_STEP)
        d_ref[...] = -ADAM_LR * (m_hat / (jnp.sqrt(v_hat) + ADAM_EPS) + ADAM_WD * w_ref[...])
        nm_ref[...] = nm
        nv_ref[...] = nv

    blk = pl.BlockSpec((tr, c), lambda i: (i, 0))
    return pl.pallas_call(
        body, name=name, grid=(r // tr,), in_specs=[blk] * 4, out_specs=[blk] * 3,
        out_shape=[jax.ShapeDtypeStruct((r, c), F32)] * 3,
        compiler_params=_params(("parallel",)),
    )(w, g, m, v)


def _att_scores(qi, kj, scale, row0):
    z = _dot(qi, kj, NT) * scale
    lb = -_softplus(-z)
    lrm = lb - z
    if row0 is None:
        return lb, lrm, None
    mask = _iota2(z.shape, 1) < _iota2(z.shape, 0) + row0
    return lb, jnp.where(mask, lrm, 0.0), mask


def _masked(mask, v):
    return v if mask is None else jnp.where(mask, v, 0.0)


def _attention_fwd(qkv, qg, kg, name, tile=None, rider=None):
    bsz, s, w3 = qkv.shape
    hd = ATT_HEAD_DIM
    heads = w3 // (3 * hd)
    tile = min(tile or ATT_TILE, s)
    strip = min(ATT_STRIP, tile)
    nb = s // tile
    scale = hd ** -0.5

    def body(qkv_ref, qg_ref, kg_ref, o_ref, r_ref, qn_s, kn_s, vb_s, acc_s, c_s):
        qn_s[...] = _rms_fwd(qkv_ref[0, :, 0:hd], qg_ref[...]).astype(BF16)
        kn_s[...] = _rms_fwd(qkv_ref[0, :, hd:2 * hd], kg_ref[...]).astype(BF16)
        vb_s[...] = qkv_ref[0, :, 2 * hd:3 * hd].astype(BF16)
        after = _tri(LANES, lambda r, c: r > c)

        def q_loop(i, _):
            rows = pl.ds(pl.multiple_of(i * tile, tile), tile)
            acc_s[...] = jnp.zeros_like(acc_s)
            c_s[...] = jnp.zeros_like(c_s)

            def key_tile(j, diagonal):
                cols = pl.ds(pl.multiple_of(j * tile, tile), tile)
                kj, vj = kn_s[cols, :], vb_s[cols, :]
                strips = range(tile // strip)
                subs = [slice(r * strip, (r + 1) * strip) for r in strips]
                srows = [pl.ds(pl.multiple_of(i * tile + r * strip, strip), strip) for r in strips]
                sc = [_att_scores(qn_s[srows[r], :], kj, scale, r * strip if diagonal else None) for r in strips]
                later = [_scan_lanes(sc[r][1], after, 2, reverse=True) for r in strips]
                for r in strips:
                    w = _masked(sc[r][2], jnp.exp(sc[r][0] + (later[r][0] + c_s[subs[r], :])))
                    acc_s[subs[r], :] += _dot(w, vj)
                    c_s[subs[r], :] += later[r][1]

            def k_loop(jj, _):
                key_tile(i - jj, False)
                return 0

            key_tile(i, True)
            lax.fori_loop(1, i + 1, k_loop, 0)
            o_ref[0, rows, :] = acc_s[...]
            r_ref[0, 0, rows, :] = c_s[...]
            return 0

        lax.fori_loop(0, nb, q_loop, 0)

    return _call(
        body, name=name, grid=(bsz, heads),
        in_specs=[pl.BlockSpec((1, s, 3 * hd), lambda b, h: (b, 0, h)),
                  pl.BlockSpec((1, hd), lambda b, h: (0, 0)), pl.BlockSpec((1, hd), lambda b, h: (0, 0))],
        out_specs=[pl.BlockSpec((1, s, hd), lambda b, h: (b, 0, h)),
                   pl.BlockSpec((1, 1, s, 1), lambda b, h: (b, h, 0, 0))],
        out_shape=[jax.ShapeDtypeStruct((bsz, s, heads * hd), F32),
                   jax.ShapeDtypeStruct((bsz, heads, s, 1), F32)],
        scratch_shapes=[pltpu.VMEM((s, hd), BF16), pltpu.VMEM((s, hd), BF16), pltpu.VMEM((s, hd), BF16),
                        pltpu.VMEM((tile, hd), F32), pltpu.VMEM((tile, 1), F32)],
        sem=("parallel", "parallel"), args=(qkv, qg.reshape(1, hd), kg.reshape(1, hd)), rider=rider)


def _attention_bwd(qkv, qg, kg, rtot, do, name, tile=None, rider=None):
    bsz, s, w3 = qkv.shape
    hd = ATT_HEAD_DIM
    heads = w3 // (3 * hd)
    tile = min(tile or ATT_TILE, s)
    strip = min(ATT_STRIP, tile)
    nb = s // tile
    scale = hd ** -0.5

    def body(qkv_ref, qg_ref, kg_ref, r_ref, do_ref, dqkv_ref, dqg_ref, dkg_ref,
             qn_s, kn_s, vb_s, dob_s, dqn_s, dkn_s, dv_s, c1_s, c2_s, wb_s, dzb_s):
        qn_s[...] = _rms_fwd(qkv_ref[0, :, 0:hd], qg_ref[...]).astype(BF16)
        kn_s[...] = _rms_fwd(qkv_ref[0, :, hd:2 * hd], kg_ref[...]).astype(BF16)
        vb_s[...] = qkv_ref[0, :, 2 * hd:3 * hd].astype(BF16)
        dob_s[...] = do_ref[0].astype(BF16)
        dqn_s[...] = jnp.zeros_like(dqn_s)
        dkn_s[...] = jnp.zeros_like(dkn_s)
        dv_s[...] = jnp.zeros_like(dv_s)
        upto = _tri(LANES, lambda r, c: r <= c)
        before = _tri(LANES, lambda r, c: r < c)

        def q_loop(i, _):
            rows = pl.ds(pl.multiple_of(i * tile, tile), tile)
            c1_s[...] = jnp.zeros_like(c1_s)
            c2_s[...] = jnp.zeros_like(c2_s)

            def key_tile(j, diagonal):
                cols = pl.ds(pl.multiple_of(j * tile, tile), tile)
                kj, vj = kn_s[cols, :], vb_s[cols, :]
                strips = range(tile // strip)
                subs = [slice(r * strip, (r + 1) * strip) for r in strips]
                srows = [pl.ds(pl.multiple_of(i * tile + r * strip, strip), strip) for r in strips]
                sc = [_att_scores(qn_s[srows[r], :], kj, scale, r * strip if diagonal else None) for r in strips]
                dw = [_dot(dob_s[srows[r], :], vj, NT) for r in strips]
                upto_lr = [_scan_lanes(sc[r][1], upto, 2) for r in strips]
                w = [_masked(sc[r][2], jnp.exp(sc[r][0] + (r_ref[0, 0, srows[r], :] - (upto_lr[r][0] + c1_s[subs[r], :]))))
                     for r in strips]
                e = [w[r] * dw[r] for r in strips]
                pre = [_scan_lanes(e[r], before, 1) for r in strips]
                dz = [_masked(sc[r][2], (e[r] - jnp.exp(sc[r][0]) * (e[r] + (pre[r][0] + c2_s[subs[r], :]))) * scale)
                      for r in strips]
                for r in strips:
                    wb_s[subs[r], :] = w[r].astype(BF16)
                    dzb_s[subs[r], :] = dz[r].astype(BF16)
                    c1_s[subs[r], :] += upto_lr[r][1]
                    c2_s[subs[r], :] += pre[r][1]
                dqn_s[rows, :] += _dot(dzb_s[...], kj)
                dv_s[cols, :] += _dot(wb_s[...], dob_s[rows, :], TN)
                dkn_s[cols, :] += _dot(dzb_s[...], qn_s[rows, :], TN)

            def k_loop(j, _):
                key_tile(j, False)
                return 0

            lax.fori_loop(0, i, k_loop, 0)
            key_tile(i, True)
            return 0

        lax.fori_loop(0, nb, q_loop, 0)
        dq, dgq = _rms_bwd(qkv_ref[0, :, 0:hd], qg_ref[...], dqn_s[...])
        dk, dgk = _rms_bwd(qkv_ref[0, :, hd:2 * hd], kg_ref[...], dkn_s[...])
        dqkv_ref[0, :, 0:hd] = dq.astype(BF16)
        dqkv_ref[0, :, hd:2 * hd] = dk.astype(BF16)
        dqkv_ref[0, :, 2 * hd:3 * hd] = dv_s[...].astype(BF16)
        dqg_ref[0, 0] = jnp.sum(dgq, axis=0, keepdims=True)
        dkg_ref[0, 0] = jnp.sum(dgk, axis=0, keepdims=True)

    gain = pl.BlockSpec((1, hd), lambda b, h: (0, 0))
    dgain = pl.BlockSpec((1, 1, 1, hd), lambda b, h: (b, h, 0, 0))
    return _call(
        body, name=name, grid=(bsz, heads),
        in_specs=[pl.BlockSpec((1, s, 3 * hd), lambda b, h: (b, 0, h)), gain, gain,
                  pl.BlockSpec((1, 1, s, 1), lambda b, h: (b, h, 0, 0)),
                  pl.BlockSpec((1, s, hd), lambda b, h: (b, 0, h))],
        out_specs=[pl.BlockSpec((1, s, 3 * hd), lambda b, h: (b, 0, h)), dgain, dgain],
        out_shape=[jax.ShapeDtypeStruct((bsz, s, w3), BF16),
                   jax.ShapeDtypeStruct((bsz, heads, 1, hd), F32),
                   jax.ShapeDtypeStruct((bsz, heads, 1, hd), F32)],
        scratch_shapes=[pltpu.VMEM((s, hd), BF16)] * 4 + [pltpu.VMEM((s, hd), F32)] * 3
        + [pltpu.VMEM((tile, 1), F32)] * 2 + [pltpu.VMEM((tile, tile), BF16)] * 2,
        sem=("parallel", "parallel"), args=(qkv, qg.reshape(1, hd), kg.reshape(1, hd), rtot, do), rider=rider)


def _conv_pre(pad_ref, w_ref, b_ref, s):
    pre = b_ref[...]
    for i in range(CONV_WIDTH):
        off = SUBLANES - (CONV_WIDTH - 1) + i
        pre = pre + pad_ref[off:off + s, :] * w_ref[i:i + 1, :]
    return pre


def _conv_fwd(u, w, b, name, tc=256):
    bsz, s, c = u.shape
    tc = min(tc, c)

    def body(u_ref, w_ref, b_ref, a_ref, pad_s):
        pad_s[0:SUBLANES, :] = jnp.zeros((SUBLANES, tc), F32)
        pad_s[SUBLANES:SUBLANES + s, :] = u_ref[0]
        pre = _conv_pre(pad_s, w_ref, b_ref, s)
        a_ref[0] = pre * _sigmoid(pre)

    return pl.pallas_call(
        body, name=name, grid=(bsz, c // tc),
        in_specs=[pl.BlockSpec((1, s, tc), lambda i, j: (i, 0, j)),
                  pl.BlockSpec((CONV_WIDTH, tc), lambda i, j: (0, j)), pl.BlockSpec((1, tc), lambda i, j: (0, j))],
        out_specs=pl.BlockSpec((1, s, tc), lambda i, j: (i, 0, j)),
        out_shape=jax.ShapeDtypeStruct((bsz, s, c), F32),
        scratch_shapes=[pltpu.VMEM((s + SUBLANES, tc), F32)],
        compiler_params=_params(("parallel", "parallel")),
    )(u, w, b.reshape(1, c))


def _conv_bwd(u, w, b, da, name, tc=256):
    bsz, s, c = u.shape
    tc = min(tc, c)

    def body(u_ref, w_ref, b_ref, da_ref, du_ref, dw_ref, db_ref, pad_s, gpad_s):
        @pl.when(pl.program_id(1) == 0)
        def _():
            dw_ref[...] = jnp.zeros_like(dw_ref)
            db_ref[...] = jnp.zeros_like(db_ref)

        pad_s[0:SUBLANES, :] = jnp.zeros((SUBLANES, tc), F32)
        pad_s[SUBLANES:SUBLANES + s, :] = u_ref[0]
        pre = _conv_pre(pad_s, w_ref, b_ref, s)
        sg = _sigmoid(pre)
        dpre = da_ref[0] * (sg * (1.0 + pre * (1.0 - sg)))
        gpad_s[0:s, :] = dpre
        gpad_s[s:s + SUBLANES, :] = jnp.zeros((SUBLANES, tc), F32)
        du = jnp.zeros((s, tc), F32)
        for i in range(CONV_WIDTH):
            back = CONV_WIDTH - 1 - i
            du = du + gpad_s[back:back + s, :] * w_ref[i:i + 1, :]
            off = SUBLANES - (CONV_WIDTH - 1) + i
            dw_ref[i:i + 1, :] += jnp.sum(dpre * pad_s[off:off + s, :], axis=0, keepdims=True)
        du_ref[0] = du.astype(BF16)
        db_ref[...] += jnp.sum(dpre, axis=0, keepdims=True)

    blk = pl.BlockSpec((1, s, tc), lambda j, i: (i, 0, j))
    du, dw, db = pl.pallas_call(
        body, name=name, grid=(c // tc, bsz),
        in_specs=[blk, pl.BlockSpec((CONV_WIDTH, tc), lambda j, i: (0, j)),
                  pl.BlockSpec((1, tc), lambda j, i: (0, j)), blk],
        out_specs=[blk, pl.BlockSpec((CONV_WIDTH, tc), lambda j, i: (0, j)),
                   pl.BlockSpec((1, tc), lambda j, i: (0, j))],
        out_shape=[jax.ShapeDtypeStruct((bsz, s, c), BF16), jax.ShapeDtypeStruct((CONV_WIDTH, c), F32),
                   jax.ShapeDtypeStruct((1, c), F32)],
        scratch_shapes=[pltpu.VMEM((s + SUBLANES, tc), F32), pltpu.VMEM((s + SUBLANES, tc), F32)],
        compiler_params=_params(("parallel", "arbitrary")),
    )(u, w, b.reshape(1, c), da)
    return du, dw, db.reshape(c)


def _ssd_chunk_common(b_ref, c_ref, dt_ref, dtb_ref, alog_ref):
    bm, cm = b_ref[0], c_ref[0]
    draw = dt_ref[0] + dtb_ref[...]
    dt = _softplus(draw)
    a_row = -jnp.exp(alog_ref[...])
    da = dt * a_row
    n = SSD_CHUNK
    acum = _dot_exact(da, _tri(n, lambda r, c: r >= c), ones_left=True)
    acum_t = _dot_exact(da, _tri(n, lambda r, c: r <= c), dims=TN)
    cb = _dot(cm, bm, NT)
    return bm, cm, draw, dt, a_row, acum, acum_t, cb


def _row_totals(v):
    return _dot_exact(v, jnp.ones((v.shape[1], LANES), BF16), passes=2)


def _ssd_head_common(acum, acum_t, dt, cb, x, i):
    n, p = SSD_CHUNK, SSM_HEAD_DIM
    pick = (_iota2((LANES, LANES), 0) == i).astype(BF16)
    acol = _dot_exact(acum, pick)
    dtc = _dot_exact(dt, pick)[:, :p]
    arow = acum_t[i:i + 1, :]
    causal = _iota2((n, n), 0) >= _iota2((n, n), 1)
    lm = jnp.where(causal, jnp.exp(jnp.where(causal, acol - arow, 0.0)), 0.0)
    gm = cb * lm
    xh = x[:, i * p:(i + 1) * p]
    xdt = xh * dtc
    alast = acol[n - 1:n, :]
    dte = jnp.exp(alast - acol)
    return acol, lm, gm, dtc, xh, xdt, alast, dte


def _ssd_specs(s, wg, hg, rev):
    g, n, cl = SSM_GROUPS, SSM_STATE, SSD_CHUNK
    nc = s // cl
    boff, coff = (g * wg) // n, (g * wg) // n + g
    ci = (lambda c: nc - 1 - c) if rev else (lambda c: c)
    xblk = pl.BlockSpec((1, cl, wg), lambda b, k, c: (b, ci(c), k))
    bblk = pl.BlockSpec((1, cl, n), lambda b, k, c: (b, ci(c), boff + k))
    cblk = pl.BlockSpec((1, cl, n), lambda b, k, c: (b, ci(c), coff + k))
    nblk = pl.BlockSpec((1, cl, n), lambda b, k, c: (b, ci(c), k))
    dtblk = pl.BlockSpec((1, cl, LANES), lambda b, k, c: (b, ci(c), k))
    vec = pl.BlockSpec((1, LANES), lambda b, k, c: (0, k))
    hsblk = pl.BlockSpec((1, 1, 1, wg, n), lambda b, k, c: (b, k, ci(c), 0, 0))
    return nc, xblk, bblk, cblk, nblk, dtblk, vec, hsblk


def _ssd_fwd(xbc, dtraw, dtb, alog, dskip, hg, name, rider=None):
    bsz, s, _ = xbc.shape
    g, n, p = SSM_GROUPS, SSM_STATE, SSM_HEAD_DIM
    wg = hg * p
    nc, xblk, bblk, cblk, _, dtblk, vec, hsblk = _ssd_specs(s, wg, hg, False)

    def body(x_ref, b_ref, c_ref, dt_ref, dtb_ref, alog_ref, dsk_ref, y_ref, hs_ref, h_s):
        @pl.when(pl.program_id(2) == 0)
        def _():
            h_s[...] = jnp.zeros_like(h_s)

        bm, cm, _, dt, _, acum, acum_t, cb = _ssd_chunk_common(b_ref, c_ref, dt_ref, dtb_ref, alog_ref)
        x = x_ref[0]
        hs_ref[0, 0, 0] = h_s[...]
        hd_ = range(hg)
        hc = [_ssd_head_common(acum, acum_t, dt, cb, x, i) for i in hd_]
        hprev = [h_s[i * p:(i + 1) * p, :] for i in hd_]
        ydiag = [_dot(hc[i][2], hc[i][5]) for i in hd_]
        yoff = [_dot(cm, hprev[i], NT) for i in hd_]
        st = [_dot(hc[i][5] * hc[i][7][:, :p], bm, TN) for i in hd_]
        for i in hd_:
            acol, _, _, _, xh, _, alast, _ = hc[i]
            y_ref[0, :, i * p:(i + 1) * p] = ydiag[i] + yoff[i] * jnp.exp(acol[:, :p]) + xh * dsk_ref[:, i:i + 1]
            h_s[i * p:(i + 1) * p, :] = hprev[i] * jnp.exp(alast) + st[i]

    return _call(
        body, name=name, grid=(bsz, g, nc),
        in_specs=[xblk, bblk, cblk, dtblk, vec, vec, vec],
        out_specs=[xblk, hsblk],
        out_shape=[jax.ShapeDtypeStruct((bsz, s, g * wg), F32),
                   jax.ShapeDtypeStruct((bsz, g, nc, wg, n), F32)],
        scratch_shapes=[pltpu.VMEM((wg, n), F32)],
        sem=("parallel", "parallel", "arbitrary"), args=(xbc, xbc, xbc, dtraw, dtb, alog, dskip), rider=rider)


def _ssd_bwd(xbc, dtraw, dtb, alog, dskip, hs, dy, hg, name, rider=None):
    bsz, s, _ = xbc.shape
    g, n, p, cl = SSM_GROUPS, SSM_STATE, SSM_HEAD_DIM, SSD_CHUNK
    wg = hg * p
    nc, xblk, bblk, cblk, nblk, dtblk, vec, hsblk = _ssd_specs(s, wg, hg, True)

    def body(x_ref, b_ref, c_ref, dt_ref, dtb_ref, alog_ref, dsk_ref, hs_ref, dy_ref,
             dx_ref, db_ref, dc_ref, ddt_ref, dvec_ref, dh_s):
        @pl.when(pl.program_id(2) == 0)
        def _():
            dh_s[...] = jnp.zeros_like(dh_s)
            dvec_ref[...] = jnp.zeros_like(dvec_ref)

        lane = _iota2((cl, LANES), 1)
        sub = _iota2((LANES, cl), 0)
        lane1 = _iota2((1, LANES), 1)
        last_row = _iota2((cl, 1), 0) == cl - 1
        bm, cm, draw, dt, a_row, acum, acum_t, cb = _ssd_chunk_common(b_ref, c_ref, dt_ref, dtb_ref, alog_ref)
        x = x_ref[0]
        dyc = dy_ref[0]
        hd_ = range(hg)
        hc = [_ssd_head_common(acum, acum_t, dt, cb, x, i) for i in hd_]
        dyh = [dyc[:, i * p:(i + 1) * p] for i in hd_]
        hprev = [hs_ref[0, 0, 0, i * p:(i + 1) * p, :] for i in hd_]
        dhn = [dh_s[i * p:(i + 1) * p, :] for i in hd_]
        ea = [jnp.exp(hc[i][0]) for i in hd_]
        cd = [jnp.exp(hc[i][6]) for i in hd_]
        y0 = [_dot(cm, hprev[i], NT) for i in hd_]
        dxe = [_dot(bm, dhn[i], NT) for i in hd_]
        dgm = [_dot(dyh[i], hc[i][5], NT) for i in hd_]
        gdy = [_dot(hc[i][2], dyh[i], TN) for i in hd_]
        dy0 = [dyh[i] * ea[i][:, :p] for i in hd_]
        dcm_h = [_dot(dy0[i], hprev[i]) for i in hd_]
        dh_new = [_dot(dy0[i], cm, TN) + dhn[i] * cd[i] for i in hd_]
        dbm_h = [_dot(hc[i][5] * hc[i][7][:, :p], dhn[i]) for i in hd_]
        ws = [dgm[i] * hc[i][2] for i in hd_]
        dxdt = [dxe[i] * hc[i][7][:, :p] + gdy[i] for i in hd_]
        s_y0 = [_row_totals(dyh[i] * y0[i]) for i in hd_]
        s_xe = [_row_totals(dxe[i] * hc[i][5]) for i in hd_]
        s_ws = [_row_totals(ws[i]) for i in hd_]
        s_dt = [_row_totals(dxdt[i] * hc[i][4]) for i in hd_]
        s_dd = [_row_totals(dyh[i] * hc[i][4]) for i in hd_]
        s_hh = [_row_totals(dhn[i] * hprev[i]) for i in hd_]
        dcb = jnp.zeros((cl, cl), F32)
        dcm = jnp.zeros((cl, n), F32)
        dbm = jnp.zeros((cl, n), F32)
        da_col = jnp.zeros((cl, LANES), F32)
        da_row = jnp.zeros((LANES, cl), F32)
        ddt = jnp.zeros((cl, LANES), F32)
        dd = jnp.zeros((1, LANES), F32)
        for i in hd_:
            _, lm, _, dtc, _, _, _, dte = hc[i]
            dh_s[i * p:(i + 1) * p, :] = dh_new[i]
            dd = dd + jnp.where(lane1 == i, jnp.sum(s_dd[i], axis=0, keepdims=True), 0.0)
            t1 = s_xe[i] * dte
            d_alast = jnp.sum(s_hh[i], axis=0, keepdims=True) * cd[i] + jnp.sum(t1, axis=0, keepdims=True)
            dacol = s_y0[i] * ea[i] - t1 + s_ws[i] + jnp.where(last_row, d_alast, 0.0)
            dcb = dcb + dgm[i] * lm
            dcm = dcm + dcm_h[i]
            dbm = dbm + dbm_h[i]
            dx_ref[0, :, i * p:(i + 1) * p] = dxdt[i] * dtc + dyh[i] * dsk_ref[:, i:i + 1]
            da_col = jnp.where(lane == i, dacol, da_col)
            da_row = jnp.where(sub == i, -jnp.sum(ws[i], axis=0, keepdims=True), da_row)
            ddt = jnp.where(lane == i, s_dt[i], ddt)
        dc_ref[0] = dcm + _dot(dcb, bm)
        db_ref[0] = dbm + _dot(dcb, cm, TN)
        upper = _tri(cl, lambda r, k: r <= k)
        dda = _dot_exact(da_col, upper, ones_left=True) + _dot_exact(da_row, upper, dims=NT, ones_left=True)
        ddt = ddt + dda * a_row
        ddraw = ddt * _sigmoid(draw)
        ddt_ref[0] = ddraw.astype(BF16)
        dvec_ref[0, 0, 0:1, :] += jnp.sum(ddraw, axis=0, keepdims=True)
        dvec_ref[0, 0, 1:2, :] += jnp.sum(dda * dt, axis=0, keepdims=True) * a_row
        dvec_ref[0, 0, 2:3, :] += dd

    return _call(
        body, name=name, grid=(bsz, g, nc),
        in_specs=[xblk, bblk, cblk, dtblk, vec, vec, vec, hsblk, xblk],
        out_specs=[xblk, nblk, nblk, dtblk,
                   pl.BlockSpec((1, 1, SUBLANES, LANES), lambda b, k, c: (b, k, 0, 0))],
        out_shape=[jax.ShapeDtypeStruct((bsz, s, g * wg), F32), jax.ShapeDtypeStruct((bsz, s, g * n), F32),
                   jax.ShapeDtypeStruct((bsz, s, g * n), F32), jax.ShapeDtypeStruct((bsz, s, g * LANES), BF16),
                   jax.ShapeDtypeStruct((bsz, g, SUBLANES, LANES), F32)],
        scratch_shapes=[pltpu.VMEM((wg, n), F32)],
        sem=("parallel", "parallel", "arbitrary"), args=(xbc, xbc, xbc, dtraw, dtb, alog, dskip, hs, dy),
        rider=rider)


def _coords():
    return lax.axis_index("x"), lax.axis_index("y"), lax.axis_index("c")


def _other_chips(x, y):
    return [(1 - x, y), (x, 1 - y), (1 - x, 1 - y)]


def _remote(src, dst, send_sems, recv_sems, k, to):
    return pltpu.make_async_remote_copy(src_ref=src, dst_ref=dst, send_sem=send_sems.at[k],
                                        recv_sem=recv_sems.at[k], device_id=to, device_id_type=MESH)


def _standalone(rider, name):
    r_in, r_out, n_sems = len(rider["ins"]), len(rider["outs"]), rider["n_sems"]

    def body(*refs):
        rins, routs, (send_sems, recv_sems) = refs[:r_in], refs[r_in:r_in + r_out], refs[r_in + r_out:]
        cps = rider["copies"](rins, routs, send_sems, recv_sems)
        for cp in cps:
            cp.start()
        for cp in cps:
            cp.wait()

    res = pl.pallas_call(
        body, name=name, in_specs=[ANY] * r_in, out_specs=[ANY] * r_out, out_shape=list(rider["outs"]),
        scratch_shapes=[pltpu.SemaphoreType.DMA((n_sems,)), pltpu.SemaphoreType.DMA((n_sems,))],
        input_output_aliases=dict(rider["aliases"]),
    )(*rider["ins"])
    return list(res)


def _rows_of(shape):
    return shape[1] if len(shape) == 3 else shape[0]


def _slot(ref, j, rows):
    if len(ref.shape) == 3:
        return ref.at[j, rows]
    c = ref.shape[1] // N_CHIPS
    return ref.at[rows, pl.ds(pl.multiple_of(j * c, LANES), c)]


def _gather_chips_rider(shards, side_by_side):
    def copies(rins, routs, send_sems, recv_sems):
        x, y, c = _coords()
        me = 2 * x + y
        cps = []
        for q, (w_ref, o_ref) in enumerate(zip(rins, routs, strict=True)):
            rh = w_ref.shape[0] // 2
            rows = pl.ds(c * rh, rh)
            for k, (px, py) in enumerate(_other_chips(x, y)):
                cps.append(_remote(w_ref.at[rows], _slot(o_ref, me, rows), send_sems, recv_sems, 3 * q + k,
                                   (px, py, c)))
        return cps

    outs = [jax.ShapeDtypeStruct((w.shape[0], N_CHIPS * w.shape[1]) if side else (N_CHIPS,) + w.shape, w.dtype)
            for w, side in zip(shards, side_by_side, strict=True)]
    return dict(ins=list(shards), outs=outs, aliases={}, n_sems=3 * len(shards), copies=copies)


def _gather_pair_rider(gathered):
    def copies(rins, routs, send_sems, recv_sems):
        x, y, c = _coords()
        cps = []
        for q, o_ref in enumerate(routs):
            rh = _rows_of(o_ref.shape) // 2
            for k, (px, py) in enumerate(_other_chips(x, y)):
                part = _slot(o_ref, 2 * px + py, pl.ds(c * rh, rh))
                cps.append(_remote(part, part, send_sems, recv_sems, 3 * q + k, (x, y, 1 - c)))
        return cps

    return dict(ins=list(gathered), outs=[jax.ShapeDtypeStruct(g.shape, g.dtype) for g in gathered],
                aliases={i: i for i in range(len(gathered))}, n_sems=3 * len(gathered), copies=copies)


def _reduce_pair_rider(grads):
    def copies(rins, routs, send_sems, recv_sems):
        x, y, c = _coords()
        cps, k = [], 0
        for g_ref, r_ref in zip(rins, routs, strict=True):
            rh = _rows_of(g_ref.shape) // 2
            rows = pl.ds((1 - c) * rh, rh)
            if len(g_ref.shape) == 3:
                cps.append(_remote(g_ref.at[:, rows], r_ref, send_sems, recv_sems, k, (x, y, 1 - c)))
                k += 1
            else:
                for j in range(N_CHIPS):
                    cps.append(_remote(_slot(g_ref, j, rows), r_ref.at[j], send_sems, recv_sems, k, (x, y, 1 - c)))
                    k += 1
        return cps

    def out_of(g):
        r, c = (g.shape[1], g.shape[2]) if g.ndim == 3 else (g.shape[0], g.shape[1] // N_CHIPS)
        return jax.ShapeDtypeStruct((N_CHIPS, r // 2, c), g.dtype)

    return dict(ins=list(grads), outs=[out_of(g) for g in grads], aliases={},
                n_sems=sum(1 if g.ndim == 3 else N_CHIPS for g in grads), copies=copies)


def _reduce_chips_rider(pair_sums):
    def copies(rins, routs, send_sems, recv_sems):
        x, y, c = _coords()
        cps = []
        for q, (p_ref, r_ref) in enumerate(zip(rins, routs, strict=True)):
            for k, (px, py) in enumerate(_other_chips(x, y)):
                cps.append(_remote(p_ref.at[2 * px + py], r_ref.at[k], send_sems, recv_sems, 3 * q + k, (px, py, c)))
        return cps

    return dict(ins=list(pair_sums), outs=[jax.ShapeDtypeStruct((3,) + p.shape[1:], p.dtype) for p in pair_sums],
                aliases={}, n_sems=3 * len(pair_sums), copies=copies)


def _reduce_finish_rider(sums, totals, layers, depth):
    def copies(rins, routs, send_sems, recv_sems):
        x, y, c = _coords()
        cps = []
        for q, (f_ref, o_ref) in enumerate(zip(rins[:len(sums)], routs, strict=True)):
            rh = f_ref.shape[0]
            cps.append(_remote(f_ref, o_ref.at[layers[q], pl.ds(c * rh, rh)], send_sems, recv_sems, q, (x, y, 1 - c)))
        return cps

    kept = [q for q, t in enumerate(totals) if t is not None]
    outs = [jax.ShapeDtypeStruct((depth, 2 * f.shape[0], f.shape[1]), F32) for f in sums]
    return dict(ins=list(sums) + [totals[q] for q in kept], outs=outs,
                aliases={len(sums) + k: q for k, q in enumerate(kept)}, n_sems=len(sums), copies=copies)


def _pair_add(gj, r1, ids, name, tr=256):
    stacked = gj.ndim == 3
    nj, rh, c = r1.shape
    tr = _fit(tr, rh, 16)
    nt = rh // tr

    def body(c_ref, chip_ref, g_ref, r_ref, p_ref, pb_ref):
        s = (g_ref[0] if stacked else g_ref[...]) + r_ref[0]
        pb_ref[0] = s.astype(BF16)

        @pl.when(pl.program_id(1) == chip_ref[0])
        def _():
            p_ref[...] = s

    blk_r = pl.BlockSpec((1, tr, c), lambda i, j, cr, jr: (j, i, 0))
    blk_g = (pl.BlockSpec((1, tr, c), lambda i, j, cr, jr: (j, cr[0] * nt + i, 0)) if stacked
             else pl.BlockSpec((tr, c), lambda i, j, cr, jr: (cr[0] * nt + i, j)))
    return pl.pallas_call(
        body, name=name,
        grid_spec=pltpu.PrefetchScalarGridSpec(
            num_scalar_prefetch=2, grid=(nt, nj), in_specs=[blk_g, blk_r],
            out_specs=[pl.BlockSpec((tr, c), lambda i, j, cr, jr: (i, 0)), blk_r]),
        out_shape=[jax.ShapeDtypeStruct((rh, c), F32), jax.ShapeDtypeStruct((nj, rh, c), BF16)],
        compiler_params=_params(("parallel", "arbitrary")),
    )(*ids, gj, r1)


def _chip_add(p, r2, name, tr=256):
    rh, c = p.shape
    tr = _fit(tr, rh, 16)

    def body(o_ref, r_ref, f_ref):
        f_ref[...] = ((o_ref[...] + r_ref[0].astype(F32)) + r_ref[1].astype(F32)) + r_ref[2].astype(F32)

    blk = pl.BlockSpec((tr, c), lambda i: (i, 0))
    return pl.pallas_call(
        body, name=name, grid=(rh // tr,),
        in_specs=[blk, pl.BlockSpec((3, tr, c), lambda i: (0, i, 0))], out_specs=blk,
        out_shape=jax.ShapeDtypeStruct((rh, c), F32),
        compiler_params=_params(("parallel",)),
    )(p, r2)


BIG = ["w_in", "w_out", "w_gate", "w_up", "w_down"]
LATE = ["w_out", "w_gate", "w_up", "w_down"]
FFN = ["w_gate", "w_up", "w_down"]
MIX = ["w_in", "w_out"]


SIDE_BY_SIDE = ("w_gate", "w_up")


def _set_own_slot(gathered, shard):
    my_chip = 2 * lax.axis_index("x") + lax.axis_index("y")
    if gathered.ndim == 3:
        return lax.dynamic_update_slice(gathered, shard[None], (my_chip, 0, 0))
    return lax.dynamic_update_slice(gathered, shard, (0, my_chip * shard.shape[1]))


class _GatherPlan:
    def __init__(self, late, next_in):
        self.late, self.next_in, self.parts = late, next_in, {}

    def rider(self, host):
        if self.late is None:
            return None
        if host == "attention_fwd":
            return _gather_chips_rider([self.late[n] for n in LATE], [n in SIDE_BY_SIDE for n in LATE])
        if host == "ssd_fwd":
            return _gather_pair_rider([self.parts[n] for n in LATE])
        if self.next_in is None:
            return None
        return (_gather_chips_rider([self.next_in], [False]) if host == "ffn_gate_up"
                else _gather_pair_rider([self.parts["w_in"]]))

    def collect(self, host, outs):
        if outs:
            self.parts.update(zip(LATE if host in ("attention_fwd", "ssd_fwd") else ["w_in"], outs, strict=True))

    def late_gathered(self):
        return {n: _set_own_slot(self.parts[n], self.late[n]) for n in LATE}

    def next_gathered(self):
        return _set_own_slot(self.parts["w_in"], self.next_in)


def _gather_now(shard, tag):
    part = _standalone(_gather_chips_rider([shard], [False]), f"allgather_chips_{tag}")
    return _set_own_slot(_standalone(_gather_pair_rider(part), f"allgather_pair_{tag}")[0], shard)


class _ReducePlan:
    PAIR = {"ffn_down_dgrad": "mix", "ffn_gate_up_dgrad": "ffn"}
    CHIPS = {"ssd_bwd": "mix", "attention_bwd": "ffn"}

    def __init__(self, mix, mix_layer, ffn_layer, totals, depth, ids):
        self.groups = {} if mix is None else {"mix": (MIX, mix, mix_layer)}
        self.ffn_layer, self.totals, self.depth, self.ids = ffn_layer, dict(totals), depth, ids
        self.p, self.pb, self.f = {}, {}, {}

    def add_ffn(self, grads):
        if self.ids is not None:
            self.groups["ffn"] = (FFN, grads, self.ffn_layer)

    def _present(self):
        return [(n, layer) for names, _, layer in self.groups.values() for n in names if n in self.f]

    def rider(self, host):
        if host == "proj_in_dgrad":
            done = self._present()
            if not done:
                return None
            return _reduce_finish_rider([self.f[n] for n, _ in done], [self.totals.get(n) for n, _ in done],
                                        [layer for _, layer in done], self.depth)
        group = self.groups.get(self.PAIR.get(host) or self.CHIPS.get(host))
        if group is None:
            return None
        names, grads, _ = group
        return (_reduce_pair_rider([grads[n] for n in names]) if host in self.PAIR
                else _reduce_chips_rider([self.pb[n] for n in names]))

    def collect(self, host, outs):
        if not outs:
            return
        if host == "proj_in_dgrad":
            c = lax.axis_index("c")
            for (n, layer), t in zip(self._present(), outs, strict=True):
                self.totals[n] = lax.dynamic_update_slice(t, self.f[n][None], (layer, c * self.f[n].shape[0], 0))
            return
        names, grads, layer = self.groups[self.PAIR.get(host) or self.CHIPS.get(host)]
        for n, got in zip(names, outs, strict=True):
            if host in self.PAIR:
                self.p[n], self.pb[n] = _pair_add(grads[n], got, self.ids, f"rs_pair_add_{n}_layer{layer}")
            else:
                self.f[n] = _chip_add(self.p[n], got, f"rs_chip_add_{n}_layer{layer}")

    def run_now(self, tag):
        self.collect("ffn_down_dgrad", _standalone(self.rider("ffn_down_dgrad"), f"rs_pair_{tag}"))
        self.collect("ssd_bwd", _standalone(self.rider("ssd_bwd"), f"rs_chips_{tag}"))
        self.collect("proj_in_dgrad", _standalone(self.rider("proj_in_dgrad"), f"rs_finish_{tag}"))
        return self.totals


def _small_exchange(v, name, reduce):
    rows = v.shape[0]

    def body(v_ref, o_ref, *rest):
        buf = rest[0] if reduce else o_ref
        send_sems, recv_sems = rest[-2], rest[-1]
        x, y, c = _coords()
        me = 4 * x + 2 * y + c
        buf[me] = v_ref[...]
        cps = []
        for r in range(1, N_DEV):
            peer = (lax.bitwise_xor(x, (r >> 2) & 1), lax.bitwise_xor(y, (r >> 1) & 1), lax.bitwise_xor(c, r & 1))
            cps.append(_remote(v_ref, buf.at[me], send_sems, recv_sems, r - 1, peer))
        for cp in cps:
            cp.start()
        for r in range(1, N_DEV):
            src = buf.at[lax.bitwise_xor(me, r)]
            _remote(src, src, send_sems, recv_sems, r - 1, (x, y, c)).wait_recv()
        for cp in cps:
            cp.wait_send()
        if reduce:
            acc = buf[0]
            for d in range(1, N_DEV):
                acc = acc + buf[d]
            o_ref[...] = acc
            o_ref[0:1, :] = jnp.broadcast_to(jnp.sum(acc[0:1, :], axis=1, keepdims=True), (1, LANES))

    scratch = [pltpu.SemaphoreType.DMA((N_DEV - 1,)), pltpu.SemaphoreType.DMA((N_DEV - 1,))]
    if reduce:
        scratch = [pltpu.VMEM((N_DEV, rows, LANES), F32)] + scratch
    out_shape = (rows, LANES) if reduce else (N_DEV, rows, LANES)
    return pl.pallas_call(
        body, name=name, in_specs=[VMEM], out_specs=VMEM,
        out_shape=jax.ShapeDtypeStruct(out_shape, F32), scratch_shapes=scratch,
    )(v)


def _pack(parts):
    flat = []
    for a in parts:
        a = a.reshape(-1)
        flat.append(jnp.pad(a, (0, (-a.shape[0]) % LANES)))
    v = jnp.concatenate(flat)
    v = jnp.pad(v, (0, (-v.shape[0]) % (SUBLANES * LANES)))
    return v.reshape(-1, LANES)


def _unpack(slab, shapes):
    flat = slab.reshape(-1)
    out, off = [], 0
    for shp in shapes:
        size = 1
        for d in shp:
            size *= d
        out.append(flat[off:off + size].reshape(shp))
        off += size + (-size) % LANES
    return out


def _group_slots(a, hg):
    lead = a.shape[:-1]
    a = a.reshape(lead + (SSM_GROUPS, hg))
    a = jnp.pad(a, [(0, 0)] * len(lead) + [(0, 0), (0, LANES - hg)])
    return a.reshape(lead + (SSM_GROUPS * LANES,))


def _ungroup_slots(a, hg):
    lead = a.shape[:-1]
    return a.reshape(lead + (SSM_GROUPS, LANES))[..., :hg].reshape(lead + (SSM_GROUPS * hg,))


def _layer_fwd(x, p, bsz, s, plan, late_params=None):
    t, d = x.shape
    aw, sw, cd, hg = p["aw"], p["sw"], p["cd"], p["hg"]
    h = _rmsnorm_fwd(x, p["norm_mix"], "norm_mix_fwd")
    qkv = _mm(h, p["wqkv"], "nn", t, 3 * aw, F32, "proj_qkv", tm=1024, tn=512)
    z = _mm(h, p["wz"], "nn", t, sw, F32, "proj_z", tm=1024, tn=512)
    xbc = _mm(h, p["wxbc"], "nn", t, cd, F32, "proj_xbc", tm=1024, tn=512)
    dtraw = _mm(h, p["wdt"], "nn", t, SSM_GROUPS * LANES, F32, "proj_dt", tm=1024, tn=SSM_GROUPS * LANES)
    qkv3 = qkv.reshape(bsz, s, 3 * aw)
    (o_att, rtot), sent = _attention_fwd(qkv3, p["q_gain"], p["k_gain"], "attention_fwd",
                                         rider=plan.rider("attention_fwd"))
    plan.collect("attention_fwd", sent)
    xbc3 = xbc.reshape(bsz, s, cd)
    xact = _conv_fwd(xbc3, p["conv_w"], p["conv_b"], "conv_fwd")
    dt3 = dtraw.reshape(bsz, s, SSM_GROUPS * LANES)
    (y, hs), sent = _ssd_fwd(xact, dt3, p["dt_bias"], p["a_log"], p["d_skip"], hg, "ssd_fwd",
                             rider=plan.rider("ssd_fwd"))
    plan.collect("ssd_fwd", sent)
    if plan.late is not None:
        p = {**p, **late_params(plan.late_gathered())}
    dff = p["wg"].shape[1]
    o2, y2 = o_att.reshape(t, aw), y.reshape(t, sw)
    mix = _merge_fwd(o2, y2, z, p["attn_out_gain"], p["ssm_out_gain"], "merge_fwd")
    x1 = _mm(mix, p["wout"], "nn", t, d, F32, "proj_out", tm=1024, tn=512, res=x)
    h2 = _rmsnorm_fwd(x1, p["norm_ffn"], "norm_ffn_fwd")
    (gate, up, act), _ = _hosted_matmul(
        [[(h2, p["wg"], "nn")], [(h2, p["wu"], "nn")]], [], _swiglu_fwd_epilogue,
        [F32, F32, BF16], t, dff, 1024, 512, "ffn_gate_up", plan)
    (x2,), _ = _hosted_matmul([[(act, p["wd"], "nn")]], [x1], lambda accs, ex: (accs[0] + ex[0],),
                              [F32], t, d, 512, 512, "ffn_down", plan)
    saved = dict(x=x, h=h, qkv3=qkv3, z=z, xbc3=xbc3, dt3=dt3, o2=o2, rtot=rtot, xact=xact, hs=hs, y2=y2,
                 mix=mix, x1=x1, h2=h2, gate=gate, up=up, act=act)
    return x2, saved, p


def _hosted_matmul(groups, extras, epilogue, out_dtypes, m, n, tm, tn, name, plan):
    rider = plan.rider(name)
    if rider is None:
        return _matmul(groups, extras, epilogue, out_dtypes, m, n, tm, tn, name), []
    outs, sent = _matmul(groups, extras, epilogue, out_dtypes, m, n, tm, tn, name, rider=rider)
    plan.collect(name, sent)
    return outs, sent


def _layer_bwd(dx2, dx2b, p, sv, bsz, s, plan, ffn_to_chips=None):
    t, d = dx2.shape
    aw, sw, cd, hg = p["aw"], p["sw"], p["cd"], p["hg"]
    dff = p["wg"].shape[1]
    gr = {}
    (dgate, dup), _ = _hosted_matmul([[(dx2b, p["wd"], "nt")]], [sv["gate"], sv["up"]], _swiglu_bwd_epilogue,
                                     [BF16, BF16], t, dff, 1024, 512, "ffn_down_dgrad", plan)
    gr["wd"] = _mm(sv["act"], dx2b, "tn", dff, d, F32, "ffn_down_wgrad")
    gr["wg"] = _mm(sv["h2"], dgate, "tn", d, dff, F32, "ffn_gate_wgrad")
    gr["wu"] = _mm(sv["h2"], dup, "tn", d, dff, F32, "ffn_up_wgrad")
    if ffn_to_chips is not None:
        plan.add_ffn(ffn_to_chips(gr))
    (dh2,), _ = _hosted_matmul([[(dgate, p["wg"], "nt"), (dup, p["wu"], "nt")]], [], lambda accs, ex: (accs[0],),
                               [F32], t, d, 512, 256, "ffn_gate_up_dgrad", plan)
    dx1, dx1b, gr["norm_ffn"] = _rmsnorm_bwd(sv["x1"], p["norm_ffn"], dh2, dx2, "norm_ffn_bwd")
    dmix = _mm(dx1b, p["wout"], "nt", t, aw + sw, F32, "proj_out_dgrad", tm=1024)
    gr["wout"] = _mm(sv["mix"], dx1b, "tn", aw + sw, d, F32, "proj_out_wgrad")
    do, dy, dz, gr["attn_out_gain"], gr["ssm_out_gain"] = _merge_bwd(
        sv["o2"], sv["y2"], sv["z"], p["attn_out_gain"], p["ssm_out_gain"], dmix, "merge_bwd")
    (dxs, dbm, dcm, ddt, dvec), sent = _ssd_bwd(sv["xact"], sv["dt3"], p["dt_bias"], p["a_log"], p["d_skip"],
                                                sv["hs"], dy.reshape(bsz, s, sw), hg, "ssd_bwd",
                                                rider=plan.rider("ssd_bwd"))
    plan.collect("ssd_bwd", sent)
    dvec = jnp.sum(dvec, axis=0).reshape(SSM_GROUPS, SUBLANES, LANES)
    gr["dt_bias"], gr["a_log"], gr["d_skip"] = (dvec[:, k, :hg].reshape(-1) for k in range(3))
    dxact = jnp.concatenate([dxs, dbm, dcm], axis=-1)
    dxbc, gr["conv_w"], gr["conv_b"] = _conv_bwd(sv["xbc3"], p["conv_w"], p["conv_b"], dxact, "conv_bwd")
    (dqkv, dqg, dkg), sent = _attention_bwd(sv["qkv3"], p["q_gain"], p["k_gain"], sv["rtot"],
                                            do.reshape(bsz, s, aw), "attention_bwd",
                                            rider=plan.rider("attention_bwd"))
    plan.collect("attention_bwd", sent)
    gr["q_gain"] = jnp.sum(dqg, axis=(0, 1, 2))
    gr["k_gain"] = jnp.sum(dkg, axis=(0, 1, 2))
    dqkv, dxbc, ddt = dqkv.reshape(t, 3 * aw), dxbc.reshape(t, cd), ddt.reshape(t, SSM_GROUPS * LANES)
    (dh,), _ = _hosted_matmul(
        [[(dqkv, p["wqkv"], "nt"), (dz, p["wz"], "nt"), (dxbc, p["wxbc"], "nt"), (ddt, p["wdt"], "nt")]],
        [], lambda accs, ex: (accs[0],), [F32], t, d, 512, 512, "proj_in_dgrad", plan)
    h = sv["h"]
    gr["wqkv"] = _mm(h, dqkv, "tn", d, 3 * aw, F32, "proj_qkv_wgrad")
    gr["wz"] = _mm(h, dz, "tn", d, sw, F32, "proj_z_wgrad")
    gr["wxbc"] = _mm(h, dxbc, "tn", d, cd, F32, "proj_xbc_wgrad")
    gr["wdt"] = _mm(h, ddt, "tn", d, SSM_GROUPS * LANES, F32, "proj_dt_wgrad", tn=SSM_GROUPS * LANES)
    dx, dxb, gr["norm_mix"] = _rmsnorm_bwd(sv["x"], p["norm_mix"], dh, dx1, "norm_mix_bwd")
    return dx, dxb, gr


SMALL = ["norm_mix", "q_gain", "k_gain", "conv_w", "conv_b", "dt_bias", "a_log", "d_skip",
         "attn_out_gain", "ssm_out_gain", "norm_ffn"]
ORDER = ["norm_mix", "w_in", "q_gain", "k_gain", "conv_w", "conv_b", "dt_bias", "a_log", "d_skip",
         "attn_out_gain", "ssm_out_gain", "w_out", "norm_ffn", "w_gate", "w_up", "w_down"]


def kernel(x, norm_mix, w_in, q_gain, k_gain, conv_w, conv_b, dt_bias, a_log, d_skip, attn_out_gain, ssm_out_gain, w_out, norm_ffn, w_gate, w_up, w_down, loss_target, m_norm_mix, m_w_in, m_q_gain, m_k_gain, m_conv_w, m_conv_b, m_dt_bias, m_a_log, m_d_skip, m_attn_out_gain, m_ssm_out_gain, m_w_out, m_norm_ffn, m_w_gate, m_w_up, m_w_down, v_norm_mix, v_w_in, v_q_gain, v_k_gain, v_conv_w, v_conv_b, v_dt_bias, v_a_log, v_d_skip, v_attn_out_gain, v_ssm_out_gain, v_w_out, v_norm_ffn, v_w_gate, v_w_up, v_w_down):
    w = dict(norm_mix=norm_mix, w_in=w_in, q_gain=q_gain, k_gain=k_gain, conv_w=conv_w, conv_b=conv_b,
             dt_bias=dt_bias, a_log=a_log, d_skip=d_skip, attn_out_gain=attn_out_gain, ssm_out_gain=ssm_out_gain,
             w_out=w_out, norm_ffn=norm_ffn, w_gate=w_gate, w_up=w_up, w_down=w_down)
    mom = dict(norm_mix=m_norm_mix, w_in=m_w_in, q_gain=m_q_gain, k_gain=m_k_gain, conv_w=m_conv_w,
               conv_b=m_conv_b, dt_bias=m_dt_bias, a_log=m_a_log, d_skip=m_d_skip,
               attn_out_gain=m_attn_out_gain, ssm_out_gain=m_ssm_out_gain, w_out=m_w_out, norm_ffn=m_norm_ffn,
               w_gate=m_w_gate, w_up=m_w_up, w_down=m_w_down)
    var = dict(norm_mix=v_norm_mix, w_in=v_w_in, q_gain=v_q_gain, k_gain=v_k_gain, conv_w=v_conv_w,
               conv_b=v_conv_b, dt_bias=v_dt_bias, a_log=v_a_log, d_skip=v_d_skip,
               attn_out_gain=v_attn_out_gain, ssm_out_gain=v_ssm_out_gain, w_out=v_w_out, norm_ffn=v_norm_ffn,
               w_gate=v_w_gate, w_up=v_w_up, w_down=v_w_down)

    bsz, s, d = x.shape
    t = bsz * s
    depth = norm_mix.shape[0]
    aw = attn_out_gain.shape[1]
    sw = ssm_out_gain.shape[1]
    cd = conv_b.shape[1]
    hs_n = dt_bias.shape[1]
    hg = hs_n // SSM_GROUPS
    heads = aw // ATT_HEAD_DIM
    in_dim = 3 * aw + sw + cd + hs_n
    dff = w_gate.shape[2] * N_CHIPS
    cs = conv_w.shape[2]
    my_chip = 2 * lax.axis_index("x") + lax.axis_index("y")
    ids = (lax.axis_index("c").astype(jnp.int32).reshape(1), my_chip.astype(jnp.int32).reshape(1))
    wb ={n: w[n].astype(BF16) for n in BIG}
    conv_all = _small_exchange(_pack([conv_w]), "allgather_conv_w", False)
    conv_full = jnp.concatenate(
        [_unpack(conv_all[2 * j], [conv_w.shape])[0] for j in range(N_CHIPS)], axis=-1)

    def in_params(l, gat_in):
        win = jnp.transpose(gat_in, (1, 0, 2)).reshape(d, in_dim)
        wqkv = win[:, :3 * aw].reshape(d, 3, heads, ATT_HEAD_DIM)
        wqkv = jnp.transpose(wqkv, (0, 2, 1, 3)).reshape(d, 3 * aw)
        return dict(aw=aw, sw=sw, cd=cd, hg=hg, norm_mix=norm_mix[l], wqkv=wqkv, wz=win[:, 3 * aw:3 * aw + sw],
                    wxbc=win[:, 3 * aw + sw:3 * aw + sw + cd], wdt=_group_slots(win[:, 3 * aw + sw + cd:], hg),
                    q_gain=q_gain[l], k_gain=k_gain[l], conv_w=conv_full[l], conv_b=conv_b[l],
                    dt_bias=_group_slots(dt_bias[l], hg).reshape(1, -1),
                    a_log=_group_slots(a_log[l], hg).reshape(1, -1),
                    d_skip=_group_slots(d_skip[l], hg).reshape(1, -1),
                    attn_out_gain=attn_out_gain[l], ssm_out_gain=ssm_out_gain[l], norm_ffn=norm_ffn[l])

    def late_params(gat):
        return dict(wout=gat["w_out"].reshape(aw + sw, d), wd=gat["w_down"].reshape(dff, d),
                    wg=gat["w_gate"], wu=gat["w_up"])

    def ffn_to_chips(gr):
        return {"w_gate": gr["wg"], "w_up": gr["wu"], "w_down": gr["wd"].reshape(N_CHIPS, dff // N_CHIPS, d)}

    def mix_to_chips(gr):
        gqkv = gr["wqkv"].reshape(d, heads, 3, ATT_HEAD_DIM)
        gqkv = jnp.transpose(gqkv, (0, 2, 1, 3)).reshape(d, 3 * aw)
        gin = jnp.concatenate([gqkv, gr["wz"], gr["wxbc"], _ungroup_slots(gr["wdt"], hg)], axis=-1)
        return {"w_in": jnp.transpose(gin.reshape(d, N_CHIPS, in_dim // N_CHIPS), (1, 0, 2)),
                "w_out": gr["wout"].reshape(N_CHIPS, (aw + sw) // N_CHIPS, d)}

    xt = x.reshape(t, d)
    saved, params = [], []
    gat_in = _gather_now(wb["w_in"][0], "first")
    for l in range(depth):
        plan = _GatherPlan({n: wb[n][l] for n in LATE}, wb["w_in"][l + 1] if l + 1 < depth else None)
        xt, sv, p = _layer_fwd(xt, in_params(l, gat_in), bsz, s, plan, late_params)
        saved.append(sv)
        params.append(p)
        if l + 1 < depth:
            gat_in = plan.next_gathered()
    dxt, dxb, loss_lanes = _loss_head(xt, loss_target.reshape(t, d), "loss_head")

    grads = [None] * depth
    pending, totals = None, {}
    for l in reversed(range(depth)):
        plan = _ReducePlan(pending, l + 1, l, totals, depth, ids)
        dxt, dxb, grads[l] = _layer_bwd(dxt, dxb, params[l], saved[l], bsz, s, plan, ffn_to_chips)
        totals = plan.totals
        pending = mix_to_chips(grads[l])
    g = _ReducePlan(pending, 0, None, totals, depth, ids).run_now("first_layer")
    grad_x = dxt.reshape(bsz, s, d)

    def stack(name):
        return jnp.stack([grads[l][name] for l in range(depth)])

    small_shapes = [(1, LANES)] + [(depth, CONV_WIDTH, cd) if n == "conv_w" else w[n].shape for n in SMALL]
    small = _small_exchange(_pack([loss_lanes] + [stack(n) for n in SMALL]), "allreduce_small", True)
    small = _unpack(small, small_shapes)
    loss = small[0][0, 0]
    for n, a in zip(SMALL, small[1:], strict=True):
        g[n] = a
    g["conv_w"] = lax.dynamic_slice_in_dim(g["conv_w"], my_chip * cs, cs, axis=2)

    delta, new_m, new_v = {}, {}, {}
    for n in BIG:
        shp = w[n].shape
        two_d = (shp[0] * shp[1], shp[2])
        dl, nm, nv = _adamw(w[n].reshape(two_d), g[n].reshape(two_d), mom[n].reshape(two_d),
                            var[n].reshape(two_d), f"adamw_{n}")
        delta[n], new_m[n], new_v[n] = dl.reshape(shp), nm.reshape(shp), nv.reshape(shp)
    shapes = [w[n].shape for n in SMALL]
    dl, nm, nv = _adamw(_pack([w[n] for n in SMALL]), _pack([g[n] for n in SMALL]),
                        _pack([mom[n] for n in SMALL]), _pack([var[n] for n in SMALL]), "adamw_small")
    for n, a, b, c in zip(SMALL, _unpack(dl, shapes), _unpack(nm, shapes), _unpack(nv, shapes), strict=True):
        delta[n], new_m[n], new_v[n] = a, b, c

    return (loss, grad_x, *[g[n] for n in ORDER], *[delta[n] for n in ORDER],
            *[new_m[n] for n in ORDER], *[new_v[n] for n in ORDER])
```

```python
import jax
import jax.numpy as jnp
from jax import lax
from jax.experimental import pallas as pl
from jax.experimental.pallas import tpu as pltpu

F32 = jnp.float32
BF16 = jnp.bfloat16
MESH = pl.DeviceIdType.MESH
ANY = pl.BlockSpec(memory_space=pl.ANY)
VMEM = pl.BlockSpec(memory_space=pltpu.VMEM)

EPS = 1e-6
ATT_HEAD_DIM = 128
SSM_HEAD_DIM = 64
SSM_GROUPS = 2
SSM_STATE = 128
SSD_CHUNK = 128
CONV_WIDTH = 4
LANES = 128
SUBLANES = 8
ATT_TILE = 256
ATT_STRIP = 128
N_CHIPS = 4
N_DEV = 8

ADAM_LR = 0.001
ADAM_B1 = 0.9
ADAM_B2 = 0.999
ADAM_EPS = 1e-08
ADAM_WD = 0.01
ADAM_STEP = 10

VMEM_LIMIT = 48 * 1024 * 1024

NN = (((1,), (0,)), ((), ()))
NT = (((1,), (1,)), ((), ()))
TN = (((0,), (0,)), ((), ()))


def _dot(a, b, dims=NN):
    return lax.dot_general(a.astype(BF16), b.astype(BF16), dims, preferred_element_type=F32)


def _dot_exact(x, ones, dims=NN, passes=3, ones_left=False):
    acc = None
    rem = x
    for _ in range(passes):
        piece = rem.astype(BF16)
        rem = rem - piece.astype(F32)
        p = (lax.dot_general(ones, piece, dims, preferred_element_type=F32) if ones_left
             else lax.dot_general(piece, ones, dims, preferred_element_type=F32))
        acc = p if acc is None else acc + p
    return acc


def _scan_lanes(x, tri, passes, reverse=False):
    nblk = x.shape[1] // LANES
    blocks = [x[:, k * LANES:(k + 1) * LANES] for k in range(nblk)]
    out, carry = [None] * nblk, None
    for k in (reversed(range(nblk)) if reverse else range(nblk)):
        p = _dot_exact(blocks[k], tri, passes=passes)
        out[k] = p if carry is None else p + carry
        tot = jnp.sum(blocks[k], axis=1, keepdims=True)
        carry = tot if carry is None else carry + tot
    return (out[0] if nblk == 1 else jnp.concatenate(out, axis=1)), carry


def _iota2(shape, axis):
    return lax.broadcasted_iota(jnp.int32, shape, axis)


def _tri(n, cmp):
    return cmp(_iota2((n, n), 0), _iota2((n, n), 1)).astype(BF16)


def _sum_all(v):
    return jnp.sum(jnp.sum(v, axis=1, keepdims=True), axis=0, keepdims=True)


def _fit(tile, dim, unit=LANES):
    if dim <= tile:
        return dim
    return max(k for k in range(unit, tile + 1, unit) if dim % k == 0)


def _params(sem):
    return pltpu.CompilerParams(dimension_semantics=sem, vmem_limit_bytes=VMEM_LIMIT)


def _call(body, *, name, grid, in_specs, out_specs, out_shape, sem, args, scratch_shapes=(), rider=None):
    in_specs, out_specs, out_shape = list(in_specs), list(out_specs), list(out_shape)
    scratch_shapes = list(scratch_shapes)
    if rider is None:
        res = pl.pallas_call(body, name=name, grid=grid, in_specs=in_specs, out_specs=out_specs,
                             out_shape=out_shape, scratch_shapes=scratch_shapes,
                             compiler_params=_params(sem))(*args)
        return list(res), []
    n_in, n_out, n_scr = len(in_specs), len(out_specs), len(scratch_shapes)
    r_in, r_out, n_sems = len(rider["ins"]), len(rider["outs"]), rider["n_sems"]

    def hosted(*refs):
        ins, rest = refs[:n_in], refs[n_in:]
        rins, rest = rest[:r_in], rest[r_in:]
        outs, rest = rest[:n_out], rest[n_out:]
        routs, rest = rest[:r_out], rest[r_out:]
        scr, (send_sems, recv_sems) = rest[:n_scr], rest[n_scr:]
        first, last = None, None
        for d, size in enumerate(grid):
            f, e = pl.program_id(d) == 0, pl.program_id(d) == size - 1
            first = f if first is None else jnp.logical_and(first, f)
            last = e if last is None else jnp.logical_and(last, e)

        @pl.when(first)
        def _():
            for cp in rider["copies"](rins, routs, send_sems, recv_sems):
                cp.start()

        body(*ins, *outs, *scr)

        @pl.when(last)
        def _():
            for cp in rider["copies"](rins, routs, send_sems, recv_sems):
                cp.wait()

    res = pl.pallas_call(
        hosted, name=name, grid=grid, in_specs=in_specs + [ANY] * r_in, out_specs=out_specs + [ANY] * r_out,
        out_shape=out_shape + list(rider["outs"]),
        scratch_shapes=scratch_shapes + [pltpu.SemaphoreType.DMA((n_sems,)), pltpu.SemaphoreType.DMA((n_sems,))],
        input_output_aliases={n_in + i: n_out + o for i, o in rider["aliases"].items()},
        compiler_params=_params(("arbitrary",) * len(grid)),
    )(*args, *rider["ins"])
    return list(res[:n_out]), list(res[n_out:])


def _softplus(x):
    return jnp.maximum(x, 0.0) + jnp.log(1.0 + jnp.exp(-jnp.abs(x)))


def _sigmoid(x):
    return 1.0 / (1.0 + jnp.exp(-x))


def _rms_fwd(x, g):
    r = lax.rsqrt(jnp.mean(x * x, axis=-1, keepdims=True) + EPS)
    return (x * r) * g


def _rms_bwd(x, g, dh):
    r = lax.rsqrt(jnp.mean(x * x, axis=-1, keepdims=True) + EPS)
    y = x * r
    dy = dh * g
    dx = r * (dy - y * jnp.mean(dy * y, axis=-1, keepdims=True))
    return dx, dh * y


def _matmul(groups, extras, epilogue, out_dtypes, m, n, tm, tn, name, rider=None):
    tm, tn = _fit(tm, m), _fit(tn, n)
    flat = [t for grp in groups for t in grp]
    n_terms, n_extra = len(flat), len(extras)

    def body(*refs):
        outs = refs[2 * n_terms + n_extra:]
        accs, pos = [], 0
        for grp in groups:
            acc = None
            for (_, _, mode) in grp:
                dims = {"nn": NN, "nt": NT, "tn": TN}[mode]
                p = _dot(refs[2 * pos][...], refs[2 * pos + 1][...], dims)
                acc = p if acc is None else acc + p
                pos += 1
            accs.append(acc)
        ex = [refs[2 * n_terms + i][...] for i in range(n_extra)]
        res = epilogue(accs, ex)
        for o_ref, r in zip(outs, res, strict=True):
            o_ref[...] = r.astype(o_ref.dtype)

    in_specs, args = [], []
    for (a, b, mode) in flat:
        if mode == "nn":
            k = a.shape[1]
            in_specs += [pl.BlockSpec((tm, k), lambda i, j: (i, 0)), pl.BlockSpec((k, tn), lambda i, j: (0, j))]
        elif mode == "nt":
            k = a.shape[1]
            in_specs += [pl.BlockSpec((tm, k), lambda i, j: (i, 0)), pl.BlockSpec((tn, k), lambda i, j: (j, 0))]
        else:
            k = a.shape[0]
            in_specs += [pl.BlockSpec((k, tm), lambda i, j: (0, i)), pl.BlockSpec((k, tn), lambda i, j: (0, j))]
        args += [a, b]
    for e in extras:
        in_specs.append(pl.BlockSpec((tm, tn), lambda i, j: (i, j)))
        args.append(e)
    outs, routs = _call(
        body, name=name, grid=(m // tm, n // tn), in_specs=in_specs,
        out_specs=[pl.BlockSpec((tm, tn), lambda i, j: (i, j)) for _ in out_dtypes],
        out_shape=[jax.ShapeDtypeStruct((m, n), d) for d in out_dtypes],
        sem=("parallel", "parallel"), args=args, rider=rider)
    return outs if rider is None else (outs, routs)


def _mm(a, b, mode, m, n, out_dtype, name, tm=512, tn=512, res=None):
    extras = [] if res is None else [res]
    epi = (lambda accs, ex: (accs[0],)) if res is None else (lambda accs, ex: (accs[0] + ex[0],))
    return _matmul([[(a, b, mode)]], extras, epi, [out_dtype], m, n, tm, tn, name)[0]


def _swiglu_fwd_epilogue(accs, ex):
    g, u = accs
    return g, u, (g * _sigmoid(g)) * u


def _swiglu_bwd_epilogue(accs, ex):
    dact, (g, u) = accs[0], ex
    sg = _sigmoid(g)
    silu = g * sg
    return dact * u * (sg * (1.0 + g * (1.0 - sg))), dact * silu


def _rmsnorm_fwd(x, g, name, tr=512):
    t, d = x.shape
    tr = min(tr, t)

    def body(x_ref, g_ref, h_ref):
        h_ref[...] = _rms_fwd(x_ref[...], g_ref[...]).astype(BF16)

    return pl.pallas_call(
        body, name=name, grid=(t // tr,),
        in_specs=[pl.BlockSpec((tr, d), lambda i: (i, 0)), pl.BlockSpec((1, d), lambda i: (0, 0))],
        out_specs=pl.BlockSpec((tr, d), lambda i: (i, 0)),
        out_shape=jax.ShapeDtypeStruct((t, d), BF16),
        compiler_params=_params(("parallel",)),
    )(x, g.reshape(1, d))


def _rmsnorm_bwd(x, g, dh, dres, name, tr=256):
    t, d = x.shape
    tr = min(tr, t)

    def body(x_ref, g_ref, dh_ref, dres_ref, dx_ref, dxb_ref, dg_ref):
        dx, dgr = _rms_bwd(x_ref[...], g_ref[...], dh_ref[...])
        dx = dx + dres_ref[...]
        dx_ref[...] = dx
        dxb_ref[...] = dx.astype(BF16)

        @pl.when(pl.program_id(0) == 0)
        def _():
            dg_ref[...] = jnp.zeros_like(dg_ref)

        dg_ref[...] += jnp.sum(dgr, axis=0, keepdims=True)

    row = pl.BlockSpec((tr, d), lambda i: (i, 0))
    vec = pl.BlockSpec((1, d), lambda i: (0, 0))
    dx, dxb, dg = pl.pallas_call(
        body, name=name, grid=(t // tr,),
        in_specs=[row, vec, row, row], out_specs=[row, row, vec],
        out_shape=[jax.ShapeDtypeStruct((t, d), F32), jax.ShapeDtypeStruct((t, d), BF16),
                   jax.ShapeDtypeStruct((1, d), F32)],
        compiler_params=_params(("arbitrary",)),
    )(x, g.reshape(1, d), dh, dres)
    return dx, dxb, dg.reshape(d)


def _merge_fwd(o_att, y, z, ga, gs, name, tr=256):
    t, wa = o_att.shape
    ws = y.shape[1]
    wg = ws // SSM_GROUPS
    tr = min(tr, t)

    def body(o_ref, y_ref, z_ref, ga_ref, gs_ref, m_ref):
        m_ref[:, 0:wa] = _rms_fwd(o_ref[...], ga_ref[...]).astype(BF16)
        for g in range(SSM_GROUPS):
            sl = slice(g * wg, (g + 1) * wg)
            zz = z_ref[:, sl]
            yz = y_ref[:, sl] * (zz * _sigmoid(zz))
            m_ref[:, wa + g * wg:wa + (g + 1) * wg] = _rms_fwd(yz, gs_ref[:, sl]).astype(BF16)

    return pl.pallas_call(
        body, name=name, grid=(t // tr,),
        in_specs=[pl.BlockSpec((tr, wa), lambda i: (i, 0)), pl.BlockSpec((tr, ws), lambda i: (i, 0)),
                  pl.BlockSpec((tr, ws), lambda i: (i, 0)), pl.BlockSpec((1, wa), lambda i: (0, 0)),
                  pl.BlockSpec((1, ws), lambda i: (0, 0))],
        out_specs=pl.BlockSpec((tr, wa + ws), lambda i: (i, 0)),
        out_shape=jax.ShapeDtypeStruct((t, wa + ws), BF16),
        compiler_params=_params(("parallel",)),
    )(o_att, y, z, ga.reshape(1, wa), gs.reshape(1, ws))


def _merge_bwd(o_att, y, z, ga, gs, dmix, name, tr=256):
    t, wa = o_att.shape
    ws = y.shape[1]
    wg = ws // SSM_GROUPS
    tr = min(tr, t)

    def body(o_ref, y_ref, z_ref, ga_ref, gs_ref, dm_ref, do_ref, dy_ref, dz_ref, dga_ref, dgs_ref):
        @pl.when(pl.program_id(0) == 0)
        def _():
            dga_ref[...] = jnp.zeros_like(dga_ref)
            dgs_ref[...] = jnp.zeros_like(dgs_ref)

        do, dgr = _rms_bwd(o_ref[...], ga_ref[...], dm_ref[:, 0:wa])
        do_ref[...] = do
        dga_ref[...] += jnp.sum(dgr, axis=0, keepdims=True)
        for g in range(SSM_GROUPS):
            sl = slice(g * wg, (g + 1) * wg)
            zz, yy = z_ref[:, sl], y_ref[:, sl]
            sg = _sigmoid(zz)
            silu = zz * sg
            dyz, dgr = _rms_bwd(yy * silu, gs_ref[:, sl], dm_ref[:, wa + g * wg:wa + (g + 1) * wg])
            dy_ref[:, sl] = dyz * silu
            dz_ref[:, sl] = (dyz * yy * (sg + silu * (1.0 - sg))).astype(BF16)
            dgs_ref[:, sl] += jnp.sum(dgr, axis=0, keepdims=True)

    rowa = pl.BlockSpec((tr, wa), lambda i: (i, 0))
    rows = pl.BlockSpec((tr, ws), lambda i: (i, 0))
    veca = pl.BlockSpec((1, wa), lambda i: (0, 0))
    vecs = pl.BlockSpec((1, ws), lambda i: (0, 0))
    do, dy, dz, dga, dgs = pl.pallas_call(
        body, name=name, grid=(t // tr,),
        in_specs=[rowa, rows, rows, veca, vecs, pl.BlockSpec((tr, wa + ws), lambda i: (i, 0))],
        out_specs=[rowa, rows, rows, veca, vecs],
        out_shape=[jax.ShapeDtypeStruct((t, wa), F32), jax.ShapeDtypeStruct((t, ws), F32),
                   jax.ShapeDtypeStruct((t, ws), BF16), jax.ShapeDtypeStruct((1, wa), F32),
                   jax.ShapeDtypeStruct((1, ws), F32)],
        compiler_params=_params(("arbitrary",)),
    )(o_att, y, z, ga.reshape(1, wa), gs.reshape(1, ws), dmix)
    return do, dy, dz, dga.reshape(wa), dgs.reshape(ws)


def _loss_head(y, target, name, tr=256):
    t, d = y.shape
    tr = min(tr, t)

    def body(y_ref, t_ref, dy_ref, dyb_ref, l_ref):
        @pl.when(pl.program_id(0) == 0)
        def _():
            l_ref[...] = jnp.zeros_like(l_ref)

        diff = y_ref[...] - t_ref[...]
        dy = diff * (1.0 / d)
        dy_ref[...] = dy
        dyb_ref[...] = dy.astype(BF16)
        part = jnp.sum(diff * diff, axis=0, keepdims=True)
        fold = part[:, 0:LANES]
        for k in range(1, d // LANES):
            fold = fold + part[:, k * LANES:(k + 1) * LANES]
        l_ref[...] += fold * (0.5 / d)

    row = pl.BlockSpec((tr, d), lambda i: (i, 0))
    return pl.pallas_call(
        body, name=name, grid=(t // tr,), in_specs=[row, row],
        out_specs=[row, row, pl.BlockSpec((1, LANES), lambda i: (0, 0))],
        out_shape=[jax.ShapeDtypeStruct((t, d), F32), jax.ShapeDtypeStruct((t, d), BF16),
                   jax.ShapeDtypeStruct((1, LANES), F32)],
        compiler_params=_params(("arbitrary",)),
    )(y, target)


def _adamw(w, g, m, v, name, tr=256):
    r, c = w.shape
    tr = _fit(tr, r, 16)

    def body(w_ref, g_ref, m_ref, v_ref, d_ref, nm_ref, nv_ref):
        gg = g_ref[...]
        nm = ADAM_B1 * m_ref[...] + (1.0 - ADAM_B1) * gg
        nv = ADAM_B2 * v_ref[...] + (1.0 - ADAM_B2) * (gg * gg)
        m_hat = nm / (1.0 - ADAM_B1 ** ADAM_STEP)
        v_hat = nv / (1.0 - ADAM_B2 ** ADAM_STEP)
        d_ref[...] = -ADAM_LR * (m_hat / (jnp.sqrt(v_hat) + ADAM_EPS) + ADAM_WD * w_ref[...])
        nm_ref[...] = nm
        nv_ref[...] = nv

    blk = pl.BlockSpec((tr, c), lambda i: (i, 0))
    return pl.pallas_call(
        body, name=name, grid=(r // tr,), in_specs=[blk] * 4, out_specs=[blk] * 3,
        out_shape=[jax.ShapeDtypeStruct((r, c), F32)] * 3,
        compiler_params=_params(("parallel",)),
    )(w, g, m, v)


def _att_scores(qi, kj, scale, row0):
    z = _dot(qi, kj, NT) * scale
    lb = -_softplus(-z)
    lrm = lb - z
    if row0 is None:
        return lb, lrm, None
    mask = _iota2(z.shape, 1) < _iota2(z.shape, 0) + row0
    return lb, jnp.where(mask, lrm, 0.0), mask


def _masked(mask, v):
    return v if mask is None else jnp.where(mask, v, 0.0)


def _attention_fwd(qkv, qg, kg, name, tile=None, rider=None):
    bsz, s, w3 = qkv.shape
    hd = ATT_HEAD_DIM
    heads = w3 // (3 * hd)
    tile = min(tile or ATT_TILE, s)
    strip = min(ATT_STRIP, tile)
    nb = s // tile
    scale = hd ** -0.5

    def body(qkv_ref, qg_ref, kg_ref, o_ref, r_ref, qn_s, kn_s, vb_s, acc_s, c_s):
        qn_s[...] = _rms_fwd(qkv_ref[0, :, 0:hd], qg_ref[...]).astype(BF16)
        kn_s[...] = _rms_fwd(qkv_ref[0, :, hd:2 * hd], kg_ref[...]).astype(BF16)
        vb_s[...] = qkv_ref[0, :, 2 * hd:3 * hd].astype(BF16)
        after = _tri(LANES, lambda r, c: r > c)

        def q_loop(i, _):
            rows = pl.ds(pl.multiple_of(i * tile, tile), tile)
            acc_s[...] = jnp.zeros_like(acc_s)
            c_s[...] = jnp.zeros_like(c_s)

            def key_tile(j, diagonal):
                cols = pl.ds(pl.multiple_of(j * tile, tile), tile)
                kj, vj = kn_s[cols, :], vb_s[cols, :]
                strips = range(tile // strip)
                subs = [slice(r * strip, (r + 1) * strip) for r in strips]
                srows = [pl.ds(pl.multiple_of(i * tile + r * strip, strip), strip) for r in strips]
                sc = [_att_scores(qn_s[srows[r], :], kj, scale, r * strip if diagonal else None) for r in strips]
                later = [_scan_lanes(sc[r][1], after, 2, reverse=True) for r in strips]
                for r in strips:
                    w = _masked(sc[r][2], jnp.exp(sc[r][0] + (later[r][0] + c_s[subs[r], :])))
                    acc_s[subs[r], :] += _dot(w, vj)
                    c_s[subs[r], :] += later[r][1]

            def k_loop(jj, _):
                key_tile(i - jj, False)
                return 0

            key_tile(i, True)
            lax.fori_loop(1, i + 1, k_loop, 0)
            o_ref[0, rows, :] = acc_s[...]
            r_ref[0, 0, rows, :] = c_s[...]
            return 0

        lax.fori_loop(0, nb, q_loop, 0)

    return _call(
        body, name=name, grid=(bsz, heads),
        in_specs=[pl.BlockSpec((1, s, 3 * hd), lambda b, h: (b, 0, h)),
                  pl.BlockSpec((1, hd), lambda b, h: (0, 0)), pl.BlockSpec((1, hd), lambda b, h: (0, 0))],
        out_specs=[pl.BlockSpec((1, s, hd), lambda b, h: (b, 0, h)),
                   pl.BlockSpec((1, 1, s, 1), lambda b, h: (b, h, 0, 0))],
        out_shape=[jax.ShapeDtypeStruct((bsz, s, heads * hd), F32),
                   jax.ShapeDtypeStruct((bsz, heads, s, 1), F32)],
        scratch_shapes=[pltpu.VMEM((s, hd), BF16), pltpu.VMEM((s, hd), BF16), pltpu.VMEM((s, hd), BF16),
                        pltpu.VMEM((tile, hd), F32), pltpu.VMEM((tile, 1), F32)],
        sem=("parallel", "parallel"), args=(qkv, qg.reshape(1, hd), kg.reshape(1, hd)), rider=rider)


def _attention_bwd(qkv, qg, kg, rtot, do, name, tile=None, rider=None):
    bsz, s, w3 = qkv.shape
    hd = ATT_HEAD_DIM
    heads = w3 // (3 * hd)
    tile = min(tile or ATT_TILE, s)
    strip = min(ATT_STRIP, tile)
    nb = s // tile
    scale = hd ** -0.5

    def body(qkv_ref, qg_ref, kg_ref, r_ref, do_ref, dqkv_ref, dqg_ref, dkg_ref,
             qn_s, kn_s, vb_s, dob_s, dqn_s, dkn_s, dv_s, c1_s, c2_s, wb_s, dzb_s):
        qn_s[...] = _rms_fwd(qkv_ref[0, :, 0:hd], qg_ref[...]).astype(BF16)
        kn_s[...] = _rms_fwd(qkv_ref[0, :, hd:2 * hd], kg_ref[...]).astype(BF16)
        vb_s[...] = qkv_ref[0, :, 2 * hd:3 * hd].astype(BF16)
        dob_s[...] = do_ref[0].astype(BF16)
        dqn_s[...] = jnp.zeros_like(dqn_s)
        dkn_s[...] = jnp.zeros_like(dkn_s)
        dv_s[...] = jnp.zeros_like(dv_s)
        upto = _tri(LANES, lambda r, c: r <= c)
        before = _tri(LANES, lambda r, c: r < c)

        def q_loop(i, _):
            rows = pl.ds(pl.multiple_of(i * tile, tile), tile)
            c1_s[...] = jnp.zeros_like(c1_s)
            c2_s[...] = jnp.zeros_like(c2_s)

            def key_tile(j, diagonal):
                cols = pl.ds(pl.multiple_of(j * tile, tile), tile)
                kj, vj = kn_s[cols, :], vb_s[cols, :]
                strips = range(tile // strip)
                subs = [slice(r * strip, (r + 1) * strip) for r in strips]
                srows = [pl.ds(pl.multiple_of(i * tile + r * strip, strip), strip) for r in strips]
                sc = [_att_scores(qn_s[srows[r], :], kj, scale, r * strip if diagonal else None) for r in strips]
                dw = [_dot(dob_s[srows[r], :], vj, NT) for r in strips]
                upto_lr = [_scan_lanes(sc[r][1], upto, 2) for r in strips]
                w = [_masked(sc[r][2], jnp.exp(sc[r][0] + (r_ref[0, 0, srows[r], :] - (upto_lr[r][0] + c1_s[subs[r], :]))))
                     for r in strips]
                e = [w[r] * dw[r] for r in strips]
                pre = [_scan_lanes(e[r], before, 1) for r in strips]
                dz = [_masked(sc[r][2], (e[r] - jnp.exp(sc[r][0]) * (e[r] + (pre[r][0] + c2_s[subs[r], :]))) * scale)
                      for r in strips]
                for r in strips:
                    wb_s[subs[r], :] = w[r].astype(BF16)
                    dzb_s[subs[r], :] = dz[r].astype(BF16)
                    c1_s[subs[r], :] += upto_lr[r][1]
                    c2_s[subs[r], :] += pre[r][1]
                dqn_s[rows, :] += _dot(dzb_s[...], kj)
                dv_s[cols, :] += _dot(wb_s[...], dob_s[rows, :], TN)
                dkn_s[cols, :] += _dot(dzb_s[...], qn_s[rows, :], TN)

            def k_loop(j, _):
                key_tile(j, False)
                return 0

            lax.fori_loop(0, i, k_loop, 0)
            key_tile(i, True)
            return 0

        lax.fori_loop(0, nb, q_loop, 0)
        dq, dgq = _rms_bwd(qkv_ref[0, :, 0:hd], qg_ref[...], dqn_s[...])
        dk, dgk = _rms_bwd(qkv_ref[0, :, hd:2 * hd], kg_ref[...], dkn_s[...])
        dqkv_ref[0, :, 0:hd] = dq.astype(BF16)
        dqkv_ref[0, :, hd:2 * hd] = dk.astype(BF16)
        dqkv_ref[0, :, 2 * hd:3 * hd] = dv_s[...].astype(BF16)
        dqg_ref[0, 0] = jnp.sum(dgq, axis=0, keepdims=True)
        dkg_ref[0, 0] = jnp.sum(dgk, axis=0, keepdims=True)

    gain = pl.BlockSpec((1, hd), lambda b, h: (0, 0))
    dgain = pl.BlockSpec((1, 1, 1, hd), lambda b, h: (b, h, 0, 0))
    return _call(
        body, name=name, grid=(bsz, heads),
        in_specs=[pl.BlockSpec((1, s, 3 * hd), lambda b, h: (b, 0, h)), gain, gain,
                  pl.BlockSpec((1, 1, s, 1), lambda b, h: (b, h, 0, 0)),
                  pl.BlockSpec((1, s, hd), lambda b, h: (b, 0, h))],
        out_specs=[pl.BlockSpec((1, s, 3 * hd), lambda b, h: (b, 0, h)), dgain, dgain],
        out_shape=[jax.ShapeDtypeStruct((bsz, s, w3), BF16),
                   jax.ShapeDtypeStruct((bsz, heads, 1, hd), F32),
                   jax.ShapeDtypeStruct((bsz, heads, 1, hd), F32)],
        scratch_shapes=[pltpu.VMEM((s, hd), BF16)] * 4 + [pltpu.VMEM((s, hd), F32)] * 3
        + [pltpu.VMEM((tile, 1), F32)] * 2 + [pltpu.VMEM((tile, tile), BF16)] * 2,
        sem=("parallel", "parallel"), args=(qkv, qg.reshape(1, hd), kg.reshape(1, hd), rtot, do), rider=rider)


def _conv_pre(pad_ref, w_ref, b_ref, s):
    pre = b_ref[...]
    for i in range(CONV_WIDTH):
        off = SUBLANES - (CONV_WIDTH - 1) + i
        pre = pre + pad_ref[off:off + s, :] * w_ref[i:i + 1, :]
    return pre


def _conv_fwd(u, w, b, name, tc=256):
    bsz, s, c = u.shape
    tc = min(tc, c)

    def body(u_ref, w_ref, b_ref, a_ref, pad_s):
        pad_s[0:SUBLANES, :] = jnp.zeros((SUBLANES, tc), F32)
        pad_s[SUBLANES:SUBLANES + s, :] = u_ref[0]
        pre = _conv_pre(pad_s, w_ref, b_ref, s)
        a_ref[0] = pre * _sigmoid(pre)

    return pl.pallas_call(
        body, name=name, grid=(bsz, c // tc),
        in_specs=[pl.BlockSpec((1, s, tc), lambda i, j: (i, 0, j)),
                  pl.BlockSpec((CONV_WIDTH, tc), lambda i, j: (0, j)), pl.BlockSpec((1, tc), lambda i, j: (0, j))],
        out_specs=pl.BlockSpec((1, s, tc), lambda i, j: (i, 0, j)),
        out_shape=jax.ShapeDtypeStruct((bsz, s, c), F32),
        scratch_shapes=[pltpu.VMEM((s + SUBLANES, tc), F32)],
        compiler_params=_params(("parallel", "parallel")),
    )(u, w, b.reshape(1, c))


def _conv_bwd(u, w, b, da, name, tc=256):
    bsz, s, c = u.shape
    tc = min(tc, c)

    def body(u_ref, w_ref, b_ref, da_ref, du_ref, dw_ref, db_ref, pad_s, gpad_s):
        @pl.when(pl.program_id(1) == 0)
        def _():
            dw_ref[...] = jnp.zeros_like(dw_ref)
            db_ref[...] = jnp.zeros_like(db_ref)

        pad_s[0:SUBLANES, :] = jnp.zeros((SUBLANES, tc), F32)
        pad_s[SUBLANES:SUBLANES + s, :] = u_ref[0]
        pre = _conv_pre(pad_s, w_ref, b_ref, s)
        sg = _sigmoid(pre)
        dpre = da_ref[0] * (sg * (1.0 + pre * (1.0 - sg)))
        gpad_s[0:s, :] = dpre
        gpad_s[s:s + SUBLANES, :] = jnp.zeros((SUBLANES, tc), F32)
        du = jnp.zeros((s, tc), F32)
        for i in range(CONV_WIDTH):
            back = CONV_WIDTH - 1 - i
            du = du + gpad_s[back:back + s, :] * w_ref[i:i + 1, :]
            off = SUBLANES - (CONV_WIDTH - 1) + i
            dw_ref[i:i + 1, :] += jnp.sum(dpre * pad_s[off:off + s, :], axis=0, keepdims=True)
        du_ref[0] = du.astype(BF16)
        db_ref[...] += jnp.sum(dpre, axis=0, keepdims=True)

    blk = pl.BlockSpec((1, s, tc), lambda j, i: (i, 0, j))
    du, dw, db = pl.pallas_call(
        body, name=name, grid=(c // tc, bsz),
        in_specs=[blk, pl.BlockSpec((CONV_WIDTH, tc), lambda j, i: (0, j)),
                  pl.BlockSpec((1, tc), lambda j, i: (0, j)), blk],
        out_specs=[blk, pl.BlockSpec((CONV_WIDTH, tc), lambda j, i: (0, j)),
                   pl.BlockSpec((1, tc), lambda j, i: (0, j))],
        out_shape=[jax.ShapeDtypeStruct((bsz, s, c), BF16), jax.ShapeDtypeStruct((CONV_WIDTH, c), F32),
                   jax.ShapeDtypeStruct((1, c), F32)],
        scratch_shapes=[pltpu.VMEM((s + SUBLANES, tc), F32), pltpu.VMEM((s + SUBLANES, tc), F32)],
        compiler_params=_params(("parallel", "arbitrary")),
    )(u, w, b.reshape(1, c), da)
    return du, dw, db.reshape(c)


def _ssd_chunk_common(b_ref, c_ref, dt_ref, dtb_ref, alog_ref):
    bm, cm = b_ref[0], c_ref[0]
    draw = dt_ref[0] + dtb_ref[...]
    dt = _softplus(draw)
    a_row = -jnp.exp(alog_ref[...])
    da = dt * a_row
    n = SSD_CHUNK
    acum = _dot_exact(da, _tri(n, lambda r, c: r >= c), ones_left=True)
    acum_t = _dot_exact(da, _tri(n, lambda r, c: r <= c), dims=TN)
    cb = _dot(cm, bm, NT)
    return bm, cm, draw, dt, a_row, acum, acum_t, cb


def _row_totals(v):
    return _dot_exact(v, jnp.ones((v.shape[1], LANES), BF16), passes=2)


def _ssd_head_common(acum, acum_t, dt, cb, x, i):
    n, p = SSD_CHUNK, SSM_HEAD_DIM
    pick = (_iota2((LANES, LANES), 0) == i).astype(BF16)
    acol = _dot_exact(acum, pick)
    dtc = _dot_exact(dt, pick)[:, :p]
    arow = acum_t[i:i + 1, :]
    causal = _iota2((n, n), 0) >= _iota2((n, n), 1)
    lm = jnp.where(causal, jnp.exp(jnp.where(causal, acol - arow, 0.0)), 0.0)
    gm = cb * lm
    xh = x[:, i * p:(i + 1) * p]
    xdt = xh * dtc
    alast = acol[n - 1:n, :]
    dte = jnp.exp(alast - acol)
    return acol, lm, gm, dtc, xh, xdt, alast, dte


def _ssd_specs(s, wg, hg, rev):
    g, n, cl = SSM_GROUPS, SSM_STATE, SSD_CHUNK
    nc = s // cl
    boff, coff = (g * wg) // n, (g * wg) // n + g
    ci = (lambda c: nc - 1 - c) if rev else (lambda c: c)
    xblk = pl.BlockSpec((1, cl, wg), lambda b, k, c: (b, ci(c), k))
    bblk = pl.BlockSpec((1, cl, n), lambda b, k, c: (b, ci(c), boff + k))
    cblk = pl.BlockSpec((1, cl, n), lambda b, k, c: (b, ci(c), coff + k))
    nblk = pl.BlockSpec((1, cl, n), lambda b, k, c: (b, ci(c), k))
    dtblk = pl.BlockSpec((1, cl, LANES), lambda b, k, c: (b, ci(c), k))
    vec = pl.BlockSpec((1, LANES), lambda b, k, c: (0, k))
    hsblk = pl.BlockSpec((1, 1, 1, wg, n), lambda b, k, c: (b, k, ci(c), 0, 0))
    return nc, xblk, bblk, cblk, nblk, dtblk, vec, hsblk


def _ssd_fwd(xbc, dtraw, dtb, alog, dskip, hg, name, rider=None):
    bsz, s, _ = xbc.shape
    g, n, p = SSM_GROUPS, SSM_STATE, SSM_HEAD_DIM
    wg = hg * p
    nc, xblk, bblk, cblk, _, dtblk, vec, hsblk = _ssd_specs(s, wg, hg, False)

    def body(x_ref, b_ref, c_ref, dt_ref, dtb_ref, alog_ref, dsk_ref, y_ref, hs_ref, h_s):
        @pl.when(pl.program_id(2) == 0)
        def _():
            h_s[...] = jnp.zeros_like(h_s)

        bm, cm, _, dt, _, acum, acum_t, cb = _ssd_chunk_common(b_ref, c_ref, dt_ref, dtb_ref, alog_ref)
        x = x_ref[0]
        hs_ref[0, 0, 0] = h_s[...]
        hd_ = range(hg)
        hc = [_ssd_head_common(acum, acum_t, dt, cb, x, i) for i in hd_]
        hprev = [h_s[i * p:(i + 1) * p, :] for i in hd_]
        ydiag = [_dot(hc[i][2], hc[i][5]) for i in hd_]
        yoff = [_dot(cm, hprev[i], NT) for i in hd_]
        st = [_dot(hc[i][5] * hc[i][7][:, :p], bm, TN) for i in hd_]
        for i in hd_:
            acol, _, _, _, xh, _, alast, _ = hc[i]
            y_ref[0, :, i * p:(i + 1) * p] = ydiag[i] + yoff[i] * jnp.exp(acol[:, :p]) + xh * dsk_ref[:, i:i + 1]
            h_s[i * p:(i + 1) * p, :] = hprev[i] * jnp.exp(alast) + st[i]

    return _call(
        body, name=name, grid=(bsz, g, nc),
        in_specs=[xblk, bblk, cblk, dtblk, vec, vec, vec],
        out_specs=[xblk, hsblk],
        out_shape=[jax.ShapeDtypeStruct((bsz, s, g * wg), F32),
                   jax.ShapeDtypeStruct((bsz, g, nc, wg, n), F32)],
        scratch_shapes=[pltpu.VMEM((wg, n), F32)],
        sem=("parallel", "parallel", "arbitrary"), args=(xbc, xbc, xbc, dtraw, dtb, alog, dskip), rider=rider)


def _ssd_bwd(xbc, dtraw, dtb, alog, dskip, hs, dy, hg, name, rider=None):
    bsz, s, _ = xbc.shape
    g, n, p, cl = SSM_GROUPS, SSM_STATE, SSM_HEAD_DIM, SSD_CHUNK
    wg = hg * p
    nc, xblk, bblk, cblk, nblk, dtblk, vec, hsblk = _ssd_specs(s, wg, hg, True)

    def body(x_ref, b_ref, c_ref, dt_ref, dtb_ref, alog_ref, dsk_ref, hs_ref, dy_ref,
             dx_ref, db_ref, dc_ref, ddt_ref, dvec_ref, dh_s):
        @pl.when(pl.program_id(2) == 0)
        def _():
            dh_s[...] = jnp.zeros_like(dh_s)
            dvec_ref[...] = jnp.zeros_like(dvec_ref)

        lane = _iota2((cl, LANES), 1)
        sub = _iota2((LANES, cl), 0)
        lane1 = _iota2((1, LANES), 1)
        last_row = _iota2((cl, 1), 0) == cl - 1
        bm, cm, draw, dt, a_row, acum, acum_t, cb = _ssd_chunk_common(b_ref, c_ref, dt_ref, dtb_ref, alog_ref)
        x = x_ref[0]
        dyc = dy_ref[0]
        hd_ = range(hg)
        hc = [_ssd_head_common(acum, acum_t, dt, cb, x, i) for i in hd_]
        dyh = [dyc[:, i * p:(i + 1) * p] for i in hd_]
        hprev = [hs_ref[0, 0, 0, i * p:(i + 1) * p, :] for i in hd_]
        dhn = [dh_s[i * p:(i + 1) * p, :] for i in hd_]
        ea = [jnp.exp(hc[i][0]) for i in hd_]
        cd = [jnp.exp(hc[i][6]) for i in hd_]
        y0 = [_dot(cm, hprev[i], NT) for i in hd_]
        dxe = [_dot(bm, dhn[i], NT) for i in hd_]
        dgm = [_dot(dyh[i], hc[i][5], NT) for i in hd_]
        gdy = [_dot(hc[i][2], dyh[i], TN) for i in hd_]
        dy0 = [dyh[i] * ea[i][:, :p] for i in hd_]
        dcm_h = [_dot(dy0[i], hprev[i]) for i in hd_]
        dh_new = [_dot(dy0[i], cm, TN) + dhn[i] * cd[i] for i in hd_]
        dbm_h = [_dot(hc[i][5] * hc[i][7][:, :p], dhn[i]) for i in hd_]
        ws = [dgm[i] * hc[i][2] for i in hd_]
        dxdt = [dxe[i] * hc[i][7][:, :p] + gdy[i] for i in hd_]
        s_y0 = [_row_totals(dyh[i] * y0[i]) for i in hd_]
        s_xe = [_row_totals(dxe[i] * hc[i][5]) for i in hd_]
        s_ws = [_row_totals(ws[i]) for i in hd_]
        s_dt = [_row_totals(dxdt[i] * hc[i][4]) for i in hd_]
        s_dd = [_row_totals(dyh[i] * hc[i][4]) for i in hd_]
        s_hh = [_row_totals(dhn[i] * hprev[i]) for i in hd_]
        dcb = jnp.zeros((cl, cl), F32)
        dcm = jnp.zeros((cl, n), F32)
        dbm = jnp.zeros((cl, n), F32)
        da_col = jnp.zeros((cl, LANES), F32)
        da_row = jnp.zeros((LANES, cl), F32)
        ddt = jnp.zeros((cl, LANES), F32)
        dd = jnp.zeros((1, LANES), F32)
        for i in hd_:
            _, lm, _, dtc, _, _, _, dte = hc[i]
            dh_s[i * p:(i + 1) * p, :] = dh_new[i]
            dd = dd + jnp.where(lane1 == i, jnp.sum(s_dd[i], axis=0, keepdims=True), 0.0)
            t1 = s_xe[i] * dte
            d_alast = jnp.sum(s_hh[i], axis=0, keepdims=True) * cd[i] + jnp.sum(t1, axis=0, keepdims=True)
            dacol = s_y0[i] * ea[i] - t1 + s_ws[i] + jnp.where(last_row, d_alast, 0.0)
            dcb = dcb + dgm[i] * lm
            dcm = dcm + dcm_h[i]
            dbm = dbm + dbm_h[i]
            dx_ref[0, :, i * p:(i + 1) * p] = dxdt[i] * dtc + dyh[i] * dsk_ref[:, i:i + 1]
            da_col = jnp.where(lane == i, dacol, da_col)
            da_row = jnp.where(sub == i, -jnp.sum(ws[i], axis=0, keepdims=True), da_row)
            ddt = jnp.where(lane == i, s_dt[i], ddt)
        dc_ref[0] = dcm + _dot(dcb, bm)
        db_ref[0] = dbm + _dot(dcb, cm, TN)
        upper = _tri(cl, lambda r, k: r <= k)
        dda = _dot_exact(da_col, upper, ones_left=True) + _dot_exact(da_row, upper, dims=NT, ones_left=True)
        ddt = ddt + dda * a_row
        ddraw = ddt * _sigmoid(draw)
        ddt_ref[0] = ddraw.astype(BF16)
        dvec_ref[0, 0, 0:1, :] += jnp.sum(ddraw, axis=0, keepdims=True)
        dvec_ref[0, 0, 1:2, :] += jnp.sum(dda * dt, axis=0, keepdims=True) * a_row
        dvec_ref[0, 0, 2:3, :] += dd

    return _call(
        body, name=name, grid=(bsz, g, nc),
        in_specs=[xblk, bblk, cblk, dtblk, vec, vec, vec, hsblk, xblk],
        out_specs=[xblk, nblk, nblk, dtblk,
                   pl.BlockSpec((1, 1, SUBLANES, LANES), lambda b, k, c: (b, k, 0, 0))],
        out_shape=[jax.ShapeDtypeStruct((bsz, s, g * wg), F32), jax.ShapeDtypeStruct((bsz, s, g * n), F32),
                   jax.ShapeDtypeStruct((bsz, s, g * n), F32), jax.ShapeDtypeStruct((bsz, s, g * LANES), BF16),
                   jax.ShapeDtypeStruct((bsz, g, SUBLANES, LANES), F32)],
        scratch_shapes=[pltpu.VMEM((wg, n), F32)],
        sem=("parallel", "parallel", "arbitrary"), args=(xbc, xbc, xbc, dtraw, dtb, alog, dskip, hs, dy),
        rider=rider)


def _coords():
    return lax.axis_index("x"), lax.axis_index("y"), lax.axis_index("c")


def _other_chips(x, y):
    return [(1 - x, y), (x, 1 - y), (1 - x, 1 - y)]


def _remote(src, dst, send_sems, recv_sems, k, to):
    return pltpu.make_async_remote_copy(src_ref=src, dst_ref=dst, send_sem=send_sems.at[k],
                                        recv_sem=recv_sems.at[k], device_id=to, device_id_type=MESH)


def _standalone(rider, name):
    r_in, r_out, n_sems = len(rider["ins"]), len(rider["outs"]), rider["n_sems"]

    def body(*refs):
        rins, routs, (send_sems, recv_sems) = refs[:r_in], refs[r_in:r_in + r_out], refs[r_in + r_out:]
        cps = rider["copies"](rins, routs, send_sems, recv_sems)
        for cp in cps:
            cp.start()
        for cp in cps:
            cp.wait()

    res = pl.pallas_call(
        body, name=name, in_specs=[ANY] * r_in, out_specs=[ANY] * r_out, out_shape=list(rider["outs"]),
        scratch_shapes=[pltpu.SemaphoreType.DMA((n_sems,)), pltpu.SemaphoreType.DMA((n_sems,))],
        input_output_aliases=dict(rider["aliases"]),
    )(*rider["ins"])
    return list(res)


def _rows_of(shape):
    return shape[1] if len(shape) == 3 else shape[0]


def _slot(ref, j, rows):
    if len(ref.shape) == 3:
        return ref.at[j, rows]
    c = ref.shape[1] // N_CHIPS
    return ref.at[rows, pl.ds(pl.multiple_of(j * c, LANES), c)]


def _own_slot_set(shard, side_by_side):
    my_chip = 2 * lax.axis_index("x") + lax.axis_index("y")
    r, c = shard.shape
    if side_by_side:
        return lax.dynamic_update_slice(jnp.zeros((r, N_CHIPS * c), shard.dtype), shard, (0, my_chip * c))
    return lax.dynamic_update_slice(jnp.zeros((N_CHIPS, r, c), shard.dtype), shard[None], (my_chip, 0, 0))


def _gather_chips_rider(shards, side_by_side):
    def copies(rins, routs, send_sems, recv_sems):
        x, y, c = _coords()
        me = 2 * x + y
        cps = []
        for q, (w_ref, o_ref) in enumerate(zip(rins[:len(shards)], routs, strict=True)):
            rh = w_ref.shape[0] // 2
            rows = pl.ds(c * rh, rh)
            for k, (px, py) in enumerate(_other_chips(x, y)):
                cps.append(_remote(w_ref.at[rows], _slot(o_ref, me, rows), send_sems, recv_sems, 3 * q + k,
                                   (px, py, c)))
        return cps

    bases = [_own_slot_set(w, side) for w, side in zip(shards, side_by_side, strict=True)]
    return dict(ins=list(shards) + bases, outs=[jax.ShapeDtypeStruct(b.shape, b.dtype) for b in bases],
                aliases={len(shards) + i: i for i in range(len(shards))}, n_sems=3 * len(shards), copies=copies)


def _gather_pair_rider(gathered):
    def copies(rins, routs, send_sems, recv_sems):
        x, y, c = _coords()
        cps = []
        for q, o_ref in enumerate(routs):
            rh = _rows_of(o_ref.shape) // 2
            for k, (px, py) in enumerate(_other_chips(x, y)):
                part = _slot(o_ref, 2 * px + py, pl.ds(c * rh, rh))
                cps.append(_remote(part, part, send_sems, recv_sems, 3 * q + k, (x, y, 1 - c)))
        return cps

    return dict(ins=list(gathered), outs=[jax.ShapeDtypeStruct(g.shape, g.dtype) for g in gathered],
                aliases={i: i for i in range(len(gathered))}, n_sems=3 * len(gathered), copies=copies)


def _reduce_pair_rider(grads):
    def copies(rins, routs, send_sems, recv_sems):
        x, y, c = _coords()
        cps, k = [], 0
        for g_ref, r_ref in zip(rins, routs, strict=True):
            rh = _rows_of(g_ref.shape) // 2
            rows = pl.ds((1 - c) * rh, rh)
            if len(g_ref.shape) == 3:
                cps.append(_remote(g_ref.at[:, rows], r_ref, send_sems, recv_sems, k, (x, y, 1 - c)))
                k += 1
            else:
                for j in range(N_CHIPS):
                    cps.append(_remote(_slot(g_ref, j, rows), r_ref.at[j], send_sems, recv_sems, k, (x, y, 1 - c)))
                    k += 1
        return cps

    def out_of(g):
        r, c = (g.shape[1], g.shape[2]) if g.ndim == 3 else (g.shape[0], g.shape[1] // N_CHIPS)
        return jax.ShapeDtypeStruct((N_CHIPS, r // 2, c), g.dtype)

    return dict(ins=list(grads), outs=[out_of(g) for g in grads], aliases={},
                n_sems=sum(1 if g.ndim == 3 else N_CHIPS for g in grads), copies=copies)


def _reduce_chips_rider(pair_sums):
    def copies(rins, routs, send_sems, recv_sems):
        x, y, c = _coords()
        cps = []
        for q, (p_ref, r_ref) in enumerate(zip(rins, routs, strict=True)):
            for k, (px, py) in enumerate(_other_chips(x, y)):
                cps.append(_remote(p_ref.at[2 * px + py], r_ref.at[k], send_sems, recv_sems, 3 * q + k, (px, py, c)))
        return cps

    return dict(ins=list(pair_sums), outs=[jax.ShapeDtypeStruct((3,) + p.shape[1:], p.dtype) for p in pair_sums],
                aliases={}, n_sems=3 * len(pair_sums), copies=copies)


def _reduce_finish_rider(sums, totals, layers, depth):
    def copies(rins, routs, send_sems, recv_sems):
        x, y, c = _coords()
        cps = []
        for q, (f_ref, o_ref) in enumerate(zip(rins[:len(sums)], routs, strict=True)):
            rh = f_ref.shape[0]
            cps.append(_remote(f_ref, o_ref.at[layers[q], pl.ds(c * rh, rh)], send_sems, recv_sems, q, (x, y, 1 - c)))
        return cps

    kept = [q for q, t in enumerate(totals) if t is not None]
    outs = [jax.ShapeDtypeStruct((depth, 2 * f.shape[0], f.shape[1]), F32) for f in sums]
    return dict(ins=list(sums) + [totals[q] for q in kept], outs=outs,
                aliases={len(sums) + k: q for k, q in enumerate(kept)}, n_sems=len(sums), copies=copies)


def _pair_add(gj, r1, ids, name, tr=256):
    stacked = gj.ndim == 3
    nj, rh, c = r1.shape
    tr = _fit(tr, rh, 16)
    nt = rh // tr

    def body(c_ref, chip_ref, g_ref, r_ref, p_ref, pb_ref):
        s = (g_ref[0] if stacked else g_ref[...]) + r_ref[0]
        pb_ref[0] = s.astype(BF16)

        @pl.when(pl.program_id(1) == chip_ref[0])
        def _():
            p_ref[...] = s

    blk_r = pl.BlockSpec((1, tr, c), lambda i, j, cr, jr: (j, i, 0))
    blk_g = (pl.BlockSpec((1, tr, c), lambda i, j, cr, jr: (j, cr[0] * nt + i, 0)) if stacked
             else pl.BlockSpec((tr, c), lambda i, j, cr, jr: (cr[0] * nt + i, j)))
    return pl.pallas_call(
        body, name=name,
        grid_spec=pltpu.PrefetchScalarGridSpec(
            num_scalar_prefetch=2, grid=(nt, nj), in_specs=[blk_g, blk_r],
            out_specs=[pl.BlockSpec((tr, c), lambda i, j, cr, jr: (i, 0)), blk_r]),
        out_shape=[jax.ShapeDtypeStruct((rh, c), F32), jax.ShapeDtypeStruct((nj, rh, c), BF16)],
        compiler_params=_params(("parallel", "arbitrary")),
    )(*ids, gj, r1)


def _chip_add(p, r2, name, tr=256):
    rh, c = p.shape
    tr = _fit(tr, rh, 16)

    def body(o_ref, r_ref, f_ref):
        f_ref[...] = ((o_ref[...] + r_ref[0].astype(F32)) + r_ref[1].astype(F32)) + r_ref[2].astype(F32)

    blk = pl.BlockSpec((tr, c), lambda i: (i, 0))
    return pl.pallas_call(
        body, name=name, grid=(rh // tr,),
        in_specs=[blk, pl.BlockSpec((3, tr, c), lambda i: (0, i, 0))], out_specs=blk,
        out_shape=jax.ShapeDtypeStruct((rh, c), F32),
        compiler_params=_params(("parallel",)),
    )(p, r2)


BIG = ["w_in", "w_out", "w_gate", "w_up", "w_down"]
LATE = ["w_out", "w_gate", "w_up", "w_down"]
FFN = ["w_gate", "w_up", "w_down"]
MIX = ["w_in", "w_out"]


SIDE_BY_SIDE = ("w_gate", "w_up")


class _GatherPlan:
    def __init__(self, late, next_in):
        self.late, self.next_in, self.parts = late, next_in, {}

    def rider(self, host):
        if self.late is None:
            return None
        if host == "attention_fwd":
            return _gather_chips_rider([self.late[n] for n in LATE], [n in SIDE_BY_SIDE for n in LATE])
        if host == "ssd_fwd":
            return _gather_pair_rider([self.parts[n] for n in LATE])
        if self.next_in is None:
            return None
        return (_gather_chips_rider([self.next_in], [False]) if host == "ffn_gate_up"
                else _gather_pair_rider([self.parts["w_in"]]))

    def collect(self, host, outs):
        if outs:
            self.parts.update(zip(LATE if host in ("attention_fwd", "ssd_fwd") else ["w_in"], outs, strict=True))

    def late_gathered(self):
        return {n: self.parts[n] for n in LATE}

    def next_gathered(self):
        return self.parts["w_in"]


def _gather_now(shard, tag):
    part = _standalone(_gather_chips_rider([shard], [False]), f"allgather_chips_{tag}")
    return _standalone(_gather_pair_rider(part), f"allgather_pair_{tag}")[0]


class _ReducePlan:
    PAIR = {"ffn_down_dgrad": "mix", "ffn_gate_up_dgrad": "ffn"}
    CHIPS = {"ssd_bwd": "mix", "attention_bwd": "ffn"}

    def __init__(self, mix, mix_layer, ffn_layer, totals, depth, ids):
        self.groups = {} if mix is None else {"mix": (MIX, mix, mix_layer)}
        self.ffn_layer, self.totals, self.depth, self.ids = ffn_layer, dict(totals), depth, ids
        self.p, self.pb, self.f = {}, {}, {}

    def add_ffn(self, grads):
        if self.ids is not None:
            self.groups["ffn"] = (FFN, grads, self.ffn_layer)

    def _present(self):
        return [(n, layer) for names, _, layer in self.groups.values() for n in names if n in self.f]

    def rider(self, host):
        if host == "proj_in_dgrad":
            done = self._present()
            if not done:
                return None
            return _reduce_finish_rider([self.f[n] for n, _ in done], [self.totals.get(n) for n, _ in done],
                                        [layer for _, layer in done], self.depth)
        group = self.groups.get(self.PAIR.get(host) or self.CHIPS.get(host))
        if group is None:
            return None
        names, grads, _ = group
        return (_reduce_pair_rider([grads[n] for n in names]) if host in self.PAIR
                else _reduce_chips_rider([self.pb[n] for n in names]))

    def collect(self, host, outs):
        if not outs:
            return
        if host == "proj_in_dgrad":
            c = lax.axis_index("c")
            for (n, layer), t in zip(self._present(), outs, strict=True):
                self.totals[n] = lax.dynamic_update_slice(t, self.f[n][None], (layer, c * self.f[n].shape[0], 0))
            return
        names, grads, layer = self.groups[self.PAIR.get(host) or self.CHIPS.get(host)]
        for n, got in zip(names, outs, strict=True):
            if host in self.PAIR:
                self.p[n], self.pb[n] = _pair_add(grads[n], got, self.ids, f"rs_pair_add_{n}_layer{layer}")
            else:
                self.f[n] = _chip_add(self.p[n], got, f"rs_chip_add_{n}_layer{layer}")

    def run_now(self, tag):
        self.collect("ffn_down_dgrad", _standalone(self.rider("ffn_down_dgrad"), f"rs_pair_{tag}"))
        self.collect("ssd_bwd", _standalone(self.rider("ssd_bwd"), f"rs_chips_{tag}"))
        self.collect("proj_in_dgrad", _standalone(self.rider("proj_in_dgrad"), f"rs_finish_{tag}"))
        return self.totals


def _small_exchange(v, name, reduce):
    rows = v.shape[0]

    def body(v_ref, o_ref, *rest):
        buf = rest[0] if reduce else o_ref
        send_sems, recv_sems = rest[-2], rest[-1]
        x, y, c = _coords()
        me = 4 * x + 2 * y + c
        buf[me] = v_ref[...]
        cps = []
        for r in range(1, N_DEV):
            peer = (lax.bitwise_xor(x, (r >> 2) & 1), lax.bitwise_xor(y, (r >> 1) & 1), lax.bitwise_xor(c, r & 1))
            cps.append(_remote(v_ref, buf.at[me], send_sems, recv_sems, r - 1, peer))
        for cp in cps:
            cp.start()
        for r in range(1, N_DEV):
            src = buf.at[lax.bitwise_xor(me, r)]
            _remote(src, src, send_sems, recv_sems, r - 1, (x, y, c)).wait_recv()
        for cp in cps:
            cp.wait_send()
        if reduce:
            acc = buf[0]
            for d in range(1, N_DEV):
                acc = acc + buf[d]
            o_ref[...] = acc
            o_ref[0:1, :] = jnp.broadcast_to(jnp.sum(acc[0:1, :], axis=1, keepdims=True), (1, LANES))

    scratch = [pltpu.SemaphoreType.DMA((N_DEV - 1,)), pltpu.SemaphoreType.DMA((N_DEV - 1,))]
    if reduce:
        scratch = [pltpu.VMEM((N_DEV, rows, LANES), F32)] + scratch
    out_shape = (rows, LANES) if reduce else (N_DEV, rows, LANES)
    return pl.pallas_call(
        body, name=name, in_specs=[VMEM], out_specs=VMEM,
        out_shape=jax.ShapeDtypeStruct(out_shape, F32), scratch_shapes=scratch,
    )(v)


def _pack(parts):
    flat = []
    for a in parts:
        a = a.reshape(-1)
        flat.append(jnp.pad(a, (0, (-a.shape[0]) % LANES)))
    v = jnp.concatenate(flat)
    v = jnp.pad(v, (0, (-v.shape[0]) % (SUBLANES * LANES)))
    return v.reshape(-1, LANES)


def _unpack(slab, shapes):
    flat = slab.reshape(-1)
    out, off = [], 0
    for shp in shapes:
        size = 1
        for d in shp:
            size *= d
        out.append(flat[off:off + size].reshape(shp))
        off += size + (-size) % LANES
    return out


def _group_slots(a, hg):
    lead = a.shape[:-1]
    a = a.reshape(lead + (SSM_GROUPS, hg))
    a = jnp.pad(a, [(0, 0)] * len(lead) + [(0, 0), (0, LANES - hg)])
    return a.reshape(lead + (SSM_GROUPS * LANES,))


def _ungroup_slots(a, hg):
    lead = a.shape[:-1]
    return a.reshape(lead + (SSM_GROUPS, LANES))[..., :hg].reshape(lead + (SSM_GROUPS * hg,))


def _layer_fwd(x, p, bsz, s, plan, late_params=None):
    t, d = x.shape
    aw, sw, cd, hg = p["aw"], p["sw"], p["cd"], p["hg"]
    h = _rmsnorm_fwd(x, p["norm_mix"], "norm_mix_fwd")
    qkv = _mm(h, p["wqkv"], "nn", t, 3 * aw, F32, "proj_qkv", tm=1024, tn=512)
    z = _mm(h, p["wz"], "nn", t, sw, F32, "proj_z", tm=1024, tn=512)
    xbc = _mm(h, p["wxbc"], "nn", t, cd, F32, "proj_xbc", tm=1024, tn=512)
    dtraw = _mm(h, p["wdt"], "nn", t, SSM_GROUPS * LANES, F32, "proj_dt", tm=1024, tn=SSM_GROUPS * LANES)
    qkv3 = qkv.reshape(bsz, s, 3 * aw)
    (o_att, rtot), sent = _attention_fwd(qkv3, p["q_gain"], p["k_gain"], "attention_fwd",
                                         rider=plan.rider("attention_fwd"))
    plan.collect("attention_fwd", sent)
    xbc3 = xbc.reshape(bsz, s, cd)
    xact = _conv_fwd(xbc3, p["conv_w"], p["conv_b"], "conv_fwd")
    dt3 = dtraw.reshape(bsz, s, SSM_GROUPS * LANES)
    (y, hs), sent = _ssd_fwd(xact, dt3, p["dt_bias"], p["a_log"], p["d_skip"], hg, "ssd_fwd",
                             rider=plan.rider("ssd_fwd"))
    plan.collect("ssd_fwd", sent)
    if plan.late is not None:
        p = {**p, **late_params(plan.late_gathered())}
    dff = p["wg"].shape[1]
    o2, y2 = o_att.reshape(t, aw), y.reshape(t, sw)
    mix = _merge_fwd(o2, y2, z, p["attn_out_gain"], p["ssm_out_gain"], "merge_fwd")
    x1 = _mm(mix, p["wout"], "nn", t, d, F32, "proj_out", tm=1024, tn=512, res=x)
    h2 = _rmsnorm_fwd(x1, p["norm_ffn"], "norm_ffn_fwd")
    (gate, up, act), _ = _hosted_matmul(
        [[(h2, p["wg"], "nn")], [(h2, p["wu"], "nn")]], [], _swiglu_fwd_epilogue,
        [F32, F32, BF16], t, dff, 1024, 512, "ffn_gate_up", plan)
    (x2,), _ = _hosted_matmul([[(act, p["wd"], "nn")]], [x1], lambda accs, ex: (accs[0] + ex[0],),
                              [F32], t, d, 512, 512, "ffn_down", plan)
    saved = dict(x=x, h=h, qkv3=qkv3, z=z, xbc3=xbc3, dt3=dt3, o2=o2, rtot=rtot, xact=xact, hs=hs, y2=y2,
                 mix=mix, x1=x1, h2=h2, gate=gate, up=up, act=act)
    return x2, saved, p


def _hosted_matmul(groups, extras, epilogue, out_dtypes, m, n, tm, tn, name, plan):
    rider = plan.rider(name)
    if rider is None:
        return _matmul(groups, extras, epilogue, out_dtypes, m, n, tm, tn, name), []
    outs, sent = _matmul(groups, extras, epilogue, out_dtypes, m, n, tm, tn, name, rider=rider)
    plan.collect(name, sent)
    return outs, sent


def _layer_bwd(dx2, dx2b, p, sv, bsz, s, plan, ffn_to_chips=None):
    t, d = dx2.shape
    aw, sw, cd, hg = p["aw"], p["sw"], p["cd"], p["hg"]
    dff = p["wg"].shape[1]
    gr = {}
    (dgate, dup), _ = _hosted_matmul([[(dx2b, p["wd"], "nt")]], [sv["gate"], sv["up"]], _swiglu_bwd_epilogue,
                                     [BF16, BF16], t, dff, 1024, 512, "ffn_down_dgrad", plan)
    gr["wd"] = _mm(sv["act"], dx2b, "tn", dff, d, F32, "ffn_down_wgrad")
    gr["wg"] = _mm(sv["h2"], dgate, "tn", d, dff, F32, "ffn_gate_wgrad")
    gr["wu"] = _mm(sv["h2"], dup, "tn", d, dff, F32, "ffn_up_wgrad")
    if ffn_to_chips is not None:
        plan.add_ffn(ffn_to_chips(gr))
    (dh2,), _ = _hosted_matmul([[(dgate, p["wg"], "nt"), (dup, p["wu"], "nt")]], [], lambda accs, ex: (accs[0],),
                               [F32], t, d, 512, 256, "ffn_gate_up_dgrad", plan)
    dx1, dx1b, gr["norm_ffn"] = _rmsnorm_bwd(sv["x1"], p["norm_ffn"], dh2, dx2, "norm_ffn_bwd")
    dmix = _mm(dx1b, p["wout"], "nt", t, aw + sw, F32, "proj_out_dgrad", tm=1024)
    gr["wout"] = _mm(sv["mix"], dx1b, "tn", aw + sw, d, F32, "proj_out_wgrad")
    do, dy, dz, gr["attn_out_gain"], gr["ssm_out_gain"] = _merge_bwd(
        sv["o2"], sv["y2"], sv["z"], p["attn_out_gain"], p["ssm_out_gain"], dmix, "merge_bwd")
    (dxs, dbm, dcm, ddt, dvec), sent = _ssd_bwd(sv["xact"], sv["dt3"], p["dt_bias"], p["a_log"], p["d_skip"],
                                                sv["hs"], dy.reshape(bsz, s, sw), hg, "ssd_bwd",
                                                rider=plan.rider("ssd_bwd"))
    plan.collect("ssd_bwd", sent)
    dvec = jnp.sum(dvec, axis=0).reshape(SSM_GROUPS, SUBLANES, LANES)
    gr["dt_bias"], gr["a_log"], gr["d_skip"] = (dvec[:, k, :hg].reshape(-1) for k in range(3))
    dxact = jnp.concatenate([dxs, dbm, dcm], axis=-1)
    dxbc, gr["conv_w"], gr["conv_b"] = _conv_bwd(sv["xbc3"], p["conv_w"], p["conv_b"], dxact, "conv_bwd")
    (dqkv, dqg, dkg), sent = _attention_bwd(sv["qkv3"], p["q_gain"], p["k_gain"], sv["rtot"],
                                            do.reshape(bsz, s, aw), "attention_bwd",
                                            rider=plan.rider("attention_bwd"))
    plan.collect("attention_bwd", sent)
    gr["q_gain"] = jnp.sum(dqg, axis=(0, 1, 2))
    gr["k_gain"] = jnp.sum(dkg, axis=(0, 1, 2))
    dqkv, dxbc, ddt = dqkv.reshape(t, 3 * aw), dxbc.reshape(t, cd), ddt.reshape(t, SSM_GROUPS * LANES)
    (dh,), _ = _hosted_matmul(
        [[(dqkv, p["wqkv"], "nt"), (dz, p["wz"], "nt"), (dxbc, p["wxbc"], "nt"), (ddt, p["wdt"], "nt")]],
        [], lambda accs, ex: (accs[0],), [F32], t, d, 512, 512, "proj_in_dgrad", plan)
    h = sv["h"]
    gr["wqkv"] = _mm(h, dqkv, "tn", d, 3 * aw, F32, "proj_qkv_wgrad")
    gr["wz"] = _mm(h, dz, "tn", d, sw, F32, "proj_z_wgrad")
    gr["wxbc"] = _mm(h, dxbc, "tn", d, cd, F32, "proj_xbc_wgrad")
    gr["wdt"] = _mm(h, ddt, "tn", d, SSM_GROUPS * LANES, F32, "proj_dt_wgrad", tn=SSM_GROUPS * LANES)
    dx, dxb, gr["norm_mix"] = _rmsnorm_bwd(sv["x"], p["norm_mix"], dh, dx1, "norm_mix_bwd")
    return dx, dxb, gr


SMALL = ["norm_mix", "q_gain", "k_gain", "conv_w", "conv_b", "dt_bias", "a_log", "d_skip",
         "attn_out_gain", "ssm_out_gain", "norm_ffn"]
ORDER = ["norm_mix", "w_in", "q_gain", "k_gain", "conv_w", "conv_b", "dt_bias", "a_log", "d_skip",
         "attn_out_gain", "ssm_out_gain", "w_out", "norm_ffn", "w_gate", "w_up", "w_down"]


def kernel(x, norm_mix, w_in, q_gain, k_gain, conv_w, conv_b, dt_bias, a_log, d_skip, attn_out_gain, ssm_out_gain, w_out, norm_ffn, w_gate, w_up, w_down, loss_target, m_norm_mix, m_w_in, m_q_gain, m_k_gain, m_conv_w, m_conv_b, m_dt_bias, m_a_log, m_d_skip, m_attn_out_gain, m_ssm_out_gain, m_w_out, m_norm_ffn, m_w_gate, m_w_up, m_w_down, v_norm_mix, v_w_in, v_q_gain, v_k_gain, v_conv_w, v_conv_b, v_dt_bias, v_a_log, v_d_skip, v_attn_out_gain, v_ssm_out_gain, v_w_out, v_norm_ffn, v_w_gate, v_w_up, v_w_down):
    w = dict(norm_mix=norm_mix, w_in=w_in, q_gain=q_gain, k_gain=k_gain, conv_w=conv_w, conv_b=conv_b,
             dt_bias=dt_bias, a_log=a_log, d_skip=d_skip, attn_out_gain=attn_out_gain, ssm_out_gain=ssm_out_gain,
             w_out=w_out, norm_ffn=norm_ffn, w_gate=w_gate, w_up=w_up, w_down=w_down)
    mom = dict(norm_mix=m_norm_mix, w_in=m_w_in, q_gain=m_q_gain, k_gain=m_k_gain, conv_w=m_conv_w,
               conv_b=m_conv_b, dt_bias=m_dt_bias, a_log=m_a_log, d_skip=m_d_skip,
               attn_out_gain=m_attn_out_gain, ssm_out_gain=m_ssm_out_gain, w_out=m_w_out, norm_ffn=m_norm_ffn,
               w_gate=m_w_gate, w_up=m_w_up, w_down=m_w_down)
    var = dict(norm_mix=v_norm_mix, w_in=v_w_in, q_gain=v_q_gain, k_gain=v_k_gain, conv_w=v_conv_w,
               conv_b=v_conv_b, dt_bias=v_dt_bias, a_log=v_a_log, d_skip=v_d_skip,
               attn_out_gain=v_attn_out_gain, ssm_out_gain=v_ssm_out_gain, w_out=v_w_out, norm_ffn=v_norm_ffn,
               w_gate=v_w_gate, w_up=v_w_up, w_down=v_w_down)

    bsz, s, d = x.shape
    t = bsz * s
    depth = norm_mix.shape[0]
    aw = attn_out_gain.shape[1]
    sw = ssm_out_gain.shape[1]
    cd = conv_b.shape[1]
    hs_n = dt_bias.shape[1]
    hg = hs_n // SSM_GROUPS
    heads = aw // ATT_HEAD_DIM
    in_dim = 3 * aw + sw + cd + hs_n
    dff = w_gate.shape[2] * N_CHIPS
    cs = conv_w.shape[2]
    my_chip = 2 * lax.axis_index("x") + lax.axis_index("y")
    ids = (lax.axis_index("c").astype(jnp.int32).reshape(1), my_chip.astype(jnp.int32).reshape(1))
    wb ={n: w[n].astype(BF16) for n in BIG}
    conv_all = _small_exchange(_pack([conv_w]), "allgather_conv_w", False)
    conv_full = jnp.concatenate(
        [_unpack(conv_all[2 * j], [conv_w.shape])[0] for j in range(N_CHIPS)], axis=-1)

    def in_params(l, gat_in):
        win = jnp.transpose(gat_in, (1, 0, 2)).reshape(d, in_dim)
        wqkv = win[:, :3 * aw].reshape(d, 3, heads, ATT_HEAD_DIM)
        wqkv = jnp.transpose(wqkv, (0, 2, 1, 3)).reshape(d, 3 * aw)
        return dict(aw=aw, sw=sw, cd=cd, hg=hg, norm_mix=norm_mix[l], wqkv=wqkv, wz=win[:, 3 * aw:3 * aw + sw],
                    wxbc=win[:, 3 * aw + sw:3 * aw + sw + cd], wdt=_group_slots(win[:, 3 * aw + sw + cd:], hg),
                    q_gain=q_gain[l], k_gain=k_gain[l], conv_w=conv_full[l], conv_b=conv_b[l],
                    dt_bias=_group_slots(dt_bias[l], hg).reshape(1, -1),
                    a_log=_group_slots(a_log[l], hg).reshape(1, -1),
                    d_skip=_group_slots(d_skip[l], hg).reshape(1, -1),
                    attn_out_gain=attn_out_gain[l], ssm_out_gain=ssm_out_gain[l], norm_ffn=norm_ffn[l])

    def late_params(gat):
        return dict(wout=gat["w_out"].reshape(aw + sw, d), wd=gat["w_down"].reshape(dff, d),
                    wg=gat["w_gate"], wu=gat["w_up"])

    def ffn_to_chips(gr):
        return {"w_gate": gr["wg"], "w_up": gr["wu"], "w_down": gr["wd"].reshape(N_CHIPS, dff // N_CHIPS, d)}

    def mix_to_chips(gr):
        gqkv = gr["wqkv"].reshape(d, heads, 3, ATT_HEAD_DIM)
        gqkv = jnp.transpose(gqkv, (0, 2, 1, 3)).reshape(d, 3 * aw)
        gin = jnp.concatenate([gqkv, gr["wz"], gr["wxbc"], _ungroup_slots(gr["wdt"], hg)], axis=-1)
        return {"w_in": jnp.transpose(gin.reshape(d, N_CHIPS, in_dim // N_CHIPS), (1, 0, 2)),
                "w_out": gr["wout"].reshape(N_CHIPS, (aw + sw) // N_CHIPS, d)}

    xt = x.reshape(t, d)
    saved, params = [], []
    gat_in = _gather_now(wb["w_in"][0], "first")
    for l in range(depth):
        plan = _GatherPlan({n: wb[n][l] for n in LATE}, wb["w_in"][l + 1] if l + 1 < depth else None)
        xt, sv, p = _layer_fwd(xt, in_params(l, gat_in), bsz, s, plan, late_params)
        saved.append(sv)
        params.append(p)
        if l + 1 < depth:
            gat_in = plan.next_gathered()
    dxt, dxb, loss_lanes = _loss_head(xt, loss_target.reshape(t, d), "loss_head")

    grads = [None] * depth
    pending, totals = None, {}
    for l in reversed(range(depth)):
        plan = _ReducePlan(pending, l + 1, l, totals, depth, ids)
        dxt, dxb, grads[l] = _layer_bwd(dxt, dxb, params[l], saved[l], bsz, s, plan, ffn_to_chips)
        totals = plan.totals
        pending = mix_to_chips(grads[l])
    g = _ReducePlan(pending, 0, None, totals, depth, ids).run_now("first_layer")
    grad_x = dxt.reshape(bsz, s, d)

    def stack(name):
        return jnp.stack([grads[l][name] for l in range(depth)])

    small_shapes = [(1, LANES)] + [(depth, CONV_WIDTH, cd) if n == "conv_w" else w[n].shape for n in SMALL]
    small = _small_exchange(_pack([loss_lanes] + [stack(n) for n in SMALL]), "allreduce_small", True)
    small = _unpack(small, small_shapes)
    loss = small[0][0, 0]
    for n, a in zip(SMALL, small[1:], strict=True):
        g[n] = a
    g["conv_w"] = lax.dynamic_slice_in_dim(g["conv_w"], my_chip * cs, cs, axis=2)

    delta, new_m, new_v = {}, {}, {}
    for n in BIG:
        shp = w[n].shape
        two_d = (shp[0] * shp[1], shp[2])
        dl, nm, nv = _adamw(w[n].reshape(two_d), g[n].reshape(two_d), mom[n].reshape(two_d),
                            var[n].reshape(two_d), f"adamw_{n}")
        delta[n], new_m[n], new_v[n] = dl.reshape(shp), nm.reshape(shp), nv.reshape(shp)
    shapes = [w[n].shape for n in SMALL]
    dl, nm, nv = _adamw(_pack([w[n] for n in SMALL]), _pack([g[n] for n in SMALL]),
                        _pack([mom[n] for n in SMALL]), _pack([var[n] for n in SMALL]), "adamw_small")
    for n, a, b, c in zip(SMALL, _unpack(dl, shapes), _unpack(nm, shapes), _unpack(nv, shapes), strict=True):
        delta[n], new_m[n], new_v[n] = a, b, c

    return (loss, grad_x, *[g[n] for n in ORDER], *[delta[n] for n in ORDER],
            *[new_m[n] for n in ORDER], *[new_v[n] for n in ORDER])
```

```python
import jax
import jax.numpy as jnp
from jax import lax
from jax.experimental import pallas as pl
from jax.experimental.pallas import tpu as pltpu

F32 = jnp.float32
BF16 = jnp.bfloat16
MESH = pl.DeviceIdType.MESH
ANY = pl.BlockSpec(memory_space=pl.ANY)
VMEM = pl.BlockSpec(memory_space=pltpu.VMEM)

EPS = 1e-6
ATT_HEAD_DIM = 128
SSM_HEAD_DIM = 64
SSM_GROUPS = 2
SSM_STATE = 128
SSD_CHUNK = 128
CONV_WIDTH = 4
LANES = 128
SUBLANES = 8
ATT_TILE = 256
ATT_STRIP = 128
N_CHIPS = 4
N_DEV = 8

ADAM_LR = 0.001
ADAM_B1 = 0.9
ADAM_B2 = 0.999
ADAM_EPS = 1e-08
ADAM_WD = 0.01
ADAM_STEP = 10

VMEM_LIMIT = 48 * 1024 * 1024

NN = (((1,), (0,)), ((), ()))
NT = (((1,), (1,)), ((), ()))
TN = (((0,), (0,)), ((), ()))


def _dot(a, b, dims=NN):
    return lax.dot_general(a.astype(BF16), b.astype(BF16), dims, preferred_element_type=F32)


def _dot_exact(x, ones, dims=NN, passes=3, ones_left=False):
    acc = None
    rem = x
    for _ in range(passes):
        piece = rem.astype(BF16)
        rem = rem - piece.astype(F32)
        p = (lax.dot_general(ones, piece, dims, preferred_element_type=F32) if ones_left
             else lax.dot_general(piece, ones, dims, preferred_element_type=F32))
        acc = p if acc is None else acc + p
    return acc


def _scan_lanes(x, tri, passes, reverse=False):
    nblk = x.shape[1] // LANES
    blocks = [x[:, k * LANES:(k + 1) * LANES] for k in range(nblk)]
    out, carry = [None] * nblk, None
    for k in (reversed(range(nblk)) if reverse else range(nblk)):
        p = _dot_exact(blocks[k], tri, passes=passes)
        out[k] = p if carry is None else p + carry
        tot = jnp.sum(blocks[k], axis=1, keepdims=True)
        carry = tot if carry is None else carry + tot
    return (out[0] if nblk == 1 else jnp.concatenate(out, axis=1)), carry


def _iota2(shape, axis):
    return lax.broadcasted_iota(jnp.int32, shape, axis)


def _tri(n, cmp):
    return cmp(_iota2((n, n), 0), _iota2((n, n), 1)).astype(BF16)


def _sum_all(v):
    return jnp.sum(jnp.sum(v, axis=1, keepdims=True), axis=0, keepdims=True)


def _fit(tile, dim, unit=LANES):
    if dim <= tile:
        return dim
    return max(k for k in range(unit, tile + 1, unit) if dim % k == 0)


def _params(sem):
    return pltpu.CompilerParams(dimension_semantics=sem, vmem_limit_bytes=VMEM_LIMIT)


def _call(body, *, name, grid, in_specs, out_specs, out_shape, sem, args, scratch_shapes=(), rider=None):
    in_specs, out_specs, out_shape = list(in_specs), list(out_specs), list(out_shape)
    scratch_shapes = list(scratch_shapes)
    if rider is None:
        res = pl.pallas_call(body, name=name, grid=grid, in_specs=in_specs, out_specs=out_specs,
                             out_shape=out_shape, scratch_shapes=scratch_shapes,
                             compiler_params=_params(sem))(*args)
        return list(res), []
    n_in, n_out, n_scr = len(in_specs), len(out_specs), len(scratch_shapes)
    r_in, r_out, n_sems = len(rider["ins"]), len(rider["outs"]), rider["n_sems"]

    def hosted(*refs):
        ins, rest = refs[:n_in], refs[n_in:]
        rins, rest = rest[:r_in], rest[r_in:]
        outs, rest = rest[:n_out], rest[n_out:]
        routs, rest = rest[:r_out], rest[r_out:]
        scr, (send_sems, recv_sems) = rest[:n_scr], rest[n_scr:]
        first, last = None, None
        for d, size in enumerate(grid):
            f, e = pl.program_id(d) == 0, pl.program_id(d) == size - 1
            first = f if first is None else jnp.logical_and(first, f)
            last = e if last is None else jnp.logical_and(last, e)

        @pl.when(first)
        def _():
            for cp in rider["copies"](rins, routs, send_sems, recv_sems):
                cp.start()

        body(*ins, *outs, *scr)

        @pl.when(last)
        def _():
            for cp in rider["copies"](rins, routs, send_sems, recv_sems):
                cp.wait()

    res = pl.pallas_call(
        hosted, name=name, grid=grid, in_specs=in_specs + [ANY] * r_in, out_specs=out_specs + [ANY] * r_out,
        out_shape=out_shape + list(rider["outs"]),
        scratch_shapes=scratch_shapes + [pltpu.SemaphoreType.DMA((n_sems,)), pltpu.SemaphoreType.DMA((n_sems,))],
        input_output_aliases={n_in + i: n_out + o for i, o in rider["aliases"].items()},
        compiler_params=_params(("arbitrary",) * len(grid)),
    )(*args, *rider["ins"])
    return list(res[:n_out]), list(res[n_out:])


def _softplus(x):
    return jnp.maximum(x, 0.0) + jnp.log(1.0 + jnp.exp(-jnp.abs(x)))


def _sigmoid(x):
    return 1.0 / (1.0 + jnp.exp(-x))


def _rms_fwd(x, g):
    r = lax.rsqrt(jnp.mean(x * x, axis=-1, keepdims=True) + EPS)
    return (x * r) * g


def _rms_bwd(x, g, dh):
    r = lax.rsqrt(jnp.mean(x * x, axis=-1, keepdims=True) + EPS)
    y = x * r
    dy = dh * g
    dx = r * (dy - y * jnp.mean(dy * y, axis=-1, keepdims=True))
    return dx, dh * y


def _matmul(groups, extras, epilogue, out_dtypes, m, n, tm, tn, name, rider=None):
    tm, tn = _fit(tm, m), _fit(tn, n)
    flat = [t for grp in groups for t in grp]
    n_terms, n_extra = len(flat), len(extras)

    def body(*refs):
        outs = refs[2 * n_terms + n_extra:]
        accs, pos = [], 0
        for grp in groups:
            acc = None
            for (_, _, mode) in grp:
                dims = {"nn": NN, "nt": NT, "tn": TN}[mode]
                p = _dot(refs[2 * pos][...], refs[2 * pos + 1][...], dims)
                acc = p if acc is None else acc + p
                pos += 1
            accs.append(acc)
        ex = [refs[2 * n_terms + i][...] for i in range(n_extra)]
        res = epilogue(accs, ex)
        for o_ref, r in zip(outs, res, strict=True):
            o_ref[...] = r.astype(o_ref.dtype)

    in_specs, args = [], []
    for (a, b, mode) in flat:
        if mode == "nn":
            k = a.shape[1]
            in_specs += [pl.BlockSpec((tm, k), lambda i, j: (i, 0)), pl.BlockSpec((k, tn), lambda i, j: (0, j))]
        elif mode == "nt":
            k = a.shape[1]
            in_specs += [pl.BlockSpec((tm, k), lambda i, j: (i, 0)), pl.BlockSpec((tn, k), lambda i, j: (j, 0))]
        else:
            k = a.shape[0]
            in_specs += [pl.BlockSpec((k, tm), lambda i, j: (0, i)), pl.BlockSpec((k, tn), lambda i, j: (0, j))]
        args += [a, b]
    for e in extras:
        in_specs.append(pl.BlockSpec((tm, tn), lambda i, j: (i, j)))
        args.append(e)
    outs, routs = _call(
        body, name=name, grid=(m // tm, n // tn), in_specs=in_specs,
        out_specs=[pl.BlockSpec((tm, tn), lambda i, j: (i, j)) for _ in out_dtypes],
        out_shape=[jax.ShapeDtypeStruct((m, n), d) for d in out_dtypes],
        sem=("parallel", "parallel"), args=args, rider=rider)
    return outs if rider is None else (outs, routs)


def _mm(a, b, mode, m, n, out_dtype, name, tm=512, tn=512, res=None):
    extras = [] if res is None else [res]
    epi = (lambda accs, ex: (accs[0],)) if res is None else (lambda accs, ex: (accs[0] + ex[0],))
    return _matmul([[(a, b, mode)]], extras, epi, [out_dtype], m, n, tm, tn, name)[0]


def _swiglu_fwd_epilogue(accs, ex):
    g, u = accs
    return g, u, (g * _sigmoid(g)) * u


def _swiglu_bwd_epilogue(accs, ex):
    dact, (g, u) = accs[0], ex
    sg = _sigmoid(g)
    silu = g * sg
    return dact * u * (sg * (1.0 + g * (1.0 - sg))), dact * silu


def _rmsnorm_fwd(x, g, name, tr=512):
    t, d = x.shape
    tr = min(tr, t)

    def body(x_ref, g_ref, h_ref):
        h_ref[...] = _rms_fwd(x_ref[...], g_ref[...]).astype(BF16)

    return pl.pallas_call(
        body, name=name, grid=(t // tr,),
        in_specs=[pl.BlockSpec((tr, d), lambda i: (i, 0)), pl.BlockSpec((1, d), lambda i: (0, 0))],
        out_specs=pl.BlockSpec((tr, d), lambda i: (i, 0)),
        out_shape=jax.ShapeDtypeStruct((t, d), BF16),
        compiler_params=_params(("parallel",)),
    )(x, g.reshape(1, d))


def _rmsnorm_bwd(x, g, dh, dres, name, tr=256):
    t, d = x.shape
    tr = min(tr, t)

    def body(x_ref, g_ref, dh_ref, dres_ref, dx_ref, dxb_ref, dg_ref):
        dx, dgr = _rms_bwd(x_ref[...], g_ref[...], dh_ref[...])
        dx = dx + dres_ref[...]
        dx_ref[...] = dx
        dxb_ref[...] = dx.astype(BF16)

        @pl.when(pl.program_id(0) == 0)
        def _():
            dg_ref[...] = jnp.zeros_like(dg_ref)

        dg_ref[...] += jnp.sum(dgr, axis=0, keepdims=True)

    row = pl.BlockSpec((tr, d), lambda i: (i, 0))
    vec = pl.BlockSpec((1, d), lambda i: (0, 0))
    dx, dxb, dg = pl.pallas_call(
        body, name=name, grid=(t // tr,),
        in_specs=[row, vec, row, row], out_specs=[row, row, vec],
        out_shape=[jax.ShapeDtypeStruct((t, d), F32), jax.ShapeDtypeStruct((t, d), BF16),
                   jax.ShapeDtypeStruct((1, d), F32)],
        compiler_params=_params(("arbitrary",)),
    )(x, g.reshape(1, d), dh, dres)
    return dx, dxb, dg.reshape(d)


def _merge_fwd(o_att, y, z, ga, gs, name, tr=256):
    t, wa = o_att.shape
    ws = y.shape[1]
    wg = ws // SSM_GROUPS
    tr = min(tr, t)

    def body(o_ref, y_ref, z_ref, ga_ref, gs_ref, m_ref):
        m_ref[:, 0:wa] = _rms_fwd(o_ref[...], ga_ref[...]).astype(BF16)
        for g in range(SSM_GROUPS):
            sl = slice(g * wg, (g + 1) * wg)
            zz = z_ref[:, sl]
            yz = y_ref[:, sl] * (zz * _sigmoid(zz))
            m_ref[:, wa + g * wg:wa + (g + 1) * wg] = _rms_fwd(yz, gs_ref[:, sl]).astype(BF16)

    return pl.pallas_call(
        body, name=name, grid=(t // tr,),
        in_specs=[pl.BlockSpec((tr, wa), lambda i: (i, 0)), pl.BlockSpec((tr, ws), lambda i: (i, 0)),
                  pl.BlockSpec((tr, ws), lambda i: (i, 0)), pl.BlockSpec((1, wa), lambda i: (0, 0)),
                  pl.BlockSpec((1, ws), lambda i: (0, 0))],
        out_specs=pl.BlockSpec((tr, wa + ws), lambda i: (i, 0)),
        out_shape=jax.ShapeDtypeStruct((t, wa + ws), BF16),
        compiler_params=_params(("parallel",)),
    )(o_att, y, z, ga.reshape(1, wa), gs.reshape(1, ws))


def _merge_bwd(o_att, y, z, ga, gs, dmix, name, tr=256):
    t, wa = o_att.shape
    ws = y.shape[1]
    wg = ws // SSM_GROUPS
    tr = min(tr, t)

    def body(o_ref, y_ref, z_ref, ga_ref, gs_ref, dm_ref, do_ref, dy_ref, dz_ref, dga_ref, dgs_ref):
        @pl.when(pl.program_id(0) == 0)
        def _():
            dga_ref[...] = jnp.zeros_like(dga_ref)
            dgs_ref[...] = jnp.zeros_like(dgs_ref)

        do, dgr = _rms_bwd(o_ref[...], ga_ref[...], dm_ref[:, 0:wa])
        do_ref[...] = do
        dga_ref[...] += jnp.sum(dgr, axis=0, keepdims=True)
        for g in range(SSM_GROUPS):
            sl = slice(g * wg, (g + 1) * wg)
            zz, yy = z_ref[:, sl], y_ref[:, sl]
            sg = _sigmoid(zz)
            silu = zz * sg
            dyz, dgr = _rms_bwd(yy * silu, gs_ref[:, sl], dm_ref[:, wa + g * wg:wa + (g + 1) * wg])
            dy_ref[:, sl] = dyz * silu
            dz_ref[:, sl] = (dyz * yy * (sg + silu * (1.0 - sg))).astype(BF16)
            dgs_ref[:, sl] += jnp.sum(dgr, axis=0, keepdims=True)

    rowa = pl.BlockSpec((tr, wa), lambda i: (i, 0))
    rows = pl.BlockSpec((tr, ws), lambda i: (i, 0))
    veca = pl.BlockSpec((1, wa), lambda i: (0, 0))
    vecs = pl.BlockSpec((1, ws), lambda i: (0, 0))
    do, dy, dz, dga, dgs = pl.pallas_call(
        body, name=name, grid=(t // tr,),
        in_specs=[rowa, rows, rows, veca, vecs, pl.BlockSpec((tr, wa + ws), lambda i: (i, 0))],
        out_specs=[rowa, rows, rows, veca, vecs],
        out_shape=[jax.ShapeDtypeStruct((t, wa), F32), jax.ShapeDtypeStruct((t, ws), F32),
                   jax.ShapeDtypeStruct((t, ws), BF16), jax.ShapeDtypeStruct((1, wa), F32),
                   jax.ShapeDtypeStruct((1, ws), F32)],
        compiler_params=_params(("arbitrary",)),
    )(o_att, y, z, ga.reshape(1, wa), gs.reshape(1, ws), dmix)
    return do, dy, dz, dga.reshape(wa), dgs.reshape(ws)


def _loss_head(y, target, name, tr=256):
    t, d = y.shape
    tr = min(tr, t)

    def body(y_ref, t_ref, dy_ref, dyb_ref, l_ref):
        @pl.when(pl.program_id(0) == 0)
        def _():
            l_ref[...] = jnp.zeros_like(l_ref)

        diff = y_ref[...] - t_ref[...]
        dy = diff * (1.0 / d)
        dy_ref[...] = dy
        dyb_ref[...] = dy.astype(BF16)
        part = jnp.sum(diff * diff, axis=0, keepdims=True)
        fold = part[:, 0:LANES]
        for k in range(1, d // LANES):
            fold = fold + part[:, k * LANES:(k + 1) * LANES]
        l_ref[...] += fold * (0.5 / d)

    row = pl.BlockSpec((tr, d), lambda i: (i, 0))
    return pl.pallas_call(
        body, name=name, grid=(t // tr,), in_specs=[row, row],
        out_specs=[row, row, pl.BlockSpec((1, LANES), lambda i: (0, 0))],
        out_shape=[jax.ShapeDtypeStruct((t, d), F32), jax.ShapeDtypeStruct((t, d), BF16),
                   jax.ShapeDtypeStruct((1, LANES), F32)],
        compiler_params=_params(("arbitrary",)),
    )(y, target)


def _adamw(w, g, m, v, name, tr=256):
    r, c = w.shape
    tr = _fit(tr, r, 16)

    def body(w_ref, g_ref, m_ref, v_ref, d_ref, nm_ref, nv_ref):
        gg = g_ref[...]
        nm = ADAM_B1 * m_ref[...] + (1.0 - ADAM_B1) * gg
        nv = ADAM_B2 * v_ref[...] + (1.0 - ADAM_B2) * (gg * gg)
        m_hat = nm / (1.0 - ADAM_B1 ** ADAM_STEP)
        v_hat = nv / (1.0 - ADAM_B2 ** ADAM_STEP)
        d_ref[...] = -ADAM_LR * (m_hat / (jnp.sqrt(v_hat) + ADAM_EPS) + ADAM_WD * w_ref[...])
        nm_ref[...] = nm
        nv_ref[...] = nv

    blk = pl.BlockSpec((tr, c), lambda i: (i, 0))
    return pl.pallas_call(
        body, name=name, grid=(r // tr,), in_specs=[blk] * 4, out_specs=[blk] * 3,
        out_shape=[jax.ShapeDtypeStruct((r, c), F32)] * 3,
        compiler_params=_params(("parallel",)),
    )(w, g, m, v)


def _att_scores(qi, kj, scale, row0):
    z = _dot(qi, kj, NT) * scale
    lb = -_softplus(-z)
    lrm = lb - z
    if row0 is None:
        return lb, lrm, None
    mask = _iota2(z.shape, 1) < _iota2(z.shape, 0) + row0
    return lb, jnp.where(mask, lrm, 0.0), mask


def _masked(mask, v):
    return v if mask is None else jnp.where(mask, v, 0.0)


def _attention_fwd(qkv, qg, kg, name, tile=None, rider=None):
    bsz, s, w3 = qkv.shape
    hd = ATT_HEAD_DIM
    heads = w3 // (3 * hd)
    tile = min(tile or ATT_TILE, s)
    strip = min(ATT_STRIP, tile)
    nb = s // tile
    scale = hd ** -0.5

    def body(qkv_ref, qg_ref, kg_ref, o_ref, r_ref, qn_s, kn_s, vb_s, acc_s, c_s):
        qn_s[...] = _rms_fwd(qkv_ref[0, :, 0:hd], qg_ref[...]).astype(BF16)
        kn_s[...] = _rms_fwd(qkv_ref[0, :, hd:2 * hd], kg_ref[...]).astype(BF16)
        vb_s[...] = qkv_ref[0, :, 2 * hd:3 * hd].astype(BF16)
        after = _tri(LANES, lambda r, c: r > c)

        def q_loop(i, _):
            rows = pl.ds(pl.multiple_of(i * tile, tile), tile)
            acc_s[...] = jnp.zeros_like(acc_s)
            c_s[...] = jnp.zeros_like(c_s)

            def key_tile(j, diagonal):
                cols = pl.ds(pl.multiple_of(j * tile, tile), tile)
                kj, vj = kn_s[cols, :], vb_s[cols, :]
                strips = range(tile // strip)
                subs = [slice(r * strip, (r + 1) * strip) for r in strips]
                srows = [pl.ds(pl.multiple_of(i * tile + r * strip, strip), strip) for r in strips]
                sc = [_att_scores(qn_s[srows[r], :], kj, scale, r * strip if diagonal else None) for r in strips]
                later = [_scan_lanes(sc[r][1], after, 2, reverse=True) for r in strips]
                for r in strips:
                    w = _masked(sc[r][2], jnp.exp(sc[r][0] + (later[r][0] + c_s[subs[r], :])))
                    acc_s[subs[r], :] += _dot(w, vj)
                    c_s[subs[r], :] += later[r][1]

            def k_loop(jj, _):
                key_tile(i - jj, False)
                return 0

            key_tile(i, True)
            lax.fori_loop(1, i + 1, k_loop, 0)
            o_ref[0, rows, :] = acc_s[...]
            r_ref[0, 0, rows, :] = c_s[...]
            return 0

        lax.fori_loop(0, nb, q_loop, 0)

    return _call(
        body, name=name, grid=(bsz, heads),
        in_specs=[pl.BlockSpec((1, s, 3 * hd), lambda b, h: (b, 0, h)),
                  pl.BlockSpec((1, hd), lambda b, h: (0, 0)), pl.BlockSpec((1, hd), lambda b, h: (0, 0))],
        out_specs=[pl.BlockSpec((1, s, hd), lambda b, h: (b, 0, h)),
                   pl.BlockSpec((1, 1, s, 1), lambda b, h: (b, h, 0, 0))],
        out_shape=[jax.ShapeDtypeStruct((bsz, s, heads * hd), F32),
                   jax.ShapeDtypeStruct((bsz, heads, s, 1), F32)],
        scratch_shapes=[pltpu.VMEM((s, hd), BF16), pltpu.VMEM((s, hd), BF16), pltpu.VMEM((s, hd), BF16),
                        pltpu.VMEM((tile, hd), F32), pltpu.VMEM((tile, 1), F32)],
        sem=("parallel", "parallel"), args=(qkv, qg.reshape(1, hd), kg.reshape(1, hd)), rider=rider)


def _attention_bwd(qkv, qg, kg, rtot, do, name, tile=None, rider=None):
    bsz, s, w3 = qkv.shape
    hd = ATT_HEAD_DIM
    heads = w3 // (3 * hd)
    tile = min(tile or ATT_TILE, s)
    strip = min(ATT_STRIP, tile)
    nb = s // tile
    scale = hd ** -0.5

    def body(qkv_ref, qg_ref, kg_ref, r_ref, do_ref, dqkv_ref, dqg_ref, dkg_ref,
             qn_s, kn_s, vb_s, dob_s, dqn_s, dkn_s, dv_s, c1_s, c2_s, wb_s, dzb_s):
        qn_s[...] = _rms_fwd(qkv_ref[0, :, 0:hd], qg_ref[...]).astype(BF16)
        kn_s[...] = _rms_fwd(qkv_ref[0, :, hd:2 * hd], kg_ref[...]).astype(BF16)
        vb_s[...] = qkv_ref[0, :, 2 * hd:3 * hd].astype(BF16)
        dob_s[...] = do_ref[0].astype(BF16)
        dqn_s[...] = jnp.zeros_like(dqn_s)
        dkn_s[...] = jnp.zeros_like(dkn_s)
        dv_s[...] = jnp.zeros_like(dv_s)
        upto = _tri(LANES, lambda r, c: r <= c)
        before = _tri(LANES, lambda r, c: r < c)

        def q_loop(i, _):
            rows = pl.ds(pl.multiple_of(i * tile, tile), tile)
            c1_s[...] = jnp.zeros_like(c1_s)
            c2_s[...] = jnp.zeros_like(c2_s)

            def key_tile(j, diagonal):
                cols = pl.ds(pl.multiple_of(j * tile, tile), tile)
                kj, vj = kn_s[cols, :], vb_s[cols, :]
                strips = range(tile // strip)
                subs = [slice(r * strip, (r + 1) * strip) for r in strips]
                srows = [pl.ds(pl.multiple_of(i * tile + r * strip, strip), strip) for r in strips]
                sc = [_att_scores(qn_s[srows[r], :], kj, scale, r * strip if diagonal else None) for r in strips]
                dw = [_dot(dob_s[srows[r], :], vj, NT) for r in strips]
                upto_lr = [_scan_lanes(sc[r][1], upto, 2) for r in strips]
                w = [_masked(sc[r][2], jnp.exp(sc[r][0] + (r_ref[0, 0, srows[r], :] - (upto_lr[r][0] + c1_s[subs[r], :]))))
                     for r in strips]
                e = [w[r] * dw[r] for r in strips]
                pre = [_scan_lanes(e[r], before, 1) for r in strips]
                dz = [_masked(sc[r][2], (e[r] - jnp.exp(sc[r][0]) * (e[r] + (pre[r][0] + c2_s[subs[r], :]))) * scale)
                      for r in strips]
                for r in strips:
                    wb_s[subs[r], :] = w[r].astype(BF16)
                    dzb_s[subs[r], :] = dz[r].astype(BF16)
                    c1_s[subs[r], :] += upto_lr[r][1]
                    c2_s[subs[r], :] += pre[r][1]
                dqn_s[rows, :] += _dot(dzb_s[...], kj)
                dv_s[cols, :] += _dot(wb_s[...], dob_s[rows, :], TN)
                dkn_s[cols, :] += _dot(dzb_s[...], qn_s[rows, :], TN)

            def k_loop(j, _):
                key_tile(j, False)
                return 0

            lax.fori_loop(0, i, k_loop, 0)
            key_tile(i, True)
            return 0

        lax.fori_loop(0, nb, q_loop, 0)
        dq, dgq = _rms_bwd(qkv_ref[0, :, 0:hd], qg_ref[...], dqn_s[...])
        dk, dgk = _rms_bwd(qkv_ref[0, :, hd:2 * hd], kg_ref[...], dkn_s[...])
        dqkv_ref[0, :, 0:hd] = dq.astype(BF16)
        dqkv_ref[0, :, hd:2 * hd] = dk.astype(BF16)
        dqkv_ref[0, :, 2 * hd:3 * hd] = dv_s[...].astype(BF16)
        dqg_ref[0, 0] = jnp.sum(dgq, axis=0, keepdims=True)
        dkg_ref[0, 0] = jnp.sum(dgk, axis=0, keepdims=True)

    gain = pl.BlockSpec((1, hd), lambda b, h: (0, 0))
    dgain = pl.BlockSpec((1, 1, 1, hd), lambda b, h: (b, h, 0, 0))
    return _call(
        body, name=name, grid=(bsz, heads),
        in_specs=[pl.BlockSpec((1, s, 3 * hd), lambda b, h: (b, 0, h)), gain, gain,
                  pl.BlockSpec((1, 1, s, 1), lambda b, h: (b, h, 0, 0)),
                  pl.BlockSpec((1, s, hd), lambda b, h: (b, 0, h))],
        out_specs=[pl.BlockSpec((1, s, 3 * hd), lambda b, h: (b, 0, h)), dgain, dgain],
        out_shape=[jax.ShapeDtypeStruct((bsz, s, w3), BF16),
                   jax.ShapeDtypeStruct((bsz, heads, 1, hd), F32),
                   jax.ShapeDtypeStruct((bsz, heads, 1, hd), F32)],
        scratch_shapes=[pltpu.VMEM((s, hd), BF16)] * 4 + [pltpu.VMEM((s, hd), F32)] * 3
        + [pltpu.VMEM((tile, 1), F32)] * 2 + [pltpu.VMEM((tile, tile), BF16)] * 2,
        sem=("parallel", "parallel"), args=(qkv, qg.reshape(1, hd), kg.reshape(1, hd), rtot, do), rider=rider)


def _conv_pre(pad_ref, w_ref, b_ref, s):
    pre = b_ref[...]
    for i in range(CONV_WIDTH):
        off = SUBLANES - (CONV_WIDTH - 1) + i
        pre = pre + pad_ref[off:off + s, :] * w_ref[i:i + 1, :]
    return pre


def _conv_fwd(u, w, b, name, tc=256):
    bsz, s, c = u.shape
    tc = min(tc, c)

    def body(u_ref, w_ref, b_ref, a_ref, pad_s):
        pad_s[0:SUBLANES, :] = jnp.zeros((SUBLANES, tc), F32)
        pad_s[SUBLANES:SUBLANES + s, :] = u_ref[0]
        pre = _conv_pre(pad_s, w_ref, b_ref, s)
        a_ref[0] = pre * _sigmoid(pre)

    return pl.pallas_call(
        body, name=name, grid=(bsz, c // tc),
        in_specs=[pl.BlockSpec((1, s, tc), lambda i, j: (i, 0, j)),
                  pl.BlockSpec((CONV_WIDTH, tc), lambda i, j: (0, j)), pl.BlockSpec((1, tc), lambda i, j: (0, j))],
        out_specs=pl.BlockSpec((1, s, tc), lambda i, j: (i, 0, j)),
        out_shape=jax.ShapeDtypeStruct((bsz, s, c), F32),
        scratch_shapes=[pltpu.VMEM((s + SUBLANES, tc), F32)],
        compiler_params=_params(("parallel", "parallel")),
    )(u, w, b.reshape(1, c))


def _conv_bwd(u, w, b, da, name, tc=256):
    bsz, s, c = u.shape
    tc = min(tc, c)

    def body(u_ref, w_ref, b_ref, da_ref, du_ref, dw_ref, db_ref, pad_s, gpad_s):
        @pl.when(pl.program_id(1) == 0)
        def _():
            dw_ref[...] = jnp.zeros_like(dw_ref)
            db_ref[...] = jnp.zeros_like(db_ref)

        pad_s[0:SUBLANES, :] = jnp.zeros((SUBLANES, tc), F32)
        pad_s[SUBLANES:SUBLANES + s, :] = u_ref[0]
        pre = _conv_pre(pad_s, w_ref, b_ref, s)
        sg = _sigmoid(pre)
        dpre = da_ref[0] * (sg * (1.0 + pre * (1.0 - sg)))
        gpad_s[0:s, :] = dpre
        gpad_s[s:s + SUBLANES, :] = jnp.zeros((SUBLANES, tc), F32)
        du = jnp.zeros((s, tc), F32)
        for i in range(CONV_WIDTH):
            back = CONV_WIDTH - 1 - i
            du = du + gpad_s[back:back + s, :] * w_ref[i:i + 1, :]
            off = SUBLANES - (CONV_WIDTH - 1) + i
            dw_ref[i:i + 1, :] += jnp.sum(dpre * pad_s[off:off + s, :], axis=0, keepdims=True)
        du_ref[0] = du.astype(BF16)
        db_ref[...] += jnp.sum(dpre, axis=0, keepdims=True)

    blk = pl.BlockSpec((1, s, tc), lambda j, i: (i, 0, j))
    du, dw, db = pl.pallas_call(
        body, name=name, grid=(c // tc, bsz),
        in_specs=[blk, pl.BlockSpec((CONV_WIDTH, tc), lambda j, i: (0, j)),
                  pl.BlockSpec((1, tc), lambda j, i: (0, j)), blk],
        out_specs=[blk, pl.BlockSpec((CONV_WIDTH, tc), lambda j, i: (0, j)),
                   pl.BlockSpec((1, tc), lambda j, i: (0, j))],
        out_shape=[jax.ShapeDtypeStruct((bsz, s, c), BF16), jax.ShapeDtypeStruct((CONV_WIDTH, c), F32),
                   jax.ShapeDtypeStruct((1, c), F32)],
        scratch_shapes=[pltpu.VMEM((s + SUBLANES, tc), F32), pltpu.VMEM((s + SUBLANES, tc), F32)],
        compiler_params=_params(("parallel", "arbitrary")),
    )(u, w, b.reshape(1, c), da)
    return du, dw, db.reshape(c)


def _ssd_chunk_common(b_ref, c_ref, dt_ref, dtb_ref, alog_ref):
    bm, cm = b_ref[0], c_ref[0]
    draw = dt_ref[0] + dtb_ref[...]
    dt = _softplus(draw)
    a_row = -jnp.exp(alog_ref[...])
    da = dt * a_row
    n = SSD_CHUNK
    acum = _dot_exact(da, _tri(n, lambda r, c: r >= c), ones_left=True)
    acum_t = _dot_exact(da, _tri(n, lambda r, c: r <= c), dims=TN)
    cb = _dot(cm, bm, NT)
    return bm, cm, draw, dt, a_row, acum, acum_t, cb


def _row_totals(v):
    return _dot_exact(v, jnp.ones((v.shape[1], LANES), BF16), passes=2)


def _ssd_head_common(acum, acum_t, dt, cb, x, i):
    n, p = SSD_CHUNK, SSM_HEAD_DIM
    pick = (_iota2((LANES, LANES), 0) == i).astype(BF16)
    acol = _dot_exact(acum, pick)
    dtc = _dot_exact(dt, pick)[:, :p]
    arow = acum_t[i:i + 1, :]
    causal = _iota2((n, n), 0) >= _iota2((n, n), 1)
    lm = jnp.where(causal, jnp.exp(jnp.where(causal, acol - arow, 0.0)), 0.0)
    gm = cb * lm
    xh = x[:, i * p:(i + 1) * p]
    xdt = xh * dtc
    alast = acol[n - 1:n, :]
    dte = jnp.exp(alast - acol)
    return acol, lm, gm, dtc, xh, xdt, alast, dte


def _ssd_specs(s, wg, hg, rev):
    g, n, cl = SSM_GROUPS, SSM_STATE, SSD_CHUNK
    nc = s // cl
    boff, coff = (g * wg) // n, (g * wg) // n + g
    ci = (lambda c: nc - 1 - c) if rev else (lambda c: c)
    xblk = pl.BlockSpec((1, cl, wg), lambda b, k, c: (b, ci(c), k))
    bblk = pl.BlockSpec((1, cl, n), lambda b, k, c: (b, ci(c), boff + k))
    cblk = pl.BlockSpec((1, cl, n), lambda b, k, c: (b, ci(c), coff + k))
    nblk = pl.BlockSpec((1, cl, n), lambda b, k, c: (b, ci(c), k))
    dtblk = pl.BlockSpec((1, cl, LANES), lambda b, k, c: (b, ci(c), k))
    vec = pl.BlockSpec((1, LANES), lambda b, k, c: (0, k))
    hsblk = pl.BlockSpec((1, 1, 1, wg, n), lambda b, k, c: (b, k, ci(c), 0, 0))
    return nc, xblk, bblk, cblk, nblk, dtblk, vec, hsblk


def _ssd_fwd(xbc, dtraw, dtb, alog, dskip, hg, name, rider=None):
    bsz, s, _ = xbc.shape
    g, n, p = SSM_GROUPS, SSM_STATE, SSM_HEAD_DIM
    wg = hg * p
    nc, xblk, bblk, cblk, _, dtblk, vec, hsblk = _ssd_specs(s, wg, hg, False)

    def body(x_ref, b_ref, c_ref, dt_ref, dtb_ref, alog_ref, dsk_ref, y_ref, hs_ref, h_s):
        @pl.when(pl.program_id(2) == 0)
        def _():
            h_s[...] = jnp.zeros_like(h_s)

        bm, cm, _, dt, _, acum, acum_t, cb = _ssd_chunk_common(b_ref, c_ref, dt_ref, dtb_ref, alog_ref)
        x = x_ref[0]
        hs_ref[0, 0, 0] = h_s[...]
        hd_ = range(hg)
        hc = [_ssd_head_common(acum, acum_t, dt, cb, x, i) for i in hd_]
        hprev = [h_s[i * p:(i + 1) * p, :] for i in hd_]
        ydiag = [_dot(hc[i][2], hc[i][5]) for i in hd_]
        yoff = [_dot(cm, hprev[i], NT) for i in hd_]
        st = [_dot(hc[i][5] * hc[i][7][:, :p], bm, TN) for i in hd_]
        for i in hd_:
            acol, _, _, _, xh, _, alast, _ = hc[i]
            y_ref[0, :, i * p:(i + 1) * p] = ydiag[i] + yoff[i] * jnp.exp(acol[:, :p]) + xh * dsk_ref[:, i:i + 1]
            h_s[i * p:(i + 1) * p, :] = hprev[i] * jnp.exp(alast) + st[i]

    return _call(
        body, name=name, grid=(bsz, g, nc),
        in_specs=[xblk, bblk, cblk, dtblk, vec, vec, vec],
        out_specs=[xblk, hsblk],
        out_shape=[jax.ShapeDtypeStruct((bsz, s, g * wg), F32),
                   jax.ShapeDtypeStruct((bsz, g, nc, wg, n), F32)],
        scratch_shapes=[pltpu.VMEM((wg, n), F32)],
        sem=("parallel", "parallel", "arbitrary"), args=(xbc, xbc, xbc, dtraw, dtb, alog, dskip), rider=rider)


def _ssd_bwd(xbc, dtraw, dtb, alog, dskip, hs, dy, hg, name, rider=None):
    bsz, s, _ = xbc.shape
    g, n, p, cl = SSM_GROUPS, SSM_STATE, SSM_HEAD_DIM, SSD_CHUNK
    wg = hg * p
    nc, xblk, bblk, cblk, nblk, dtblk, vec, hsblk = _ssd_specs(s, wg, hg, True)

    def body(x_ref, b_ref, c_ref, dt_ref, dtb_ref, alog_ref, dsk_ref, hs_ref, dy_ref,
             dx_ref, db_ref, dc_ref, ddt_ref, dvec_ref, dh_s):
        @pl.when(pl.program_id(2) == 0)
        def _():
            dh_s[...] = jnp.zeros_like(dh_s)
            dvec_ref[...] = jnp.zeros_like(dvec_ref)

        lane = _iota2((cl, LANES), 1)
        sub = _iota2((LANES, cl), 0)
        lane1 = _iota2((1, LANES), 1)
        last_row = _iota2((cl, 1), 0) == cl - 1
        bm, cm, draw, dt, a_row, acum, acum_t, cb = _ssd_chunk_common(b_ref, c_ref, dt_ref, dtb_ref, alog_ref)
        x = x_ref[0]
        dyc = dy_ref[0]
        hd_ = range(hg)
        hc = [_ssd_head_common(acum, acum_t, dt, cb, x, i) for i in hd_]
        dyh = [dyc[:, i * p:(i + 1) * p] for i in hd_]
        hprev = [hs_ref[0, 0, 0, i * p:(i + 1) * p, :] for i in hd_]
        dhn = [dh_s[i * p:(i + 1) * p, :] for i in hd_]
        ea = [jnp.exp(hc[i][0]) for i in hd_]
        cd = [jnp.exp(hc[i][6]) for i in hd_]
        y0 = [_dot(cm, hprev[i], NT) for i in hd_]
        dxe = [_dot(bm, dhn[i], NT) for i in hd_]
        dgm = [_dot(dyh[i], hc[i][5], NT) for i in hd_]
        gdy = [_dot(hc[i][2], dyh[i], TN) for i in hd_]
        dy0 = [dyh[i] * ea[i][:, :p] for i in hd_]
        dcm_h = [_dot(dy0[i], hprev[i]) for i in hd_]
        dh_new = [_dot(dy0[i], cm, TN) + dhn[i] * cd[i] for i in hd_]
        dbm_h = [_dot(hc[i][5] * hc[i][7][:, :p], dhn[i]) for i in hd_]
        ws = [dgm[i] * hc[i][2] for i in hd_]
        dxdt = [dxe[i] * hc[i][7][:, :p] + gdy[i] for i in hd_]
        s_y0 = [_row_totals(dyh[i] * y0[i]) for i in hd_]
        s_xe = [_row_totals(dxe[i] * hc[i][5]) for i in hd_]
        s_ws = [_row_totals(ws[i]) for i in hd_]
        s_dt = [_row_totals(dxdt[i] * hc[i][4]) for i in hd_]
        s_dd = [_row_totals(dyh[i] * hc[i][4]) for i in hd_]
        s_hh = [_row_totals(dhn[i] * hprev[i]) for i in hd_]
        dcb = jnp.zeros((cl, cl), F32)
        dcm = jnp.zeros((cl, n), F32)
        dbm = jnp.zeros((cl, n), F32)
        da_col = jnp.zeros((cl, LANES), F32)
        da_row = jnp.zeros((LANES, cl), F32)
        ddt = jnp.zeros((cl, LANES), F32)
        dd = jnp.zeros((1, LANES), F32)
        for i in hd_:
            _, lm, _, dtc, _, _, _, dte = hc[i]
            dh_s[i * p:(i + 1) * p, :] = dh_new[i]
            dd = dd + jnp.where(lane1 == i, jnp.sum(s_dd[i], axis=0, keepdims=True), 0.0)
            t1 = s_xe[i] * dte
            d_alast = jnp.sum(s_hh[i], axis=0, keepdims=True) * cd[i] + jnp.sum(t1, axis=0, keepdims=True)
            dacol = s_y0[i] * ea[i] - t1 + s_ws[i] + jnp.where(last_row, d_alast, 0.0)
            dcb = dcb + dgm[i] * lm
            dcm = dcm + dcm_h[i]
            dbm = dbm + dbm_h[i]
            dx_ref[0, :, i * p:(i + 1) * p] = dxdt[i] * dtc + dyh[i] * dsk_ref[:, i:i + 1]
            da_col = jnp.where(lane == i, dacol, da_col)
            da_row = jnp.where(sub == i, -jnp.sum(ws[i], axis=0, keepdims=True), da_row)
            ddt = jnp.where(lane == i, s_dt[i], ddt)
        dc_ref[0] = dcm + _dot(dcb, bm)
        db_ref[0] = dbm + _dot(dcb, cm, TN)
        upper = _tri(cl, lambda r, k: r <= k)
        dda = _dot_exact(da_col, upper, ones_left=True) + _dot_exact(da_row, upper, dims=NT, ones_left=True)
        ddt = ddt + dda * a_row
        ddraw = ddt * _sigmoid(draw)
        ddt_ref[0] = ddraw.astype(BF16)
        dvec_ref[0, 0, 0:1, :] += jnp.sum(ddraw, axis=0, keepdims=True)
        dvec_ref[0, 0, 1:2, :] += jnp.sum(dda * dt, axis=0, keepdims=True) * a_row
        dvec_ref[0, 0, 2:3, :] += dd

    return _call(
        body, name=name, grid=(bsz, g, nc),
        in_specs=[xblk, bblk, cblk, dtblk, vec, vec, vec, hsblk, xblk],
        out_specs=[xblk, nblk, nblk, dtblk,
                   pl.BlockSpec((1, 1, SUBLANES, LANES), lambda b, k, c: (b, k, 0, 0))],
        out_shape=[jax.ShapeDtypeStruct((bsz, s, g * wg), F32), jax.ShapeDtypeStruct((bsz, s, g * n), F32),
                   jax.ShapeDtypeStruct((bsz, s, g * n), F32), jax.ShapeDtypeStruct((bsz, s, g * LANES), BF16),
                   jax.ShapeDtypeStruct((bsz, g, SUBLANES, LANES), F32)],
        scratch_shapes=[pltpu.VMEM((wg, n), F32)],
        sem=("parallel", "parallel", "arbitrary"), args=(xbc, xbc, xbc, dtraw, dtb, alog, dskip, hs, dy),
        rider=rider)


def _coords():
    return lax.axis_index("x"), lax.axis_index("y"), lax.axis_index("c")


def _other_chips(x, y):
    return [(1 - x, y), (x, 1 - y), (1 - x, 1 - y)]


def _remote(src, dst, send_sems, recv_sems, k, to):
    return pltpu.make_async_remote_copy(src_ref=src, dst_ref=dst, send_sem=send_sems.at[k],
                                        recv_sem=recv_sems.at[k], device_id=to, device_id_type=MESH)


def _standalone(rider, name):
    r_in, r_out, n_sems = len(rider["ins"]), len(rider["outs"]), rider["n_sems"]

    def body(*refs):
        rins, routs, (send_sems, recv_sems) = refs[:r_in], refs[r_in:r_in + r_out], refs[r_in + r_out:]
        cps = rider["copies"](rins, routs, send_sems, recv_sems)
        for cp in cps:
            cp.start()
        for cp in cps:
            cp.wait()

    res = pl.pallas_call(
        body, name=name, in_specs=[ANY] * r_in, out_specs=[ANY] * r_out, out_shape=list(rider["outs"]),
        scratch_shapes=[pltpu.SemaphoreType.DMA((n_sems,)), pltpu.SemaphoreType.DMA((n_sems,))],
        input_output_aliases=dict(rider["aliases"]),
    )(*rider["ins"])
    return list(res)


def _rows_of(shape):
    return shape[1] if len(shape) == 3 else shape[0]


def _slot(ref, j, rows):
    if len(ref.shape) == 3:
        return ref.at[j, rows]
    c = ref.shape[1] // N_CHIPS
    return ref.at[rows, pl.ds(pl.multiple_of(j * c, LANES), c)]


def _own_slot_set(shard, side_by_side):
    if side_by_side:
        return jnp.tile(shard, (1, N_CHIPS))
    return jnp.broadcast_to(shard[None], (N_CHIPS,) + shard.shape)


def _gather_chips_rider(shards, side_by_side):
    def copies(rins, routs, send_sems, recv_sems):
        x, y, c = _coords()
        me = 2 * x + y
        cps = []
        for q, (w_ref, o_ref) in enumerate(zip(rins[:len(shards)], routs, strict=True)):
            rh = w_ref.shape[0] // 2
            rows = pl.ds(c * rh, rh)
            for k, (px, py) in enumerate(_other_chips(x, y)):
                cps.append(_remote(w_ref.at[rows], _slot(o_ref, me, rows), send_sems, recv_sems, 3 * q + k,
                                   (px, py, c)))
        return cps

    bases = [_own_slot_set(w, side) for w, side in zip(shards, side_by_side, strict=True)]
    return dict(ins=list(shards) + bases, outs=[jax.ShapeDtypeStruct(b.shape, b.dtype) for b in bases],
                aliases={len(shards) + i: i for i in range(len(shards))}, n_sems=3 * len(shards), copies=copies)


def _gather_pair_rider(gathered):
    def copies(rins, routs, send_sems, recv_sems):
        x, y, c = _coords()
        cps = []
        for q, o_ref in enumerate(routs):
            rh = _rows_of(o_ref.shape) // 2
            for k, (px, py) in enumerate(_other_chips(x, y)):
                part = _slot(o_ref, 2 * px + py, pl.ds(c * rh, rh))
                cps.append(_remote(part, part, send_sems, recv_sems, 3 * q + k, (x, y, 1 - c)))
        return cps

    return dict(ins=list(gathered), outs=[jax.ShapeDtypeStruct(g.shape, g.dtype) for g in gathered],
                aliases={i: i for i in range(len(gathered))}, n_sems=3 * len(gathered), copies=copies)


def _reduce_pair_rider(grads):
    def copies(rins, routs, send_sems, recv_sems):
        x, y, c = _coords()
        cps, k = [], 0
        for g_ref, r_ref in zip(rins, routs, strict=True):
            rh = _rows_of(g_ref.shape) // 2
            rows = pl.ds((1 - c) * rh, rh)
            if len(g_ref.shape) == 3:
                cps.append(_remote(g_ref.at[:, rows], r_ref, send_sems, recv_sems, k, (x, y, 1 - c)))
                k += 1
            else:
                for j in range(N_CHIPS):
                    cps.append(_remote(_slot(g_ref, j, rows), r_ref.at[j], send_sems, recv_sems, k, (x, y, 1 - c)))
                    k += 1
        return cps

    def out_of(g):
        r, c = (g.shape[1], g.shape[2]) if g.ndim == 3 else (g.shape[0], g.shape[1] // N_CHIPS)
        return jax.ShapeDtypeStruct((N_CHIPS, r // 2, c), g.dtype)

    return dict(ins=list(grads), outs=[out_of(g) for g in grads], aliases={},
                n_sems=sum(1 if g.ndim == 3 else N_CHIPS for g in grads), copies=copies)


def _reduce_chips_rider(pair_sums):
    def copies(rins, routs, send_sems, recv_sems):
        x, y, c = _coords()
        cps = []
        for q, (p_ref, r_ref) in enumerate(zip(rins, routs, strict=True)):
            for k, (px, py) in enumerate(_other_chips(x, y)):
                cps.append(_remote(p_ref.at[2 * px + py], r_ref.at[k], send_sems, recv_sems, 3 * q + k, (px, py, c)))
        return cps

    return dict(ins=list(pair_sums), outs=[jax.ShapeDtypeStruct((3,) + p.shape[1:], p.dtype) for p in pair_sums],
                aliases={}, n_sems=3 * len(pair_sums), copies=copies)


def _reduce_finish_rider(sums, totals, layers, depth):
    def copies(rins, routs, send_sems, recv_sems):
        x, y, c = _coords()
        cps = []
        for q, (f_ref, o_ref) in enumerate(zip(rins[:len(sums)], routs, strict=True)):
            rh = f_ref.shape[0]
            cps.append(_remote(f_ref, o_ref.at[layers[q], pl.ds(c * rh, rh)], send_sems, recv_sems, q, (x, y, 1 - c)))
        return cps

    kept = [q for q, t in enumerate(totals) if t is not None]
    outs = [jax.ShapeDtypeStruct((depth, 2 * f.shape[0], f.shape[1]), F32) for f in sums]
    return dict(ins=list(sums) + [totals[q] for q in kept], outs=outs,
                aliases={len(sums) + k: q for k, q in enumerate(kept)}, n_sems=len(sums), copies=copies)


def _pair_add(gj, r1, ids, name, tr=256):
    stacked = gj.ndim == 3
    nj, rh, c = r1.shape
    tr = _fit(tr, rh, 16)
    nt = rh // tr

    def body(c_ref, chip_ref, g_ref, r_ref, p_ref, pb_ref):
        s = (g_ref[0] if stacked else g_ref[...]) + r_ref[0]
        pb_ref[0] = s.astype(BF16)

        @pl.when(pl.program_id(1) == chip_ref[0])
        def _():
            p_ref[...] = s

    blk_r = pl.BlockSpec((1, tr, c), lambda i, j, cr, jr: (j, i, 0))
    blk_g = (pl.BlockSpec((1, tr, c), lambda i, j, cr, jr: (j, cr[0] * nt + i, 0)) if stacked
             else pl.BlockSpec((tr, c), lambda i, j, cr, jr: (cr[0] * nt + i, j)))
    return pl.pallas_call(
        body, name=name,
        grid_spec=pltpu.PrefetchScalarGridSpec(
            num_scalar_prefetch=2, grid=(nt, nj), in_specs=[blk_g, blk_r],
            out_specs=[pl.BlockSpec((tr, c), lambda i, j, cr, jr: (i, 0)), blk_r]),
        out_shape=[jax.ShapeDtypeStruct((rh, c), F32), jax.ShapeDtypeStruct((nj, rh, c), BF16)],
        compiler_params=_params(("parallel", "arbitrary")),
    )(*ids, gj, r1)


def _chip_add(p, r2, name, tr=256):
    rh, c = p.shape
    tr = _fit(tr, rh, 16)

    def body(o_ref, r_ref, f_ref):
        f_ref[...] = ((o_ref[...] + r_ref[0].astype(F32)) + r_ref[1].astype(F32)) + r_ref[2].astype(F32)

    blk = pl.BlockSpec((tr, c), lambda i: (i, 0))
    return pl.pallas_call(
        body, name=name, grid=(rh // tr,),
        in_specs=[blk, pl.BlockSpec((3, tr, c), lambda i: (0, i, 0))], out_specs=blk,
        out_shape=jax.ShapeDtypeStruct((rh, c), F32),
        compiler_params=_params(("parallel",)),
    )(p, r2)


BIG = ["w_in", "w_out", "w_gate", "w_up", "w_down"]
LATE = ["w_out", "w_gate", "w_up", "w_down"]
FFN = ["w_gate", "w_up", "w_down"]
MIX = ["w_in", "w_out"]


SIDE_BY_SIDE = ("w_gate", "w_up")


class _GatherPlan:
    def __init__(self, late, next_in):
        self.late, self.next_in, self.parts = late, next_in, {}

    def rider(self, host):
        if self.late is None:
            return None
        if host == "attention_fwd":
            return _gather_chips_rider([self.late[n] for n in LATE], [n in SIDE_BY_SIDE for n in LATE])
        if host == "ssd_fwd":
            return _gather_pair_rider([self.parts[n] for n in LATE])
        if self.next_in is None:
            return None
        return (_gather_chips_rider([self.next_in], [False]) if host == "ffn_gate_up"
                else _gather_pair_rider([self.parts["w_in"]]))

    def collect(self, host, outs):
        if outs:
            self.parts.update(zip(LATE if host in ("attention_fwd", "ssd_fwd") else ["w_in"], outs, strict=True))

    def late_gathered(self):
        return {n: self.parts[n] for n in LATE}

    def next_gathered(self):
        return self.parts["w_in"]


def _gather_now(shard, tag):
    part = _standalone(_gather_chips_rider([shard], [False]), f"allgather_chips_{tag}")
    return _standalone(_gather_pair_rider(part), f"allgather_pair_{tag}")[0]


class _ReducePlan:
    PAIR = {"ffn_down_dgrad": "mix", "ffn_gate_up_dgrad": "ffn"}
    CHIPS = {"ssd_bwd": "mix", "attention_bwd": "ffn"}

    def __init__(self, mix, mix_layer, ffn_layer, totals, depth, ids):
        self.groups = {} if mix is None else {"mix": (MIX, mix, mix_layer)}
        self.ffn_layer, self.totals, self.depth, self.ids = ffn_layer, dict(totals), depth, ids
        self.p, self.pb, self.f = {}, {}, {}

    def add_ffn(self, grads):
        if self.ids is not None:
            self.groups["ffn"] = (FFN, grads, self.ffn_layer)

    def _present(self):
        return [(n, layer) for names, _, layer in self.groups.values() for n in names if n in self.f]

    def rider(self, host):
        if host == "proj_in_dgrad":
            done = self._present()
            if not done:
                return None
            return _reduce_finish_rider([self.f[n] for n, _ in done], [self.totals.get(n) for n, _ in done],
                                        [layer for _, layer in done], self.depth)
        group = self.groups.get(self.PAIR.get(host) or self.CHIPS.get(host))
        if group is None:
            return None
        names, grads, _ = group
        return (_reduce_pair_rider([grads[n] for n in names]) if host in self.PAIR
                else _reduce_chips_rider([self.pb[n] for n in names]))

    def collect(self, host, outs):
        if not outs:
            return
        if host == "proj_in_dgrad":
            c = lax.axis_index("c")
            for (n, layer), t in zip(self._present(), outs, strict=True):
                self.totals[n] = lax.dynamic_update_slice(t, self.f[n][None], (layer, c * self.f[n].shape[0], 0))
            return
        names, grads, layer = self.groups[self.PAIR.get(host) or self.CHIPS.get(host)]
        for n, got in zip(names, outs, strict=True):
            if host in self.PAIR:
                self.p[n], self.pb[n] = _pair_add(grads[n], got, self.ids, f"rs_pair_add_{n}_layer{layer}")
            else:
                self.f[n] = _chip_add(self.p[n], got, f"rs_chip_add_{n}_layer{layer}")

    def run_now(self, tag):
        self.collect("ffn_down_dgrad", _standalone(self.rider("ffn_down_dgrad"), f"rs_pair_{tag}"))
        self.collect("ssd_bwd", _standalone(self.rider("ssd_bwd"), f"rs_chips_{tag}"))
        self.collect("proj_in_dgrad", _standalone(self.rider("proj_in_dgrad"), f"rs_finish_{tag}"))
        return self.totals


def _small_exchange(v, name, reduce):
    rows = v.shape[0]

    def body(v_ref, o_ref, *rest):
        buf = rest[0] if reduce else o_ref
        send_sems, recv_sems = rest[-2], rest[-1]
        x, y, c = _coords()
        me = 4 * x + 2 * y + c
        buf[me] = v_ref[...]
        cps = []
        for r in range(1, N_DEV):
            peer = (lax.bitwise_xor(x, (r >> 2) & 1), lax.bitwise_xor(y, (r >> 1) & 1), lax.bitwise_xor(c, r & 1))
            cps.append(_remote(v_ref, buf.at[me], send_sems, recv_sems, r - 1, peer))
        for cp in cps:
            cp.start()
        for r in range(1, N_DEV):
            src = buf.at[lax.bitwise_xor(me, r)]
            _remote(src, src, send_sems, recv_sems, r - 1, (x, y, c)).wait_recv()
        for cp in cps:
            cp.wait_send()
        if reduce:
            acc = buf[0]
            for d in range(1, N_DEV):
                acc = acc + buf[d]
            o_ref[...] = acc
            o_ref[0:1, :] = jnp.broadcast_to(jnp.sum(acc[0:1, :], axis=1, keepdims=True), (1, LANES))

    scratch = [pltpu.SemaphoreType.DMA((N_DEV - 1,)), pltpu.SemaphoreType.DMA((N_DEV - 1,))]
    if reduce:
        scratch = [pltpu.VMEM((N_DEV, rows, LANES), F32)] + scratch
    out_shape = (rows, LANES) if reduce else (N_DEV, rows, LANES)
    return pl.pallas_call(
        body, name=name, in_specs=[VMEM], out_specs=VMEM,
        out_shape=jax.ShapeDtypeStruct(out_shape, F32), scratch_shapes=scratch,
    )(v)


def _pack(parts):
    flat = []
    for a in parts:
        a = a.reshape(-1)
        flat.append(jnp.pad(a, (0, (-a.shape[0]) % LANES)))
    v = jnp.concatenate(flat)
    v = jnp.pad(v, (0, (-v.shape[0]) % (SUBLANES * LANES)))
    return v.reshape(-1, LANES)


def _unpack(slab, shapes):
    flat = slab.reshape(-1)
    out, off = [], 0
    for shp in shapes:
        size = 1
        for d in shp:
            size *= d
        out.append(flat[off:off + size].reshape(shp))
        off += size + (-size) % LANES
    return out


def _group_slots(a, hg):
    lead = a.shape[:-1]
    a = a.reshape(lead + (SSM_GROUPS, hg))
    a = jnp.pad(a, [(0, 0)] * len(lead) + [(0, 0), (0, LANES - hg)])
    return a.reshape(lead + (SSM_GROUPS * LANES,))


def _ungroup_slots(a, hg):
    lead = a.shape[:-1]
    return a.reshape(lead + (SSM_GROUPS, LANES))[..., :hg].reshape(lead + (SSM_GROUPS * hg,))


def _layer_fwd(x, p, bsz, s, plan, late_params=None):
    t, d = x.shape
    aw, sw, cd, hg = p["aw"], p["sw"], p["cd"], p["hg"]
    h = _rmsnorm_fwd(x, p["norm_mix"], "norm_mix_fwd")
    qkv = _mm(h, p["wqkv"], "nn", t, 3 * aw, F32, "proj_qkv", tm=1024, tn=512)
    z = _mm(h, p["wz"], "nn", t, sw, F32, "proj_z", tm=1024, tn=512)
    xbc = _mm(h, p["wxbc"], "nn", t, cd, F32, "proj_xbc", tm=1024, tn=512)
    dtraw = _mm(h, p["wdt"], "nn", t, SSM_GROUPS * LANES, F32, "proj_dt", tm=1024, tn=SSM_GROUPS * LANES)
    qkv3 = qkv.reshape(bsz, s, 3 * aw)
    (o_att, rtot), sent = _attention_fwd(qkv3, p["q_gain"], p["k_gain"], "attention_fwd",
                                         rider=plan.rider("attention_fwd"))
    plan.collect("attention_fwd", sent)
    xbc3 = xbc.reshape(bsz, s, cd)
    xact = _conv_fwd(xbc3, p["conv_w"], p["conv_b"], "conv_fwd")
    dt3 = dtraw.reshape(bsz, s, SSM_GROUPS * LANES)
    (y, hs), sent = _ssd_fwd(xact, dt3, p["dt_bias"], p["a_log"], p["d_skip"], hg, "ssd_fwd",
                             rider=plan.rider("ssd_fwd"))
    plan.collect("ssd_fwd", sent)
    if plan.late is not None:
        p = {**p, **late_params(plan.late_gathered())}
    dff = p["wg"].shape[1]
    o2, y2 = o_att.reshape(t, aw), y.reshape(t, sw)
    mix = _merge_fwd(o2, y2, z, p["attn_out_gain"], p["ssm_out_gain"], "merge_fwd")
    x1 = _mm(mix, p["wout"], "nn", t, d, F32, "proj_out", tm=1024, tn=512, res=x)
    h2 = _rmsnorm_fwd(x1, p["norm_ffn"], "norm_ffn_fwd")
    (gate, up, act), _ = _hosted_matmul(
        [[(h2, p["wg"], "nn")], [(h2, p["wu"], "nn")]], [], _swiglu_fwd_epilogue,
        [F32, F32, BF16], t, dff, 1024, 512, "ffn_gate_up", plan)
    (x2,), _ = _hosted_matmul([[(act, p["wd"], "nn")]], [x1], lambda accs, ex: (accs[0] + ex[0],),
                              [F32], t, d, 512, 512, "ffn_down", plan)
    saved = dict(x=x, h=h, qkv3=qkv3, z=z, xbc3=xbc3, dt3=dt3, o2=o2, rtot=rtot, xact=xact, hs=hs, y2=y2,
                 mix=mix, x1=x1, h2=h2, gate=gate, up=up, act=act)
    return x2, saved, p


def _hosted_matmul(groups, extras, epilogue, out_dtypes, m, n, tm, tn, name, plan):
    rider = plan.rider(name)
    if rider is None:
        return _matmul(groups, extras, epilogue, out_dtypes, m, n, tm, tn, name), []
    outs, sent = _matmul(groups, extras, epilogue, out_dtypes, m, n, tm, tn, name, rider=rider)
    plan.collect(name, sent)
    return outs, sent


def _layer_bwd(dx2, dx2b, p, sv, bsz, s, plan, ffn_to_chips=None):
    t, d = dx2.shape
    aw, sw, cd, hg = p["aw"], p["sw"], p["cd"], p["hg"]
    dff = p["wg"].shape[1]
    gr = {}
    (dgate, dup), _ = _hosted_matmul([[(dx2b, p["wd"], "nt")]], [sv["gate"], sv["up"]], _swiglu_bwd_epilogue,
                                     [BF16, BF16], t, dff, 1024, 512, "ffn_down_dgrad", plan)
    gr["wd"] = _mm(sv["act"], dx2b, "tn", dff, d, F32, "ffn_down_wgrad")
    gr["wg"] = _mm(sv["h2"], dgate, "tn", d, dff, F32, "ffn_gate_wgrad")
    gr["wu"] = _mm(sv["h2"], dup, "tn", d, dff, F32, "ffn_up_wgrad")
    if ffn_to_chips is not None:
        plan.add_ffn(ffn_to_chips(gr))
    (dh2,), _ = _hosted_matmul([[(dgate, p["wg"], "nt"), (dup, p["wu"], "nt")]], [], lambda accs, ex: (accs[0],),
                               [F32], t, d, 512, 256, "ffn_gate_up_dgrad", plan)
    dx1, dx1b, gr["norm_ffn"] = _rmsnorm_bwd(sv["x1"], p["norm_ffn"], dh2, dx2, "norm_ffn_bwd")
    dmix = _mm(dx1b, p["wout"], "nt", t, aw + sw, F32, "proj_out_dgrad", tm=1024)
    gr["wout"] = _mm(sv["mix"], dx1b, "tn", aw + sw, d, F32, "proj_out_wgrad")
    do, dy, dz, gr["attn_out_gain"], gr["ssm_out_gain"] = _merge_bwd(
        sv["o2"], sv["y2"], sv["z"], p["attn_out_gain"], p["ssm_out_gain"], dmix, "merge_bwd")
    (dxs, dbm, dcm, ddt, dvec), sent = _ssd_bwd(sv["xact"], sv["dt3"], p["dt_bias"], p["a_log"], p["d_skip"],
                                                sv["hs"], dy.reshape(bsz, s, sw), hg, "ssd_bwd",
                                                rider=plan.rider("ssd_bwd"))
    plan.collect("ssd_bwd", sent)
    dvec = jnp.sum(dvec, axis=0).reshape(SSM_GROUPS, SUBLANES, LANES)
    gr["dt_bias"], gr["a_log"], gr["d_skip"] = (dvec[:, k, :hg].reshape(-1) for k in range(3))
    dxact = jnp.concatenate([dxs, dbm, dcm], axis=-1)
    dxbc, gr["conv_w"], gr["conv_b"] = _conv_bwd(sv["xbc3"], p["conv_w"], p["conv_b"], dxact, "conv_bwd")
    (dqkv, dqg, dkg), sent = _attention_bwd(sv["qkv3"], p["q_gain"], p["k_gain"], sv["rtot"],
                                            do.reshape(bsz, s, aw), "attention_bwd",
                                            rider=plan.rider("attention_bwd"))
    plan.collect("attention_bwd", sent)
    gr["q_gain"] = jnp.sum(dqg, axis=(0, 1, 2))
    gr["k_gain"] = jnp.sum(dkg, axis=(0, 1, 2))
    dqkv, dxbc, ddt = dqkv.reshape(t, 3 * aw), dxbc.reshape(t, cd), ddt.reshape(t, SSM_GROUPS * LANES)
    (dh,), _ = _hosted_matmul(
        [[(dqkv, p["wqkv"], "nt"), (dz, p["wz"], "nt"), (dxbc, p["wxbc"], "nt"), (ddt, p["wdt"], "nt")]],
        [], lambda accs, ex: (accs[0],), [F32], t, d, 512, 512, "proj_in_dgrad", plan)
    h = sv["h"]
    gr["wqkv"] = _mm(h, dqkv, "tn", d, 3 * aw, F32, "proj_qkv_wgrad")
    gr["wz"] = _mm(h, dz, "tn", d, sw, F32, "proj_z_wgrad")
    gr["wxbc"] = _mm(h, dxbc, "tn", d, cd, F32, "proj_xbc_wgrad")
    gr["wdt"] = _mm(h, ddt, "tn", d, SSM_GROUPS * LANES, F32, "proj_dt_wgrad", tn=SSM_GROUPS * LANES)
    dx, dxb, gr["norm_mix"] = _rmsnorm_bwd(sv["x"], p["norm_mix"], dh, dx1, "norm_mix_bwd")
    return dx, dxb, gr


SMALL = ["norm_mix", "q_gain", "k_gain", "conv_w", "conv_b", "dt_bias", "a_log", "d_skip",
         "attn_out_gain", "ssm_out_gain", "norm_ffn"]
ORDER = ["norm_mix", "w_in", "q_gain", "k_gain", "conv_w", "conv_b", "dt_bias", "a_log", "d_skip",
         "attn_out_gain", "ssm_out_gain", "w_out", "norm_ffn", "w_gate", "w_up", "w_down"]


def kernel(x, norm_mix, w_in, q_gain, k_gain, conv_w, conv_b, dt_bias, a_log, d_skip, attn_out_gain, ssm_out_gain, w_out, norm_ffn, w_gate, w_up, w_down, loss_target, m_norm_mix, m_w_in, m_q_gain, m_k_gain, m_conv_w, m_conv_b, m_dt_bias, m_a_log, m_d_skip, m_attn_out_gain, m_ssm_out_gain, m_w_out, m_norm_ffn, m_w_gate, m_w_up, m_w_down, v_norm_mix, v_w_in, v_q_gain, v_k_gain, v_conv_w, v_conv_b, v_dt_bias, v_a_log, v_d_skip, v_attn_out_gain, v_ssm_out_gain, v_w_out, v_norm_ffn, v_w_gate, v_w_up, v_w_down):
    w = dict(norm_mix=norm_mix, w_in=w_in, q_gain=q_gain, k_gain=k_gain, conv_w=conv_w, conv_b=conv_b,
             dt_bias=dt_bias, a_log=a_log, d_skip=d_skip, attn_out_gain=attn_out_gain, ssm_out_gain=ssm_out_gain,
             w_out=w_out, norm_ffn=norm_ffn, w_gate=w_gate, w_up=w_up, w_down=w_down)
    mom = dict(norm_mix=m_norm_mix, w_in=m_w_in, q_gain=m_q_gain, k_gain=m_k_gain, conv_w=m_conv_w,
               conv_b=m_conv_b, dt_bias=m_dt_bias, a_log=m_a_log, d_skip=m_d_skip,
               attn_out_gain=m_attn_out_gain, ssm_out_gain=m_ssm_out_gain, w_out=m_w_out, norm_ffn=m_norm_ffn,
               w_gate=m_w_gate, w_up=m_w_up, w_down=m_w_down)
    var = dict(norm_mix=v_norm_mix, w_in=v_w_in, q_gain=v_q_gain, k_gain=v_k_gain, conv_w=v_conv_w,
               conv_b=v_conv_b, dt_bias=v_dt_bias, a_log=v_a_log, d_skip=v_d_skip,
               attn_out_gain=v_attn_out_gain, ssm_out_gain=v_ssm_out_gain, w_out=v_w_out, norm_ffn=v_norm_ffn,
               w_gate=v_w_gate, w_up=v_w_up, w_down=v_w_down)

    bsz, s, d = x.shape
    t = bsz * s
    depth = norm_mix.shape[0]
    aw = attn_out_gain.shape[1]
    sw = ssm_out_gain.shape[1]
    cd = conv_b.shape[1]
    hs_n = dt_bias.shape[1]
    hg = hs_n // SSM_GROUPS
    heads = aw // ATT_HEAD_DIM
    in_dim = 3 * aw + sw + cd + hs_n
    dff = w_gate.shape[2] * N_CHIPS
    cs = conv_w.shape[2]
    my_chip = 2 * lax.axis_index("x") + lax.axis_index("y")
    ids = (lax.axis_index("c").astype(jnp.int32).reshape(1), my_chip.astype(jnp.int32).reshape(1))
    wb ={n: w[n].astype(BF16) for n in BIG}
    conv_all = _small_exchange(_pack([conv_w]), "allgather_conv_w", False)
    conv_full = jnp.concatenate(
        [_unpack(conv_all[2 * j], [conv_w.shape])[0] for j in range(N_CHIPS)], axis=-1)

    def in_params(l, gat_in):
        win = jnp.transpose(gat_in, (1, 0, 2)).reshape(d, in_dim)
        wqkv = win[:, :3 * aw].reshape(d, 3, heads, ATT_HEAD_DIM)
        wqkv = jnp.transpose(wqkv, (0, 2, 1, 3)).reshape(d, 3 * aw)
        return dict(aw=aw, sw=sw, cd=cd, hg=hg, norm_mix=norm_mix[l], wqkv=wqkv, wz=win[:, 3 * aw:3 * aw + sw],
                    wxbc=win[:, 3 * aw + sw:3 * aw + sw + cd], wdt=_group_slots(win[:, 3 * aw + sw + cd:], hg),
                    q_gain=q_gain[l], k_gain=k_gain[l], conv_w=conv_full[l], conv_b=conv_b[l],
                    dt_bias=_group_slots(dt_bias[l], hg).reshape(1, -1),
                    a_log=_group_slots(a_log[l], hg).reshape(1, -1),
                    d_skip=_group_slots(d_skip[l], hg).reshape(1, -1),
                    attn_out_gain=attn_out_gain[l], ssm_out_gain=ssm_out_gain[l], norm_ffn=norm_ffn[l])

    def late_params(gat):
        return dict(wout=gat["w_out"].reshape(aw + sw, d), wd=gat["w_down"].reshape(dff, d),
                    wg=gat["w_gate"], wu=gat["w_up"])

    def ffn_to_chips(gr):
        return {"w_gate": gr["wg"], "w_up": gr["wu"], "w_down": gr["wd"].reshape(N_CHIPS, dff // N_CHIPS, d)}

    def mix_to_chips(gr):
        gqkv = gr["wqkv"].reshape(d, heads, 3, ATT_HEAD_DIM)
        gqkv = jnp.transpose(gqkv, (0, 2, 1, 3)).reshape(d, 3 * aw)
        gin = jnp.concatenate([gqkv, gr["wz"], gr["wxbc"], _ungroup_slots(gr["wdt"], hg)], axis=-1)
        return {"w_in": jnp.transpose(gin.reshape(d, N_CHIPS, in_dim // N_CHIPS), (1, 0, 2)),
                "w_out": gr["wout"].reshape(N_CHIPS, (aw + sw) // N_CHIPS, d)}

    xt = x.reshape(t, d)
    saved, params = [], []
    gat_in = _gather_now(wb["w_in"][0], "first")
    for l in range(depth):
        plan = _GatherPlan({n: wb[n][l] for n in LATE}, wb["w_in"][l + 1] if l + 1 < depth else None)
        xt, sv, p = _layer_fwd(xt, in_params(l, gat_in), bsz, s, plan, late_params)
        saved.append(sv)
        params.append(p)
        if l + 1 < depth:
            gat_in = plan.next_gathered()
    dxt, dxb, loss_lanes = _loss_head(xt, loss_target.reshape(t, d), "loss_head")

    grads = [None] * depth
    pending, totals = None, {}
    for l in reversed(range(depth)):
        plan = _ReducePlan(pending, l + 1, l, totals, depth, ids)
        dxt, dxb, grads[l] = _layer_bwd(dxt, dxb, params[l], saved[l], bsz, s, plan, ffn_to_chips)
        totals = plan.totals
        pending = mix_to_chips(grads[l])
    g = _ReducePlan(pending, 0, None, totals, depth, ids).run_now("first_layer")
    grad_x = dxt.reshape(bsz, s, d)

    def stack(name):
        return jnp.stack([grads[l][name] for l in range(depth)])

    small_shapes = [(1, LANES)] + [(depth, CONV_WIDTH, cd) if n == "conv_w" else w[n].shape for n in SMALL]
    small = _small_exchange(_pack([loss_lanes] + [stack(n) for n in SMALL]), "allreduce_small", True)
    small = _unpack(small, small_shapes)
    loss = small[0][0, 0]
    for n, a in zip(SMALL, small[1:], strict=True):
        g[n] = a
    g["conv_w"] = lax.dynamic_slice_in_dim(g["conv_w"], my_chip * cs, cs, axis=2)

    delta, new_m, new_v = {}, {}, {}
    for n in BIG:
        shp = w[n].shape
        two_d = (shp[0] * shp[1], shp[2])
        dl, nm, nv = _adamw(w[n].reshape(two_d), g[n].reshape(two_d), mom[n].reshape(two_d),
                            var[n].reshape(two_d), f"adamw_{n}")
        delta[n], new_m[n], new_v[n] = dl.reshape(shp), nm.reshape(shp), nv.reshape(shp)
    shapes = [w[n].shape for n in SMALL]
    dl, nm, nv = _adamw(_pack([w[n] for n in SMALL]), _pack([g[n] for n in SMALL]),
                        _pack([mom[n] for n in SMALL]), _pack([var[n] for n in SMALL]), "adamw_small")
    for n, a, b, c in zip(SMALL, _unpack(dl, shapes), _unpack(nm, shapes), _unpack(nv, shapes), strict=True):
        delta[n], new_m[n], new_v[n] = a, b, c

    return (loss, grad_x, *[g[n] for n in ORDER], *[delta[n] for n in ORDER],
            *[new_m[n] for n in ORDER], *[new_v[n] for n in ORDER])
```

```python
import jax
import jax.numpy as jnp
from jax import lax
from jax.experimental import pallas as pl
from jax.experimental.pallas import tpu as pltpu

F32 = jnp.float32
BF16 = jnp.bfloat16
MESH = pl.DeviceIdType.MESH
ANY = pl.BlockSpec(memory_space=pl.ANY)
VMEM = pl.BlockSpec(memory_space=pltpu.VMEM)

EPS = 1e-6
ATT_HEAD_DIM = 128
SSM_HEAD_DIM = 64
SSM_GROUPS = 2
SSM_STATE = 128
SSD_CHUNK = 128
CONV_WIDTH = 4
LANES = 128
SUBLANES = 8
ATT_TILE = 512
ATT_STRIP = 128
N_CHIPS = 4
N_DEV = 8

ADAM_LR = 0.001
ADAM_B1 = 0.9
ADAM_B2 = 0.999
ADAM_EPS = 1e-08
ADAM_WD = 0.01
ADAM_STEP = 10

VMEM_LIMIT = 48 * 1024 * 1024

NN = (((1,), (0,)), ((), ()))
NT = (((1,), (1,)), ((), ()))
TN = (((0,), (0,)), ((), ()))


def _dot(a, b, dims=NN):
    return lax.dot_general(a.astype(BF16), b.astype(BF16), dims, preferred_element_type=F32)


def _dot_exact(x, ones, dims=NN, passes=3, ones_left=False):
    acc = None
    rem = x
    for _ in range(passes):
        piece = rem.astype(BF16)
        rem = rem - piece.astype(F32)
        p = (lax.dot_general(ones, piece, dims, preferred_element_type=F32) if ones_left
             else lax.dot_general(piece, ones, dims, preferred_element_type=F32))
        acc = p if acc is None else acc + p
    return acc


def _scan_lanes(x, tri, passes, reverse=False):
    nblk = x.shape[1] // LANES
    blocks = [x[:, k * LANES:(k + 1) * LANES] for k in range(nblk)]
    out, carry = [None] * nblk, None
    for k in (reversed(range(nblk)) if reverse else range(nblk)):
        p = _dot_exact(blocks[k], tri, passes=passes)
        out[k] = p if carry is None else p + carry
        tot = jnp.sum(blocks[k], axis=1, keepdims=True)
        carry = tot if carry is None else carry + tot
    return (out[0] if nblk == 1 else jnp.concatenate(out, axis=1)), carry


def _iota2(shape, axis):
    return lax.broadcasted_iota(jnp.int32, shape, axis)


def _tri(n, cmp):
    return cmp(_iota2((n, n), 0), _iota2((n, n), 1)).astype(BF16)


def _sum_all(v):
    return jnp.sum(jnp.sum(v, axis=1, keepdims=True), axis=0, keepdims=True)


def _fit(tile, dim, unit=LANES):
    if dim <= tile:
        return dim
    return max(k for k in range(unit, tile + 1, unit) if dim % k == 0)


def _params(sem):
    return pltpu.CompilerParams(dimension_semantics=sem, vmem_limit_bytes=VMEM_LIMIT)


def _call(body, *, name, grid, in_specs, out_specs, out_shape, sem, args, scratch_shapes=(), rider=None):
    in_specs, out_specs, out_shape = list(in_specs), list(out_specs), list(out_shape)
    scratch_shapes = list(scratch_shapes)
    if rider is None:
        res = pl.pallas_call(body, name=name, grid=grid, in_specs=in_specs, out_specs=out_specs,
                             out_shape=out_shape, scratch_shapes=scratch_shapes,
                             compiler_params=_params(sem))(*args)
        return list(res), []
    n_in, n_out, n_scr = len(in_specs), len(out_specs), len(scratch_shapes)
    r_in, r_out, n_sems = len(rider["ins"]), len(rider["outs"]), rider["n_sems"]

    def hosted(*refs):
        ins, rest = refs[:n_in], refs[n_in:]
        rins, rest = rest[:r_in], rest[r_in:]
        outs, rest = rest[:n_out], rest[n_out:]
        routs, rest = rest[:r_out], rest[r_out:]
        scr, (send_sems, recv_sems) = rest[:n_scr], rest[n_scr:]
        first, last = None, None
        for d, size in enumerate(grid):
            f, e = pl.program_id(d) == 0, pl.program_id(d) == size - 1
            first = f if first is None else jnp.logical_and(first, f)
            last = e if last is None else jnp.logical_and(last, e)

        @pl.when(first)
        def _():
            for cp in rider["copies"](rins, routs, send_sems, recv_sems):
                cp.start()

        body(*ins, *outs, *scr)

        @pl.when(last)
        def _():
            for cp in rider["copies"](rins, routs, send_sems, recv_sems):
                cp.wait()

    res = pl.pallas_call(
        hosted, name=name, grid=grid, in_specs=in_specs + [ANY] * r_in, out_specs=out_specs + [ANY] * r_out,
        out_shape=out_shape + list(rider["outs"]),
        scratch_shapes=scratch_shapes + [pltpu.SemaphoreType.DMA((n_sems,)), pltpu.SemaphoreType.DMA((n_sems,))],
        input_output_aliases={n_in + i: n_out + o for i, o in rider["aliases"].items()},
        compiler_params=_params(("arbitrary",) * len(grid)),
    )(*args, *rider["ins"])
    return list(res[:n_out]), list(res[n_out:])


def _softplus(x):
    return jnp.maximum(x, 0.0) + jnp.log(1.0 + jnp.exp(-jnp.abs(x)))


def _sigmoid(x):
    return 1.0 / (1.0 + jnp.exp(-x))


def _rms_fwd(x, g):
    r = lax.rsqrt(jnp.mean(x * x, axis=-1, keepdims=True) + EPS)
    return (x * r) * g


def _rms_bwd(x, g, dh):
    r = lax.rsqrt(jnp.mean(x * x, axis=-1, keepdims=True) + EPS)
    y = x * r
    dy = dh * g
    dx = r * (dy - y * jnp.mean(dy * y, axis=-1, keepdims=True))
    return dx, dh * y


def _matmul(groups, extras, epilogue, out_dtypes, m, n, tm, tn, name, rider=None):
    tm, tn = _fit(tm, m), _fit(tn, n)
    flat = [t for grp in groups for t in grp]
    n_terms, n_extra = len(flat), len(extras)

    def body(*refs):
        outs = refs[2 * n_terms + n_extra:]
        accs, pos = [], 0
        for grp in groups:
            acc = None
            for (_, _, mode) in grp:
                dims = {"nn": NN, "nt": NT, "tn": TN}[mode]
                p = _dot(refs[2 * pos][...], refs[2 * pos + 1][...], dims)
                acc = p if acc is None else acc + p
                pos += 1
            accs.append(acc)
        ex = [refs[2 * n_terms + i][...] for i in range(n_extra)]
        res = epilogue(accs, ex)
        for o_ref, r in zip(outs, res, strict=True):
            o_ref[...] = r.astype(o_ref.dtype)

    in_specs, args = [], []
    for (a, b, mode) in flat:
        if mode == "nn":
            k = a.shape[1]
            in_specs += [pl.BlockSpec((tm, k), lambda i, j: (i, 0)), pl.BlockSpec((k, tn), lambda i, j: (0, j))]
        elif mode == "nt":
            k = a.shape[1]
            in_specs += [pl.BlockSpec((tm, k), lambda i, j: (i, 0)), pl.BlockSpec((tn, k), lambda i, j: (j, 0))]
        else:
            k = a.shape[0]
            in_specs += [pl.BlockSpec((k, tm), lambda i, j: (0, i)), pl.BlockSpec((k, tn), lambda i, j: (0, j))]
        args += [a, b]
    for e in extras:
        in_specs.append(pl.BlockSpec((tm, tn), lambda i, j: (i, j)))
        args.append(e)
    outs, routs = _call(
        body, name=name, grid=(m // tm, n // tn), in_specs=in_specs,
        out_specs=[pl.BlockSpec((tm, tn), lambda i, j: (i, j)) for _ in out_dtypes],
        out_shape=[jax.ShapeDtypeStruct((m, n), d) for d in out_dtypes],
        sem=("parallel", "parallel"), args=args, rider=rider)
    return outs if rider is None else (outs, routs)


def _mm(a, b, mode, m, n, out_dtype, name, tm=512, tn=512, res=None):
    extras = [] if res is None else [res]
    epi = (lambda accs, ex: (accs[0],)) if res is None else (lambda accs, ex: (accs[0] + ex[0],))
    return _matmul([[(a, b, mode)]], extras, epi, [out_dtype], m, n, tm, tn, name)[0]


def _swiglu_fwd_epilogue(accs, ex):
    g, u = accs
    return g, u, (g * _sigmoid(g)) * u


def _swiglu_bwd_epilogue(accs, ex):
    dact, (g, u) = accs[0], ex
    sg = _sigmoid(g)
    silu = g * sg
    return dact * u * (sg * (1.0 + g * (1.0 - sg))), dact * silu


def _rmsnorm_fwd(x, g, name, tr=512):
    t, d = x.shape
    tr = min(tr, t)

    def body(x_ref, g_ref, h_ref):
        h_ref[...] = _rms_fwd(x_ref[...], g_ref[...]).astype(BF16)

    return pl.pallas_call(
        body, name=name, grid=(t // tr,),
        in_specs=[pl.BlockSpec((tr, d), lambda i: (i, 0)), pl.BlockSpec((1, d), lambda i: (0, 0))],
        out_specs=pl.BlockSpec((tr, d), lambda i: (i, 0)),
        out_shape=jax.ShapeDtypeStruct((t, d), BF16),
        compiler_params=_params(("parallel",)),
    )(x, g.reshape(1, d))


def _rmsnorm_bwd(x, g, dh, dres, name, tr=256):
    t, d = x.shape
    tr = min(tr, t)

    def body(x_ref, g_ref, dh_ref, dres_ref, dx_ref, dxb_ref, dg_ref):
        dx, dgr = _rms_bwd(x_ref[...], g_ref[...], dh_ref[...])
        dx = dx + dres_ref[...]
        dx_ref[...] = dx
        dxb_ref[...] = dx.astype(BF16)

        @pl.when(pl.program_id(0) == 0)
        def _():
            dg_ref[...] = jnp.zeros_like(dg_ref)

        dg_ref[...] += jnp.sum(dgr, axis=0, keepdims=True)

    row = pl.BlockSpec((tr, d), lambda i: (i, 0))
    vec = pl.BlockSpec((1, d), lambda i: (0, 0))
    dx, dxb, dg = pl.pallas_call(
        body, name=name, grid=(t // tr,),
        in_specs=[row, vec, row, row], out_specs=[row, row, vec],
        out_shape=[jax.ShapeDtypeStruct((t, d), F32), jax.ShapeDtypeStruct((t, d), BF16),
                   jax.ShapeDtypeStruct((1, d), F32)],
        compiler_params=_params(("arbitrary",)),
    )(x, g.reshape(1, d), dh, dres)
    return dx, dxb, dg.reshape(d)


def _merge_fwd(o_att, y, z, ga, gs, name, tr=256):
    t, wa = o_att.shape
    ws = y.shape[1]
    wg = ws // SSM_GROUPS
    tr = min(tr, t)

    def body(o_ref, y_ref, z_ref, ga_ref, gs_ref, m_ref):
        m_ref[:, 0:wa] = _rms_fwd(o_ref[...], ga_ref[...]).astype(BF16)
        for g in range(SSM_GROUPS):
            sl = slice(g * wg, (g + 1) * wg)
            zz = z_ref[:, sl]
            yz = y_ref[:, sl] * (zz * _sigmoid(zz))
            m_ref[:, wa + g * wg:wa + (g + 1) * wg] = _rms_fwd(yz, gs_ref[:, sl]).astype(BF16)

    return pl.pallas_call(
        body, name=name, grid=(t // tr,),
        in_specs=[pl.BlockSpec((tr, wa), lambda i: (i, 0)), pl.BlockSpec((tr, ws), lambda i: (i, 0)),
                  pl.BlockSpec((tr, ws), lambda i: (i, 0)), pl.BlockSpec((1, wa), lambda i: (0, 0)),
                  pl.BlockSpec((1, ws), lambda i: (0, 0))],
        out_specs=pl.BlockSpec((tr, wa + ws), lambda i: (i, 0)),
        out_shape=jax.ShapeDtypeStruct((t, wa + ws), BF16),
        compiler_params=_params(("parallel",)),
    )(o_att, y, z, ga.reshape(1, wa), gs.reshape(1, ws))


def _merge_bwd(o_att, y, z, ga, gs, dmix, name, tr=256):
    t, wa = o_att.shape
    ws = y.shape[1]
    wg = ws // SSM_GROUPS
    tr = min(tr, t)

    def body(o_ref, y_ref, z_ref, ga_ref, gs_ref, dm_ref, do_ref, dy_ref, dz_ref, dga_ref, dgs_ref):
        @pl.when(pl.program_id(0) == 0)
        def _():
            dga_ref[...] = jnp.zeros_like(dga_ref)
            dgs_ref[...] = jnp.zeros_like(dgs_ref)

        do, dgr = _rms_bwd(o_ref[...], ga_ref[...], dm_ref[:, 0:wa])
        do_ref[...] = do
        dga_ref[...] += jnp.sum(dgr, axis=0, keepdims=True)
        for g in range(SSM_GROUPS):
            sl = slice(g * wg, (g + 1) * wg)
            zz, yy = z_ref[:, sl], y_ref[:, sl]
            sg = _sigmoid(zz)
            silu = zz * sg
            dyz, dgr = _rms_bwd(yy * silu, gs_ref[:, sl], dm_ref[:, wa + g * wg:wa + (g + 1) * wg])
            dy_ref[:, sl] = dyz * silu
            dz_ref[:, sl] = (dyz * yy * (sg + silu * (1.0 - sg))).astype(BF16)
            dgs_ref[:, sl] += jnp.sum(dgr, axis=0, keepdims=True)

    rowa = pl.BlockSpec((tr, wa), lambda i: (i, 0))
    rows = pl.BlockSpec((tr, ws), lambda i: (i, 0))
    veca = pl.BlockSpec((1, wa), lambda i: (0, 0))
    vecs = pl.BlockSpec((1, ws), lambda i: (0, 0))
    do, dy, dz, dga, dgs = pl.pallas_call(
        body, name=name, grid=(t // tr,),
        in_specs=[rowa, rows, rows, veca, vecs, pl.BlockSpec((tr, wa + ws), lambda i: (i, 0))],
        out_specs=[rowa, rows, rows, veca, vecs],
        out_shape=[jax.ShapeDtypeStruct((t, wa), F32), jax.ShapeDtypeStruct((t, ws), F32),
                   jax.ShapeDtypeStruct((t, ws), BF16), jax.ShapeDtypeStruct((1, wa), F32),
                   jax.ShapeDtypeStruct((1, ws), F32)],
        compiler_params=_params(("arbitrary",)),
    )(o_att, y, z, ga.reshape(1, wa), gs.reshape(1, ws), dmix)
    return do, dy, dz, dga.reshape(wa), dgs.reshape(ws)


def _loss_head(y, target, name, tr=256):
    t, d = y.shape
    tr = min(tr, t)

    def body(y_ref, t_ref, dy_ref, dyb_ref, l_ref):
        @pl.when(pl.program_id(0) == 0)
        def _():
            l_ref[...] = jnp.zeros_like(l_ref)

        diff = y_ref[...] - t_ref[...]
        dy = diff * (1.0 / d)
        dy_ref[...] = dy
        dyb_ref[...] = dy.astype(BF16)
        part = jnp.sum(diff * diff, axis=0, keepdims=True)
        fold = part[:, 0:LANES]
        for k in range(1, d // LANES):
            fold = fold + part[:, k * LANES:(k + 1) * LANES]
        l_ref[...] += fold * (0.5 / d)

    row = pl.BlockSpec((tr, d), lambda i: (i, 0))
    return pl.pallas_call(
        body, name=name, grid=(t // tr,), in_specs=[row, row],
        out_specs=[row, row, pl.BlockSpec((1, LANES), lambda i: (0, 0))],
        out_shape=[jax.ShapeDtypeStruct((t, d), F32), jax.ShapeDtypeStruct((t, d), BF16),
                   jax.ShapeDtypeStruct((1, LANES), F32)],
        compiler_params=_params(("arbitrary",)),
    )(y, target)


def _adamw(w, g, m, v, name, tr=256):
    r, c = w.shape
    tr = _fit(tr, r, 16)

    def body(w_ref, g_ref, m_ref, v_ref, d_ref, nm_ref, nv_ref):
        gg = g_ref[...]
        nm = ADAM_B1 * m_ref[...] + (1.0 - ADAM_B1) * gg
        nv = ADAM_B2 * v_ref[...] + (1.0 - ADAM_B2) * (gg * gg)
        m_hat = nm / (1.0 - ADAM_B1 ** ADAM_STEP)
        v_hat = nv / (1.0 - ADAM_B2 ** ADAM_STEP)
        d_ref[...] = -ADAM_LR * (m_hat / (jnp.sqrt(v_hat) + ADAM_EPS) + ADAM_WD * w_ref[...])
        nm_ref[...] = nm
        nv_ref[...] = nv

    blk = pl.BlockSpec((tr, c), lambda i: (i, 0))
    return pl.pallas_call(
        body, name=name, grid=(r // tr,), in_specs=[blk] * 4, out_specs=[blk] * 3,
        out_shape=[jax.ShapeDtypeStruct((r, c), F32)] * 3,
        compiler_params=_params(("parallel",)),
    )(w, g, m, v)


def _att_scores(qi, kj, scale, row0):
    z = _dot(qi, kj, NT) * scale
    lb = -_softplus(-z)
    lrm = lb - z
    if row0 is None:
        return lb, lrm, None
    mask = _iota2(z.shape, 1) < _iota2(z.shape, 0) + row0
    return lb, jnp.where(mask, lrm, 0.0), mask


def _masked(mask, v):
    return v if mask is None else jnp.where(mask, v, 0.0)


def _attention_fwd(qkv, qg, kg, name, tile=None, rider=None):
    bsz, s, w3 = qkv.shape
    hd = ATT_HEAD_DIM
    heads = w3 // (3 * hd)
    tile = min(tile or ATT_TILE, s)
    strip = min(ATT_STRIP, tile)
    nb = s // tile
    scale = hd ** -0.5

    def body(qkv_ref, qg_ref, kg_ref, o_ref, r_ref, qn_s, kn_s, vb_s, acc_s, c_s):
        qn_s[...] = _rms_fwd(qkv_ref[0, :, 0:hd], qg_ref[...]).astype(BF16)
        kn_s[...] = _rms_fwd(qkv_ref[0, :, hd:2 * hd], kg_ref[...]).astype(BF16)
        vb_s[...] = qkv_ref[0, :, 2 * hd:3 * hd].astype(BF16)
        after = _tri(LANES, lambda r, c: r > c)

        def q_loop(i, _):
            rows = pl.ds(pl.multiple_of(i * tile, tile), tile)
            acc_s[...] = jnp.zeros_like(acc_s)
            c_s[...] = jnp.zeros_like(c_s)

            def key_tile(j, diagonal):
                cols = pl.ds(pl.multiple_of(j * tile, tile), tile)
                kj, vj = kn_s[cols, :], vb_s[cols, :]
                strips = range(tile // strip)
                subs = [slice(r * strip, (r + 1) * strip) for r in strips]
                srows = [pl.ds(pl.multiple_of(i * tile + r * strip, strip), strip) for r in strips]
                sc = [_att_scores(qn_s[srows[r], :], kj, scale, r * strip if diagonal else None) for r in strips]
                later = [_scan_lanes(sc[r][1], after, 2, reverse=True) for r in strips]
                for r in strips:
                    w = _masked(sc[r][2], jnp.exp(sc[r][0] + (later[r][0] + c_s[subs[r], :])))
                    acc_s[subs[r], :] += _dot(w, vj)
                    c_s[subs[r], :] += later[r][1]

            def k_loop(jj, _):
                key_tile(i - jj, False)
                return 0

            key_tile(i, True)
            lax.fori_loop(1, i + 1, k_loop, 0)
            o_ref[0, rows, :] = acc_s[...]
            r_ref[0, 0, rows, :] = c_s[...]
            return 0

        lax.fori_loop(0, nb, q_loop, 0)

    return _call(
        body, name=name, grid=(bsz, heads),
        in_specs=[pl.BlockSpec((1, s, 3 * hd), lambda b, h: (b, 0, h)),
                  pl.BlockSpec((1, hd), lambda b, h: (0, 0)), pl.BlockSpec((1, hd), lambda b, h: (0, 0))],
        out_specs=[pl.BlockSpec((1, s, hd), lambda b, h: (b, 0, h)),
                   pl.BlockSpec((1, 1, s, 1), lambda b, h: (b, h, 0, 0))],
        out_shape=[jax.ShapeDtypeStruct((bsz, s, heads * hd), F32),
                   jax.ShapeDtypeStruct((bsz, heads, s, 1), F32)],
        scratch_shapes=[pltpu.VMEM((s, hd), BF16), pltpu.VMEM((s, hd), BF16), pltpu.VMEM((s, hd), BF16),
                        pltpu.VMEM((tile, hd), F32), pltpu.VMEM((tile, 1), F32)],
        sem=("parallel", "parallel"), args=(qkv, qg.reshape(1, hd), kg.reshape(1, hd)), rider=rider)


def _attention_bwd(qkv, qg, kg, rtot, do, name, tile=None, rider=None):
    bsz, s, w3 = qkv.shape
    hd = ATT_HEAD_DIM
    heads = w3 // (3 * hd)
    tile = min(tile or ATT_TILE, s)
    strip = min(ATT_STRIP, tile)
    nb = s // tile
    scale = hd ** -0.5

    def body(qkv_ref, qg_ref, kg_ref, r_ref, do_ref, dqkv_ref, dqg_ref, dkg_ref,
             qn_s, kn_s, vb_s, dob_s, dqn_s, dkn_s, dv_s, c1_s, c2_s, wb_s, dzb_s):
        qn_s[...] = _rms_fwd(qkv_ref[0, :, 0:hd], qg_ref[...]).astype(BF16)
        kn_s[...] = _rms_fwd(qkv_ref[0, :, hd:2 * hd], kg_ref[...]).astype(BF16)
        vb_s[...] = qkv_ref[0, :, 2 * hd:3 * hd].astype(BF16)
        dob_s[...] = do_ref[0].astype(BF16)
        dqn_s[...] = jnp.zeros_like(dqn_s)
        dkn_s[...] = jnp.zeros_like(dkn_s)
        dv_s[...] = jnp.zeros_like(dv_s)
        upto = _tri(LANES, lambda r, c: r <= c)
        before = _tri(LANES, lambda r, c: r < c)

        def q_loop(i, _):
            rows = pl.ds(pl.multiple_of(i * tile, tile), tile)
            c1_s[...] = jnp.zeros_like(c1_s)
            c2_s[...] = jnp.zeros_like(c2_s)

            def key_tile(j, diagonal):
                cols = pl.ds(pl.multiple_of(j * tile, tile), tile)
                kj, vj = kn_s[cols, :], vb_s[cols, :]
                strips = range(tile // strip)
                subs = [slice(r * strip, (r + 1) * strip) for r in strips]
                srows = [pl.ds(pl.multiple_of(i * tile + r * strip, strip), strip) for r in strips]
                sc = [_att_scores(qn_s[srows[r], :], kj, scale, r * strip if diagonal else None) for r in strips]
                dw = [_dot(dob_s[srows[r], :], vj, NT) for r in strips]
                upto_lr = [_scan_lanes(sc[r][1], upto, 2) for r in strips]
                w = [_masked(sc[r][2], jnp.exp(sc[r][0] + (r_ref[0, 0, srows[r], :] - (upto_lr[r][0] + c1_s[subs[r], :]))))
                     for r in strips]
                e = [w[r] * dw[r] for r in strips]
                pre = [_scan_lanes(e[r], before, 1) for r in strips]
                dz = [_masked(sc[r][2], (e[r] - jnp.exp(sc[r][0]) * (e[r] + (pre[r][0] + c2_s[subs[r], :]))) * scale)
                      for r in strips]
                for r in strips:
                    wb_s[subs[r], :] = w[r].astype(BF16)
                    dzb_s[subs[r], :] = dz[r].astype(BF16)
                    c1_s[subs[r], :] += upto_lr[r][1]
                    c2_s[subs[r], :] += pre[r][1]
                dqn_s[rows, :] += _dot(dzb_s[...], kj)
                dv_s[cols, :] += _dot(wb_s[...], dob_s[rows, :], TN)
                dkn_s[cols, :] += _dot(dzb_s[...], qn_s[rows, :], TN)

            def k_loop(j, _):
                key_tile(j, False)
                return 0

            lax.fori_loop(0, i, k_loop, 0)
            key_tile(i, True)
            return 0

        lax.fori_loop(0, nb, q_loop, 0)
        dq, dgq = _rms_bwd(qkv_ref[0, :, 0:hd], qg_ref[...], dqn_s[...])
        dk, dgk = _rms_bwd(qkv_ref[0, :, hd:2 * hd], kg_ref[...], dkn_s[...])
        dqkv_ref[0, :, 0:hd] = dq.astype(BF16)
        dqkv_ref[0, :, hd:2 * hd] = dk.astype(BF16)
        dqkv_ref[0, :, 2 * hd:3 * hd] = dv_s[...].astype(BF16)
        dqg_ref[0, 0] = jnp.sum(dgq, axis=0, keepdims=True)
        dkg_ref[0, 0] = jnp.sum(dgk, axis=0, keepdims=True)

    gain = pl.BlockSpec((1, hd), lambda b, h: (0, 0))
    dgain = pl.BlockSpec((1, 1, 1, hd), lambda b, h: (b, h, 0, 0))
    return _call(
        body, name=name, grid=(bsz, heads),
        in_specs=[pl.BlockSpec((1, s, 3 * hd), lambda b, h: (b, 0, h)), gain, gain,
                  pl.BlockSpec((1, 1, s, 1), lambda b, h: (b, h, 0, 0)),
                  pl.BlockSpec((1, s, hd), lambda b, h: (b, 0, h))],
        out_specs=[pl.BlockSpec((1, s, 3 * hd), lambda b, h: (b, 0, h)), dgain, dgain],
        out_shape=[jax.ShapeDtypeStruct((bsz, s, w3), BF16),
                   jax.ShapeDtypeStruct((bsz, heads, 1, hd), F32),
                   jax.ShapeDtypeStruct((bsz, heads, 1, hd), F32)],
        scratch_shapes=[pltpu.VMEM((s, hd), BF16)] * 4 + [pltpu.VMEM((s, hd), F32)] * 3
        + [pltpu.VMEM((tile, 1), F32)] * 2 + [pltpu.VMEM((tile, tile), BF16)] * 2,
        sem=("parallel", "parallel"), args=(qkv, qg.reshape(1, hd), kg.reshape(1, hd), rtot, do), rider=rider)


def _conv_pre(pad_ref, w_ref, b_ref, s):
    pre = b_ref[...]
    for i in range(CONV_WIDTH):
        off = SUBLANES - (CONV_WIDTH - 1) + i
        pre = pre + pad_ref[off:off + s, :] * w_ref[i:i + 1, :]
    return pre


def _conv_fwd(u, w, b, name, tc=256):
    bsz, s, c = u.shape
    tc = min(tc, c)

    def body(u_ref, w_ref, b_ref, a_ref, pad_s):
        pad_s[0:SUBLANES, :] = jnp.zeros((SUBLANES, tc), F32)
        pad_s[SUBLANES:SUBLANES + s, :] = u_ref[0]
        pre = _conv_pre(pad_s, w_ref, b_ref, s)
        a_ref[0] = pre * _sigmoid(pre)

    return pl.pallas_call(
        body, name=name, grid=(bsz, c // tc),
        in_specs=[pl.BlockSpec((1, s, tc), lambda i, j: (i, 0, j)),
                  pl.BlockSpec((CONV_WIDTH, tc), lambda i, j: (0, j)), pl.BlockSpec((1, tc), lambda i, j: (0, j))],
        out_specs=pl.BlockSpec((1, s, tc), lambda i, j: (i, 0, j)),
        out_shape=jax.ShapeDtypeStruct((bsz, s, c), F32),
        scratch_shapes=[pltpu.VMEM((s + SUBLANES, tc), F32)],
        compiler_params=_params(("parallel", "parallel")),
    )(u, w, b.reshape(1, c))


def _conv_bwd(u, w, b, da, name, tc=256):
    bsz, s, c = u.shape
    tc = min(tc, c)

    def body(u_ref, w_ref, b_ref, da_ref, du_ref, dw_ref, db_ref, pad_s, gpad_s):
        @pl.when(pl.program_id(1) == 0)
        def _():
            dw_ref[...] = jnp.zeros_like(dw_ref)
            db_ref[...] = jnp.zeros_like(db_ref)

        pad_s[0:SUBLANES, :] = jnp.zeros((SUBLANES, tc), F32)
        pad_s[SUBLANES:SUBLANES + s, :] = u_ref[0]
        pre = _conv_pre(pad_s, w_ref, b_ref, s)
        sg = _sigmoid(pre)
        dpre = da_ref[0] * (sg * (1.0 + pre * (1.0 - sg)))
        gpad_s[0:s, :] = dpre
        gpad_s[s:s + SUBLANES, :] = jnp.zeros((SUBLANES, tc), F32)
        du = jnp.zeros((s, tc), F32)
        for i in range(CONV_WIDTH):
            back = CONV_WIDTH - 1 - i
            du = du + gpad_s[back:back + s, :] * w_ref[i:i + 1, :]
            off = SUBLANES - (CONV_WIDTH - 1) + i
            dw_ref[i:i + 1, :] += jnp.sum(dpre * pad_s[off:off + s, :], axis=0, keepdims=True)
        du_ref[0] = du.astype(BF16)
        db_ref[...] += jnp.sum(dpre, axis=0, keepdims=True)

    blk = pl.BlockSpec((1, s, tc), lambda j, i: (i, 0, j))
    du, dw, db = pl.pallas_call(
        body, name=name, grid=(c // tc, bsz),
        in_specs=[blk, pl.BlockSpec((CONV_WIDTH, tc), lambda j, i: (0, j)),
                  pl.BlockSpec((1, tc), lambda j, i: (0, j)), blk],
        out_specs=[blk, pl.BlockSpec((CONV_WIDTH, tc), lambda j, i: (0, j)),
                   pl.BlockSpec((1, tc), lambda j, i: (0, j))],
        out_shape=[jax.ShapeDtypeStruct((bsz, s, c), BF16), jax.ShapeDtypeStruct((CONV_WIDTH, c), F32),
                   jax.ShapeDtypeStruct((1, c), F32)],
        scratch_shapes=[pltpu.VMEM((s + SUBLANES, tc), F32), pltpu.VMEM((s + SUBLANES, tc), F32)],
        compiler_params=_params(("parallel", "arbitrary")),
    )(u, w, b.reshape(1, c), da)
    return du, dw, db.reshape(c)


def _ssd_chunk_common(b_ref, c_ref, dt_ref, dtb_ref, alog_ref):
    bm, cm = b_ref[0], c_ref[0]
    draw = dt_ref[0] + dtb_ref[...]
    dt = _softplus(draw)
    a_row = -jnp.exp(alog_ref[...])
    da = dt * a_row
    n = SSD_CHUNK
    acum = _dot_exact(da, _tri(n, lambda r, c: r >= c), ones_left=True)
    acum_t = _dot_exact(da, _tri(n, lambda r, c: r <= c), dims=TN)
    cb = _dot(cm, bm, NT)
    return bm, cm, draw, dt, a_row, acum, acum_t, cb


def _row_totals(v):
    return _dot_exact(v, jnp.ones((v.shape[1], LANES), BF16), passes=2)


def _ssd_head_common(acum, acum_t, dt, cb, x, i):
    n, p = SSD_CHUNK, SSM_HEAD_DIM
    pick = (_iota2((LANES, LANES), 0) == i).astype(BF16)
    acol = _dot_exact(acum, pick)
    dtc = _dot_exact(dt, pick)[:, :p]
    arow = acum_t[i:i + 1, :]
    causal = _iota2((n, n), 0) >= _iota2((n, n), 1)
    lm = jnp.where(causal, jnp.exp(jnp.where(causal, acol - arow, 0.0)), 0.0)
    gm = cb * lm
    xh = x[:, i * p:(i + 1) * p]
    xdt = xh * dtc
    alast = acol[n - 1:n, :]
    dte = jnp.exp(alast - acol)
    return acol, lm, gm, dtc, xh, xdt, alast, dte


def _ssd_specs(s, wg, hg, rev):
    g, n, cl = SSM_GROUPS, SSM_STATE, SSD_CHUNK
    nc = s // cl
    boff, coff = (g * wg) // n, (g * wg) // n + g
    ci = (lambda c: nc - 1 - c) if rev else (lambda c: c)
    xblk = pl.BlockSpec((1, cl, wg), lambda b, k, c: (b, ci(c), k))
    bblk = pl.BlockSpec((1, cl, n), lambda b, k, c: (b, ci(c), boff + k))
    cblk = pl.BlockSpec((1, cl, n), lambda b, k, c: (b, ci(c), coff + k))
    nblk = pl.BlockSpec((1, cl, n), lambda b, k, c: (b, ci(c), k))
    dtblk = pl.BlockSpec((1, cl, LANES), lambda b, k, c: (b, ci(c), k))
    vec = pl.BlockSpec((1, LANES), lambda b, k, c: (0, k))
    hsblk = pl.BlockSpec((1, 1, 1, wg, n), lambda b, k, c: (b, k, ci(c), 0, 0))
    return nc, xblk, bblk, cblk, nblk, dtblk, vec, hsblk


def _ssd_fwd(xbc, dtraw, dtb, alog, dskip, hg, name, rider=None):
    bsz, s, _ = xbc.shape
    g, n, p = SSM_GROUPS, SSM_STATE, SSM_HEAD_DIM
    wg = hg * p
    nc, xblk, bblk, cblk, _, dtblk, vec, hsblk = _ssd_specs(s, wg, hg, False)

    def body(x_ref, b_ref, c_ref, dt_ref, dtb_ref, alog_ref, dsk_ref, y_ref, hs_ref, h_s):
        @pl.when(pl.program_id(2) == 0)
        def _():
            h_s[...] = jnp.zeros_like(h_s)

        bm, cm, _, dt, _, acum, acum_t, cb = _ssd_chunk_common(b_ref, c_ref, dt_ref, dtb_ref, alog_ref)
        x = x_ref[0]
        hs_ref[0, 0, 0] = h_s[...]
        hd_ = range(hg)
        hc = [_ssd_head_common(acum, acum_t, dt, cb, x, i) for i in hd_]
        hprev = [h_s[i * p:(i + 1) * p, :] for i in hd_]
        ydiag = [_dot(hc[i][2], hc[i][5]) for i in hd_]
        yoff = [_dot(cm, hprev[i], NT) for i in hd_]
        st = [_dot(hc[i][5] * hc[i][7][:, :p], bm, TN) for i in hd_]
        for i in hd_:
            acol, _, _, _, xh, _, alast, _ = hc[i]
            y_ref[0, :, i * p:(i + 1) * p] = ydiag[i] + yoff[i] * jnp.exp(acol[:, :p]) + xh * dsk_ref[:, i:i + 1]
            h_s[i * p:(i + 1) * p, :] = hprev[i] * jnp.exp(alast) + st[i]

    return _call(
        body, name=name, grid=(bsz, g, nc),
        in_specs=[xblk, bblk, cblk, dtblk, vec, vec, vec],
        out_specs=[xblk, hsblk],
        out_shape=[jax.ShapeDtypeStruct((bsz, s, g * wg), F32),
                   jax.ShapeDtypeStruct((bsz, g, nc, wg, n), F32)],
        scratch_shapes=[pltpu.VMEM((wg, n), F32)],
        sem=("parallel", "parallel", "arbitrary"), args=(xbc, xbc, xbc, dtraw, dtb, alog, dskip), rider=rider)


def _ssd_bwd(xbc, dtraw, dtb, alog, dskip, hs, dy, hg, name, rider=None):
    bsz, s, _ = xbc.shape
    g, n, p, cl = SSM_GROUPS, SSM_STATE, SSM_HEAD_DIM, SSD_CHUNK
    wg = hg * p
    nc, xblk, bblk, cblk, nblk, dtblk, vec, hsblk = _ssd_specs(s, wg, hg, True)

    def body(x_ref, b_ref, c_ref, dt_ref, dtb_ref, alog_ref, dsk_ref, hs_ref, dy_ref,
             dx_ref, db_ref, dc_ref, ddt_ref, dvec_ref, dh_s):
        @pl.when(pl.program_id(2) == 0)
        def _():
            dh_s[...] = jnp.zeros_like(dh_s)
            dvec_ref[...] = jnp.zeros_like(dvec_ref)

        lane = _iota2((cl, LANES), 1)
        sub = _iota2((LANES, cl), 0)
        lane1 = _iota2((1, LANES), 1)
        last_row = _iota2((cl, 1), 0) == cl - 1
        bm, cm, draw, dt, a_row, acum, acum_t, cb = _ssd_chunk_common(b_ref, c_ref, dt_ref, dtb_ref, alog_ref)
        x = x_ref[0]
        dyc = dy_ref[0]
        hd_ = range(hg)
        hc = [_ssd_head_common(acum, acum_t, dt, cb, x, i) for i in hd_]
        dyh = [dyc[:, i * p:(i + 1) * p] for i in hd_]
        hprev = [hs_ref[0, 0, 0, i * p:(i + 1) * p, :] for i in hd_]
        dhn = [dh_s[i * p:(i + 1) * p, :] for i in hd_]
        ea = [jnp.exp(hc[i][0]) for i in hd_]
        cd = [jnp.exp(hc[i][6]) for i in hd_]
        y0 = [_dot(cm, hprev[i], NT) for i in hd_]
        dxe = [_dot(bm, dhn[i], NT) for i in hd_]
        dgm = [_dot(dyh[i], hc[i][5], NT) for i in hd_]
        gdy = [_dot(hc[i][2], dyh[i], TN) for i in hd_]
        dy0 = [dyh[i] * ea[i][:, :p] for i in hd_]
        dcm_h = [_dot(dy0[i], hprev[i]) for i in hd_]
        dh_new = [_dot(dy0[i], cm, TN) + dhn[i] * cd[i] for i in hd_]
        dbm_h = [_dot(hc[i][5] * hc[i][7][:, :p], dhn[i]) for i in hd_]
        ws = [dgm[i] * hc[i][2] for i in hd_]
        dxdt = [dxe[i] * hc[i][7][:, :p] + gdy[i] for i in hd_]
        s_y0 = [_row_totals(dyh[i] * y0[i]) for i in hd_]
        s_xe = [_row_totals(dxe[i] * hc[i][5]) for i in hd_]
        s_ws = [_row_totals(ws[i]) for i in hd_]
        s_dt = [_row_totals(dxdt[i] * hc[i][4]) for i in hd_]
        s_dd = [_row_totals(dyh[i] * hc[i][4]) for i in hd_]
        s_hh = [_row_totals(dhn[i] * hprev[i]) for i in hd_]
        dcb = jnp.zeros((cl, cl), F32)
        dcm = jnp.zeros((cl, n), F32)
        dbm = jnp.zeros((cl, n), F32)
        da_col = jnp.zeros((cl, LANES), F32)
        da_row = jnp.zeros((LANES, cl), F32)
        ddt = jnp.zeros((cl, LANES), F32)
        dd = jnp.zeros((1, LANES), F32)
        for i in hd_:
            _, lm, _, dtc, _, _, _, dte = hc[i]
            dh_s[i * p:(i + 1) * p, :] = dh_new[i]
            dd = dd + jnp.where(lane1 == i, jnp.sum(s_dd[i], axis=0, keepdims=True), 0.0)
            t1 = s_xe[i] * dte
            d_alast = jnp.sum(s_hh[i], axis=0, keepdims=True) * cd[i] + jnp.sum(t1, axis=0, keepdims=True)
            dacol = s_y0[i] * ea[i] - t1 + s_ws[i] + jnp.where(last_row, d_alast, 0.0)
            dcb = dcb + dgm[i] * lm
            dcm = dcm + dcm_h[i]
            dbm = dbm + dbm_h[i]
            dx_ref[0, :, i * p:(i + 1) * p] = dxdt[i] * dtc + dyh[i] * dsk_ref[:, i:i + 1]
            da_col = jnp.where(lane == i, dacol, da_col)
            da_row = jnp.where(sub == i, -jnp.sum(ws[i], axis=0, keepdims=True), da_row)
            ddt = jnp.where(lane == i, s_dt[i], ddt)
        dc_ref[0] = dcm + _dot(dcb, bm)
        db_ref[0] = dbm + _dot(dcb, cm, TN)
        upper = _tri(cl, lambda r, k: r <= k)
        dda = _dot_exact(da_col, upper, ones_left=True) + _dot_exact(da_row, upper, dims=NT, ones_left=True)
        ddt = ddt + dda * a_row
        ddraw = ddt * _sigmoid(draw)
        ddt_ref[0] = ddraw.astype(BF16)
        dvec_ref[0, 0, 0:1, :] += jnp.sum(ddraw, axis=0, keepdims=True)
        dvec_ref[0, 0, 1:2, :] += jnp.sum(dda * dt, axis=0, keepdims=True) * a_row
        dvec_ref[0, 0, 2:3, :] += dd

    return _call(
        body, name=name, grid=(bsz, g, nc),
        in_specs=[xblk, bblk, cblk, dtblk, vec, vec, vec, hsblk, xblk],
        out_specs=[xblk, nblk, nblk, dtblk,
                   pl.BlockSpec((1, 1, SUBLANES, LANES), lambda b, k, c: (b, k, 0, 0))],
        out_shape=[jax.ShapeDtypeStruct((bsz, s, g * wg), F32), jax.ShapeDtypeStruct((bsz, s, g * n), F32),
                   jax.ShapeDtypeStruct((bsz, s, g * n), F32), jax.ShapeDtypeStruct((bsz, s, g * LANES), BF16),
                   jax.ShapeDtypeStruct((bsz, g, SUBLANES, LANES), F32)],
        scratch_shapes=[pltpu.VMEM((wg, n), F32)],
        sem=("parallel", "parallel", "arbitrary"), args=(xbc, xbc, xbc, dtraw, dtb, alog, dskip, hs, dy),
        rider=rider)


def _coords():
    return lax.axis_index("x"), lax.axis_index("y"), lax.axis_index("c")


def _other_chips(x, y):
    return [(1 - x, y), (x, 1 - y), (1 - x, 1 - y)]


def _remote(src, dst, send_sems, recv_sems, k, to):
    return pltpu.make_async_remote_copy(src_ref=src, dst_ref=dst, send_sem=send_sems.at[k],
                                        recv_sem=recv_sems.at[k], device_id=to, device_id_type=MESH)


def _standalone(rider, name):
    r_in, r_out, n_sems = len(rider["ins"]), len(rider["outs"]), rider["n_sems"]

    def body(*refs):
        rins, routs, (send_sems, recv_sems) = refs[:r_in], refs[r_in:r_in + r_out], refs[r_in + r_out:]
        cps = rider["copies"](rins, routs, send_sems, recv_sems)
        for cp in cps:
            cp.start()
        for cp in cps:
            cp.wait()

    res = pl.pallas_call(
        body, name=name, in_specs=[ANY] * r_in, out_specs=[ANY] * r_out, out_shape=list(rider["outs"]),
        scratch_shapes=[pltpu.SemaphoreType.DMA((n_sems,)), pltpu.SemaphoreType.DMA((n_sems,))],
        input_output_aliases=dict(rider["aliases"]),
    )(*rider["ins"])
    return list(res)


def _rows_of(shape):
    return shape[1] if len(shape) == 3 else shape[0]


def _slot(ref, j, rows):
    if len(ref.shape) == 3:
        return ref.at[j, rows]
    c = ref.shape[1] // N_CHIPS
    return ref.at[rows, pl.ds(pl.multiple_of(j * c, LANES), c)]


def _own_slot_set(shard, side_by_side):
    if side_by_side:
        return jnp.tile(shard, (1, N_CHIPS))
    return jnp.broadcast_to(shard[None], (N_CHIPS,) + shard.shape)


def _gather_chips_rider(shards, side_by_side):
    def copies(rins, routs, send_sems, recv_sems):
        x, y, c = _coords()
        me = 2 * x + y
        cps = []
        for q, (w_ref, o_ref) in enumerate(zip(rins[:len(shards)], routs, strict=True)):
            rh = w_ref.shape[0] // 2
            rows = pl.ds(c * rh, rh)
            for k, (px, py) in enumerate(_other_chips(x, y)):
                cps.append(_remote(w_ref.at[rows], _slot(o_ref, me, rows), send_sems, recv_sems, 3 * q + k,
                                   (px, py, c)))
        return cps

    bases = [_own_slot_set(w, side) for w, side in zip(shards, side_by_side, strict=True)]
    return dict(ins=list(shards) + bases, outs=[jax.ShapeDtypeStruct(b.shape, b.dtype) for b in bases],
                aliases={len(shards) + i: i for i in range(len(shards))}, n_sems=3 * len(shards), copies=copies)


def _gather_pair_rider(gathered):
    def copies(rins, routs, send_sems, recv_sems):
        x, y, c = _coords()
        cps = []
        for q, o_ref in enumerate(routs):
            rh = _rows_of(o_ref.shape) // 2
            for k, (px, py) in enumerate(_other_chips(x, y)):
                part = _slot(o_ref, 2 * px + py, pl.ds(c * rh, rh))
                cps.append(_remote(part, part, send_sems, recv_sems, 3 * q + k, (x, y, 1 - c)))
        return cps

    return dict(ins=list(gathered), outs=[jax.ShapeDtypeStruct(g.shape, g.dtype) for g in gathered],
                aliases={i: i for i in range(len(gathered))}, n_sems=3 * len(gathered), copies=copies)


def _reduce_pair_rider(grads):
    def copies(rins, routs, send_sems, recv_sems):
        x, y, c = _coords()
        cps, k = [], 0
        for g_ref, r_ref in zip(rins, routs, strict=True):
            rh = _rows_of(g_ref.shape) // 2
            rows = pl.ds((1 - c) * rh, rh)
            if len(g_ref.shape) == 3:
                cps.append(_remote(g_ref.at[:, rows], r_ref, send_sems, recv_sems, k, (x, y, 1 - c)))
                k += 1
            else:
                for j in range(N_CHIPS):
                    cps.append(_remote(_slot(g_ref, j, rows), r_ref.at[j], send_sems, recv_sems, k, (x, y, 1 - c)))
                    k += 1
        return cps

    def out_of(g):
        r, c = (g.shape[1], g.shape[2]) if g.ndim == 3 else (g.shape[0], g.shape[1] // N_CHIPS)
        return jax.ShapeDtypeStruct((N_CHIPS, r // 2, c), g.dtype)

    return dict(ins=list(grads), outs=[out_of(g) for g in grads], aliases={},
                n_sems=sum(1 if g.ndim == 3 else N_CHIPS for g in grads), copies=copies)


def _reduce_chips_rider(pair_sums):
    def copies(rins, routs, send_sems, recv_sems):
        x, y, c = _coords()
        cps = []
        for q, (p_ref, r_ref) in enumerate(zip(rins, routs, strict=True)):
            for k, (px, py) in enumerate(_other_chips(x, y)):
                cps.append(_remote(p_ref.at[2 * px + py], r_ref.at[k], send_sems, recv_sems, 3 * q + k, (px, py, c)))
        return cps

    return dict(ins=list(pair_sums), outs=[jax.ShapeDtypeStruct((3,) + p.shape[1:], p.dtype) for p in pair_sums],
                aliases={}, n_sems=3 * len(pair_sums), copies=copies)


def _reduce_finish_rider(sums, totals, layers, depth):
    def copies(rins, routs, send_sems, recv_sems):
        x, y, c = _coords()
        cps = []
        for q, (f_ref, o_ref) in enumerate(zip(rins[:len(sums)], routs, strict=True)):
            rh = f_ref.shape[0]
            cps.append(_remote(f_ref, o_ref.at[layers[q], pl.ds(c * rh, rh)], send_sems, recv_sems, q, (x, y, 1 - c)))
        return cps

    kept = [q for q, t in enumerate(totals) if t is not None]
    outs = [jax.ShapeDtypeStruct((depth, 2 * f.shape[0], f.shape[1]), F32) for f in sums]
    return dict(ins=list(sums) + [totals[q] for q in kept], outs=outs,
                aliases={len(sums) + k: q for k, q in enumerate(kept)}, n_sems=len(sums), copies=copies)


def _pair_add(gj, r1, ids, name, tr=256):
    stacked = gj.ndim == 3
    nj, rh, c = r1.shape
    tr = _fit(tr, rh, 16)
    nt = rh // tr

    def body(c_ref, chip_ref, g_ref, r_ref, p_ref, pb_ref):
        s = (g_ref[0] if stacked else g_ref[...]) + r_ref[0]
        pb_ref[0] = s.astype(BF16)

        @pl.when(pl.program_id(1) == chip_ref[0])
        def _():
            p_ref[...] = s

    blk_r = pl.BlockSpec((1, tr, c), lambda i, j, cr, jr: (j, i, 0))
    blk_g = (pl.BlockSpec((1, tr, c), lambda i, j, cr, jr: (j, cr[0] * nt + i, 0)) if stacked
             else pl.BlockSpec((tr, c), lambda i, j, cr, jr: (cr[0] * nt + i, j)))
    return pl.pallas_call(
        body, name=name,
        grid_spec=pltpu.PrefetchScalarGridSpec(
            num_scalar_prefetch=2, grid=(nt, nj), in_specs=[blk_g, blk_r],
            out_specs=[pl.BlockSpec((tr, c), lambda i, j, cr, jr: (i, 0)), blk_r]),
        out_shape=[jax.ShapeDtypeStruct((rh, c), F32), jax.ShapeDtypeStruct((nj, rh, c), BF16)],
        compiler_params=_params(("parallel", "arbitrary")),
    )(*ids, gj, r1)


def _chip_add(p, r2, name, tr=256):
    rh, c = p.shape
    tr = _fit(tr, rh, 16)

    def body(o_ref, r_ref, f_ref):
        f_ref[...] = ((o_ref[...] + r_ref[0].astype(F32)) + r_ref[1].astype(F32)) + r_ref[2].astype(F32)

    blk = pl.BlockSpec((tr, c), lambda i: (i, 0))
    return pl.pallas_call(
        body, name=name, grid=(rh // tr,),
        in_specs=[blk, pl.BlockSpec((3, tr, c), lambda i: (0, i, 0))], out_specs=blk,
        out_shape=jax.ShapeDtypeStruct((rh, c), F32),
        compiler_params=_params(("parallel",)),
    )(p, r2)


BIG = ["w_in", "w_out", "w_gate", "w_up", "w_down"]
LATE = ["w_out", "w_gate", "w_up", "w_down"]
FFN = ["w_gate", "w_up", "w_down"]
MIX = ["w_in", "w_out"]


SIDE_BY_SIDE = ("w_gate", "w_up")


class _GatherPlan:
    def __init__(self, late, next_in):
        self.late, self.next_in, self.parts = late, next_in, {}

    def rider(self, host):
        if self.late is None:
            return None
        if host == "attention_fwd":
            return _gather_chips_rider([self.late[n] for n in LATE], [n in SIDE_BY_SIDE for n in LATE])
        if host == "ssd_fwd":
            return _gather_pair_rider([self.parts[n] for n in LATE])
        if self.next_in is None:
            return None
        return (_gather_chips_rider([self.next_in], [False]) if host == "ffn_gate_up"
                else _gather_pair_rider([self.parts["w_in"]]))

    def collect(self, host, outs):
        if outs:
            self.parts.update(zip(LATE if host in ("attention_fwd", "ssd_fwd") else ["w_in"], outs, strict=True))

    def late_gathered(self):
        return {n: self.parts[n] for n in LATE}

    def next_gathered(self):
        return self.parts["w_in"]


def _gather_now(shard, tag):
    part = _standalone(_gather_chips_rider([shard], [False]), f"allgather_chips_{tag}")
    return _standalone(_gather_pair_rider(part), f"allgather_pair_{tag}")[0]


class _ReducePlan:
    PAIR = {"ffn_down_dgrad": "mix", "ffn_gate_up_dgrad": "ffn"}
    CHIPS = {"ssd_bwd": "mix", "attention_bwd": "ffn"}

    def __init__(self, mix, mix_layer, ffn_layer, totals, depth, ids):
        self.groups = {} if mix is None else {"mix": (MIX, mix, mix_layer)}
        self.ffn_layer, self.totals, self.depth, self.ids = ffn_layer, dict(totals), depth, ids
        self.p, self.pb, self.f = {}, {}, {}

    def add_ffn(self, grads):
        if self.ids is not None:
            self.groups["ffn"] = (FFN, grads, self.ffn_layer)

    def _present(self):
        return [(n, layer) for names, _, layer in self.groups.values() for n in names if n in self.f]

    def rider(self, host):
        if host == "proj_in_dgrad":
            done = self._present()
            if not done:
                return None
            return _reduce_finish_rider([self.f[n] for n, _ in done], [self.totals.get(n) for n, _ in done],
                                        [layer for _, layer in done], self.depth)
        group = self.groups.get(self.PAIR.get(host) or self.CHIPS.get(host))
        if group is None:
            return None
        names, grads, _ = group
        return (_reduce_pair_rider([grads[n] for n in names]) if host in self.PAIR
                else _reduce_chips_rider([self.pb[n] for n in names]))

    def collect(self, host, outs):
        if not outs:
            return
        if host == "proj_in_dgrad":
            c = lax.axis_index("c")
            for (n, layer), t in zip(self._present(), outs, strict=True):
                self.totals[n] = lax.dynamic_update_slice(t, self.f[n][None], (layer, c * self.f[n].shape[0], 0))
            return
        names, grads, layer = self.groups[self.PAIR.get(host) or self.CHIPS.get(host)]
        for n, got in zip(names, outs, strict=True):
            if host in self.PAIR:
                self.p[n], self.pb[n] = _pair_add(grads[n], got, self.ids, f"rs_pair_add_{n}_layer{layer}")
            else:
                self.f[n] = _chip_add(self.p[n], got, f"rs_chip_add_{n}_layer{layer}")

    def run_now(self, tag):
        self.collect("ffn_down_dgrad", _standalone(self.rider("ffn_down_dgrad"), f"rs_pair_{tag}"))
        self.collect("ssd_bwd", _standalone(self.rider("ssd_bwd"), f"rs_chips_{tag}"))
        self.collect("proj_in_dgrad", _standalone(self.rider("proj_in_dgrad"), f"rs_finish_{tag}"))
        return self.totals


def _small_exchange(v, name, reduce):
    rows = v.shape[0]

    def body(v_ref, o_ref, *rest):
        buf = rest[0] if reduce else o_ref
        send_sems, recv_sems = rest[-2], rest[-1]
        x, y, c = _coords()
        me = 4 * x + 2 * y + c
        buf[me] = v_ref[...]
        cps = []
        for r in range(1, N_DEV):
            peer = (lax.bitwise_xor(x, (r >> 2) & 1), lax.bitwise_xor(y, (r >> 1) & 1), lax.bitwise_xor(c, r & 1))
            cps.append(_remote(v_ref, buf.at[me], send_sems, recv_sems, r - 1, peer))
        for cp in cps:
            cp.start()
        for r in range(1, N_DEV):
            src = buf.at[lax.bitwise_xor(me, r)]
            _remote(src, src, send_sems, recv_sems, r - 1, (x, y, c)).wait_recv()
        for cp in cps:
            cp.wait_send()
        if reduce:
            acc = buf[0]
            for d in range(1, N_DEV):
                acc = acc + buf[d]
            o_ref[...] = acc
            o_ref[0:1, :] = jnp.broadcast_to(jnp.sum(acc[0:1, :], axis=1, keepdims=True), (1, LANES))

    scratch = [pltpu.SemaphoreType.DMA((N_DEV - 1,)), pltpu.SemaphoreType.DMA((N_DEV - 1,))]
    if reduce:
        scratch = [pltpu.VMEM((N_DEV, rows, LANES), F32)] + scratch
    out_shape = (rows, LANES) if reduce else (N_DEV, rows, LANES)
    return pl.pallas_call(
        body, name=name, in_specs=[VMEM], out_specs=VMEM,
        out_shape=jax.ShapeDtypeStruct(out_shape, F32), scratch_shapes=scratch,
    )(v)


def _pack(parts):
    flat = []
    for a in parts:
        a = a.reshape(-1)
        flat.append(jnp.pad(a, (0, (-a.shape[0]) % LANES)))
    v = jnp.concatenate(flat)
    v = jnp.pad(v, (0, (-v.shape[0]) % (SUBLANES * LANES)))
    return v.reshape(-1, LANES)


def _unpack(slab, shapes):
    flat = slab.reshape(-1)
    out, off = [], 0
    for shp in shapes:
        size = 1
        for d in shp:
            size *= d
        out.append(flat[off:off + size].reshape(shp))
        off += size + (-size) % LANES
    return out


def _group_slots(a, hg):
    lead = a.shape[:-1]
    a = a.reshape(lead + (SSM_GROUPS, hg))
    a = jnp.pad(a, [(0, 0)] * len(lead) + [(0, 0), (0, LANES - hg)])
    return a.reshape(lead + (SSM_GROUPS * LANES,))


def _ungroup_slots(a, hg):
    lead = a.shape[:-1]
    return a.reshape(lead + (SSM_GROUPS, LANES))[..., :hg].reshape(lead + (SSM_GROUPS * hg,))


def _layer_fwd(x, p, bsz, s, plan, late_params=None):
    t, d = x.shape
    aw, sw, cd, hg = p["aw"], p["sw"], p["cd"], p["hg"]
    h = _rmsnorm_fwd(x, p["norm_mix"], "norm_mix_fwd")
    qkv = _mm(h, p["wqkv"], "nn", t, 3 * aw, F32, "proj_qkv", tm=1024, tn=512)
    z = _mm(h, p["wz"], "nn", t, sw, F32, "proj_z", tm=1024, tn=512)
    xbc = _mm(h, p["wxbc"], "nn", t, cd, F32, "proj_xbc", tm=1024, tn=512)
    dtraw = _mm(h, p["wdt"], "nn", t, SSM_GROUPS * LANES, F32, "proj_dt", tm=1024, tn=SSM_GROUPS * LANES)
    qkv3 = qkv.reshape(bsz, s, 3 * aw)
    (o_att, rtot), sent = _attention_fwd(qkv3, p["q_gain"], p["k_gain"], "attention_fwd",
                                         rider=plan.rider("attention_fwd"))
    plan.collect("attention_fwd", sent)
    xbc3 = xbc.reshape(bsz, s, cd)
    xact = _conv_fwd(xbc3, p["conv_w"], p["conv_b"], "conv_fwd")
    dt3 = dtraw.reshape(bsz, s, SSM_GROUPS * LANES)
    (y, hs), sent = _ssd_fwd(xact, dt3, p["dt_bias"], p["a_log"], p["d_skip"], hg, "ssd_fwd",
                             rider=plan.rider("ssd_fwd"))
    plan.collect("ssd_fwd", sent)
    if plan.late is not None:
        p = {**p, **late_params(plan.late_gathered())}
    dff = p["wg"].shape[1]
    o2, y2 = o_att.reshape(t, aw), y.reshape(t, sw)
    mix = _merge_fwd(o2, y2, z, p["attn_out_gain"], p["ssm_out_gain"], "merge_fwd")
    x1 = _mm(mix, p["wout"], "nn", t, d, F32, "proj_out", tm=1024, tn=512, res=x)
    h2 = _rmsnorm_fwd(x1, p["norm_ffn"], "norm_ffn_fwd")
    (gate, up, act), _ = _hosted_matmul(
        [[(h2, p["wg"], "nn")], [(h2, p["wu"], "nn")]], [], _swiglu_fwd_epilogue,
        [F32, F32, BF16], t, dff, 1024, 512, "ffn_gate_up", plan)
    (x2,), _ = _hosted_matmul([[(act, p["wd"], "nn")]], [x1], lambda accs, ex: (accs[0] + ex[0],),
                              [F32], t, d, 512, 512, "ffn_down", plan)
    saved = dict(x=x, h=h, qkv3=qkv3, z=z, xbc3=xbc3, dt3=dt3, o2=o2, rtot=rtot, xact=xact, hs=hs, y2=y2,
                 mix=mix, x1=x1, h2=h2, gate=gate, up=up, act=act)
    return x2, saved, p


def _hosted_matmul(groups, extras, epilogue, out_dtypes, m, n, tm, tn, name, plan):
    rider = plan.rider(name)
    if rider is None:
        return _matmul(groups, extras, epilogue, out_dtypes, m, n, tm, tn, name), []
    outs, sent = _matmul(groups, extras, epilogue, out_dtypes, m, n, tm, tn, name, rider=rider)
    plan.collect(name, sent)
    return outs, sent


def _layer_bwd(dx2, dx2b, p, sv, bsz, s, plan, ffn_to_chips=None):
    t, d = dx2.shape
    aw, sw, cd, hg = p["aw"], p["sw"], p["cd"], p["hg"]
    dff = p["wg"].shape[1]
    gr = {}
    (dgate, dup), _ = _hosted_matmul([[(dx2b, p["wd"], "nt")]], [sv["gate"], sv["up"]], _swiglu_bwd_epilogue,
                                     [BF16, BF16], t, dff, 1024, 512, "ffn_down_dgrad", plan)
    gr["wd"] = _mm(sv["act"], dx2b, "tn", dff, d, F32, "ffn_down_wgrad")
    gr["wg"] = _mm(sv["h2"], dgate, "tn", d, dff, F32, "ffn_gate_wgrad")
    gr["wu"] = _mm(sv["h2"], dup, "tn", d, dff, F32, "ffn_up_wgrad")
    if ffn_to_chips is not None:
        plan.add_ffn(ffn_to_chips(gr))
    (dh2,), _ = _hosted_matmul([[(dgate, p["wg"], "nt"), (dup, p["wu"], "nt")]], [], lambda accs, ex: (accs[0],),
                               [F32], t, d, 512, 256, "ffn_gate_up_dgrad", plan)
    dx1, dx1b, gr["norm_ffn"] = _rmsnorm_bwd(sv["x1"], p["norm_ffn"], dh2, dx2, "norm_ffn_bwd")
    dmix = _mm(dx1b, p["wout"], "nt", t, aw + sw, F32, "proj_out_dgrad", tm=1024)
    gr["wout"] = _mm(sv["mix"], dx1b, "tn", aw + sw, d, F32, "proj_out_wgrad")
    do, dy, dz, gr["attn_out_gain"], gr["ssm_out_gain"] = _merge_bwd(
        sv["o2"], sv["y2"], sv["z"], p["attn_out_gain"], p["ssm_out_gain"], dmix, "merge_bwd")
    (dxs, dbm, dcm, ddt, dvec), sent = _ssd_bwd(sv["xact"], sv["dt3"], p["dt_bias"], p["a_log"], p["d_skip"],
                                                sv["hs"], dy.reshape(bsz, s, sw), hg, "ssd_bwd",
                                                rider=plan.rider("ssd_bwd"))
    plan.collect("ssd_bwd", sent)
    dvec = jnp.sum(dvec, axis=0).reshape(SSM_GROUPS, SUBLANES, LANES)
    gr["dt_bias"], gr["a_log"], gr["d_skip"] = (dvec[:, k, :hg].reshape(-1) for k in range(3))
    dxact = jnp.concatenate([dxs, dbm, dcm], axis=-1)
    dxbc, gr["conv_w"], gr["conv_b"] = _conv_bwd(sv["xbc3"], p["conv_w"], p["conv_b"], dxact, "conv_bwd")
    (dqkv, dqg, dkg), sent = _attention_bwd(sv["qkv3"], p["q_gain"], p["k_gain"], sv["rtot"],
                                            do.reshape(bsz, s, aw), "attention_bwd",
                                            rider=plan.rider("attention_bwd"))
    plan.collect("attention_bwd", sent)
    gr["q_gain"] = jnp.sum(dqg, axis=(0, 1, 2))
    gr["k_gain"] = jnp.sum(dkg, axis=(0, 1, 2))
    dqkv, dxbc, ddt = dqkv.reshape(t, 3 * aw), dxbc.reshape(t, cd), ddt.reshape(t, SSM_GROUPS * LANES)
    (dh,), _ = _hosted_matmul(
        [[(dqkv, p["wqkv"], "nt"), (dz, p["wz"], "nt"), (dxbc, p["wxbc"], "nt"), (ddt, p["wdt"], "nt")]],
        [], lambda accs, ex: (accs[0],), [F32], t, d, 512, 512, "proj_in_dgrad", plan)
    h = sv["h"]
    gr["wqkv"] = _mm(h, dqkv, "tn", d, 3 * aw, F32, "proj_qkv_wgrad")
    gr["wz"] = _mm(h, dz, "tn", d, sw, F32, "proj_z_wgrad")
    gr["wxbc"] = _mm(h, dxbc, "tn", d, cd, F32, "proj_xbc_wgrad")
    gr["wdt"] = _mm(h, ddt, "tn", d, SSM_GROUPS * LANES, F32, "proj_dt_wgrad", tn=SSM_GROUPS * LANES)
    dx, dxb, gr["norm_mix"] = _rmsnorm_bwd(sv["x"], p["norm_mix"], dh, dx1, "norm_mix_bwd")
    return dx, dxb, gr


SMALL = ["norm_mix", "q_gain", "k_gain", "conv_w", "conv_b", "dt_bias", "a_log", "d_skip",
         "attn_out_gain", "ssm_out_gain", "norm_ffn"]
ORDER = ["norm_mix", "w_in", "q_gain", "k_gain", "conv_w", "conv_b", "dt_bias", "a_log", "d_skip",
         "attn_out_gain", "ssm_out_gain", "w_out", "norm_ffn", "w_gate", "w_up", "w_down"]


def kernel(x, norm_mix, w_in, q_gain, k_gain, conv_w, conv_b, dt_bias, a_log, d_skip, attn_out_gain, ssm_out_gain, w_out, norm_ffn, w_gate, w_up, w_down, loss_target, m_norm_mix, m_w_in, m_q_gain, m_k_gain, m_conv_w, m_conv_b, m_dt_bias, m_a_log, m_d_skip, m_attn_out_gain, m_ssm_out_gain, m_w_out, m_norm_ffn, m_w_gate, m_w_up, m_w_down, v_norm_mix, v_w_in, v_q_gain, v_k_gain, v_conv_w, v_conv_b, v_dt_bias, v_a_log, v_d_skip, v_attn_out_gain, v_ssm_out_gain, v_w_out, v_norm_ffn, v_w_gate, v_w_up, v_w_down):
    w = dict(norm_mix=norm_mix, w_in=w_in, q_gain=q_gain, k_gain=k_gain, conv_w=conv_w, conv_b=conv_b,
             dt_bias=dt_bias, a_log=a_log, d_skip=d_skip, attn_out_gain=attn_out_gain, ssm_out_gain=ssm_out_gain,
             w_out=w_out, norm_ffn=norm_ffn, w_gate=w_gate, w_up=w_up, w_down=w_down)
    mom = dict(norm_mix=m_norm_mix, w_in=m_w_in, q_gain=m_q_gain, k_gain=m_k_gain, conv_w=m_conv_w,
               conv_b=m_conv_b, dt_bias=m_dt_bias, a_log=m_a_log, d_skip=m_d_skip,
               attn_out_gain=m_attn_out_gain, ssm_out_gain=m_ssm_out_gain, w_out=m_w_out, norm_ffn=m_norm_ffn,
               w_gate=m_w_gate, w_up=m_w_up, w_down=m_w_down)
    var = dict(norm_mix=v_norm_mix, w_in=v_w_in, q_gain=v_q_gain, k_gain=v_k_gain, conv_w=v_conv_w,
               conv_b=v_conv_b, dt_bias=v_dt_bias, a_log=v_a_log, d_skip=v_d_skip,
               attn_out_gain=v_attn_out_gain, ssm_out_gain=v_ssm_out_gain, w_out=v_w_out, norm_ffn=v_norm_ffn,
               w_gate=v_w_gate, w_up=v_w_up, w_down=v_w_down)

    bsz, s, d = x.shape
    t = bsz * s
    depth = norm_mix.shape[0]
    aw = attn_out_gain.shape[1]
    sw = ssm_out_gain.shape[1]
    cd = conv_b.shape[1]
    hs_n = dt_bias.shape[1]
    hg = hs_n // SSM_GROUPS
    heads = aw // ATT_HEAD_DIM
    in_dim = 3 * aw + sw + cd + hs_n
    dff = w_gate.shape[2] * N_CHIPS
    cs = conv_w.shape[2]
    my_chip = 2 * lax.axis_index("x") + lax.axis_index("y")
    ids = (lax.axis_index("c").astype(jnp.int32).reshape(1), my_chip.astype(jnp.int32).reshape(1))
    wb ={n: w[n].astype(BF16) for n in BIG}
    conv_all = _small_exchange(_pack([conv_w]), "allgather_conv_w", False)
    conv_full = jnp.concatenate(
        [_unpack(conv_all[2 * j], [conv_w.shape])[0] for j in range(N_CHIPS)], axis=-1)

    def in_params(l, gat_in):
        win = jnp.transpose(gat_in, (1, 0, 2)).reshape(d, in_dim)
        wqkv = win[:, :3 * aw].reshape(d, 3, heads, ATT_HEAD_DIM)
        wqkv = jnp.transpose(wqkv, (0, 2, 1, 3)).reshape(d, 3 * aw)
        return dict(aw=aw, sw=sw, cd=cd, hg=hg, norm_mix=norm_mix[l], wqkv=wqkv, wz=win[:, 3 * aw:3 * aw + sw],
                    wxbc=win[:, 3 * aw + sw:3 * aw + sw + cd], wdt=_group_slots(win[:, 3 * aw + sw + cd:], hg),
                    q_gain=q_gain[l], k_gain=k_gain[l], conv_w=conv_full[l], conv_b=conv_b[l],
                    dt_bias=_group_slots(dt_bias[l], hg).reshape(1, -1),
                    a_log=_group_slots(a_log[l], hg).reshape(1, -1),
                    d_skip=_group_slots(d_skip[l], hg).reshape(1, -1),
                    attn_out_gain=attn_out_gain[l], ssm_out_gain=ssm_out_gain[l], norm_ffn=norm_ffn[l])

    def late_params(gat):
        return dict(wout=gat["w_out"].reshape(aw + sw, d), wd=gat["w_down"].reshape(dff, d),
                    wg=gat["w_gate"], wu=gat["w_up"])

    def ffn_to_chips(gr):
        return {"w_gate": gr["wg"], "w_up": gr["wu"], "w_down": gr["wd"].reshape(N_CHIPS, dff // N_CHIPS, d)}

    def mix_to_chips(gr):
        gqkv = gr["wqkv"].reshape(d, heads, 3, ATT_HEAD_DIM)
        gqkv = jnp.transpose(gqkv, (0, 2, 1, 3)).reshape(d, 3 * aw)
        gin = jnp.concatenate([gqkv, gr["wz"], gr["wxbc"], _ungroup_slots(gr["wdt"], hg)], axis=-1)
        return {"w_in": jnp.transpose(gin.reshape(d, N_CHIPS, in_dim // N_CHIPS), (1, 0, 2)),
                "w_out": gr["wout"].reshape(N_CHIPS, (aw + sw) // N_CHIPS, d)}

    xt = x.reshape(t, d)
    saved, params = [], []
    gat_in = _gather_now(wb["w_in"][0], "first")
    for l in range(depth):
        plan = _GatherPlan({n: wb[n][l] for n in LATE}, wb["w_in"][l + 1] if l + 1 < depth else None)
        xt, sv, p = _layer_fwd(xt, in_params(l, gat_in), bsz, s, plan, late_params)
        saved.append(sv)
        params.append(p)
        if l + 1 < depth:
            gat_in = plan.next_gathered()
    dxt, dxb, loss_lanes = _loss_head(xt, loss_target.reshape(t, d), "loss_head")

    grads = [None] * depth
    pending, totals = None, {}
    for l in reversed(range(depth)):
        plan = _ReducePlan(pending, l + 1, l, totals, depth, ids)
        dxt, dxb, grads[l] = _layer_bwd(dxt, dxb, params[l], saved[l], bsz, s, plan, ffn_to_chips)
        totals = plan.totals
        pending = mix_to_chips(grads[l])
    g = _ReducePlan(pending, 0, None, totals, depth, ids).run_now("first_layer")
    grad_x = dxt.reshape(bsz, s, d)

    def stack(name):
        return jnp.stack([grads[l][name] for l in range(depth)])

    small_shapes = [(1, LANES)] + [(depth, CONV_WIDTH, cd) if n == "conv_w" else w[n].shape for n in SMALL]
    small = _small_exchange(_pack([loss_lanes] + [stack(n) for n in SMALL]), "allreduce_small", True)
    small = _unpack(small, small_shapes)
    loss = small[0][0, 0]
    for n, a in zip(SMALL, small[1:], strict=True):
        g[n] = a
    g["conv_w"] = lax.dynamic_slice_in_dim(g["conv_w"], my_chip * cs, cs, axis=2)

    delta, new_m, new_v = {}, {}, {}
    for n in BIG:
        shp = w[n].shape
        two_d = (shp[0] * shp[1], shp[2])
        dl, nm, nv = _adamw(w[n].reshape(two_d), g[n].reshape(two_d), mom[n].reshape(two_d),
                            var[n].reshape(two_d), f"adamw_{n}")
        delta[n], new_m[n], new_v[n] = dl.reshape(shp), nm.reshape(shp), nv.reshape(shp)
    shapes = [w[n].shape for n in SMALL]
    dl, nm, nv = _adamw(_pack([w[n] for n in SMALL]), _pack([g[n] for n in SMALL]),
                        _pack([mom[n] for n in SMALL]), _pack([var[n] for n in SMALL]), "adamw_small")
    for n, a, b, c in zip(SMALL, _unpack(dl, shapes), _unpack(nm, shapes), _unpack(nv, shapes), strict=True):
        delta[n], new_m[n], new_v[n] = a, b, c

    return (loss, grad_x, *[g[n] for n in ORDER], *[delta[n] for n in ORDER],
            *[new_m[n] for n in ORDER], *[new_v[n] for n in ORDER])
```

```python
import jax
import jax.numpy as jnp
from jax import lax
from jax.experimental import pallas as pl
from jax.experimental.pallas import tpu as pltpu

F32 = jnp.float32
BF16 = jnp.bfloat16
MESH = pl.DeviceIdType.MESH
ANY = pl.BlockSpec(memory_space=pl.ANY)
VMEM = pl.BlockSpec(memory_space=pltpu.VMEM)

EPS = 1e-6
ATT_HEAD_DIM = 128
SSM_HEAD_DIM = 64
SSM_GROUPS = 2
SSM_STATE = 128
SSD_CHUNK = 128
CONV_WIDTH = 4
LANES = 128
SUBLANES = 8
ATT_TILE = 512
ATT_STRIP = 128
N_CHIPS = 4
N_DEV = 8

ADAM_LR = 0.001
ADAM_B1 = 0.9
ADAM_B2 = 0.999
ADAM_EPS = 1e-08
ADAM_WD = 0.01
ADAM_STEP = 10

VMEM_LIMIT = 48 * 1024 * 1024

NN = (((1,), (0,)), ((), ()))
NT = (((1,), (1,)), ((), ()))
TN = (((0,), (0,)), ((), ()))


def _dot(a, b, dims=NN):
    return lax.dot_general(a.astype(BF16), b.astype(BF16), dims, preferred_element_type=F32)


def _dot_exact(x, ones, dims=NN, passes=3, ones_left=False):
    acc = None
    rem = x
    for _ in range(passes):
        piece = rem.astype(BF16)
        rem = rem - piece.astype(F32)
        p = (lax.dot_general(ones, piece, dims, preferred_element_type=F32) if ones_left
             else lax.dot_general(piece, ones, dims, preferred_element_type=F32))
        acc = p if acc is None else acc + p
    return acc


def _scan_lanes(x, tri, passes, reverse=False):
    nblk = x.shape[1] // LANES
    blocks = [x[:, k * LANES:(k + 1) * LANES] for k in range(nblk)]
    out, carry = [None] * nblk, None
    for k in (reversed(range(nblk)) if reverse else range(nblk)):
        p = _dot_exact(blocks[k], tri, passes=passes)
        out[k] = p if carry is None else p + carry
        tot = jnp.sum(blocks[k], axis=1, keepdims=True)
        carry = tot if carry is None else carry + tot
    return (out[0] if nblk == 1 else jnp.concatenate(out, axis=1)), carry


def _iota2(shape, axis):
    return lax.broadcasted_iota(jnp.int32, shape, axis)


def _tri(n, cmp):
    return cmp(_iota2((n, n), 0), _iota2((n, n), 1)).astype(BF16)


def _sum_all(v):
    return jnp.sum(jnp.sum(v, axis=1, keepdims=True), axis=0, keepdims=True)


def _fit(tile, dim, unit=LANES):
    if dim <= tile:
        return dim
    return max(k for k in range(unit, tile + 1, unit) if dim % k == 0)


def _params(sem):
    return pltpu.CompilerParams(dimension_semantics=sem, vmem_limit_bytes=VMEM_LIMIT)


def _call(body, *, name, grid, in_specs, out_specs, out_shape, sem, args, scratch_shapes=(), rider=None):
    in_specs, out_specs, out_shape = list(in_specs), list(out_specs), list(out_shape)
    scratch_shapes = list(scratch_shapes)
    if rider is None:
        res = pl.pallas_call(body, name=name, grid=grid, in_specs=in_specs, out_specs=out_specs,
                             out_shape=out_shape, scratch_shapes=scratch_shapes,
                             compiler_params=_params(sem))(*args)
        return list(res), []
    n_in, n_out, n_scr = len(in_specs), len(out_specs), len(scratch_shapes)
    r_in, r_out, n_sems = len(rider["ins"]), len(rider["outs"]), rider["n_sems"]

    def hosted(*refs):
        ins, rest = refs[:n_in], refs[n_in:]
        rins, rest = rest[:r_in], rest[r_in:]
        outs, rest = rest[:n_out], rest[n_out:]
        routs, rest = rest[:r_out], rest[r_out:]
        scr, (send_sems, recv_sems) = rest[:n_scr], rest[n_scr:]
        first, last = None, None
        for d, size in enumerate(grid):
            f, e = pl.program_id(d) == 0, pl.program_id(d) == size - 1
            first = f if first is None else jnp.logical_and(first, f)
            last = e if last is None else jnp.logical_and(last, e)

        @pl.when(first)
        def _():
            for cp in rider["copies"](rins, routs, send_sems, recv_sems):
                cp.start()

        body(*ins, *outs, *scr)

        @pl.when(last)
        def _():
            for cp in rider["copies"](rins, routs, send_sems, recv_sems):
                cp.wait()

    res = pl.pallas_call(
        hosted, name=name, grid=grid, in_specs=in_specs + [ANY] * r_in, out_specs=out_specs + [ANY] * r_out,
        out_shape=out_shape + list(rider["outs"]),
        scratch_shapes=scratch_shapes + [pltpu.SemaphoreType.DMA((n_sems,)), pltpu.SemaphoreType.DMA((n_sems,))],
        input_output_aliases={n_in + i: n_out + o for i, o in rider["aliases"].items()},
        compiler_params=_params(("arbitrary",) * len(grid)),
    )(*args, *rider["ins"])
    return list(res[:n_out]), list(res[n_out:])


def _softplus(x):
    return jnp.maximum(x, 0.0) + jnp.log(1.0 + jnp.exp(-jnp.abs(x)))


def _sigmoid(x):
    return 1.0 / (1.0 + jnp.exp(-x))


def _rms_fwd(x, g):
    r = lax.rsqrt(jnp.mean(x * x, axis=-1, keepdims=True) + EPS)
    return (x * r) * g


def _rms_bwd(x, g, dh):
    r = lax.rsqrt(jnp.mean(x * x, axis=-1, keepdims=True) + EPS)
    y = x * r
    dy = dh * g
    dx = r * (dy - y * jnp.mean(dy * y, axis=-1, keepdims=True))
    return dx, dh * y


def _matmul(groups, extras, epilogue, out_dtypes, m, n, tm, tn, name, rider=None):
    tm, tn = _fit(tm, m), _fit(tn, n)
    flat = [t for grp in groups for t in grp]
    n_terms, n_extra = len(flat), len(extras)

    def body(*refs):
        outs = refs[2 * n_terms + n_extra:]
        accs, pos = [], 0
        for grp in groups:
            acc = None
            for (_, _, mode) in grp:
                dims = {"nn": NN, "nt": NT, "tn": TN}[mode]
                p = _dot(refs[2 * pos][...], refs[2 * pos + 1][...], dims)
                acc = p if acc is None else acc + p
                pos += 1
            accs.append(acc)
        ex = [refs[2 * n_terms + i][...] for i in range(n_extra)]
        res = epilogue(accs, ex)
        for o_ref, r in zip(outs, res, strict=True):
            o_ref[...] = r.astype(o_ref.dtype)

    in_specs, args = [], []
    for (a, b, mode) in flat:
        if mode == "nn":
            k = a.shape[1]
            in_specs += [pl.BlockSpec((tm, k), lambda i, j: (i, 0)), pl.BlockSpec((k, tn), lambda i, j: (0, j))]
        elif mode == "nt":
            k = a.shape[1]
            in_specs += [pl.BlockSpec((tm, k), lambda i, j: (i, 0)), pl.BlockSpec((tn, k), lambda i, j: (j, 0))]
        else:
            k = a.shape[0]
            in_specs += [pl.BlockSpec((k, tm), lambda i, j: (0, i)), pl.BlockSpec((k, tn), lambda i, j: (0, j))]
        args += [a, b]
    for e in extras:
        in_specs.append(pl.BlockSpec((tm, tn), lambda i, j: (i, j)))
        args.append(e)
    outs, routs = _call(
        body, name=name, grid=(m // tm, n // tn), in_specs=in_specs,
        out_specs=[pl.BlockSpec((tm, tn), lambda i, j: (i, j)) for _ in out_dtypes],
        out_shape=[jax.ShapeDtypeStruct((m, n), d) for d in out_dtypes],
        sem=("parallel", "parallel"), args=args, rider=rider)
    return outs if rider is None else (outs, routs)


def _mm(a, b, mode, m, n, out_dtype, name, tm=512, tn=512, res=None):
    extras = [] if res is None else [res]
    epi = (lambda accs, ex: (accs[0],)) if res is None else (lambda accs, ex: (accs[0] + ex[0],))
    return _matmul([[(a, b, mode)]], extras, epi, [out_dtype], m, n, tm, tn, name)[0]


def _swiglu_fwd_epilogue(accs, ex):
    g, u = accs
    return g, u, (g * _sigmoid(g)) * u


def _swiglu_bwd_epilogue(accs, ex):
    dact, (g, u) = accs[0], ex
    sg = _sigmoid(g)
    silu = g * sg
    return dact * u * (sg * (1.0 + g * (1.0 - sg))), dact * silu


def _rmsnorm_fwd(x, g, name, tr=512):
    t, d = x.shape
    tr = min(tr, t)

    def body(x_ref, g_ref, h_ref):
        h_ref[...] = _rms_fwd(x_ref[...], g_ref[...]).astype(BF16)

    return pl.pallas_call(
        body, name=name, grid=(t // tr,),
        in_specs=[pl.BlockSpec((tr, d), lambda i: (i, 0)), pl.BlockSpec((1, d), lambda i: (0, 0))],
        out_specs=pl.BlockSpec((tr, d), lambda i: (i, 0)),
        out_shape=jax.ShapeDtypeStruct((t, d), BF16),
        compiler_params=_params(("parallel",)),
    )(x, g.reshape(1, d))


def _rmsnorm_bwd(x, g, dh, dres, name, tr=256):
    t, d = x.shape
    tr = min(tr, t)

    def body(x_ref, g_ref, dh_ref, dres_ref, dx_ref, dxb_ref, dg_ref):
        dx, dgr = _rms_bwd(x_ref[...], g_ref[...], dh_ref[...])
        dx = dx + dres_ref[...]
        dx_ref[...] = dx
        dxb_ref[...] = dx.astype(BF16)

        @pl.when(pl.program_id(0) == 0)
        def _():
            dg_ref[...] = jnp.zeros_like(dg_ref)

        dg_ref[...] += jnp.sum(dgr, axis=0, keepdims=True)

    row = pl.BlockSpec((tr, d), lambda i: (i, 0))
    vec = pl.BlockSpec((1, d), lambda i: (0, 0))
    dx, dxb, dg = pl.pallas_call(
        body, name=name, grid=(t // tr,),
        in_specs=[row, vec, row, row], out_specs=[row, row, vec],
        out_shape=[jax.ShapeDtypeStruct((t, d), F32), jax.ShapeDtypeStruct((t, d), BF16),
                   jax.ShapeDtypeStruct((1, d), F32)],
        compiler_params=_params(("arbitrary",)),
    )(x, g.reshape(1, d), dh, dres)
    return dx, dxb, dg.reshape(d)


def _merge_fwd(o_att, y, z, ga, gs, name, tr=256, rider=None):
    t, wa = o_att.shape
    ws = y.shape[1]
    wg = ws // SSM_GROUPS
    tr = min(tr, t)

    def body(o_ref, y_ref, z_ref, ga_ref, gs_ref, m_ref):
        m_ref[:, 0:wa] = _rms_fwd(o_ref[...], ga_ref[...]).astype(BF16)
        for g in range(SSM_GROUPS):
            sl = slice(g * wg, (g + 1) * wg)
            zz = z_ref[:, sl]
            yz = y_ref[:, sl] * (zz * _sigmoid(zz))
            m_ref[:, wa + g * wg:wa + (g + 1) * wg] = _rms_fwd(yz, gs_ref[:, sl]).astype(BF16)

    return _call(
        body, name=name, grid=(t // tr,),
        in_specs=[pl.BlockSpec((tr, wa), lambda i: (i, 0)), pl.BlockSpec((tr, ws), lambda i: (i, 0)),
                  pl.BlockSpec((tr, ws), lambda i: (i, 0)), pl.BlockSpec((1, wa), lambda i: (0, 0)),
                  pl.BlockSpec((1, ws), lambda i: (0, 0))],
        out_specs=[pl.BlockSpec((tr, wa + ws), lambda i: (i, 0))],
        out_shape=[jax.ShapeDtypeStruct((t, wa + ws), BF16)],
        sem=("parallel",), args=(o_att, y, z, ga.reshape(1, wa), gs.reshape(1, ws)), rider=rider)


def _merge_bwd(o_att, y, z, ga, gs, dmix, name, tr=256):
    t, wa = o_att.shape
    ws = y.shape[1]
    wg = ws // SSM_GROUPS
    tr = min(tr, t)

    def body(o_ref, y_ref, z_ref, ga_ref, gs_ref, dm_ref, do_ref, dy_ref, dz_ref, dga_ref, dgs_ref):
        @pl.when(pl.program_id(0) == 0)
        def _():
            dga_ref[...] = jnp.zeros_like(dga_ref)
            dgs_ref[...] = jnp.zeros_like(dgs_ref)

        do, dgr = _rms_bwd(o_ref[...], ga_ref[...], dm_ref[:, 0:wa])
        do_ref[...] = do
        dga_ref[...] += jnp.sum(dgr, axis=0, keepdims=True)
        for g in range(SSM_GROUPS):
            sl = slice(g * wg, (g + 1) * wg)
            zz, yy = z_ref[:, sl], y_ref[:, sl]
            sg = _sigmoid(zz)
            silu = zz * sg
            dyz, dgr = _rms_bwd(yy * silu, gs_ref[:, sl], dm_ref[:, wa + g * wg:wa + (g + 1) * wg])
            dy_ref[:, sl] = dyz * silu
            dz_ref[:, sl] = (dyz * yy * (sg + silu * (1.0 - sg))).astype(BF16)
            dgs_ref[:, sl] += jnp.sum(dgr, axis=0, keepdims=True)

    rowa = pl.BlockSpec((tr, wa), lambda i: (i, 0))
    rows = pl.BlockSpec((tr, ws), lambda i: (i, 0))
    veca = pl.BlockSpec((1, wa), lambda i: (0, 0))
    vecs = pl.BlockSpec((1, ws), lambda i: (0, 0))
    do, dy, dz, dga, dgs = pl.pallas_call(
        body, name=name, grid=(t // tr,),
        in_specs=[rowa, rows, rows, veca, vecs, pl.BlockSpec((tr, wa + ws), lambda i: (i, 0))],
        out_specs=[rowa, rows, rows, veca, vecs],
        out_shape=[jax.ShapeDtypeStruct((t, wa), F32), jax.ShapeDtypeStruct((t, ws), F32),
                   jax.ShapeDtypeStruct((t, ws), BF16), jax.ShapeDtypeStruct((1, wa), F32),
                   jax.ShapeDtypeStruct((1, ws), F32)],
        compiler_params=_params(("arbitrary",)),
    )(o_att, y, z, ga.reshape(1, wa), gs.reshape(1, ws), dmix)
    return do, dy, dz, dga.reshape(wa), dgs.reshape(ws)


def _loss_head(y, target, name, tr=256):
    t, d = y.shape
    tr = min(tr, t)

    def body(y_ref, t_ref, dy_ref, dyb_ref, l_ref):
        @pl.when(pl.program_id(0) == 0)
        def _():
            l_ref[...] = jnp.zeros_like(l_ref)

        diff = y_ref[...] - t_ref[...]
        dy = diff * (1.0 / d)
        dy_ref[...] = dy
        dyb_ref[...] = dy.astype(BF16)
        part = jnp.sum(diff * diff, axis=0, keepdims=True)
        fold = part[:, 0:LANES]
        for k in range(1, d // LANES):
            fold = fold + part[:, k * LANES:(k + 1) * LANES]
        l_ref[...] += fold * (0.5 / d)

    row = pl.BlockSpec((tr, d), lambda i: (i, 0))
    return pl.pallas_call(
        body, name=name, grid=(t // tr,), in_specs=[row, row],
        out_specs=[row, row, pl.BlockSpec((1, LANES), lambda i: (0, 0))],
        out_shape=[jax.ShapeDtypeStruct((t, d), F32), jax.ShapeDtypeStruct((t, d), BF16),
                   jax.ShapeDtypeStruct((1, LANES), F32)],
        compiler_params=_params(("arbitrary",)),
    )(y, target)


def _adamw(w, g, m, v, name, tr=256):
    r, c = w.shape
    tr = _fit(tr, r, 16)

    def body(w_ref, g_ref, m_ref, v_ref, d_ref, nm_ref, nv_ref):
        gg = g_ref[...]
        nm = ADAM_B1 * m_ref[...] + (1.0 - ADAM_B1) * gg
        nv = ADAM_B2 * v_ref[...] + (1.0 - ADAM_B2) * (gg * gg)
        m_hat = nm / (1.0 - ADAM_B1 ** ADAM_STEP)
        v_hat = nv / (1.0 - ADAM_B2 ** ADAM_STEP)
        d_ref[...] = -ADAM_LR * (m_hat / (jnp.sqrt(v_hat) + ADAM_EPS) + ADAM_WD * w_ref[...])
        nm_ref[...] = nm
        nv_ref[...] = nv

    blk = pl.BlockSpec((tr, c), lambda i: (i, 0))
    return pl.pallas_call(
        body, name=name, grid=(r // tr,), in_specs=[blk] * 4, out_specs=[blk] * 3,
        out_shape=[jax.ShapeDtypeStruct((r, c), F32)] * 3,
        compiler_params=_params(("parallel",)),
    )(w, g, m, v)


def _att_scores(qi, kj, scale, row0):
    z = _dot(qi, kj, NT) * scale
    lb = -_softplus(-z)
    lrm = lb - z
    if row0 is None:
        return lb, lrm, None
    mask = _iota2(z.shape, 1) < _iota2(z.shape, 0) + row0
    return lb, jnp.where(mask, lrm, 0.0), mask


def _masked(mask, v):
    return v if mask is None else jnp.where(mask, v, 0.0)


def _attention_fwd(qkv, qg, kg, name, tile=None, rider=None):
    bsz, s, w3 = qkv.shape
    hd = ATT_HEAD_DIM
    heads = w3 // (3 * hd)
    tile = min(tile or ATT_TILE, s)
    strip = min(ATT_STRIP, tile)
    nb = s // tile
    scale = hd ** -0.5

    def body(qkv_ref, qg_ref, kg_ref, o_ref, r_ref, qn_s, kn_s, vb_s, acc_s, c_s):
        qn_s[...] = _rms_fwd(qkv_ref[0, :, 0:hd], qg_ref[...]).astype(BF16)
        kn_s[...] = _rms_fwd(qkv_ref[0, :, hd:2 * hd], kg_ref[...]).astype(BF16)
        vb_s[...] = qkv_ref[0, :, 2 * hd:3 * hd].astype(BF16)
        after = _tri(LANES, lambda r, c: r > c)

        def q_loop(i, _):
            rows = pl.ds(pl.multiple_of(i * tile, tile), tile)
            acc_s[...] = jnp.zeros_like(acc_s)
            c_s[...] = jnp.zeros_like(c_s)

            def key_tile(j, diagonal):
                cols = pl.ds(pl.multiple_of(j * tile, tile), tile)
                kj, vj = kn_s[cols, :], vb_s[cols, :]
                strips = range(tile // strip)
                subs = [slice(r * strip, (r + 1) * strip) for r in strips]
                srows = [pl.ds(pl.multiple_of(i * tile + r * strip, strip), strip) for r in strips]
                sc = [_att_scores(qn_s[srows[r], :], kj, scale, r * strip if diagonal else None) for r in strips]
                later = [_scan_lanes(sc[r][1], after, 2, reverse=True) for r in strips]
                for r in strips:
                    w = _masked(sc[r][2], jnp.exp(sc[r][0] + (later[r][0] + c_s[subs[r], :])))
                    acc_s[subs[r], :] += _dot(w, vj)
                    c_s[subs[r], :] += later[r][1]

            def k_loop(jj, _):
                key_tile(i - jj, False)
                return 0

            key_tile(i, True)
            lax.fori_loop(1, i + 1, k_loop, 0)
            o_ref[0, rows, :] = acc_s[...]
            r_ref[0, 0, rows, :] = c_s[...]
            return 0

        lax.fori_loop(0, nb, q_loop, 0)

    return _call(
        body, name=name, grid=(bsz, heads),
        in_specs=[pl.BlockSpec((1, s, 3 * hd), lambda b, h: (b, 0, h)),
                  pl.BlockSpec((1, hd), lambda b, h: (0, 0)), pl.BlockSpec((1, hd), lambda b, h: (0, 0))],
        out_specs=[pl.BlockSpec((1, s, hd), lambda b, h: (b, 0, h)),
                   pl.BlockSpec((1, 1, s, 1), lambda b, h: (b, h, 0, 0))],
        out_shape=[jax.ShapeDtypeStruct((bsz, s, heads * hd), F32),
                   jax.ShapeDtypeStruct((bsz, heads, s, 1), F32)],
        scratch_shapes=[pltpu.VMEM((s, hd), BF16), pltpu.VMEM((s, hd), BF16), pltpu.VMEM((s, hd), BF16),
                        pltpu.VMEM((tile, hd), F32), pltpu.VMEM((tile, 1), F32)],
        sem=("parallel", "parallel"), args=(qkv, qg.reshape(1, hd), kg.reshape(1, hd)), rider=rider)


def _attention_bwd(qkv, qg, kg, rtot, do, name, tile=None, rider=None):
    bsz, s, w3 = qkv.shape
    hd = ATT_HEAD_DIM
    heads = w3 // (3 * hd)
    tile = min(tile or ATT_TILE, s)
    strip = min(ATT_STRIP, tile)
    nb = s // tile
    scale = hd ** -0.5

    def body(qkv_ref, qg_ref, kg_ref, r_ref, do_ref, dqkv_ref, dqg_ref, dkg_ref,
             qn_s, kn_s, vb_s, dob_s, dqn_s, dkn_s, dv_s, c1_s, c2_s, wb_s, dzb_s):
        qn_s[...] = _rms_fwd(qkv_ref[0, :, 0:hd], qg_ref[...]).astype(BF16)
        kn_s[...] = _rms_fwd(qkv_ref[0, :, hd:2 * hd], kg_ref[...]).astype(BF16)
        vb_s[...] = qkv_ref[0, :, 2 * hd:3 * hd].astype(BF16)
        dob_s[...] = do_ref[0].astype(BF16)
        dqn_s[...] = jnp.zeros_like(dqn_s)
        dkn_s[...] = jnp.zeros_like(dkn_s)
        dv_s[...] = jnp.zeros_like(dv_s)
        upto = _tri(LANES, lambda r, c: r <= c)
        before = _tri(LANES, lambda r, c: r < c)

        def q_loop(i, _):
            rows = pl.ds(pl.multiple_of(i * tile, tile), tile)
            c1_s[...] = jnp.zeros_like(c1_s)
            c2_s[...] = jnp.zeros_like(c2_s)

            def key_tile(j, diagonal):
                cols = pl.ds(pl.multiple_of(j * tile, tile), tile)
                kj, vj = kn_s[cols, :], vb_s[cols, :]
                strips = range(tile // strip)
                subs = [slice(r * strip, (r + 1) * strip) for r in strips]
                srows = [pl.ds(pl.multiple_of(i * tile + r * strip, strip), strip) for r in strips]
                sc = [_att_scores(qn_s[srows[r], :], kj, scale, r * strip if diagonal else None) for r in strips]
                dw = [_dot(dob_s[srows[r], :], vj, NT) for r in strips]
                upto_lr = [_scan_lanes(sc[r][1], upto, 2) for r in strips]
                w = [_masked(sc[r][2], jnp.exp(sc[r][0] + (r_ref[0, 0, srows[r], :] - (upto_lr[r][0] + c1_s[subs[r], :]))))
                     for r in strips]
                e = [w[r] * dw[r] for r in strips]
                pre = [_scan_lanes(e[r], before, 1) for r in strips]
                dz = [_masked(sc[r][2], (e[r] - jnp.exp(sc[r][0]) * (e[r] + (pre[r][0] + c2_s[subs[r], :]))) * scale)
                      for r in strips]
                for r in strips:
                    wb_s[subs[r], :] = w[r].astype(BF16)
                    dzb_s[subs[r], :] = dz[r].astype(BF16)
                    c1_s[subs[r], :] += upto_lr[r][1]
                    c2_s[subs[r], :] += pre[r][1]
                dqn_s[rows, :] += _dot(dzb_s[...], kj)
                dv_s[cols, :] += _dot(wb_s[...], dob_s[rows, :], TN)
                dkn_s[cols, :] += _dot(dzb_s[...], qn_s[rows, :], TN)

            def k_loop(j, _):
                key_tile(j, False)
                return 0

            lax.fori_loop(0, i, k_loop, 0)
            key_tile(i, True)
            return 0

        lax.fori_loop(0, nb, q_loop, 0)
        dq, dgq = _rms_bwd(qkv_ref[0, :, 0:hd], qg_ref[...], dqn_s[...])
        dk, dgk = _rms_bwd(qkv_ref[0, :, hd:2 * hd], kg_ref[...], dkn_s[...])
        dqkv_ref[0, :, 0:hd] = dq.astype(BF16)
        dqkv_ref[0, :, hd:2 * hd] = dk.astype(BF16)
        dqkv_ref[0, :, 2 * hd:3 * hd] = dv_s[...].astype(BF16)
        dqg_ref[0, 0] = jnp.sum(dgq, axis=0, keepdims=True)
        dkg_ref[0, 0] = jnp.sum(dgk, axis=0, keepdims=True)

    gain = pl.BlockSpec((1, hd), lambda b, h: (0, 0))
    dgain = pl.BlockSpec((1, 1, 1, hd), lambda b, h: (b, h, 0, 0))
    return _call(
        body, name=name, grid=(bsz, heads),
        in_specs=[pl.BlockSpec((1, s, 3 * hd), lambda b, h: (b, 0, h)), gain, gain,
                  pl.BlockSpec((1, 1, s, 1), lambda b, h: (b, h, 0, 0)),
                  pl.BlockSpec((1, s, hd), lambda b, h: (b, 0, h))],
        out_specs=[pl.BlockSpec((1, s, 3 * hd), lambda b, h: (b, 0, h)), dgain, dgain],
        out_shape=[jax.ShapeDtypeStruct((bsz, s, w3), BF16),
                   jax.ShapeDtypeStruct((bsz, heads, 1, hd), F32),
                   jax.ShapeDtypeStruct((bsz, heads, 1, hd), F32)],
        scratch_shapes=[pltpu.VMEM((s, hd), BF16)] * 4 + [pltpu.VMEM((s, hd), F32)] * 3
        + [pltpu.VMEM((tile, 1), F32)] * 2 + [pltpu.VMEM((tile, tile), BF16)] * 2,
        sem=("parallel", "parallel"), args=(qkv, qg.reshape(1, hd), kg.reshape(1, hd), rtot, do), rider=rider)


def _conv_pre(pad_ref, w_ref, b_ref, s):
    pre = b_ref[...]
    for i in range(CONV_WIDTH):
        off = SUBLANES - (CONV_WIDTH - 1) + i
        pre = pre + pad_ref[off:off + s, :] * w_ref[i:i + 1, :]
    return pre


def _conv_fwd(u, w, b, name, tc=256):
    bsz, s, c = u.shape
    tc = min(tc, c)

    def body(u_ref, w_ref, b_ref, a_ref, pad_s):
        pad_s[0:SUBLANES, :] = jnp.zeros((SUBLANES, tc), F32)
        pad_s[SUBLANES:SUBLANES + s, :] = u_ref[0]
        pre = _conv_pre(pad_s, w_ref, b_ref, s)
        a_ref[0] = pre * _sigmoid(pre)

    return pl.pallas_call(
        body, name=name, grid=(bsz, c // tc),
        in_specs=[pl.BlockSpec((1, s, tc), lambda i, j: (i, 0, j)),
                  pl.BlockSpec((CONV_WIDTH, tc), lambda i, j: (0, j)), pl.BlockSpec((1, tc), lambda i, j: (0, j))],
        out_specs=pl.BlockSpec((1, s, tc), lambda i, j: (i, 0, j)),
        out_shape=jax.ShapeDtypeStruct((bsz, s, c), F32),
        scratch_shapes=[pltpu.VMEM((s + SUBLANES, tc), F32)],
        compiler_params=_params(("parallel", "parallel")),
    )(u, w, b.reshape(1, c))


def _conv_bwd(u, w, b, da, name, tc=256):
    bsz, s, c = u.shape
    tc = min(tc, c)

    def body(u_ref, w_ref, b_ref, da_ref, du_ref, dw_ref, db_ref, pad_s, gpad_s):
        @pl.when(pl.program_id(1) == 0)
        def _():
            dw_ref[...] = jnp.zeros_like(dw_ref)
            db_ref[...] = jnp.zeros_like(db_ref)

        pad_s[0:SUBLANES, :] = jnp.zeros((SUBLANES, tc), F32)
        pad_s[SUBLANES:SUBLANES + s, :] = u_ref[0]
        pre = _conv_pre(pad_s, w_ref, b_ref, s)
        sg = _sigmoid(pre)
        dpre = da_ref[0] * (sg * (1.0 + pre * (1.0 - sg)))
        gpad_s[0:s, :] = dpre
        gpad_s[s:s + SUBLANES, :] = jnp.zeros((SUBLANES, tc), F32)
        du = jnp.zeros((s, tc), F32)
        for i in range(CONV_WIDTH):
            back = CONV_WIDTH - 1 - i
            du = du + gpad_s[back:back + s, :] * w_ref[i:i + 1, :]
            off = SUBLANES - (CONV_WIDTH - 1) + i
            dw_ref[i:i + 1, :] += jnp.sum(dpre * pad_s[off:off + s, :], axis=0, keepdims=True)
        du_ref[0] = du.astype(BF16)
        db_ref[...] += jnp.sum(dpre, axis=0, keepdims=True)

    blk = pl.BlockSpec((1, s, tc), lambda j, i: (i, 0, j))
    du, dw, db = pl.pallas_call(
        body, name=name, grid=(c // tc, bsz),
        in_specs=[blk, pl.BlockSpec((CONV_WIDTH, tc), lambda j, i: (0, j)),
                  pl.BlockSpec((1, tc), lambda j, i: (0, j)), blk],
        out_specs=[blk, pl.BlockSpec((CONV_WIDTH, tc), lambda j, i: (0, j)),
                   pl.BlockSpec((1, tc), lambda j, i: (0, j))],
        out_shape=[jax.ShapeDtypeStruct((bsz, s, c), BF16), jax.ShapeDtypeStruct((CONV_WIDTH, c), F32),
                   jax.ShapeDtypeStruct((1, c), F32)],
        scratch_shapes=[pltpu.VMEM((s + SUBLANES, tc), F32), pltpu.VMEM((s + SUBLANES, tc), F32)],
        compiler_params=_params(("parallel", "arbitrary")),
    )(u, w, b.reshape(1, c), da)
    return du, dw, db.reshape(c)


def _ssd_chunk_common(b_ref, c_ref, dt_ref, dtb_ref, alog_ref):
    bm, cm = b_ref[0], c_ref[0]
    draw = dt_ref[0] + dtb_ref[...]
    dt = _softplus(draw)
    a_row = -jnp.exp(alog_ref[...])
    da = dt * a_row
    n = SSD_CHUNK
    acum = _dot_exact(da, _tri(n, lambda r, c: r >= c), ones_left=True)
    acum_t = _dot_exact(da, _tri(n, lambda r, c: r <= c), dims=TN)
    cb = _dot(cm, bm, NT)
    return bm, cm, draw, dt, a_row, acum, acum_t, cb


def _row_totals(v):
    return _dot_exact(v, jnp.ones((v.shape[1], LANES), BF16), passes=2)


def _ssd_head_common(acum, acum_t, dt, cb, x, i):
    n, p = SSD_CHUNK, SSM_HEAD_DIM
    pick = (_iota2((LANES, LANES), 0) == i).astype(BF16)
    acol = _dot_exact(acum, pick)
    dtc = _dot_exact(dt, pick)[:, :p]
    arow = acum_t[i:i + 1, :]
    causal = _iota2((n, n), 0) >= _iota2((n, n), 1)
    lm = jnp.where(causal, jnp.exp(jnp.where(causal, acol - arow, 0.0)), 0.0)
    gm = cb * lm
    xh = x[:, i * p:(i + 1) * p]
    xdt = xh * dtc
    alast = acol[n - 1:n, :]
    dte = jnp.exp(alast - acol)
    return acol, lm, gm, dtc, xh, xdt, alast, dte


def _ssd_specs(s, wg, hg, rev):
    g, n, cl = SSM_GROUPS, SSM_STATE, SSD_CHUNK
    nc = s // cl
    boff, coff = (g * wg) // n, (g * wg) // n + g
    ci = (lambda c: nc - 1 - c) if rev else (lambda c: c)
    xblk = pl.BlockSpec((1, cl, wg), lambda b, k, c: (b, ci(c), k))
    bblk = pl.BlockSpec((1, cl, n), lambda b, k, c: (b, ci(c), boff + k))
    cblk = pl.BlockSpec((1, cl, n), lambda b, k, c: (b, ci(c), coff + k))
    nblk = pl.BlockSpec((1, cl, n), lambda b, k, c: (b, ci(c), k))
    dtblk = pl.BlockSpec((1, cl, LANES), lambda b, k, c: (b, ci(c), k))
    vec = pl.BlockSpec((1, LANES), lambda b, k, c: (0, k))
    hsblk = pl.BlockSpec((1, 1, 1, wg, n), lambda b, k, c: (b, k, ci(c), 0, 0))
    return nc, xblk, bblk, cblk, nblk, dtblk, vec, hsblk


def _ssd_fwd(xbc, dtraw, dtb, alog, dskip, hg, name, rider=None):
    bsz, s, _ = xbc.shape
    g, n, p = SSM_GROUPS, SSM_STATE, SSM_HEAD_DIM
    wg = hg * p
    nc, xblk, bblk, cblk, _, dtblk, vec, hsblk = _ssd_specs(s, wg, hg, False)

    def body(x_ref, b_ref, c_ref, dt_ref, dtb_ref, alog_ref, dsk_ref, y_ref, hs_ref, h_s):
        @pl.when(pl.program_id(2) == 0)
        def _():
            h_s[...] = jnp.zeros_like(h_s)

        bm, cm, _, dt, _, acum, acum_t, cb = _ssd_chunk_common(b_ref, c_ref, dt_ref, dtb_ref, alog_ref)
        x = x_ref[0]
        hs_ref[0, 0, 0] = h_s[...]
        hd_ = range(hg)
        hc = [_ssd_head_common(acum, acum_t, dt, cb, x, i) for i in hd_]
        hprev = [h_s[i * p:(i + 1) * p, :] for i in hd_]
        ydiag = [_dot(hc[i][2], hc[i][5]) for i in hd_]
        yoff = [_dot(cm, hprev[i], NT) for i in hd_]
        st = [_dot(hc[i][5] * hc[i][7][:, :p], bm, TN) for i in hd_]
        for i in hd_:
            acol, _, _, _, xh, _, alast, _ = hc[i]
            y_ref[0, :, i * p:(i + 1) * p] = ydiag[i] + yoff[i] * jnp.exp(acol[:, :p]) + xh * dsk_ref[:, i:i + 1]
            h_s[i * p:(i + 1) * p, :] = hprev[i] * jnp.exp(alast) + st[i]

    return _call(
        body, name=name, grid=(bsz, g, nc),
        in_specs=[xblk, bblk, cblk, dtblk, vec, vec, vec],
        out_specs=[xblk, hsblk],
        out_shape=[jax.ShapeDtypeStruct((bsz, s, g * wg), F32),
                   jax.ShapeDtypeStruct((bsz, g, nc, wg, n), F32)],
        scratch_shapes=[pltpu.VMEM((wg, n), F32)],
        sem=("parallel", "parallel", "arbitrary"), args=(xbc, xbc, xbc, dtraw, dtb, alog, dskip), rider=rider)


def _ssd_bwd(xbc, dtraw, dtb, alog, dskip, hs, dy, hg, name, rider=None):
    bsz, s, _ = xbc.shape
    g, n, p, cl = SSM_GROUPS, SSM_STATE, SSM_HEAD_DIM, SSD_CHUNK
    wg = hg * p
    nc, xblk, bblk, cblk, nblk, dtblk, vec, hsblk = _ssd_specs(s, wg, hg, True)

    def body(x_ref, b_ref, c_ref, dt_ref, dtb_ref, alog_ref, dsk_ref, hs_ref, dy_ref,
             dx_ref, db_ref, dc_ref, ddt_ref, dvec_ref, dh_s):
        @pl.when(pl.program_id(2) == 0)
        def _():
            dh_s[...] = jnp.zeros_like(dh_s)
            dvec_ref[...] = jnp.zeros_like(dvec_ref)

        lane = _iota2((cl, LANES), 1)
        sub = _iota2((LANES, cl), 0)
        lane1 = _iota2((1, LANES), 1)
        last_row = _iota2((cl, 1), 0) == cl - 1
        bm, cm, draw, dt, a_row, acum, acum_t, cb = _ssd_chunk_common(b_ref, c_ref, dt_ref, dtb_ref, alog_ref)
        x = x_ref[0]
        dyc = dy_ref[0]
        hd_ = range(hg)
        hc = [_ssd_head_common(acum, acum_t, dt, cb, x, i) for i in hd_]
        dyh = [dyc[:, i * p:(i + 1) * p] for i in hd_]
        hprev = [hs_ref[0, 0, 0, i * p:(i + 1) * p, :] for i in hd_]
        dhn = [dh_s[i * p:(i + 1) * p, :] for i in hd_]
        ea = [jnp.exp(hc[i][0]) for i in hd_]
        cd = [jnp.exp(hc[i][6]) for i in hd_]
        y0 = [_dot(cm, hprev[i], NT) for i in hd_]
        dxe = [_dot(bm, dhn[i], NT) for i in hd_]
        dgm = [_dot(dyh[i], hc[i][5], NT) for i in hd_]
        gdy = [_dot(hc[i][2], dyh[i], TN) for i in hd_]
        dy0 = [dyh[i] * ea[i][:, :p] for i in hd_]
        dcm_h = [_dot(dy0[i], hprev[i]) for i in hd_]
        dh_new = [_dot(dy0[i], cm, TN) + dhn[i] * cd[i] for i in hd_]
        dbm_h = [_dot(hc[i][5] * hc[i][7][:, :p], dhn[i]) for i in hd_]
        ws = [dgm[i] * hc[i][2] for i in hd_]
        dxdt = [dxe[i] * hc[i][7][:, :p] + gdy[i] for i in hd_]
        s_y0 = [_row_totals(dyh[i] * y0[i]) for i in hd_]
        s_xe = [_row_totals(dxe[i] * hc[i][5]) for i in hd_]
        s_ws = [_row_totals(ws[i]) for i in hd_]
        s_dt = [_row_totals(dxdt[i] * hc[i][4]) for i in hd_]
        s_dd = [_row_totals(dyh[i] * hc[i][4]) for i in hd_]
        s_hh = [_row_totals(dhn[i] * hprev[i]) for i in hd_]
        dcb = jnp.zeros((cl, cl), F32)
        dcm = jnp.zeros((cl, n), F32)
        dbm = jnp.zeros((cl, n), F32)
        da_col = jnp.zeros((cl, LANES), F32)
        da_row = jnp.zeros((LANES, cl), F32)
        ddt = jnp.zeros((cl, LANES), F32)
        dd = jnp.zeros((1, LANES), F32)
        for i in hd_:
            _, lm, _, dtc, _, _, _, dte = hc[i]
            dh_s[i * p:(i + 1) * p, :] = dh_new[i]
            dd = dd + jnp.where(lane1 == i, jnp.sum(s_dd[i], axis=0, keepdims=True), 0.0)
            t1 = s_xe[i] * dte
            d_alast = jnp.sum(s_hh[i], axis=0, keepdims=True) * cd[i] + jnp.sum(t1, axis=0, keepdims=True)
            dacol = s_y0[i] * ea[i] - t1 + s_ws[i] + jnp.where(last_row, d_alast, 0.0)
            dcb = dcb + dgm[i] * lm
            dcm = dcm + dcm_h[i]
            dbm = dbm + dbm_h[i]
            dx_ref[0, :, i * p:(i + 1) * p] = dxdt[i] * dtc + dyh[i] * dsk_ref[:, i:i + 1]
            da_col = jnp.where(lane == i, dacol, da_col)
            da_row = jnp.where(sub == i, -jnp.sum(ws[i], axis=0, keepdims=True), da_row)
            ddt = jnp.where(lane == i, s_dt[i], ddt)
        dc_ref[0] = dcm + _dot(dcb, bm)
        db_ref[0] = dbm + _dot(dcb, cm, TN)
        upper = _tri(cl, lambda r, k: r <= k)
        dda = _dot_exact(da_col, upper, ones_left=True) + _dot_exact(da_row, upper, dims=NT, ones_left=True)
        ddt = ddt + dda * a_row
        ddraw = ddt * _sigmoid(draw)
        ddt_ref[0] = ddraw.astype(BF16)
        dvec_ref[0, 0, 0:1, :] += jnp.sum(ddraw, axis=0, keepdims=True)
        dvec_ref[0, 0, 1:2, :] += jnp.sum(dda * dt, axis=0, keepdims=True) * a_row
        dvec_ref[0, 0, 2:3, :] += dd

    return _call(
        body, name=name, grid=(bsz, g, nc),
        in_specs=[xblk, bblk, cblk, dtblk, vec, vec, vec, hsblk, xblk],
        out_specs=[xblk, nblk, nblk, dtblk,
                   pl.BlockSpec((1, 1, SUBLANES, LANES), lambda b, k, c: (b, k, 0, 0))],
        out_shape=[jax.ShapeDtypeStruct((bsz, s, g * wg), F32), jax.ShapeDtypeStruct((bsz, s, g * n), F32),
                   jax.ShapeDtypeStruct((bsz, s, g * n), F32), jax.ShapeDtypeStruct((bsz, s, g * LANES), BF16),
                   jax.ShapeDtypeStruct((bsz, g, SUBLANES, LANES), F32)],
        scratch_shapes=[pltpu.VMEM((wg, n), F32)],
        sem=("parallel", "parallel", "arbitrary"), args=(xbc, xbc, xbc, dtraw, dtb, alog, dskip, hs, dy),
        rider=rider)


def _coords():
    return lax.axis_index("x"), lax.axis_index("y"), lax.axis_index("c")


def _other_chips(x, y):
    return [(1 - x, y), (x, 1 - y), (1 - x, 1 - y)]


def _remote(src, dst, send_sems, recv_sems, k, to):
    return pltpu.make_async_remote_copy(src_ref=src, dst_ref=dst, send_sem=send_sems.at[k],
                                        recv_sem=recv_sems.at[k], device_id=to, device_id_type=MESH)


def _standalone(rider, name):
    r_in, r_out, n_sems = len(rider["ins"]), len(rider["outs"]), rider["n_sems"]

    def body(*refs):
        rins, routs, (send_sems, recv_sems) = refs[:r_in], refs[r_in:r_in + r_out], refs[r_in + r_out:]
        cps = rider["copies"](rins, routs, send_sems, recv_sems)
        for cp in cps:
            cp.start()
        for cp in cps:
            cp.wait()

    res = pl.pallas_call(
        body, name=name, in_specs=[ANY] * r_in, out_specs=[ANY] * r_out, out_shape=list(rider["outs"]),
        scratch_shapes=[pltpu.SemaphoreType.DMA((n_sems,)), pltpu.SemaphoreType.DMA((n_sems,))],
        input_output_aliases=dict(rider["aliases"]),
    )(*rider["ins"])
    return list(res)


def _rows_of(shape):
    return shape[1] if len(shape) == 3 else shape[0]


def _slot(ref, j, rows):
    if len(ref.shape) == 3:
        return ref.at[j, rows]
    c = ref.shape[1] // N_CHIPS
    return ref.at[rows, pl.ds(pl.multiple_of(j * c, LANES), c)]


def _own_slot_set(shard, side_by_side):
    if side_by_side:
        return jnp.tile(shard, (1, N_CHIPS))
    return jnp.broadcast_to(shard[None], (N_CHIPS,) + shard.shape)


def _gather_chips_rider(shards, side_by_side):
    def copies(rins, routs, send_sems, recv_sems):
        x, y, c = _coords()
        me = 2 * x + y
        cps = []
        for q, (w_ref, o_ref) in enumerate(zip(rins[:len(shards)], routs, strict=True)):
            rh = w_ref.shape[0] // 2
            rows = pl.ds(c * rh, rh)
            for k, (px, py) in enumerate(_other_chips(x, y)):
                cps.append(_remote(w_ref.at[rows], _slot(o_ref, me, rows), send_sems, recv_sems, 3 * q + k,
                                   (px, py, c)))
        return cps

    bases = [_own_slot_set(w, side) for w, side in zip(shards, side_by_side, strict=True)]
    return dict(ins=list(shards) + bases, outs=[jax.ShapeDtypeStruct(b.shape, b.dtype) for b in bases],
                aliases={len(shards) + i: i for i in range(len(shards))}, n_sems=3 * len(shards), copies=copies)


def _gather_pair_rider(gathered):
    def copies(rins, routs, send_sems, recv_sems):
        x, y, c = _coords()
        cps = []
        for q, o_ref in enumerate(routs):
            rh = _rows_of(o_ref.shape) // 2
            for k, (px, py) in enumerate(_other_chips(x, y)):
                part = _slot(o_ref, 2 * px + py, pl.ds(c * rh, rh))
                cps.append(_remote(part, part, send_sems, recv_sems, 3 * q + k, (x, y, 1 - c)))
        return cps

    return dict(ins=list(gathered), outs=[jax.ShapeDtypeStruct(g.shape, g.dtype) for g in gathered],
                aliases={i: i for i in range(len(gathered))}, n_sems=3 * len(gathered), copies=copies)


def _reduce_pair_rider(grads):
    def copies(rins, routs, send_sems, recv_sems):
        x, y, c = _coords()
        cps, k = [], 0
        for g_ref, r_ref in zip(rins, routs, strict=True):
            rh = _rows_of(g_ref.shape) // 2
            rows = pl.ds((1 - c) * rh, rh)
            if len(g_ref.shape) == 3:
                cps.append(_remote(g_ref.at[:, rows], r_ref, send_sems, recv_sems, k, (x, y, 1 - c)))
                k += 1
            else:
                for j in range(N_CHIPS):
                    cps.append(_remote(_slot(g_ref, j, rows), r_ref.at[j], send_sems, recv_sems, k, (x, y, 1 - c)))
                    k += 1
        return cps

    def out_of(g):
        r, c = (g.shape[1], g.shape[2]) if g.ndim == 3 else (g.shape[0], g.shape[1] // N_CHIPS)
        return jax.ShapeDtypeStruct((N_CHIPS, r // 2, c), g.dtype)

    return dict(ins=list(grads), outs=[out_of(g) for g in grads], aliases={},
                n_sems=sum(1 if g.ndim == 3 else N_CHIPS for g in grads), copies=copies)


def _reduce_chips_rider(pair_sums):
    def copies(rins, routs, send_sems, recv_sems):
        x, y, c = _coords()
        cps = []
        for q, (p_ref, r_ref) in enumerate(zip(rins, routs, strict=True)):
            for k, (px, py) in enumerate(_other_chips(x, y)):
                cps.append(_remote(p_ref.at[2 * px + py], r_ref.at[k], send_sems, recv_sems, 3 * q + k, (px, py, c)))
        return cps

    return dict(ins=list(pair_sums), outs=[jax.ShapeDtypeStruct((3,) + p.shape[1:], p.dtype) for p in pair_sums],
                aliases={}, n_sems=3 * len(pair_sums), copies=copies)


def _reduce_finish_rider(sums, totals, layers, depth):
    def copies(rins, routs, send_sems, recv_sems):
        x, y, c = _coords()
        cps = []
        for q, (f_ref, o_ref) in enumerate(zip(rins[:len(sums)], routs, strict=True)):
            rh = f_ref.shape[0]
            cps.append(_remote(f_ref, o_ref.at[layers[q], pl.ds(c * rh, rh)], send_sems, recv_sems, q, (x, y, 1 - c)))
        return cps

    kept = [q for q, t in enumerate(totals) if t is not None]
    outs = [jax.ShapeDtypeStruct((depth, 2 * f.shape[0], f.shape[1]), F32) for f in sums]
    return dict(ins=list(sums) + [totals[q] for q in kept], outs=outs,
                aliases={len(sums) + k: q for k, q in enumerate(kept)}, n_sems=len(sums), copies=copies)


def _pair_add(gj, r1, ids, name, tr=256):
    stacked = gj.ndim == 3
    nj, rh, c = r1.shape
    tr = _fit(tr, rh, 16)
    nt = rh // tr

    def body(c_ref, chip_ref, g_ref, r_ref, p_ref, pb_ref):
        s = (g_ref[0] if stacked else g_ref[...]) + r_ref[0]
        pb_ref[0] = s.astype(BF16)

        @pl.when(pl.program_id(1) == chip_ref[0])
        def _():
            p_ref[...] = s

    blk_r = pl.BlockSpec((1, tr, c), lambda i, j, cr, jr: (j, i, 0))
    blk_g = (pl.BlockSpec((1, tr, c), lambda i, j, cr, jr: (j, cr[0] * nt + i, 0)) if stacked
             else pl.BlockSpec((tr, c), lambda i, j, cr, jr: (cr[0] * nt + i, j)))
    return pl.pallas_call(
        body, name=name,
        grid_spec=pltpu.PrefetchScalarGridSpec(
            num_scalar_prefetch=2, grid=(nt, nj), in_specs=[blk_g, blk_r],
            out_specs=[pl.BlockSpec((tr, c), lambda i, j, cr, jr: (i, 0)), blk_r]),
        out_shape=[jax.ShapeDtypeStruct((rh, c), F32), jax.ShapeDtypeStruct((nj, rh, c), BF16)],
        compiler_params=_params(("parallel", "arbitrary")),
    )(*ids, gj, r1)


def _chip_add(p, r2, name, tr=256):
    rh, c = p.shape
    tr = _fit(tr, rh, 16)

    def body(o_ref, r_ref, f_ref):
        f_ref[...] = ((o_ref[...] + r_ref[0].astype(F32)) + r_ref[1].astype(F32)) + r_ref[2].astype(F32)

    blk = pl.BlockSpec((tr, c), lambda i: (i, 0))
    return pl.pallas_call(
        body, name=name, grid=(rh // tr,),
        in_specs=[blk, pl.BlockSpec((3, tr, c), lambda i: (0, i, 0))], out_specs=blk,
        out_shape=jax.ShapeDtypeStruct((rh, c), F32),
        compiler_params=_params(("parallel",)),
    )(p, r2)


BIG = ["w_in", "w_out", "w_gate", "w_up", "w_down"]
LATE = ["w_out", "w_gate", "w_up", "w_down"]
FFN = ["w_gate", "w_up", "w_down"]
MIX = ["w_in", "w_out"]


SIDE_BY_SIDE = ("w_gate", "w_up")


class _GatherPlan:
    def __init__(self, late, next_in):
        self.late, self.next_in, self.parts = late, next_in, {}

    ICI = {"attention_fwd": ["w_out", "w_gate", "w_up"], "ssd_fwd": ["w_down"], "ffn_gate_up": ["w_in"]}
    D2D = {"merge_fwd": ["w_out", "w_gate"], "proj_out": ["w_up", "w_down"], "ffn_down": ["w_in"]}

    def _names(self, host):
        names = self.ICI.get(host) or self.D2D.get(host)
        return None if self.late is None or (names == ["w_in"] and self.next_in is None) else names

    def rider(self, host):
        names = self._names(host)
        if names is None:
            return None
        if host in self.D2D:
            return _gather_pair_rider([self.parts[n] for n in names])
        shards = [self.next_in] if names == ["w_in"] else [self.late[n] for n in names]
        return _gather_chips_rider(shards, [n in SIDE_BY_SIDE for n in names])

    def collect(self, host, outs):
        if outs:
            self.parts.update(zip(self._names(host), outs, strict=True))

    def late_gathered(self, names):
        return {n: self.parts[n] for n in names}

    def next_gathered(self):
        return self.parts["w_in"]


def _gather_now(shard, tag):
    part = _standalone(_gather_chips_rider([shard], [False]), f"allgather_chips_{tag}")
    return _standalone(_gather_pair_rider(part), f"allgather_pair_{tag}")[0]


class _ReducePlan:
    PAIR = {"ffn_down_dgrad": "mix", "ffn_gate_up_dgrad": "ffn"}
    CHIPS = {"ssd_bwd": "mix", "attention_bwd": "ffn"}

    def __init__(self, mix, mix_layer, ffn_layer, totals, depth, ids):
        self.groups = {} if mix is None else {"mix": (MIX, mix, mix_layer)}
        self.ffn_layer, self.totals, self.depth, self.ids = ffn_layer, dict(totals), depth, ids
        self.p, self.pb, self.f = {}, {}, {}

    def add_ffn(self, grads):
        if self.ids is not None:
            self.groups["ffn"] = (FFN, grads, self.ffn_layer)

    def _present(self):
        return [(n, layer) for names, _, layer in self.groups.values() for n in names if n in self.f]

    def rider(self, host):
        if host == "proj_in_dgrad":
            done = self._present()
            if not done:
                return None
            return _reduce_finish_rider([self.f[n] for n, _ in done], [self.totals.get(n) for n, _ in done],
                                        [layer for _, layer in done], self.depth)
        group = self.groups.get(self.PAIR.get(host) or self.CHIPS.get(host))
        if group is None:
            return None
        names, grads, _ = group
        return (_reduce_pair_rider([grads[n] for n in names]) if host in self.PAIR
                else _reduce_chips_rider([self.pb[n] for n in names]))

    def collect(self, host, outs):
        if not outs:
            return
        if host == "proj_in_dgrad":
            c = lax.axis_index("c")
            for (n, layer), t in zip(self._present(), outs, strict=True):
                self.totals[n] = lax.dynamic_update_slice(t, self.f[n][None], (layer, c * self.f[n].shape[0], 0))
            return
        names, grads, layer = self.groups[self.PAIR.get(host) or self.CHIPS.get(host)]
        for n, got in zip(names, outs, strict=True):
            if host in self.PAIR:
                self.p[n], self.pb[n] = _pair_add(grads[n], got, self.ids, f"rs_pair_add_{n}_layer{layer}")
            else:
                self.f[n] = _chip_add(self.p[n], got, f"rs_chip_add_{n}_layer{layer}")

    def run_now(self, tag):
        self.collect("ffn_down_dgrad", _standalone(self.rider("ffn_down_dgrad"), f"rs_pair_{tag}"))
        self.collect("ssd_bwd", _standalone(self.rider("ssd_bwd"), f"rs_chips_{tag}"))
        self.collect("proj_in_dgrad", _standalone(self.rider("proj_in_dgrad"), f"rs_finish_{tag}"))
        return self.totals


def _small_exchange(v, name, reduce):
    rows = v.shape[0]

    def body(v_ref, o_ref, *rest):
        buf = rest[0] if reduce else o_ref
        send_sems, recv_sems = rest[-2], rest[-1]
        x, y, c = _coords()
        me = 4 * x + 2 * y + c
        buf[me] = v_ref[...]
        cps = []
        for r in range(1, N_DEV):
            peer = (lax.bitwise_xor(x, (r >> 2) & 1), lax.bitwise_xor(y, (r >> 1) & 1), lax.bitwise_xor(c, r & 1))
            cps.append(_remote(v_ref, buf.at[me], send_sems, recv_sems, r - 1, peer))
        for cp in cps:
            cp.start()
        for r in range(1, N_DEV):
            src = buf.at[lax.bitwise_xor(me, r)]
            _remote(src, src, send_sems, recv_sems, r - 1, (x, y, c)).wait_recv()
        for cp in cps:
            cp.wait_send()
        if reduce:
            acc = buf[0]
            for d in range(1, N_DEV):
                acc = acc + buf[d]
            o_ref[...] = acc
            o_ref[0:1, :] = jnp.broadcast_to(jnp.sum(acc[0:1, :], axis=1, keepdims=True), (1, LANES))

    scratch = [pltpu.SemaphoreType.DMA((N_DEV - 1,)), pltpu.SemaphoreType.DMA((N_DEV - 1,))]
    if reduce:
        scratch = [pltpu.VMEM((N_DEV, rows, LANES), F32)] + scratch
    out_shape = (rows, LANES) if reduce else (N_DEV, rows, LANES)
    return pl.pallas_call(
        body, name=name, in_specs=[VMEM], out_specs=VMEM,
        out_shape=jax.ShapeDtypeStruct(out_shape, F32), scratch_shapes=scratch,
    )(v)


def _pack(parts):
    flat = []
    for a in parts:
        a = a.reshape(-1)
        flat.append(jnp.pad(a, (0, (-a.shape[0]) % LANES)))
    v = jnp.concatenate(flat)
    v = jnp.pad(v, (0, (-v.shape[0]) % (SUBLANES * LANES)))
    return v.reshape(-1, LANES)


def _unpack(slab, shapes):
    flat = slab.reshape(-1)
    out, off = [], 0
    for shp in shapes:
        size = 1
        for d in shp:
            size *= d
        out.append(flat[off:off + size].reshape(shp))
        off += size + (-size) % LANES
    return out


def _group_slots(a, hg):
    lead = a.shape[:-1]
    a = a.reshape(lead + (SSM_GROUPS, hg))
    a = jnp.pad(a, [(0, 0)] * len(lead) + [(0, 0), (0, LANES - hg)])
    return a.reshape(lead + (SSM_GROUPS * LANES,))


def _ungroup_slots(a, hg):
    lead = a.shape[:-1]
    return a.reshape(lead + (SSM_GROUPS, LANES))[..., :hg].reshape(lead + (SSM_GROUPS * hg,))


def _layer_fwd(x, p, bsz, s, plan, late_params=None):
    t, d = x.shape
    aw, sw, cd, hg = p["aw"], p["sw"], p["cd"], p["hg"]
    h = _rmsnorm_fwd(x, p["norm_mix"], "norm_mix_fwd")
    qkv = _mm(h, p["wqkv"], "nn", t, 3 * aw, F32, "proj_qkv", tm=1024, tn=512)
    z = _mm(h, p["wz"], "nn", t, sw, F32, "proj_z", tm=1024, tn=512)
    xbc = _mm(h, p["wxbc"], "nn", t, cd, F32, "proj_xbc", tm=1024, tn=512)
    dtraw = _mm(h, p["wdt"], "nn", t, SSM_GROUPS * LANES, F32, "proj_dt", tm=1024, tn=SSM_GROUPS * LANES)
    qkv3 = qkv.reshape(bsz, s, 3 * aw)
    (o_att, rtot), sent = _attention_fwd(qkv3, p["q_gain"], p["k_gain"], "attention_fwd",
                                         rider=plan.rider("attention_fwd"))
    plan.collect("attention_fwd", sent)
    xbc3 = xbc.reshape(bsz, s, cd)
    xact = _conv_fwd(xbc3, p["conv_w"], p["conv_b"], "conv_fwd")
    dt3 = dtraw.reshape(bsz, s, SSM_GROUPS * LANES)
    (y, hs), sent = _ssd_fwd(xact, dt3, p["dt_bias"], p["a_log"], p["d_skip"], hg, "ssd_fwd",
                             rider=plan.rider("ssd_fwd"))
    plan.collect("ssd_fwd", sent)
    o2, y2 = o_att.reshape(t, aw), y.reshape(t, sw)
    (mix,), sent = _merge_fwd(o2, y2, z, p["attn_out_gain"], p["ssm_out_gain"], "merge_fwd",
                              rider=plan.rider("merge_fwd"))
    plan.collect("merge_fwd", sent)
    if plan.late is not None:
        p = {**p, **late_params(plan.late_gathered(["w_out", "w_gate"]))}
    (x1,), _ = _hosted_matmul([[(mix, p["wout"], "nn")]], [x], lambda accs, ex: (accs[0] + ex[0],),
                              [F32], t, d, 1024, 512, "proj_out", plan)
    if plan.late is not None:
        p = {**p, **late_params(plan.late_gathered(["w_up", "w_down"]))}
    dff = p["wg"].shape[1]
    h2 = _rmsnorm_fwd(x1, p["norm_ffn"], "norm_ffn_fwd")
    (gate, up, act), _ = _hosted_matmul(
        [[(h2, p["wg"], "nn")], [(h2, p["wu"], "nn")]], [], _swiglu_fwd_epilogue,
        [F32, F32, BF16], t, dff, 1024, 512, "ffn_gate_up", plan)
    (x2,), _ = _hosted_matmul([[(act, p["wd"], "nn")]], [x1], lambda accs, ex: (accs[0] + ex[0],),
                              [F32], t, d, 512, 512, "ffn_down", plan)
    saved = dict(x=x, h=h, qkv3=qkv3, z=z, xbc3=xbc3, dt3=dt3, o2=o2, rtot=rtot, xact=xact, hs=hs, y2=y2,
                 mix=mix, x1=x1, h2=h2, gate=gate, up=up, act=act)
    return x2, saved, p


def _hosted_matmul(groups, extras, epilogue, out_dtypes, m, n, tm, tn, name, plan):
    rider = plan.rider(name)
    if rider is None:
        return _matmul(groups, extras, epilogue, out_dtypes, m, n, tm, tn, name), []
    outs, sent = _matmul(groups, extras, epilogue, out_dtypes, m, n, tm, tn, name, rider=rider)
    plan.collect(name, sent)
    return outs, sent


def _layer_bwd(dx2, dx2b, p, sv, bsz, s, plan, ffn_to_chips=None):
    t, d = dx2.shape
    aw, sw, cd, hg = p["aw"], p["sw"], p["cd"], p["hg"]
    dff = p["wg"].shape[1]
    gr = {}
    (dgate, dup), _ = _hosted_matmul([[(dx2b, p["wd"], "nt")]], [sv["gate"], sv["up"]], _swiglu_bwd_epilogue,
                                     [BF16, BF16], t, dff, 1024, 512, "ffn_down_dgrad", plan)
    gr["wd"] = _mm(sv["act"], dx2b, "tn", dff, d, F32, "ffn_down_wgrad")
    gr["wg"] = _mm(sv["h2"], dgate, "tn", d, dff, F32, "ffn_gate_wgrad")
    gr["wu"] = _mm(sv["h2"], dup, "tn", d, dff, F32, "ffn_up_wgrad")
    if ffn_to_chips is not None:
        plan.add_ffn(ffn_to_chips(gr))
    (dh2,), _ = _hosted_matmul([[(dgate, p["wg"], "nt"), (dup, p["wu"], "nt")]], [], lambda accs, ex: (accs[0],),
                               [F32], t, d, 512, 256, "ffn_gate_up_dgrad", plan)
    dx1, dx1b, gr["norm_ffn"] = _rmsnorm_bwd(sv["x1"], p["norm_ffn"], dh2, dx2, "norm_ffn_bwd")
    dmix = _mm(dx1b, p["wout"], "nt", t, aw + sw, F32, "proj_out_dgrad", tm=1024)
    gr["wout"] = _mm(sv["mix"], dx1b, "tn", aw + sw, d, F32, "proj_out_wgrad")
    do, dy, dz, gr["attn_out_gain"], gr["ssm_out_gain"] = _merge_bwd(
        sv["o2"], sv["y2"], sv["z"], p["attn_out_gain"], p["ssm_out_gain"], dmix, "merge_bwd")
    (dxs, dbm, dcm, ddt, dvec), sent = _ssd_bwd(sv["xact"], sv["dt3"], p["dt_bias"], p["a_log"], p["d_skip"],
                                                sv["hs"], dy.reshape(bsz, s, sw), hg, "ssd_bwd",
                                                rider=plan.rider("ssd_bwd"))
    plan.collect("ssd_bwd", sent)
    dvec = jnp.sum(dvec, axis=0).reshape(SSM_GROUPS, SUBLANES, LANES)
    gr["dt_bias"], gr["a_log"], gr["d_skip"] = (dvec[:, k, :hg].reshape(-1) for k in range(3))
    dxact = jnp.concatenate([dxs, dbm, dcm], axis=-1)
    dxbc, gr["conv_w"], gr["conv_b"] = _conv_bwd(sv["xbc3"], p["conv_w"], p["conv_b"], dxact, "conv_bwd")
    (dqkv, dqg, dkg), sent = _attention_bwd(sv["qkv3"], p["q_gain"], p["k_gain"], sv["rtot"],
                                            do.reshape(bsz, s, aw), "attention_bwd",
                                            rider=plan.rider("attention_bwd"))
    plan.collect("attention_bwd", sent)
    gr["q_gain"] = jnp.sum(dqg, axis=(0, 1, 2))
    gr["k_gain"] = jnp.sum(dkg, axis=(0, 1, 2))
    dqkv, dxbc, ddt = dqkv.reshape(t, 3 * aw), dxbc.reshape(t, cd), ddt.reshape(t, SSM_GROUPS * LANES)
    (dh,), _ = _hosted_matmul(
        [[(dqkv, p["wqkv"], "nt"), (dz, p["wz"], "nt"), (dxbc, p["wxbc"], "nt"), (ddt, p["wdt"], "nt")]],
        [], lambda accs, ex: (accs[0],), [F32], t, d, 512, 512, "proj_in_dgrad", plan)
    h = sv["h"]
    gr["wqkv"] = _mm(h, dqkv, "tn", d, 3 * aw, F32, "proj_qkv_wgrad")
    gr["wz"] = _mm(h, dz, "tn", d, sw, F32, "proj_z_wgrad")
    gr["wxbc"] = _mm(h, dxbc, "tn", d, cd, F32, "proj_xbc_wgrad")
    gr["wdt"] = _mm(h, ddt, "tn", d, SSM_GROUPS * LANES, F32, "proj_dt_wgrad", tn=SSM_GROUPS * LANES)
    dx, dxb, gr["norm_mix"] = _rmsnorm_bwd(sv["x"], p["norm_mix"], dh, dx1, "norm_mix_bwd")
    return dx, dxb, gr


SMALL = ["norm_mix", "q_gain", "k_gain", "conv_w", "conv_b", "dt_bias", "a_log", "d_skip",
         "attn_out_gain", "ssm_out_gain", "norm_ffn"]
ORDER = ["norm_mix", "w_in", "q_gain", "k_gain", "conv_w", "conv_b", "dt_bias", "a_log", "d_skip",
         "attn_out_gain", "ssm_out_gain", "w_out", "norm_ffn", "w_gate", "w_up", "w_down"]


def kernel(x, norm_mix, w_in, q_gain, k_gain, conv_w, conv_b, dt_bias, a_log, d_skip, attn_out_gain, ssm_out_gain, w_out, norm_ffn, w_gate, w_up, w_down, loss_target, m_norm_mix, m_w_in, m_q_gain, m_k_gain, m_conv_w, m_conv_b, m_dt_bias, m_a_log, m_d_skip, m_attn_out_gain, m_ssm_out_gain, m_w_out, m_norm_ffn, m_w_gate, m_w_up, m_w_down, v_norm_mix, v_w_in, v_q_gain, v_k_gain, v_conv_w, v_conv_b, v_dt_bias, v_a_log, v_d_skip, v_attn_out_gain, v_ssm_out_gain, v_w_out, v_norm_ffn, v_w_gate, v_w_up, v_w_down):
    w = dict(norm_mix=norm_mix, w_in=w_in, q_gain=q_gain, k_gain=k_gain, conv_w=conv_w, conv_b=conv_b,
             dt_bias=dt_bias, a_log=a_log, d_skip=d_skip, attn_out_gain=attn_out_gain, ssm_out_gain=ssm_out_gain,
             w_out=w_out, norm_ffn=norm_ffn, w_gate=w_gate, w_up=w_up, w_down=w_down)
    mom = dict(norm_mix=m_norm_mix, w_in=m_w_in, q_gain=m_q_gain, k_gain=m_k_gain, conv_w=m_conv_w,
               conv_b=m_conv_b, dt_bias=m_dt_bias, a_log=m_a_log, d_skip=m_d_skip,
               attn_out_gain=m_attn_out_gain, ssm_out_gain=m_ssm_out_gain, w_out=m_w_out, norm_ffn=m_norm_ffn,
               w_gate=m_w_gate, w_up=m_w_up, w_down=m_w_down)
    var = dict(norm_mix=v_norm_mix, w_in=v_w_in, q_gain=v_q_gain, k_gain=v_k_gain, conv_w=v_conv_w,
               conv_b=v_conv_b, dt_bias=v_dt_bias, a_log=v_a_log, d_skip=v_d_skip,
               attn_out_gain=v_attn_out_gain, ssm_out_gain=v_ssm_out_gain, w_out=v_w_out, norm_ffn=v_norm_ffn,
               w_gate=v_w_gate, w_up=v_w_up, w_down=v_w_down)

    bsz, s, d = x.shape
    t = bsz * s
    depth = norm_mix.shape[0]
    aw = attn_out_gain.shape[1]
    sw = ssm_out_gain.shape[1]
    cd = conv_b.shape[1]
    hs_n = dt_bias.shape[1]
    hg = hs_n // SSM_GROUPS
    heads = aw // ATT_HEAD_DIM
    in_dim = 3 * aw + sw + cd + hs_n
    dff = w_gate.shape[2] * N_CHIPS
    cs = conv_w.shape[2]
    my_chip = 2 * lax.axis_index("x") + lax.axis_index("y")
    ids = (lax.axis_index("c").astype(jnp.int32).reshape(1), my_chip.astype(jnp.int32).reshape(1))
    wb ={n: w[n].astype(BF16) for n in BIG}
    conv_all = _small_exchange(_pack([conv_w]), "allgather_conv_w", False)
    conv_full = jnp.concatenate(
        [_unpack(conv_all[2 * j], [conv_w.shape])[0] for j in range(N_CHIPS)], axis=-1)

    def in_params(l, gat_in):
        win = jnp.transpose(gat_in, (1, 0, 2)).reshape(d, in_dim)
        wqkv = win[:, :3 * aw].reshape(d, 3, heads, ATT_HEAD_DIM)
        wqkv = jnp.transpose(wqkv, (0, 2, 1, 3)).reshape(d, 3 * aw)
        return dict(aw=aw, sw=sw, cd=cd, hg=hg, norm_mix=norm_mix[l], wqkv=wqkv, wz=win[:, 3 * aw:3 * aw + sw],
                    wxbc=win[:, 3 * aw + sw:3 * aw + sw + cd], wdt=_group_slots(win[:, 3 * aw + sw + cd:], hg),
                    q_gain=q_gain[l], k_gain=k_gain[l], conv_w=conv_full[l], conv_b=conv_b[l],
                    dt_bias=_group_slots(dt_bias[l], hg).reshape(1, -1),
                    a_log=_group_slots(a_log[l], hg).reshape(1, -1),
                    d_skip=_group_slots(d_skip[l], hg).reshape(1, -1),
                    attn_out_gain=attn_out_gain[l], ssm_out_gain=ssm_out_gain[l], norm_ffn=norm_ffn[l])

    def late_params(gat):
        shape = {"w_out": ("wout", (aw + sw, d)), "w_down": ("wd", (dff, d)),
                 "w_gate": ("wg", (d, dff)), "w_up": ("wu", (d, dff))}
        return {shape[n][0]: a.reshape(shape[n][1]) for n, a in gat.items()}

    def ffn_to_chips(gr):
        return {"w_gate": gr["wg"], "w_up": gr["wu"], "w_down": gr["wd"].reshape(N_CHIPS, dff // N_CHIPS, d)}

    def mix_to_chips(gr):
        gqkv = gr["wqkv"].reshape(d, heads, 3, ATT_HEAD_DIM)
        gqkv = jnp.transpose(gqkv, (0, 2, 1, 3)).reshape(d, 3 * aw)
        gin = jnp.concatenate([gqkv, gr["wz"], gr["wxbc"], _ungroup_slots(gr["wdt"], hg)], axis=-1)
        return {"w_in": jnp.transpose(gin.reshape(d, N_CHIPS, in_dim // N_CHIPS), (1, 0, 2)),
                "w_out": gr["wout"].reshape(N_CHIPS, (aw + sw) // N_CHIPS, d)}

    xt = x.reshape(t, d)
    saved, params = [], []
    gat_in = _gather_now(wb["w_in"][0], "first")
    for l in range(depth):
        plan = _GatherPlan({n: wb[n][l] for n in LATE}, wb["w_in"][l + 1] if l + 1 < depth else None)
        xt, sv, p = _layer_fwd(xt, in_params(l, gat_in), bsz, s, plan, late_params)
        saved.append(sv)
        params.append(p)
        if l + 1 < depth:
            gat_in = plan.next_gathered()
    dxt, dxb, loss_lanes = _loss_head(xt, loss_target.reshape(t, d), "loss_head")

    grads = [None] * depth
    pending, totals = None, {}
    for l in reversed(range(depth)):
        plan = _ReducePlan(pending, l + 1, l, totals, depth, ids)
        dxt, dxb, grads[l] = _layer_bwd(dxt, dxb, params[l], saved[l], bsz, s, plan, ffn_to_chips)
        totals = plan.totals
        pending = mix_to_chips(grads[l])
    g = _ReducePlan(pending, 0, None, totals, depth, ids).run_now("first_layer")
    grad_x = dxt.reshape(bsz, s, d)

    def stack(name):
        return jnp.stack([grads[l][name] for l in range(depth)])

    small_shapes = [(1, LANES)] + [(depth, CONV_WIDTH, cd) if n == "conv_w" else w[n].shape for n in SMALL]
    small = _small_exchange(_pack([loss_lanes] + [stack(n) for n in SMALL]), "allreduce_small", True)
    small = _unpack(small, small_shapes)
    loss = small[0][0, 0]
    for n, a in zip(SMALL, small[1:], strict=True):
        g[n] = a
    g["conv_w"] = lax.dynamic_slice_in_dim(g["conv_w"], my_chip * cs, cs, axis=2)

    delta, new_m, new_v = {}, {}, {}
    for n in BIG:
        shp = w[n].shape
        two_d = (shp[0] * shp[1], shp[2])
        dl, nm, nv = _adamw(w[n].reshape(two_d), g[n].reshape(two_d), mom[n].reshape(two_d),
                            var[n].reshape(two_d), f"adamw_{n}")
        delta[n], new_m[n], new_v[n] = dl.reshape(shp), nm.reshape(shp), nv.reshape(shp)
    shapes = [w[n].shape for n in SMALL]
    dl, nm, nv = _adamw(_pack([w[n] for n in SMALL]), _pack([g[n] for n in SMALL]),
                        _pack([mom[n] for n in SMALL]), _pack([var[n] for n in SMALL]), "adamw_small")
    for n, a, b, c in zip(SMALL, _unpack(dl, shapes), _unpack(nm, shapes), _unpack(nv, shapes), strict=True):
        delta[n], new_m[n], new_v[n] = a, b, c

    return (loss, grad_x, *[g[n] for n in ORDER], *[delta[n] for n in ORDER],
            *[new_m[n] for n in ORDER], *[new_v[n] for n in ORDER])
```

```python
import jax
import jax.numpy as jnp
from jax import lax
from jax.experimental import pallas as pl
from jax.experimental.pallas import tpu as pltpu

F32 = jnp.float32
BF16 = jnp.bfloat16
MESH = pl.DeviceIdType.MESH
ANY = pl.BlockSpec(memory_space=pl.ANY)
VMEM = pl.BlockSpec(memory_space=pltpu.VMEM)

EPS = 1e-6
ATT_HEAD_DIM = 128
SSM_HEAD_DIM = 64
SSM_GROUPS = 2
SSM_STATE = 128
SSD_CHUNK = 128
CONV_WIDTH = 4
LANES = 128
SUBLANES = 8
ATT_TILE = 512
ATT_STRIP = 128
N_CHIPS = 4
N_DEV = 8

ADAM_LR = 0.001
ADAM_B1 = 0.9
ADAM_B2 = 0.999
ADAM_EPS = 1e-08
ADAM_WD = 0.01
ADAM_STEP = 10

VMEM_LIMIT = 48 * 1024 * 1024

NN = (((1,), (0,)), ((), ()))
NT = (((1,), (1,)), ((), ()))
TN = (((0,), (0,)), ((), ()))


def _dot(a, b, dims=NN):
    return lax.dot_general(a.astype(BF16), b.astype(BF16), dims, preferred_element_type=F32)


def _dot_exact(x, ones, dims=NN, passes=3, ones_left=False):
    acc = None
    rem = x
    for _ in range(passes):
        piece = rem.astype(BF16)
        rem = rem - piece.astype(F32)
        p = (lax.dot_general(ones, piece, dims, preferred_element_type=F32) if ones_left
             else lax.dot_general(piece, ones, dims, preferred_element_type=F32))
        acc = p if acc is None else acc + p
    return acc


def _scan_lanes(x, tri, passes, reverse=False):
    nblk = x.shape[1] // LANES
    blocks = [x[:, k * LANES:(k + 1) * LANES] for k in range(nblk)]
    out, carry = [None] * nblk, None
    for k in (reversed(range(nblk)) if reverse else range(nblk)):
        p = _dot_exact(blocks[k], tri, passes=passes)
        out[k] = p if carry is None else p + carry
        tot = jnp.sum(blocks[k], axis=1, keepdims=True)
        carry = tot if carry is None else carry + tot
    return (out[0] if nblk == 1 else jnp.concatenate(out, axis=1)), carry


def _iota2(shape, axis):
    return lax.broadcasted_iota(jnp.int32, shape, axis)


def _tri(n, cmp):
    return cmp(_iota2((n, n), 0), _iota2((n, n), 1)).astype(BF16)


def _sum_all(v):
    return jnp.sum(jnp.sum(v, axis=1, keepdims=True), axis=0, keepdims=True)


def _fit(tile, dim, unit=LANES):
    if dim <= tile:
        return dim
    return max(k for k in range(unit, tile + 1, unit) if dim % k == 0)


def _params(sem):
    return pltpu.CompilerParams(dimension_semantics=sem, vmem_limit_bytes=VMEM_LIMIT)


def _call(body, *, name, grid, in_specs, out_specs, out_shape, sem, args, scratch_shapes=(), rider=None):
    in_specs, out_specs, out_shape = list(in_specs), list(out_specs), list(out_shape)
    scratch_shapes = list(scratch_shapes)
    if rider is None:
        res = pl.pallas_call(body, name=name, grid=grid, in_specs=in_specs, out_specs=out_specs,
                             out_shape=out_shape, scratch_shapes=scratch_shapes,
                             compiler_params=_params(sem))(*args)
        return list(res), []
    n_in, n_out, n_scr = len(in_specs), len(out_specs), len(scratch_shapes)
    r_in, r_out, n_sems = len(rider["ins"]), len(rider["outs"]), rider["n_sems"]

    def hosted(*refs):
        ins, rest = refs[:n_in], refs[n_in:]
        rins, rest = rest[:r_in], rest[r_in:]
        outs, rest = rest[:n_out], rest[n_out:]
        routs, rest = rest[:r_out], rest[r_out:]
        scr, (send_sems, recv_sems) = rest[:n_scr], rest[n_scr:]
        first, last = None, None
        for d, size in enumerate(grid):
            f, e = pl.program_id(d) == 0, pl.program_id(d) == size - 1
            first = f if first is None else jnp.logical_and(first, f)
            last = e if last is None else jnp.logical_and(last, e)

        @pl.when(first)
        def _():
            for cp in rider["copies"](rins, routs, send_sems, recv_sems):
                cp.start()

        body(*ins, *outs, *scr)

        @pl.when(last)
        def _():
            for cp in rider["copies"](rins, routs, send_sems, recv_sems):
                cp.wait()

    res = pl.pallas_call(
        hosted, name=name, grid=grid, in_specs=in_specs + [ANY] * r_in, out_specs=out_specs + [ANY] * r_out,
        out_shape=out_shape + list(rider["outs"]),
        scratch_shapes=scratch_shapes + [pltpu.SemaphoreType.DMA((n_sems,)), pltpu.SemaphoreType.DMA((n_sems,))],
        input_output_aliases={n_in + i: n_out + o for i, o in rider["aliases"].items()},
        compiler_params=_params(("arbitrary",) * len(grid)),
    )(*args, *rider["ins"])
    return list(res[:n_out]), list(res[n_out:])


def _softplus(x):
    return jnp.maximum(x, 0.0) + jnp.log(1.0 + jnp.exp(-jnp.abs(x)))


def _sigmoid(x):
    return 1.0 / (1.0 + jnp.exp(-x))


def _rms_fwd(x, g):
    r = lax.rsqrt(jnp.mean(x * x, axis=-1, keepdims=True) + EPS)
    return (x * r) * g


def _rms_bwd(x, g, dh):
    r = lax.rsqrt(jnp.mean(x * x, axis=-1, keepdims=True) + EPS)
    y = x * r
    dy = dh * g
    dx = r * (dy - y * jnp.mean(dy * y, axis=-1, keepdims=True))
    return dx, dh * y


def _matmul(groups, extras, epilogue, out_dtypes, m, n, tm, tn, name, rider=None):
    tm, tn = _fit(tm, m), _fit(tn, n)
    flat = [t for grp in groups for t in grp]
    n_terms, n_extra = len(flat), len(extras)

    def body(*refs):
        outs = refs[2 * n_terms + n_extra:]
        accs, pos = [], 0
        for grp in groups:
            acc = None
            for (_, _, mode) in grp:
                dims = {"nn": NN, "nt": NT, "tn": TN}[mode]
                p = _dot(refs[2 * pos][...], refs[2 * pos + 1][...], dims)
                acc = p if acc is None else acc + p
                pos += 1
            accs.append(acc)
        ex = [refs[2 * n_terms + i][...] for i in range(n_extra)]
        res = epilogue(accs, ex)
        for o_ref, r in zip(outs, res, strict=True):
            o_ref[...] = r.astype(o_ref.dtype)

    in_specs, args = [], []
    for (a, b, mode) in flat:
        if mode == "nn":
            k = a.shape[1]
            in_specs += [pl.BlockSpec((tm, k), lambda i, j: (i, 0)), pl.BlockSpec((k, tn), lambda i, j: (0, j))]
        elif mode == "nt":
            k = a.shape[1]
            in_specs += [pl.BlockSpec((tm, k), lambda i, j: (i, 0)), pl.BlockSpec((tn, k), lambda i, j: (j, 0))]
        else:
            k = a.shape[0]
            in_specs += [pl.BlockSpec((k, tm), lambda i, j: (0, i)), pl.BlockSpec((k, tn), lambda i, j: (0, j))]
        args += [a, b]
    for e in extras:
        in_specs.append(pl.BlockSpec((tm, tn), lambda i, j: (i, j)))
        args.append(e)
    outs, routs = _call(
        body, name=name, grid=(m // tm, n // tn), in_specs=in_specs,
        out_specs=[pl.BlockSpec((tm, tn), lambda i, j: (i, j)) for _ in out_dtypes],
        out_shape=[jax.ShapeDtypeStruct((m, n), d) for d in out_dtypes],
        sem=("parallel", "parallel"), args=args, rider=rider)
    return outs if rider is None else (outs, routs)


def _mm(a, b, mode, m, n, out_dtype, name, tm=512, tn=512, res=None):
    extras = [] if res is None else [res]
    epi = (lambda accs, ex: (accs[0],)) if res is None else (lambda accs, ex: (accs[0] + ex[0],))
    return _matmul([[(a, b, mode)]], extras, epi, [out_dtype], m, n, tm, tn, name)[0]


def _swiglu_fwd_epilogue(accs, ex):
    g, u = accs
    return g, u, (g * _sigmoid(g)) * u


def _swiglu_bwd_epilogue(accs, ex):
    dact, (g, u) = accs[0], ex
    sg = _sigmoid(g)
    silu = g * sg
    return dact * u * (sg * (1.0 + g * (1.0 - sg))), dact * silu


def _rmsnorm_fwd(x, g, name, tr=512):
    t, d = x.shape
    tr = min(tr, t)

    def body(x_ref, g_ref, h_ref):
        h_ref[...] = _rms_fwd(x_ref[...], g_ref[...]).astype(BF16)

    return pl.pallas_call(
        body, name=name, grid=(t // tr,),
        in_specs=[pl.BlockSpec((tr, d), lambda i: (i, 0)), pl.BlockSpec((1, d), lambda i: (0, 0))],
        out_specs=pl.BlockSpec((tr, d), lambda i: (i, 0)),
        out_shape=jax.ShapeDtypeStruct((t, d), BF16),
        compiler_params=_params(("parallel",)),
    )(x, g.reshape(1, d))


def _rmsnorm_bwd(x, g, dh, dres, name, tr=256):
    t, d = x.shape
    tr = min(tr, t)

    def body(x_ref, g_ref, dh_ref, dres_ref, dx_ref, dxb_ref, dg_ref):
        dx, dgr = _rms_bwd(x_ref[...], g_ref[...], dh_ref[...])
        dx = dx + dres_ref[...]
        dx_ref[...] = dx
        dxb_ref[...] = dx.astype(BF16)

        @pl.when(pl.program_id(0) == 0)
        def _():
            dg_ref[...] = jnp.zeros_like(dg_ref)

        dg_ref[...] += jnp.sum(dgr, axis=0, keepdims=True)

    row = pl.BlockSpec((tr, d), lambda i: (i, 0))
    vec = pl.BlockSpec((1, d), lambda i: (0, 0))
    dx, dxb, dg = pl.pallas_call(
        body, name=name, grid=(t // tr,),
        in_specs=[row, vec, row, row], out_specs=[row, row, vec],
        out_shape=[jax.ShapeDtypeStruct((t, d), F32), jax.ShapeDtypeStruct((t, d), BF16),
                   jax.ShapeDtypeStruct((1, d), F32)],
        compiler_params=_params(("arbitrary",)),
    )(x, g.reshape(1, d), dh, dres)
    return dx, dxb, dg.reshape(d)


def _merge_fwd(o_att, y, z, ga, gs, name, tr=256, rider=None):
    t, wa = o_att.shape
    ws = y.shape[1]
    wg = ws // SSM_GROUPS
    tr = min(tr, t)

    def body(o_ref, y_ref, z_ref, ga_ref, gs_ref, m_ref):
        m_ref[:, 0:wa] = _rms_fwd(o_ref[...], ga_ref[...]).astype(BF16)
        for g in range(SSM_GROUPS):
            sl = slice(g * wg, (g + 1) * wg)
            zz = z_ref[:, sl]
            yz = y_ref[:, sl] * (zz * _sigmoid(zz))
            m_ref[:, wa + g * wg:wa + (g + 1) * wg] = _rms_fwd(yz, gs_ref[:, sl]).astype(BF16)

    return _call(
        body, name=name, grid=(t // tr,),
        in_specs=[pl.BlockSpec((tr, wa), lambda i: (i, 0)), pl.BlockSpec((tr, ws), lambda i: (i, 0)),
                  pl.BlockSpec((tr, ws), lambda i: (i, 0)), pl.BlockSpec((1, wa), lambda i: (0, 0)),
                  pl.BlockSpec((1, ws), lambda i: (0, 0))],
        out_specs=[pl.BlockSpec((tr, wa + ws), lambda i: (i, 0))],
        out_shape=[jax.ShapeDtypeStruct((t, wa + ws), BF16)],
        sem=("parallel",), args=(o_att, y, z, ga.reshape(1, wa), gs.reshape(1, ws)), rider=rider)


def _merge_bwd(o_att, y, z, ga, gs, dmix, name, tr=256):
    t, wa = o_att.shape
    ws = y.shape[1]
    wg = ws // SSM_GROUPS
    tr = min(tr, t)

    def body(o_ref, y_ref, z_ref, ga_ref, gs_ref, dm_ref, do_ref, dy_ref, dz_ref, dga_ref, dgs_ref):
        @pl.when(pl.program_id(0) == 0)
        def _():
            dga_ref[...] = jnp.zeros_like(dga_ref)
            dgs_ref[...] = jnp.zeros_like(dgs_ref)

        do, dgr = _rms_bwd(o_ref[...], ga_ref[...], dm_ref[:, 0:wa])
        do_ref[...] = do
        dga_ref[...] += jnp.sum(dgr, axis=0, keepdims=True)
        for g in range(SSM_GROUPS):
            sl = slice(g * wg, (g + 1) * wg)
            zz, yy = z_ref[:, sl], y_ref[:, sl]
            sg = _sigmoid(zz)
            silu = zz * sg
            dyz, dgr = _rms_bwd(yy * silu, gs_ref[:, sl], dm_ref[:, wa + g * wg:wa + (g + 1) * wg])
            dy_ref[:, sl] = dyz * silu
            dz_ref[:, sl] = (dyz * yy * (sg + silu * (1.0 - sg))).astype(BF16)
            dgs_ref[:, sl] += jnp.sum(dgr, axis=0, keepdims=True)

    rowa = pl.BlockSpec((tr, wa), lambda i: (i, 0))
    rows = pl.BlockSpec((tr, ws), lambda i: (i, 0))
    veca = pl.BlockSpec((1, wa), lambda i: (0, 0))
    vecs = pl.BlockSpec((1, ws), lambda i: (0, 0))
    do, dy, dz, dga, dgs = pl.pallas_call(
        body, name=name, grid=(t // tr,),
        in_specs=[rowa, rows, rows, veca, vecs, pl.BlockSpec((tr, wa + ws), lambda i: (i, 0))],
        out_specs=[rowa, rows, rows, veca, vecs],
        out_shape=[jax.ShapeDtypeStruct((t, wa), F32), jax.ShapeDtypeStruct((t, ws), F32),
                   jax.ShapeDtypeStruct((t, ws), BF16), jax.ShapeDtypeStruct((1, wa), F32),
                   jax.ShapeDtypeStruct((1, ws), F32)],
        compiler_params=_params(("arbitrary",)),
    )(o_att, y, z, ga.reshape(1, wa), gs.reshape(1, ws), dmix)
    return do, dy, dz, dga.reshape(wa), dgs.reshape(ws)


def _loss_head(y, target, name, tr=256):
    t, d = y.shape
    tr = min(tr, t)

    def body(y_ref, t_ref, dy_ref, dyb_ref, l_ref):
        @pl.when(pl.program_id(0) == 0)
        def _():
            l_ref[...] = jnp.zeros_like(l_ref)

        diff = y_ref[...] - t_ref[...]
        dy = diff * (1.0 / d)
        dy_ref[...] = dy
        dyb_ref[...] = dy.astype(BF16)
        part = jnp.sum(diff * diff, axis=0, keepdims=True)
        fold = part[:, 0:LANES]
        for k in range(1, d // LANES):
            fold = fold + part[:, k * LANES:(k + 1) * LANES]
        l_ref[...] += fold * (0.5 / d)

    row = pl.BlockSpec((tr, d), lambda i: (i, 0))
    return pl.pallas_call(
        body, name=name, grid=(t // tr,), in_specs=[row, row],
        out_specs=[row, row, pl.BlockSpec((1, LANES), lambda i: (0, 0))],
        out_shape=[jax.ShapeDtypeStruct((t, d), F32), jax.ShapeDtypeStruct((t, d), BF16),
                   jax.ShapeDtypeStruct((1, LANES), F32)],
        compiler_params=_params(("arbitrary",)),
    )(y, target)


def _adamw(w, g, m, v, name, tr=256):
    r, c = w.shape
    tr = _fit(tr, r, 16)

    def body(w_ref, g_ref, m_ref, v_ref, d_ref, nm_ref, nv_ref):
        gg = g_ref[...]
        nm = ADAM_B1 * m_ref[...] + (1.0 - ADAM_B1) * gg
        nv = ADAM_B2 * v_ref[...] + (1.0 - ADAM_B2) * (gg * gg)
        m_hat = nm / (1.0 - ADAM_B1 ** ADAM_STEP)
        v_hat = nv / (1.0 - ADAM_B2 ** ADAM_STEP)
        d_ref[...] = -ADAM_LR * (m_hat / (jnp.sqrt(v_hat) + ADAM_EPS) + ADAM_WD * w_ref[...])
        nm_ref[...] = nm
        nv_ref[...] = nv

    blk = pl.BlockSpec((tr, c), lambda i: (i, 0))
    return pl.pallas_call(
        body, name=name, grid=(r // tr,), in_specs=[blk] * 4, out_specs=[blk] * 3,
        out_shape=[jax.ShapeDtypeStruct((r, c), F32)] * 3,
        compiler_params=_params(("parallel",)),
    )(w, g, m, v)


def _att_scores(qi, kj, scale, row0):
    z = _dot(qi, kj, NT) * scale
    lb = -_softplus(-z)
    lrm = lb - z
    if row0 is None:
        return lb, lrm, None
    mask = _iota2(z.shape, 1) < _iota2(z.shape, 0) + row0
    return lb, jnp.where(mask, lrm, 0.0), mask


def _masked(mask, v):
    return v if mask is None else jnp.where(mask, v, 0.0)


def _attention_fwd(qkv, qg, kg, name, tile=None, rider=None):
    bsz, s, w3 = qkv.shape
    hd = ATT_HEAD_DIM
    heads = w3 // (3 * hd)
    tile = min(tile or ATT_TILE, s)
    strip = min(ATT_STRIP, tile)
    nb = s // tile
    scale = hd ** -0.5

    def body(qkv_ref, qg_ref, kg_ref, o_ref, r_ref, qn_s, kn_s, vb_s, acc_s, c_s):
        qn_s[...] = _rms_fwd(qkv_ref[0, :, 0:hd], qg_ref[...]).astype(BF16)
        kn_s[...] = _rms_fwd(qkv_ref[0, :, hd:2 * hd], kg_ref[...]).astype(BF16)
        vb_s[...] = qkv_ref[0, :, 2 * hd:3 * hd].astype(BF16)
        after = _tri(LANES, lambda r, c: r > c)

        def q_loop(i, _):
            rows = pl.ds(pl.multiple_of(i * tile, tile), tile)
            acc_s[...] = jnp.zeros_like(acc_s)
            c_s[...] = jnp.zeros_like(c_s)

            def key_tile(j, diagonal):
                cols = pl.ds(pl.multiple_of(j * tile, tile), tile)
                kj, vj = kn_s[cols, :], vb_s[cols, :]
                strips = range(tile // strip)
                subs = [slice(r * strip, (r + 1) * strip) for r in strips]
                srows = [pl.ds(pl.multiple_of(i * tile + r * strip, strip), strip) for r in strips]
                sc = [_att_scores(qn_s[srows[r], :], kj, scale, r * strip if diagonal else None) for r in strips]
                later = [_scan_lanes(sc[r][1], after, 2, reverse=True) for r in strips]
                for r in strips:
                    w = _masked(sc[r][2], jnp.exp(sc[r][0] + (later[r][0] + c_s[subs[r], :])))
                    acc_s[subs[r], :] += _dot(w, vj)
                    c_s[subs[r], :] += later[r][1]

            def k_loop(jj, _):
                key_tile(i - jj, False)
                return 0

            key_tile(i, True)
            lax.fori_loop(1, i + 1, k_loop, 0)
            o_ref[0, rows, :] = acc_s[...]
            r_ref[0, 0, rows, :] = c_s[...]
            return 0

        lax.fori_loop(0, nb, q_loop, 0)

    return _call(
        body, name=name, grid=(bsz, heads),
        in_specs=[pl.BlockSpec((1, s, 3 * hd), lambda b, h: (b, 0, h)),
                  pl.BlockSpec((1, hd), lambda b, h: (0, 0)), pl.BlockSpec((1, hd), lambda b, h: (0, 0))],
        out_specs=[pl.BlockSpec((1, s, hd), lambda b, h: (b, 0, h)),
                   pl.BlockSpec((1, 1, s, 1), lambda b, h: (b, h, 0, 0))],
        out_shape=[jax.ShapeDtypeStruct((bsz, s, heads * hd), F32),
                   jax.ShapeDtypeStruct((bsz, heads, s, 1), F32)],
        scratch_shapes=[pltpu.VMEM((s, hd), BF16), pltpu.VMEM((s, hd), BF16), pltpu.VMEM((s, hd), BF16),
                        pltpu.VMEM((tile, hd), F32), pltpu.VMEM((tile, 1), F32)],
        sem=("parallel", "parallel"), args=(qkv, qg.reshape(1, hd), kg.reshape(1, hd)), rider=rider)


def _attention_bwd(qkv, qg, kg, rtot, do, name, tile=None, rider=None):
    bsz, s, w3 = qkv.shape
    hd = ATT_HEAD_DIM
    heads = w3 // (3 * hd)
    tile = min(tile or ATT_TILE, s)
    strip = min(ATT_STRIP, tile)
    nb = s // tile
    scale = hd ** -0.5

    def body(qkv_ref, qg_ref, kg_ref, r_ref, do_ref, dqkv_ref, dqg_ref, dkg_ref,
             qn_s, kn_s, vb_s, dob_s, dqn_s, dkn_s, dv_s, c1_s, c2_s, wb_s, dzb_s):
        qn_s[...] = _rms_fwd(qkv_ref[0, :, 0:hd], qg_ref[...]).astype(BF16)
        kn_s[...] = _rms_fwd(qkv_ref[0, :, hd:2 * hd], kg_ref[...]).astype(BF16)
        vb_s[...] = qkv_ref[0, :, 2 * hd:3 * hd].astype(BF16)
        dob_s[...] = do_ref[0].astype(BF16)
        dqn_s[...] = jnp.zeros_like(dqn_s)
        dkn_s[...] = jnp.zeros_like(dkn_s)
        dv_s[...] = jnp.zeros_like(dv_s)
        upto = _tri(LANES, lambda r, c: r <= c)
        before = _tri(LANES, lambda r, c: r < c)

        def q_loop(i, _):
            rows = pl.ds(pl.multiple_of(i * tile, tile), tile)
            c1_s[...] = jnp.zeros_like(c1_s)
            c2_s[...] = jnp.zeros_like(c2_s)

            def key_tile(j, diagonal):
                cols = pl.ds(pl.multiple_of(j * tile, tile), tile)
                kj, vj = kn_s[cols, :], vb_s[cols, :]
                strips = range(tile // strip)
                subs = [slice(r * strip, (r + 1) * strip) for r in strips]
                srows = [pl.ds(pl.multiple_of(i * tile + r * strip, strip), strip) for r in strips]
                sc = [_att_scores(qn_s[srows[r], :], kj, scale, r * strip if diagonal else None) for r in strips]
                dw = [_dot(dob_s[srows[r], :], vj, NT) for r in strips]
                upto_lr = [_scan_lanes(sc[r][1], upto, 2) for r in strips]
                w = [_masked(sc[r][2], jnp.exp(sc[r][0] + (r_ref[0, 0, srows[r], :] - (upto_lr[r][0] + c1_s[subs[r], :]))))
                     for r in strips]
                e = [w[r] * dw[r] for r in strips]
                pre = [_scan_lanes(e[r], before, 1) for r in strips]
                dz = [_masked(sc[r][2], (e[r] - jnp.exp(sc[r][0]) * (e[r] + (pre[r][0] + c2_s[subs[r], :]))) * scale)
                      for r in strips]
                for r in strips:
                    wb_s[subs[r], :] = w[r].astype(BF16)
                    dzb_s[subs[r], :] = dz[r].astype(BF16)
                    c1_s[subs[r], :] += upto_lr[r][1]
                    c2_s[subs[r], :] += pre[r][1]
                dqn_s[rows, :] += _dot(dzb_s[...], kj)
                dv_s[cols, :] += _dot(wb_s[...], dob_s[rows, :], TN)
                dkn_s[cols, :] += _dot(dzb_s[...], qn_s[rows, :], TN)

            def k_loop(j, _):
                key_tile(j, False)
                return 0

            lax.fori_loop(0, i, k_loop, 0)
            key_tile(i, True)
            return 0

        lax.fori_loop(0, nb, q_loop, 0)
        dq, dgq = _rms_bwd(qkv_ref[0, :, 0:hd], qg_ref[...], dqn_s[...])
        dk, dgk = _rms_bwd(qkv_ref[0, :, hd:2 * hd], kg_ref[...], dkn_s[...])
        dqkv_ref[0, :, 0:hd] = dq.astype(BF16)
        dqkv_ref[0, :, hd:2 * hd] = dk.astype(BF16)
        dqkv_ref[0, :, 2 * hd:3 * hd] = dv_s[...].astype(BF16)
        dqg_ref[0, 0] = jnp.sum(dgq, axis=0, keepdims=True)
        dkg_ref[0, 0] = jnp.sum(dgk, axis=0, keepdims=True)

    gain = pl.BlockSpec((1, hd), lambda b, h: (0, 0))
    dgain = pl.BlockSpec((1, 1, 1, hd), lambda b, h: (b, h, 0, 0))
    return _call(
        body, name=name, grid=(bsz, heads),
        in_specs=[pl.BlockSpec((1, s, 3 * hd), lambda b, h: (b, 0, h)), gain, gain,
                  pl.BlockSpec((1, 1, s, 1), lambda b, h: (b, h, 0, 0)),
                  pl.BlockSpec((1, s, hd), lambda b, h: (b, 0, h))],
        out_specs=[pl.BlockSpec((1, s, 3 * hd), lambda b, h: (b, 0, h)), dgain, dgain],
        out_shape=[jax.ShapeDtypeStruct((bsz, s, w3), BF16),
                   jax.ShapeDtypeStruct((bsz, heads, 1, hd), F32),
                   jax.ShapeDtypeStruct((bsz, heads, 1, hd), F32)],
        scratch_shapes=[pltpu.VMEM((s, hd), BF16)] * 4 + [pltpu.VMEM((s, hd), F32)] * 3
        + [pltpu.VMEM((tile, 1), F32)] * 2 + [pltpu.VMEM((tile, tile), BF16)] * 2,
        sem=("parallel", "parallel"), args=(qkv, qg.reshape(1, hd), kg.reshape(1, hd), rtot, do), rider=rider)


def _conv_pre(pad_ref, w_ref, b_ref, s):
    pre = b_ref[...]
    for i in range(CONV_WIDTH):
        off = SUBLANES - (CONV_WIDTH - 1) + i
        pre = pre + pad_ref[off:off + s, :] * w_ref[i:i + 1, :]
    return pre


def _conv_fwd(u, w, b, name, tc=256):
    bsz, s, c = u.shape
    tc = min(tc, c)

    def body(u_ref, w_ref, b_ref, a_ref, pad_s):
        pad_s[0:SUBLANES, :] = jnp.zeros((SUBLANES, tc), F32)
        pad_s[SUBLANES:SUBLANES + s, :] = u_ref[0]
        pre = _conv_pre(pad_s, w_ref, b_ref, s)
        a_ref[0] = pre * _sigmoid(pre)

    return pl.pallas_call(
        body, name=name, grid=(bsz, c // tc),
        in_specs=[pl.BlockSpec((1, s, tc), lambda i, j: (i, 0, j)),
                  pl.BlockSpec((CONV_WIDTH, tc), lambda i, j: (0, j)), pl.BlockSpec((1, tc), lambda i, j: (0, j))],
        out_specs=pl.BlockSpec((1, s, tc), lambda i, j: (i, 0, j)),
        out_shape=jax.ShapeDtypeStruct((bsz, s, c), F32),
        scratch_shapes=[pltpu.VMEM((s + SUBLANES, tc), F32)],
        compiler_params=_params(("parallel", "parallel")),
    )(u, w, b.reshape(1, c))


def _conv_bwd(u, w, b, da, name, tc=256):
    bsz, s, c = u.shape
    tc = min(tc, c)

    def body(u_ref, w_ref, b_ref, da_ref, du_ref, dw_ref, db_ref, pad_s, gpad_s):
        @pl.when(pl.program_id(1) == 0)
        def _():
            dw_ref[...] = jnp.zeros_like(dw_ref)
            db_ref[...] = jnp.zeros_like(db_ref)

        pad_s[0:SUBLANES, :] = jnp.zeros((SUBLANES, tc), F32)
        pad_s[SUBLANES:SUBLANES + s, :] = u_ref[0]
        pre = _conv_pre(pad_s, w_ref, b_ref, s)
        sg = _sigmoid(pre)
        dpre = da_ref[0] * (sg * (1.0 + pre * (1.0 - sg)))
        gpad_s[0:s, :] = dpre
        gpad_s[s:s + SUBLANES, :] = jnp.zeros((SUBLANES, tc), F32)
        du = jnp.zeros((s, tc), F32)
        for i in range(CONV_WIDTH):
            back = CONV_WIDTH - 1 - i
            du = du + gpad_s[back:back + s, :] * w_ref[i:i + 1, :]
            off = SUBLANES - (CONV_WIDTH - 1) + i
            dw_ref[i:i + 1, :] += jnp.sum(dpre * pad_s[off:off + s, :], axis=0, keepdims=True)
        du_ref[0] = du.astype(BF16)
        db_ref[...] += jnp.sum(dpre, axis=0, keepdims=True)

    blk = pl.BlockSpec((1, s, tc), lambda j, i: (i, 0, j))
    du, dw, db = pl.pallas_call(
        body, name=name, grid=(c // tc, bsz),
        in_specs=[blk, pl.BlockSpec((CONV_WIDTH, tc), lambda j, i: (0, j)),
                  pl.BlockSpec((1, tc), lambda j, i: (0, j)), blk],
        out_specs=[blk, pl.BlockSpec((CONV_WIDTH, tc), lambda j, i: (0, j)),
                   pl.BlockSpec((1, tc), lambda j, i: (0, j))],
        out_shape=[jax.ShapeDtypeStruct((bsz, s, c), BF16), jax.ShapeDtypeStruct((CONV_WIDTH, c), F32),
                   jax.ShapeDtypeStruct((1, c), F32)],
        scratch_shapes=[pltpu.VMEM((s + SUBLANES, tc), F32), pltpu.VMEM((s + SUBLANES, tc), F32)],
        compiler_params=_params(("parallel", "arbitrary")),
    )(u, w, b.reshape(1, c), da)
    return du, dw, db.reshape(c)


def _ssd_chunk_common(b_ref, c_ref, dt_ref, dtb_ref, alog_ref):
    bm, cm = b_ref[0], c_ref[0]
    draw = dt_ref[0] + dtb_ref[...]
    dt = _softplus(draw)
    a_row = -jnp.exp(alog_ref[...])
    da = dt * a_row
    n = SSD_CHUNK
    acum = _dot_exact(da, _tri(n, lambda r, c: r >= c), ones_left=True)
    acum_t = _dot_exact(da, _tri(n, lambda r, c: r <= c), dims=TN)
    cb = _dot(cm, bm, NT)
    return bm, cm, draw, dt, a_row, acum, acum_t, cb


def _row_totals(v):
    return _dot_exact(v, jnp.ones((v.shape[1], LANES), BF16), passes=2)


def _ssd_head_common(acum, acum_t, dt, cb, x, i):
    n, p = SSD_CHUNK, SSM_HEAD_DIM
    pick = (_iota2((LANES, LANES), 0) == i).astype(BF16)
    acol = _dot_exact(acum, pick)
    dtc = _dot_exact(dt, pick)[:, :p]
    arow = acum_t[i:i + 1, :]
    causal = _iota2((n, n), 0) >= _iota2((n, n), 1)
    lm = jnp.where(causal, jnp.exp(jnp.where(causal, acol - arow, 0.0)), 0.0)
    gm = cb * lm
    xh = x[:, i * p:(i + 1) * p]
    xdt = xh * dtc
    alast = acol[n - 1:n, :]
    dte = jnp.exp(alast - acol)
    return acol, lm, gm, dtc, xh, xdt, alast, dte


def _ssd_specs(s, wg, hg, rev):
    g, n, cl = SSM_GROUPS, SSM_STATE, SSD_CHUNK
    nc = s // cl
    boff, coff = (g * wg) // n, (g * wg) // n + g
    ci = (lambda c: nc - 1 - c) if rev else (lambda c: c)
    xblk = pl.BlockSpec((1, cl, wg), lambda b, k, c: (b, ci(c), k))
    bblk = pl.BlockSpec((1, cl, n), lambda b, k, c: (b, ci(c), boff + k))
    cblk = pl.BlockSpec((1, cl, n), lambda b, k, c: (b, ci(c), coff + k))
    nblk = pl.BlockSpec((1, cl, n), lambda b, k, c: (b, ci(c), k))
    dtblk = pl.BlockSpec((1, cl, LANES), lambda b, k, c: (b, ci(c), k))
    vec = pl.BlockSpec((1, LANES), lambda b, k, c: (0, k))
    hsblk = pl.BlockSpec((1, 1, 1, wg, n), lambda b, k, c: (b, k, ci(c), 0, 0))
    return nc, xblk, bblk, cblk, nblk, dtblk, vec, hsblk


def _ssd_fwd(xbc, dtraw, dtb, alog, dskip, hg, name, rider=None):
    bsz, s, _ = xbc.shape
    g, n, p = SSM_GROUPS, SSM_STATE, SSM_HEAD_DIM
    wg = hg * p
    nc, xblk, bblk, cblk, _, dtblk, vec, hsblk = _ssd_specs(s, wg, hg, False)

    def body(x_ref, b_ref, c_ref, dt_ref, dtb_ref, alog_ref, dsk_ref, y_ref, hs_ref, h_s):
        @pl.when(pl.program_id(2) == 0)
        def _():
            h_s[...] = jnp.zeros_like(h_s)

        bm, cm, _, dt, _, acum, acum_t, cb = _ssd_chunk_common(b_ref, c_ref, dt_ref, dtb_ref, alog_ref)
        x = x_ref[0]
        hs_ref[0, 0, 0] = h_s[...]
        hd_ = range(hg)
        hc = [_ssd_head_common(acum, acum_t, dt, cb, x, i) for i in hd_]
        hprev = [h_s[i * p:(i + 1) * p, :] for i in hd_]
        ydiag = [_dot(hc[i][2], hc[i][5]) for i in hd_]
        yoff = [_dot(cm, hprev[i], NT) for i in hd_]
        st = [_dot(hc[i][5] * hc[i][7][:, :p], bm, TN) for i in hd_]
        for i in hd_:
            acol, _, _, _, xh, _, alast, _ = hc[i]
            y_ref[0, :, i * p:(i + 1) * p] = ydiag[i] + yoff[i] * jnp.exp(acol[:, :p]) + xh * dsk_ref[:, i:i + 1]
            h_s[i * p:(i + 1) * p, :] = hprev[i] * jnp.exp(alast) + st[i]

    return _call(
        body, name=name, grid=(bsz, g, nc),
        in_specs=[xblk, bblk, cblk, dtblk, vec, vec, vec],
        out_specs=[xblk, hsblk],
        out_shape=[jax.ShapeDtypeStruct((bsz, s, g * wg), F32),
                   jax.ShapeDtypeStruct((bsz, g, nc, wg, n), F32)],
        scratch_shapes=[pltpu.VMEM((wg, n), F32)],
        sem=("parallel", "parallel", "arbitrary"), args=(xbc, xbc, xbc, dtraw, dtb, alog, dskip), rider=rider)


def _ssd_bwd(xbc, dtraw, dtb, alog, dskip, hs, dy, hg, name, rider=None):
    bsz, s, _ = xbc.shape
    g, n, p, cl = SSM_GROUPS, SSM_STATE, SSM_HEAD_DIM, SSD_CHUNK
    wg = hg * p
    nc, xblk, bblk, cblk, nblk, dtblk, vec, hsblk = _ssd_specs(s, wg, hg, True)

    def body(x_ref, b_ref, c_ref, dt_ref, dtb_ref, alog_ref, dsk_ref, hs_ref, dy_ref,
             dx_ref, db_ref, dc_ref, ddt_ref, dvec_ref, dh_s):
        @pl.when(pl.program_id(2) == 0)
        def _():
            dh_s[...] = jnp.zeros_like(dh_s)
            dvec_ref[...] = jnp.zeros_like(dvec_ref)

        lane = _iota2((cl, LANES), 1)
        sub = _iota2((LANES, cl), 0)
        lane1 = _iota2((1, LANES), 1)
        last_row = _iota2((cl, 1), 0) == cl - 1
        bm, cm, draw, dt, a_row, acum, acum_t, cb = _ssd_chunk_common(b_ref, c_ref, dt_ref, dtb_ref, alog_ref)
        x = x_ref[0]
        dyc = dy_ref[0]
        hd_ = range(hg)
        hc = [_ssd_head_common(acum, acum_t, dt, cb, x, i) for i in hd_]
        dyh = [dyc[:, i * p:(i + 1) * p] for i in hd_]
        hprev = [hs_ref[0, 0, 0, i * p:(i + 1) * p, :] for i in hd_]
        dhn = [dh_s[i * p:(i + 1) * p, :] for i in hd_]
        ea = [jnp.exp(hc[i][0]) for i in hd_]
        cd = [jnp.exp(hc[i][6]) for i in hd_]
        y0 = [_dot(cm, hprev[i], NT) for i in hd_]
        dxe = [_dot(bm, dhn[i], NT) for i in hd_]
        dgm = [_dot(dyh[i], hc[i][5], NT) for i in hd_]
        gdy = [_dot(hc[i][2], dyh[i], TN) for i in hd_]
        dy0 = [dyh[i] * ea[i][:, :p] for i in hd_]
        dcm_h = [_dot(dy0[i], hprev[i]) for i in hd_]
        dh_new = [_dot(dy0[i], cm, TN) + dhn[i] * cd[i] for i in hd_]
        dbm_h = [_dot(hc[i][5] * hc[i][7][:, :p], dhn[i]) for i in hd_]
        ws = [dgm[i] * hc[i][2] for i in hd_]
        dxdt = [dxe[i] * hc[i][7][:, :p] + gdy[i] for i in hd_]
        s_y0 = [_row_totals(dyh[i] * y0[i]) for i in hd_]
        s_xe = [_row_totals(dxe[i] * hc[i][5]) for i in hd_]
        s_ws = [_row_totals(ws[i]) for i in hd_]
        s_dt = [_row_totals(dxdt[i] * hc[i][4]) for i in hd_]
        s_dd = [_row_totals(dyh[i] * hc[i][4]) for i in hd_]
        s_hh = [_row_totals(dhn[i] * hprev[i]) for i in hd_]
        dcb = jnp.zeros((cl, cl), F32)
        dcm = jnp.zeros((cl, n), F32)
        dbm = jnp.zeros((cl, n), F32)
        da_col = jnp.zeros((cl, LANES), F32)
        da_row = jnp.zeros((LANES, cl), F32)
        ddt = jnp.zeros((cl, LANES), F32)
        dd = jnp.zeros((1, LANES), F32)
        for i in hd_:
            _, lm, _, dtc, _, _, _, dte = hc[i]
            dh_s[i * p:(i + 1) * p, :] = dh_new[i]
            dd = dd + jnp.where(lane1 == i, jnp.sum(s_dd[i], axis=0, keepdims=True), 0.0)
            t1 = s_xe[i] * dte
            d_alast = jnp.sum(s_hh[i], axis=0, keepdims=True) * cd[i] + jnp.sum(t1, axis=0, keepdims=True)
            dacol = s_y0[i] * ea[i] - t1 + s_ws[i] + jnp.where(last_row, d_alast, 0.0)
            dcb = dcb + dgm[i] * lm
            dcm = dcm + dcm_h[i]
            dbm = dbm + dbm_h[i]
            dx_ref[0, :, i * p:(i + 1) * p] = dxdt[i] * dtc + dyh[i] * dsk_ref[:, i:i + 1]
            da_col = jnp.where(lane == i, dacol, da_col)
            da_row = jnp.where(sub == i, -jnp.sum(ws[i], axis=0, keepdims=True), da_row)
            ddt = jnp.where(lane == i, s_dt[i], ddt)
        dc_ref[0] = dcm + _dot(dcb, bm)
        db_ref[0] = dbm + _dot(dcb, cm, TN)
        upper = _tri(cl, lambda r, k: r <= k)
        dda = _dot_exact(da_col, upper, ones_left=True) + _dot_exact(da_row, upper, dims=NT, ones_left=True)
        ddt = ddt + dda * a_row
        ddraw = ddt * _sigmoid(draw)
        ddt_ref[0] = ddraw.astype(BF16)
        dvec_ref[0, 0, 0:1, :] += jnp.sum(ddraw, axis=0, keepdims=True)
        dvec_ref[0, 0, 1:2, :] += jnp.sum(dda * dt, axis=0, keepdims=True) * a_row
        dvec_ref[0, 0, 2:3, :] += dd

    return _call(
        body, name=name, grid=(bsz, g, nc),
        in_specs=[xblk, bblk, cblk, dtblk, vec, vec, vec, hsblk, xblk],
        out_specs=[xblk, nblk, nblk, dtblk,
                   pl.BlockSpec((1, 1, SUBLANES, LANES), lambda b, k, c: (b, k, 0, 0))],
        out_shape=[jax.ShapeDtypeStruct((bsz, s, g * wg), F32), jax.ShapeDtypeStruct((bsz, s, g * n), F32),
                   jax.ShapeDtypeStruct((bsz, s, g * n), F32), jax.ShapeDtypeStruct((bsz, s, g * LANES), BF16),
                   jax.ShapeDtypeStruct((bsz, g, SUBLANES, LANES), F32)],
        scratch_shapes=[pltpu.VMEM((wg, n), F32)],
        sem=("parallel", "parallel", "arbitrary"), args=(xbc, xbc, xbc, dtraw, dtb, alog, dskip, hs, dy),
        rider=rider)


def _coords():
    return lax.axis_index("x"), lax.axis_index("y"), lax.axis_index("c")


def _other_chips(x, y):
    return [(1 - x, y), (x, 1 - y), (1 - x, 1 - y)]


def _remote(src, dst, send_sems, recv_sems, k, to):
    return pltpu.make_async_remote_copy(src_ref=src, dst_ref=dst, send_sem=send_sems.at[k],
                                        recv_sem=recv_sems.at[k], device_id=to, device_id_type=MESH)


def _standalone(rider, name):
    r_in, r_out, n_sems = len(rider["ins"]), len(rider["outs"]), rider["n_sems"]

    def body(*refs):
        rins, routs, (send_sems, recv_sems) = refs[:r_in], refs[r_in:r_in + r_out], refs[r_in + r_out:]
        cps = rider["copies"](rins, routs, send_sems, recv_sems)
        for cp in cps:
            cp.start()
        for cp in cps:
            cp.wait()

    res = pl.pallas_call(
        body, name=name, in_specs=[ANY] * r_in, out_specs=[ANY] * r_out, out_shape=list(rider["outs"]),
        scratch_shapes=[pltpu.SemaphoreType.DMA((n_sems,)), pltpu.SemaphoreType.DMA((n_sems,))],
        input_output_aliases=dict(rider["aliases"]),
    )(*rider["ins"])
    return list(res)


def _rows_of(shape):
    return shape[1] if len(shape) == 3 else shape[0]


def _slot(ref, j, rows):
    if len(ref.shape) == 3:
        return ref.at[j, rows]
    c = ref.shape[1] // N_CHIPS
    return ref.at[rows, pl.ds(pl.multiple_of(j * c, LANES), c)]


def _own_slot_set(shard, side_by_side):
    if side_by_side:
        return jnp.tile(shard, (1, N_CHIPS))
    return jnp.broadcast_to(shard[None], (N_CHIPS,) + shard.shape)


def _gather_chips_rider(shards, side_by_side):
    def copies(rins, routs, send_sems, recv_sems):
        x, y, c = _coords()
        me = 2 * x + y
        cps = []
        for q, (w_ref, o_ref) in enumerate(zip(rins[:len(shards)], routs, strict=True)):
            rh = w_ref.shape[0] // 2
            rows = pl.ds(c * rh, rh)
            for k, (px, py) in enumerate(_other_chips(x, y)):
                cps.append(_remote(w_ref.at[rows], _slot(o_ref, me, rows), send_sems, recv_sems, 3 * q + k,
                                   (px, py, c)))
        return cps

    bases = [_own_slot_set(w, side) for w, side in zip(shards, side_by_side, strict=True)]
    return dict(ins=list(shards) + bases, outs=[jax.ShapeDtypeStruct(b.shape, b.dtype) for b in bases],
                aliases={len(shards) + i: i for i in range(len(shards))}, n_sems=3 * len(shards), copies=copies)


def _gather_pair_rider(gathered):
    def copies(rins, routs, send_sems, recv_sems):
        x, y, c = _coords()
        cps = []
        for q, o_ref in enumerate(routs):
            rh = _rows_of(o_ref.shape) // 2
            for k, (px, py) in enumerate(_other_chips(x, y)):
                part = _slot(o_ref, 2 * px + py, pl.ds(c * rh, rh))
                cps.append(_remote(part, part, send_sems, recv_sems, 3 * q + k, (x, y, 1 - c)))
        return cps

    return dict(ins=list(gathered), outs=[jax.ShapeDtypeStruct(g.shape, g.dtype) for g in gathered],
                aliases={i: i for i in range(len(gathered))}, n_sems=3 * len(gathered), copies=copies)


def _reduce_pair_rider(grads):
    def copies(rins, routs, send_sems, recv_sems):
        x, y, c = _coords()
        cps, k = [], 0
        for g_ref, r_ref in zip(rins, routs, strict=True):
            rh = _rows_of(g_ref.shape) // 2
            rows = pl.ds((1 - c) * rh, rh)
            if len(g_ref.shape) == 3:
                cps.append(_remote(g_ref.at[:, rows], r_ref, send_sems, recv_sems, k, (x, y, 1 - c)))
                k += 1
            else:
                for j in range(N_CHIPS):
                    cps.append(_remote(_slot(g_ref, j, rows), r_ref.at[j], send_sems, recv_sems, k, (x, y, 1 - c)))
                    k += 1
        return cps

    def out_of(g):
        r, c = (g.shape[1], g.shape[2]) if g.ndim == 3 else (g.shape[0], g.shape[1] // N_CHIPS)
        return jax.ShapeDtypeStruct((N_CHIPS, r // 2, c), g.dtype)

    return dict(ins=list(grads), outs=[out_of(g) for g in grads], aliases={},
                n_sems=sum(1 if g.ndim == 3 else N_CHIPS for g in grads), copies=copies)


def _reduce_chips_rider(pair_sums):
    def copies(rins, routs, send_sems, recv_sems):
        x, y, c = _coords()
        cps = []
        for q, (p_ref, r_ref) in enumerate(zip(rins, routs, strict=True)):
            for k, (px, py) in enumerate(_other_chips(x, y)):
                cps.append(_remote(p_ref.at[2 * px + py], r_ref.at[k], send_sems, recv_sems, 3 * q + k, (px, py, c)))
        return cps

    return dict(ins=list(pair_sums), outs=[jax.ShapeDtypeStruct((3,) + p.shape[1:], p.dtype) for p in pair_sums],
                aliases={}, n_sems=3 * len(pair_sums), copies=copies)


def _reduce_finish_rider(sums, totals, layers, depth):
    def copies(rins, routs, send_sems, recv_sems):
        x, y, c = _coords()
        cps = []
        for q, (f_ref, o_ref) in enumerate(zip(rins[:len(sums)], routs, strict=True)):
            rh = f_ref.shape[0]
            cps.append(_remote(f_ref, o_ref.at[layers[q], pl.ds(c * rh, rh)], send_sems, recv_sems, q, (x, y, 1 - c)))
        return cps

    kept = [q for q, t in enumerate(totals) if t is not None]
    outs = [jax.ShapeDtypeStruct((depth, 2 * f.shape[0], f.shape[1]), F32) for f in sums]
    return dict(ins=list(sums) + [totals[q] for q in kept], outs=outs,
                aliases={len(sums) + k: q for k, q in enumerate(kept)}, n_sems=len(sums), copies=copies)


def _pair_add(gj, r1, ids, name, tr=256):
    stacked = gj.ndim == 3
    nj, rh, c = r1.shape
    tr = _fit(tr, rh, 16)
    nt = rh // tr

    def body(c_ref, chip_ref, g_ref, r_ref, p_ref, pb_ref):
        s = (g_ref[0] if stacked else g_ref[...]) + r_ref[0]
        pb_ref[0] = s.astype(BF16)

        @pl.when(pl.program_id(1) == chip_ref[0])
        def _():
            p_ref[...] = s

    blk_r = pl.BlockSpec((1, tr, c), lambda i, j, cr, jr: (j, i, 0))
    blk_g = (pl.BlockSpec((1, tr, c), lambda i, j, cr, jr: (j, cr[0] * nt + i, 0)) if stacked
             else pl.BlockSpec((tr, c), lambda i, j, cr, jr: (cr[0] * nt + i, j)))
    return pl.pallas_call(
        body, name=name,
        grid_spec=pltpu.PrefetchScalarGridSpec(
            num_scalar_prefetch=2, grid=(nt, nj), in_specs=[blk_g, blk_r],
            out_specs=[pl.BlockSpec((tr, c), lambda i, j, cr, jr: (i, 0)), blk_r]),
        out_shape=[jax.ShapeDtypeStruct((rh, c), F32), jax.ShapeDtypeStruct((nj, rh, c), BF16)],
        compiler_params=_params(("parallel", "arbitrary")),
    )(*ids, gj, r1)


def _chip_add(p, r2, name, tr=256):
    rh, c = p.shape
    tr = _fit(tr, rh, 16)

    def body(o_ref, r_ref, f_ref):
        f_ref[...] = ((o_ref[...] + r_ref[0].astype(F32)) + r_ref[1].astype(F32)) + r_ref[2].astype(F32)

    blk = pl.BlockSpec((tr, c), lambda i: (i, 0))
    return pl.pallas_call(
        body, name=name, grid=(rh // tr,),
        in_specs=[blk, pl.BlockSpec((3, tr, c), lambda i: (0, i, 0))], out_specs=blk,
        out_shape=jax.ShapeDtypeStruct((rh, c), F32),
        compiler_params=_params(("parallel",)),
    )(p, r2)


BIG = ["w_in", "w_out", "w_gate", "w_up", "w_down"]
LATE = ["w_out", "w_gate", "w_up", "w_down"]
FFN = ["w_gate", "w_up", "w_down"]
MIX = ["w_in", "w_out"]


SIDE_BY_SIDE = ("w_gate", "w_up")


class _GatherPlan:
    def __init__(self, late, next_in):
        self.late, self.next_in, self.parts = late, next_in, {}

    ICI = {"proj_qkv": ["w_out"], "attention_fwd": ["w_gate", "w_up"], "ssd_fwd": ["w_down"],
           "ffn_gate_up": ["w_in"]}
    D2D = {"merge_fwd": ["w_out", "w_gate"], "proj_out": ["w_up", "w_down"], "ffn_down": ["w_in"]}

    def _names(self, host):
        names = self.ICI.get(host) or self.D2D.get(host)
        return None if self.late is None or (names == ["w_in"] and self.next_in is None) else names

    def rider(self, host):
        names = self._names(host)
        if names is None:
            return None
        if host in self.D2D:
            return _gather_pair_rider([self.parts[n] for n in names])
        shards = [self.next_in] if names == ["w_in"] else [self.late[n] for n in names]
        return _gather_chips_rider(shards, [n in SIDE_BY_SIDE for n in names])

    def collect(self, host, outs):
        if outs:
            self.parts.update(zip(self._names(host), outs, strict=True))

    def late_gathered(self, names):
        return {n: self.parts[n] for n in names}

    def next_gathered(self):
        return self.parts["w_in"]


def _gather_now(shard, tag):
    part = _standalone(_gather_chips_rider([shard], [False]), f"allgather_chips_{tag}")
    return _standalone(_gather_pair_rider(part), f"allgather_pair_{tag}")[0]


class _ReducePlan:
    PAIR = {"ffn_down_dgrad": "mix", "ffn_gate_up_dgrad": "ffn"}
    CHIPS = {"ssd_bwd": "mix", "attention_bwd": "ffn"}

    def __init__(self, mix, mix_layer, ffn_layer, totals, depth, ids):
        self.groups = {} if mix is None else {"mix": (MIX, mix, mix_layer)}
        self.ffn_layer, self.totals, self.depth, self.ids = ffn_layer, dict(totals), depth, ids
        self.p, self.pb, self.f = {}, {}, {}

    def add_ffn(self, grads):
        if self.ids is not None:
            self.groups["ffn"] = (FFN, grads, self.ffn_layer)

    def _present(self):
        return [(n, layer) for names, _, layer in self.groups.values() for n in names if n in self.f]

    def rider(self, host):
        if host == "proj_in_dgrad":
            done = self._present()
            if not done:
                return None
            return _reduce_finish_rider([self.f[n] for n, _ in done], [self.totals.get(n) for n, _ in done],
                                        [layer for _, layer in done], self.depth)
        group = self.groups.get(self.PAIR.get(host) or self.CHIPS.get(host))
        if group is None:
            return None
        names, grads, _ = group
        return (_reduce_pair_rider([grads[n] for n in names]) if host in self.PAIR
                else _reduce_chips_rider([self.pb[n] for n in names]))

    def collect(self, host, outs):
        if not outs:
            return
        if host == "proj_in_dgrad":
            c = lax.axis_index("c")
            for (n, layer), t in zip(self._present(), outs, strict=True):
                self.totals[n] = lax.dynamic_update_slice(t, self.f[n][None], (layer, c * self.f[n].shape[0], 0))
            return
        names, grads, layer = self.groups[self.PAIR.get(host) or self.CHIPS.get(host)]
        for n, got in zip(names, outs, strict=True):
            if host in self.PAIR:
                self.p[n], self.pb[n] = _pair_add(grads[n], got, self.ids, f"rs_pair_add_{n}_layer{layer}")
            else:
                self.f[n] = _chip_add(self.p[n], got, f"rs_chip_add_{n}_layer{layer}")

    def run_now(self, tag):
        self.collect("ffn_down_dgrad", _standalone(self.rider("ffn_down_dgrad"), f"rs_pair_{tag}"))
        self.collect("ssd_bwd", _standalone(self.rider("ssd_bwd"), f"rs_chips_{tag}"))
        self.collect("proj_in_dgrad", _standalone(self.rider("proj_in_dgrad"), f"rs_finish_{tag}"))
        return self.totals


def _small_exchange(v, name, reduce):
    rows = v.shape[0]

    def body(v_ref, o_ref, *rest):
        buf = rest[0] if reduce else o_ref
        send_sems, recv_sems = rest[-2], rest[-1]
        x, y, c = _coords()
        me = 4 * x + 2 * y + c
        buf[me] = v_ref[...]
        cps = []
        for r in range(1, N_DEV):
            peer = (lax.bitwise_xor(x, (r >> 2) & 1), lax.bitwise_xor(y, (r >> 1) & 1), lax.bitwise_xor(c, r & 1))
            cps.append(_remote(v_ref, buf.at[me], send_sems, recv_sems, r - 1, peer))
        for cp in cps:
            cp.start()
        for r in range(1, N_DEV):
            src = buf.at[lax.bitwise_xor(me, r)]
            _remote(src, src, send_sems, recv_sems, r - 1, (x, y, c)).wait_recv()
        for cp in cps:
            cp.wait_send()
        if reduce:
            acc = buf[0]
            for d in range(1, N_DEV):
                acc = acc + buf[d]
            o_ref[...] = acc
            o_ref[0:1, :] = jnp.broadcast_to(jnp.sum(acc[0:1, :], axis=1, keepdims=True), (1, LANES))

    scratch = [pltpu.SemaphoreType.DMA((N_DEV - 1,)), pltpu.SemaphoreType.DMA((N_DEV - 1,))]
    if reduce:
        scratch = [pltpu.VMEM((N_DEV, rows, LANES), F32)] + scratch
    out_shape = (rows, LANES) if reduce else (N_DEV, rows, LANES)
    return pl.pallas_call(
        body, name=name, in_specs=[VMEM], out_specs=VMEM,
        out_shape=jax.ShapeDtypeStruct(out_shape, F32), scratch_shapes=scratch,
    )(v)


def _pack(parts):
    flat = []
    for a in parts:
        a = a.reshape(-1)
        flat.append(jnp.pad(a, (0, (-a.shape[0]) % LANES)))
    v = jnp.concatenate(flat)
    v = jnp.pad(v, (0, (-v.shape[0]) % (SUBLANES * LANES)))
    return v.reshape(-1, LANES)


def _unpack(slab, shapes):
    flat = slab.reshape(-1)
    out, off = [], 0
    for shp in shapes:
        size = 1
        for d in shp:
            size *= d
        out.append(flat[off:off + size].reshape(shp))
        off += size + (-size) % LANES
    return out


def _group_slots(a, hg):
    lead = a.shape[:-1]
    a = a.reshape(lead + (SSM_GROUPS, hg))
    a = jnp.pad(a, [(0, 0)] * len(lead) + [(0, 0), (0, LANES - hg)])
    return a.reshape(lead + (SSM_GROUPS * LANES,))


def _ungroup_slots(a, hg):
    lead = a.shape[:-1]
    return a.reshape(lead + (SSM_GROUPS, LANES))[..., :hg].reshape(lead + (SSM_GROUPS * hg,))


def _layer_fwd(x, p, bsz, s, plan, late_params=None):
    t, d = x.shape
    aw, sw, cd, hg = p["aw"], p["sw"], p["cd"], p["hg"]
    h = _rmsnorm_fwd(x, p["norm_mix"], "norm_mix_fwd")
    (qkv,), _ = _hosted_matmul([[(h, p["wqkv"], "nn")]], [], lambda accs, ex: (accs[0],),
                               [F32], t, 3 * aw, 1024, 512, "proj_qkv", plan)
    z = _mm(h, p["wz"], "nn", t, sw, F32, "proj_z", tm=1024, tn=512)
    xbc = _mm(h, p["wxbc"], "nn", t, cd, F32, "proj_xbc", tm=1024, tn=512)
    dtraw = _mm(h, p["wdt"], "nn", t, SSM_GROUPS * LANES, F32, "proj_dt", tm=1024, tn=SSM_GROUPS * LANES)
    qkv3 = qkv.reshape(bsz, s, 3 * aw)
    (o_att, rtot), sent = _attention_fwd(qkv3, p["q_gain"], p["k_gain"], "attention_fwd",
                                         rider=plan.rider("attention_fwd"))
    plan.collect("attention_fwd", sent)
    xbc3 = xbc.reshape(bsz, s, cd)
    xact = _conv_fwd(xbc3, p["conv_w"], p["conv_b"], "conv_fwd")
    dt3 = dtraw.reshape(bsz, s, SSM_GROUPS * LANES)
    (y, hs), sent = _ssd_fwd(xact, dt3, p["dt_bias"], p["a_log"], p["d_skip"], hg, "ssd_fwd",
                             rider=plan.rider("ssd_fwd"))
    plan.collect("ssd_fwd", sent)
    o2, y2 = o_att.reshape(t, aw), y.reshape(t, sw)
    (mix,), sent = _merge_fwd(o2, y2, z, p["attn_out_gain"], p["ssm_out_gain"], "merge_fwd",
                              rider=plan.rider("merge_fwd"))
    plan.collect("merge_fwd", sent)
    if plan.late is not None:
        p = {**p, **late_params(plan.late_gathered(["w_out", "w_gate"]))}
    (x1,), _ = _hosted_matmul([[(mix, p["wout"], "nn")]], [x], lambda accs, ex: (accs[0] + ex[0],),
                              [F32], t, d, 1024, 512, "proj_out", plan)
    if plan.late is not None:
        p = {**p, **late_params(plan.late_gathered(["w_up", "w_down"]))}
    dff = p["wg"].shape[1]
    h2 = _rmsnorm_fwd(x1, p["norm_ffn"], "norm_ffn_fwd")
    (gate, up, act), _ = _hosted_matmul(
        [[(h2, p["wg"], "nn")], [(h2, p["wu"], "nn")]], [], _swiglu_fwd_epilogue,
        [F32, F32, BF16], t, dff, 1024, 512, "ffn_gate_up", plan)
    (x2,), _ = _hosted_matmul([[(act, p["wd"], "nn")]], [x1], lambda accs, ex: (accs[0] + ex[0],),
                              [F32], t, d, 512, 512, "ffn_down", plan)
    saved = dict(x=x, h=h, qkv3=qkv3, z=z, xbc3=xbc3, dt3=dt3, o2=o2, rtot=rtot, xact=xact, hs=hs, y2=y2,
                 mix=mix, x1=x1, h2=h2, gate=gate, up=up, act=act)
    return x2, saved, p


def _hosted_matmul(groups, extras, epilogue, out_dtypes, m, n, tm, tn, name, plan):
    rider = plan.rider(name)
    if rider is None:
        return _matmul(groups, extras, epilogue, out_dtypes, m, n, tm, tn, name), []
    outs, sent = _matmul(groups, extras, epilogue, out_dtypes, m, n, tm, tn, name, rider=rider)
    plan.collect(name, sent)
    return outs, sent


def _layer_bwd(dx2, dx2b, p, sv, bsz, s, plan, ffn_to_chips=None):
    t, d = dx2.shape
    aw, sw, cd, hg = p["aw"], p["sw"], p["cd"], p["hg"]
    dff = p["wg"].shape[1]
    gr = {}
    (dgate, dup), _ = _hosted_matmul([[(dx2b, p["wd"], "nt")]], [sv["gate"], sv["up"]], _swiglu_bwd_epilogue,
                                     [BF16, BF16], t, dff, 1024, 512, "ffn_down_dgrad", plan)
    gr["wd"] = _mm(sv["act"], dx2b, "tn", dff, d, F32, "ffn_down_wgrad")
    gr["wg"] = _mm(sv["h2"], dgate, "tn", d, dff, F32, "ffn_gate_wgrad")
    gr["wu"] = _mm(sv["h2"], dup, "tn", d, dff, F32, "ffn_up_wgrad")
    if ffn_to_chips is not None:
        plan.add_ffn(ffn_to_chips(gr))
    (dh2,), _ = _hosted_matmul([[(dgate, p["wg"], "nt"), (dup, p["wu"], "nt")]], [], lambda accs, ex: (accs[0],),
                               [F32], t, d, 512, 256, "ffn_gate_up_dgrad", plan)
    dx1, dx1b, gr["norm_ffn"] = _rmsnorm_bwd(sv["x1"], p["norm_ffn"], dh2, dx2, "norm_ffn_bwd")
    dmix = _mm(dx1b, p["wout"], "nt", t, aw + sw, F32, "proj_out_dgrad", tm=1024)
    gr["wout"] = _mm(sv["mix"], dx1b, "tn", aw + sw, d, F32, "proj_out_wgrad")
    do, dy, dz, gr["attn_out_gain"], gr["ssm_out_gain"] = _merge_bwd(
        sv["o2"], sv["y2"], sv["z"], p["attn_out_gain"], p["ssm_out_gain"], dmix, "merge_bwd")
    (dxs, dbm, dcm, ddt, dvec), sent = _ssd_bwd(sv["xact"], sv["dt3"], p["dt_bias"], p["a_log"], p["d_skip"],
                                                sv["hs"], dy.reshape(bsz, s, sw), hg, "ssd_bwd",
                                                rider=plan.rider("ssd_bwd"))
    plan.collect("ssd_bwd", sent)
    dvec = jnp.sum(dvec, axis=0).reshape(SSM_GROUPS, SUBLANES, LANES)
    gr["dt_bias"], gr["a_log"], gr["d_skip"] = (dvec[:, k, :hg].reshape(-1) for k in range(3))
    dxact = jnp.concatenate([dxs, dbm, dcm], axis=-1)
    dxbc, gr["conv_w"], gr["conv_b"] = _conv_bwd(sv["xbc3"], p["conv_w"], p["conv_b"], dxact, "conv_bwd")
    (dqkv, dqg, dkg), sent = _attention_bwd(sv["qkv3"], p["q_gain"], p["k_gain"], sv["rtot"],
                                            do.reshape(bsz, s, aw), "attention_bwd",
                                            rider=plan.rider("attention_bwd"))
    plan.collect("attention_bwd", sent)
    gr["q_gain"] = jnp.sum(dqg, axis=(0, 1, 2))
    gr["k_gain"] = jnp.sum(dkg, axis=(0, 1, 2))
    dqkv, dxbc, ddt = dqkv.reshape(t, 3 * aw), dxbc.reshape(t, cd), ddt.reshape(t, SSM_GROUPS * LANES)
    (dh,), _ = _hosted_matmul(
        [[(dqkv, p["wqkv"], "nt"), (dz, p["wz"], "nt"), (dxbc, p["wxbc"], "nt"), (ddt, p["wdt"], "nt")]],
        [], lambda accs, ex: (accs[0],), [F32], t, d, 512, 512, "proj_in_dgrad", plan)
    h = sv["h"]
    gr["wqkv"] = _mm(h, dqkv, "tn", d, 3 * aw, F32, "proj_qkv_wgrad")
    gr["wz"] = _mm(h, dz, "tn", d, sw, F32, "proj_z_wgrad")
    gr["wxbc"] = _mm(h, dxbc, "tn", d, cd, F32, "proj_xbc_wgrad")
    gr["wdt"] = _mm(h, ddt, "tn", d, SSM_GROUPS * LANES, F32, "proj_dt_wgrad", tn=SSM_GROUPS * LANES)
    dx, dxb, gr["norm_mix"] = _rmsnorm_bwd(sv["x"], p["norm_mix"], dh, dx1, "norm_mix_bwd")
    return dx, dxb, gr


SMALL = ["norm_mix", "q_gain", "k_gain", "conv_w", "conv_b", "dt_bias", "a_log", "d_skip",
         "attn_out_gain", "ssm_out_gain", "norm_ffn"]
ORDER = ["norm_mix", "w_in", "q_gain", "k_gain", "conv_w", "conv_b", "dt_bias", "a_log", "d_skip",
         "attn_out_gain", "ssm_out_gain", "w_out", "norm_ffn", "w_gate", "w_up", "w_down"]


def kernel(x, norm_mix, w_in, q_gain, k_gain, conv_w, conv_b, dt_bias, a_log, d_skip, attn_out_gain, ssm_out_gain, w_out, norm_ffn, w_gate, w_up, w_down, loss_target, m_norm_mix, m_w_in, m_q_gain, m_k_gain, m_conv_w, m_conv_b, m_dt_bias, m_a_log, m_d_skip, m_attn_out_gain, m_ssm_out_gain, m_w_out, m_norm_ffn, m_w_gate, m_w_up, m_w_down, v_norm_mix, v_w_in, v_q_gain, v_k_gain, v_conv_w, v_conv_b, v_dt_bias, v_a_log, v_d_skip, v_attn_out_gain, v_ssm_out_gain, v_w_out, v_norm_ffn, v_w_gate, v_w_up, v_w_down):
    w = dict(norm_mix=norm_mix, w_in=w_in, q_gain=q_gain, k_gain=k_gain, conv_w=conv_w, conv_b=conv_b,
             dt_bias=dt_bias, a_log=a_log, d_skip=d_skip, attn_out_gain=attn_out_gain, ssm_out_gain=ssm_out_gain,
             w_out=w_out, norm_ffn=norm_ffn, w_gate=w_gate, w_up=w_up, w_down=w_down)
    mom = dict(norm_mix=m_norm_mix, w_in=m_w_in, q_gain=m_q_gain, k_gain=m_k_gain, conv_w=m_conv_w,
               conv_b=m_conv_b, dt_bias=m_dt_bias, a_log=m_a_log, d_skip=m_d_skip,
               attn_out_gain=m_attn_out_gain, ssm_out_gain=m_ssm_out_gain, w_out=m_w_out, norm_ffn=m_norm_ffn,
               w_gate=m_w_gate, w_up=m_w_up, w_down=m_w_down)
    var = dict(norm_mix=v_norm_mix, w_in=v_w_in, q_gain=v_q_gain, k_gain=v_k_gain, conv_w=v_conv_w,
               conv_b=v_conv_b, dt_bias=v_dt_bias, a_log=v_a_log, d_skip=v_d_skip,
               attn_out_gain=v_attn_out_gain, ssm_out_gain=v_ssm_out_gain, w_out=v_w_out, norm_ffn=v_norm_ffn,
               w_gate=v_w_gate, w_up=v_w_up, w_down=v_w_down)

    bsz, s, d = x.shape
    t = bsz * s
    depth = norm_mix.shape[0]
    aw = attn_out_gain.shape[1]
    sw = ssm_out_gain.shape[1]
    cd = conv_b.shape[1]
    hs_n = dt_bias.shape[1]
    hg = hs_n // SSM_GROUPS
    heads = aw // ATT_HEAD_DIM
    in_dim = 3 * aw + sw + cd + hs_n
    dff = w_gate.shape[2] * N_CHIPS
    cs = conv_w.shape[2]
    my_chip = 2 * lax.axis_index("x") + lax.axis_index("y")
    ids = (lax.axis_index("c").astype(jnp.int32).reshape(1), my_chip.astype(jnp.int32).reshape(1))
    wb ={n: w[n].astype(BF16) for n in BIG}
    conv_all = _small_exchange(_pack([conv_w]), "allgather_conv_w", False)
    conv_full = jnp.concatenate(
        [_unpack(conv_all[2 * j], [conv_w.shape])[0] for j in range(N_CHIPS)], axis=-1)

    def in_params(l, gat_in):
        win = jnp.transpose(gat_in, (1, 0, 2)).reshape(d, in_dim)
        wqkv = win[:, :3 * aw].reshape(d, 3, heads, ATT_HEAD_DIM)
        wqkv = jnp.transpose(wqkv, (0, 2, 1, 3)).reshape(d, 3 * aw)
        return dict(aw=aw, sw=sw, cd=cd, hg=hg, norm_mix=norm_mix[l], wqkv=wqkv, wz=win[:, 3 * aw:3 * aw + sw],
                    wxbc=win[:, 3 * aw + sw:3 * aw + sw + cd], wdt=_group_slots(win[:, 3 * aw + sw + cd:], hg),
                    q_gain=q_gain[l], k_gain=k_gain[l], conv_w=conv_full[l], conv_b=conv_b[l],
                    dt_bias=_group_slots(dt_bias[l], hg).reshape(1, -1),
                    a_log=_group_slots(a_log[l], hg).reshape(1, -1),
                    d_skip=_group_slots(d_skip[l], hg).reshape(1, -1),
                    attn_out_gain=attn_out_gain[l], ssm_out_gain=ssm_out_gain[l], norm_ffn=norm_ffn[l])

    def late_params(gat):
        shape = {"w_out": ("wout", (aw + sw, d)), "w_down": ("wd", (dff, d)),
                 "w_gate": ("wg", (d, dff)), "w_up": ("wu", (d, dff))}
        return {shape[n][0]: a.reshape(shape[n][1]) for n, a in gat.items()}

    def ffn_to_chips(gr):
        return {"w_gate": gr["wg"], "w_up": gr["wu"], "w_down": gr["wd"].reshape(N_CHIPS, dff // N_CHIPS, d)}

    def mix_to_chips(gr):
        gqkv = gr["wqkv"].reshape(d, heads, 3, ATT_HEAD_DIM)
        gqkv = jnp.transpose(gqkv, (0, 2, 1, 3)).reshape(d, 3 * aw)
        gin = jnp.concatenate([gqkv, gr["wz"], gr["wxbc"], _ungroup_slots(gr["wdt"], hg)], axis=-1)
        return {"w_in": jnp.transpose(gin.reshape(d, N_CHIPS, in_dim // N_CHIPS), (1, 0, 2)),
                "w_out": gr["wout"].reshape(N_CHIPS, (aw + sw) // N_CHIPS, d)}

    xt = x.reshape(t, d)
    saved, params = [], []
    gat_in = _gather_now(wb["w_in"][0], "first")
    for l in range(depth):
        plan = _GatherPlan({n: wb[n][l] for n in LATE}, wb["w_in"][l + 1] if l + 1 < depth else None)
        xt, sv, p = _layer_fwd(xt, in_params(l, gat_in), bsz, s, plan, late_params)
        saved.append(sv)
        params.append(p)
        if l + 1 < depth:
            gat_in = plan.next_gathered()
    dxt, dxb, loss_lanes = _loss_head(xt, loss_target.reshape(t, d), "loss_head")

    grads = [None] * depth
    pending, totals = None, {}
    for l in reversed(range(depth)):
        plan = _ReducePlan(pending, l + 1, l, totals, depth, ids)
        dxt, dxb, grads[l] = _layer_bwd(dxt, dxb, params[l], saved[l], bsz, s, plan, ffn_to_chips)
        totals = plan.totals
        pending = mix_to_chips(grads[l])
    g = _ReducePlan(pending, 0, None, totals, depth, ids).run_now("first_layer")
    grad_x = dxt.reshape(bsz, s, d)

    def stack(name):
        return jnp.stack([grads[l][name] for l in range(depth)])

    small_shapes = [(1, LANES)] + [(depth, CONV_WIDTH, cd) if n == "conv_w" else w[n].shape for n in SMALL]
    small = _small_exchange(_pack([loss_lanes] + [stack(n) for n in SMALL]), "allreduce_small", True)
    small = _unpack(small, small_shapes)
    loss = small[0][0, 0]
    for n, a in zip(SMALL, small[1:], strict=True):
        g[n] = a
    g["conv_w"] = lax.dynamic_slice_in_dim(g["conv_w"], my_chip * cs, cs, axis=2)

    delta, new_m, new_v = {}, {}, {}
    for n in BIG:
        shp = w[n].shape
        two_d = (shp[0] * shp[1], shp[2])
        dl, nm, nv = _adamw(w[n].reshape(two_d), g[n].reshape(two_d), mom[n].reshape(two_d),
                            var[n].reshape(two_d), f"adamw_{n}")
        delta[n], new_m[n], new_v[n] = dl.reshape(shp), nm.reshape(shp), nv.reshape(shp)
    shapes = [w[n].shape for n in SMALL]
    dl, nm, nv = _adamw(_pack([w[n] for n in SMALL]), _pack([g[n] for n in SMALL]),
                        _pack([mom[n] for n in SMALL]), _pack([var[n] for n in SMALL]), "adamw_small")
    for n, a, b, c in zip(SMALL, _unpack(dl, shapes), _unpack(nm, shapes), _unpack(nv, shapes), strict=True):
        delta[n], new_m[n], new_v[n] = a, b, c

    return (loss, grad_x, *[g[n] for n in ORDER], *[delta[n] for n in ORDER],
            *[new_m[n] for n in ORDER], *[new_v[n] for n in ORDER])
```
